```python
import math
import jax, jax.numpy as jnp
from jax import lax
import numpy as np

D_MODEL = 4096
BATCH = 2
SEQ = 4096
DEPTH = 2

HEAD_DIM = 128
A_HEADS = 8
A_WIDTH = A_HEADS * HEAD_DIM
A_SUB = HEAD_DIM // 2
B_GROUPS = ((128, 1), (512, 4), (2048, 16))
B_HEADS_PER_GROUP = 3
B_HEADS = B_HEADS_PER_GROUP * len(B_GROUPS)
B_WIDTH = B_HEADS * HEAD_DIM
C_WINDOWS = (2, 4, 8, 16)
C_GROUPS = len(C_WINDOWS)
C_WIDTH = 1024
C_GROUP_DIM = C_WIDTH // C_GROUPS
D_WIDTH = D_MODEL - A_WIDTH - B_WIDTH - C_WIDTH
CONV_WIDTH = 3
IN_SIZES = (A_WIDTH, A_WIDTH, A_WIDTH, B_WIDTH, B_WIDTH, B_WIDTH, C_WIDTH, D_WIDTH, D_WIDTH, D_WIDTH)
IN_WIDTH = sum(IN_SIZES)
IN_SPLITS = [int(v) for v in np.cumsum(IN_SIZES)[:-1]]
D_FF = 4 * D_MODEL
ROPE_THETA = 10000.0
Q_BLOCK = 128
NORM_EPS = 1e-6
DIFF_EPS = 1e-5

kernel_name = "hybrid_parallel_heads_diffattn_dilated_pool_shortconv"


def rms_norm(x, g, eps=NORM_EPS):
    xf = x.astype(jnp.float32)
    y = xf * lax.rsqrt(jnp.mean(xf * xf, axis=-1, keepdims=True) + eps)
    return (y * g.astype(jnp.float32)).astype(x.dtype)


def rope_tables(seq, dim):
    inv = ROPE_THETA ** (-jnp.arange(0, dim, 2, dtype=jnp.float32) / dim)
    ang = jnp.arange(seq, dtype=jnp.float32)[:, None] * inv[None, :]
    return jnp.cos(ang), jnp.sin(ang)


def apply_rope(x, cos, sin):
    shape = (1, x.shape[1]) + (1,) * (x.ndim - 3) + (cos.shape[-1],)
    c = cos.reshape(shape)
    s = sin.reshape(shape)
    xf = x.astype(jnp.float32)
    x1, x2 = jnp.split(xf, 2, axis=-1)
    out = jnp.concatenate([x1 * c - x2 * s, x1 * s + x2 * c], axis=-1)
    return out.astype(x.dtype)


def diff_attention(q, k, v, lam, subln_g, lam_init):
    b, s, h = q.shape[:3]
    nb = s // Q_BLOCK
    scale = A_SUB ** -0.5
    qb = q.reshape(b, nb, Q_BLOCK, h, 2, A_SUB).swapaxes(0, 1)
    kpos = jnp.arange(s)

    def block(args):
        qi, start = args
        sc = jnp.einsum('bqhcd,bkhcd->bchqk', qi, k, preferred_element_type=jnp.float32) * scale
        qpos = start + jnp.arange(Q_BLOCK)
        mask = qpos[:, None] >= kpos[None, :]
        p = jax.nn.softmax(jnp.where(mask, sc, -jnp.inf), axis=-1)
        w = p[:, 0] - lam * p[:, 1]
        return jnp.einsum('bhqk,bkhd->bqhd', w.astype(v.dtype), v)

    o = lax.map(block, (qb, jnp.arange(nb) * Q_BLOCK))
    o = o.swapaxes(0, 1).reshape(b, s, h, HEAD_DIM)
    o = rms_norm(o, subln_g, DIFF_EPS) * (1.0 - lam_init)
    return o.reshape(b, s, h * HEAD_DIM)


def dilated_group_attention(q, k, v, window, dilation):
    b, s, h, dh = q.shape
    n_keys = window // dilation + 1
    nb = s // Q_BLOCK
    offs = dilation * jnp.arange(n_keys)
    scale = dh ** -0.5
    qb = q.reshape(b, nb, Q_BLOCK, h, dh).swapaxes(0, 1)

    def block(args):
        qi, start = args
        idx = (start + jnp.arange(Q_BLOCK))[:, None] - offs[None, :]
        valid = idx >= 0
        idxc = jnp.maximum(idx, 0)
        kg = k[:, idxc]
        vg = v[:, idxc]
        sc = jnp.einsum('bqhd,bqjhd->bhqj', qi, kg, preferred_element_type=jnp.float32) * scale
        sc = jnp.where(valid[None, None], sc, -jnp.inf)
        lse = jax.nn.logsumexp(sc, axis=-1, keepdims=True)
        p = jnp.exp(sc - lse)
        o = jnp.einsum('bhqj,bqjhd->bqhd', p.astype(v.dtype), vg)
        return o, lse[..., 0].transpose(0, 2, 1)

    o, lse = lax.map(block, (qb, jnp.arange(nb) * Q_BLOCK))
    o = o.swapaxes(0, 1).reshape(b, s, h, dh)
    lse = lse.swapaxes(0, 1).reshape(b, s, h)
    return o, lse


def dilated_mixture(q, k, v):
    b, s = q.shape[:2]
    n_g = len(B_GROUPS)
    q = q.reshape(b, s, n_g, B_HEADS_PER_GROUP, HEAD_DIM)
    k = k.reshape(b, s, n_g, B_HEADS_PER_GROUP, HEAD_DIM)
    v = v.reshape(b, s, n_g, B_HEADS_PER_GROUP, HEAD_DIM)
    outs, lses = [], []
    for g, (window, dilation) in enumerate(B_GROUPS):
        o, l = dilated_group_attention(q[:, :, g], k[:, :, g], v[:, :, g], window, dilation)
        outs.append(o)
        lses.append(l)
    o = jnp.stack(outs, axis=2)
    alpha = jax.nn.softmax(jnp.stack(lses, axis=2), axis=2)
    o = (o.astype(jnp.float32) * alpha[..., None]).astype(q.dtype)
    return o.reshape(b, s, B_WIDTH)


def multiscale_pool(u, pool_w, pool_scale):
    b, s, _ = u.shape
    uf = u.astype(jnp.float32).reshape(b, s, C_GROUPS, C_GROUP_DIM)
    cs = jnp.concatenate([jnp.zeros((b, 1, C_GROUPS, C_GROUP_DIM), jnp.float32),
                          jnp.cumsum(uf, axis=1)], axis=1)
    t = jnp.arange(s)[:, None]
    win = jnp.array(C_WINDOWS, dtype=jnp.int32)[None, :]
    lo = jnp.maximum(t + 1 - win, 0)
    cnt = jnp.minimum(t + 1, win).astype(jnp.float32)
    gidx = jnp.arange(C_GROUPS)[None, :]
    mean = (cs[:, 1:] - cs[:, lo, gidx]) / cnt[None, :, :, None]
    pooled = mean - uf
    y = jnp.einsum('bsgc,gcd->bsgd', pooled, pool_w.astype(jnp.float32))
    y = y.reshape(b, s, C_WIDTH) * pool_scale.astype(jnp.float32)
    return y.astype(u.dtype)


def short_conv_mixer(gate_b, gate_c, hx, conv_w):
    z = gate_c * hx
    y = lax.conv_general_dilated(z, conv_w[:, None, :].astype(z.dtype), window_strides=(1,),
                                 padding=[(CONV_WIDTH - 1, 0)],
                                 dimension_numbers=('NWC', 'WIO', 'NWC'),
                                 feature_group_count=z.shape[-1])
    return gate_b * y


def setup_inputs(seed: int = 0) -> dict:
    key = jax.random.key(seed)
    ks = jax.random.split(key, 14)
    f32 = jnp.float32
    nrm = lambda k, shape, sc: jax.random.normal(k, shape, f32) * sc
    return {
        "x": nrm(ks[0], (BATCH, SEQ, D_MODEL), 1.0),
        "w_in": nrm(ks[1], (DEPTH, D_MODEL, IN_WIDTH), D_MODEL ** -0.5),
        "w_out": nrm(ks[2], (DEPTH, D_MODEL, D_MODEL), D_MODEL ** -0.5),
        "norm_mix": 1.0 + nrm(ks[3], (DEPTH, D_MODEL), 0.02),
        "norm_mlp": 1.0 + nrm(ks[4], (DEPTH, D_MODEL), 0.02),
        "diff_lambda": nrm(ks[5], (DEPTH, 4, A_SUB), 0.1),
        "diff_subln": 1.0 + nrm(ks[6], (DEPTH, HEAD_DIM), 0.02),
        "pool_w": nrm(ks[7], (DEPTH, C_GROUPS, C_GROUP_DIM, C_GROUP_DIM), C_GROUP_DIM ** -0.5),
        "pool_scale": 1.0 + nrm(ks[8], (DEPTH, C_WIDTH), 0.1),
        "conv_w": nrm(ks[9], (DEPTH, CONV_WIDTH, D_WIDTH), CONV_WIDTH ** -0.5),
        "w_up": nrm(ks[10], (DEPTH, D_MODEL, D_FF), D_MODEL ** -0.5),
        "w_down": nrm(ks[11], (DEPTH, D_FF, D_MODEL), D_FF ** -0.5),
        "norm_final": 1.0 + nrm(ks[12], (D_MODEL,), 0.02),
    }


def reference(x, w_in, w_out, norm_mix, norm_mlp, diff_lambda, diff_subln, pool_w, pool_scale,
              conv_w, w_up, w_down, norm_final):
    b, s, _ = x.shape
    cos_a, sin_a = rope_tables(s, A_SUB)
    cos_b, sin_b = rope_tables(s, HEAD_DIM)
    for l in range(DEPTH):
        h = rms_norm(x, norm_mix[l])
        proj = jnp.einsum('bsd,de->bse', h, w_in[l])
        qa, ka, va, qb, kb, vb, u, gate_b, gate_c, hd = jnp.split(proj, IN_SPLITS, axis=-1)

        lam_init = 0.8 - 0.6 * math.exp(-0.3 * l)
        lp = diff_lambda[l].astype(jnp.float32)
        lam = jnp.exp(jnp.sum(lp[0] * lp[1])) - jnp.exp(jnp.sum(lp[2] * lp[3])) + lam_init
        qa = apply_rope(qa.reshape(b, s, A_HEADS, 2, A_SUB), cos_a, sin_a)
        ka = apply_rope(ka.reshape(b, s, A_HEADS, 2, A_SUB), cos_a, sin_a)
        va = va.reshape(b, s, A_HEADS, HEAD_DIM)
        out_a = diff_attention(qa, ka, va, lam, diff_subln[l], lam_init)

        qb = apply_rope(qb.reshape(b, s, B_HEADS, HEAD_DIM), cos_b, sin_b)
        kb = apply_rope(kb.reshape(b, s, B_HEADS, HEAD_DIM), cos_b, sin_b)
        vb = vb.reshape(b, s, B_HEADS, HEAD_DIM)
        out_b = dilated_mixture(qb, kb, vb)

        out_c = multiscale_pool(u, pool_w[l], pool_scale[l])

        out_d = short_conv_mixer(gate_b, gate_c, hd, conv_w[l])

        mix = jnp.concatenate([out_a, out_b, out_c, out_d], axis=-1)
        x = x + jnp.einsum('bse,ed->bsd', mix, w_out[l])

        h = rms_norm(x, norm_mlp[l])
        act = jnp.square(jax.nn.relu(jnp.einsum('bsd,df->bsf', h, w_up[l])))
        x = x + jnp.einsum('bsf,fd->bsd', act, w_down[l])
    return rms_norm(x, norm_final)
```

```python
import functools
import math

import jax
import jax.numpy as jnp
from jax import lax
from jax.experimental import pallas as pl
from jax.experimental.pallas import tpu as pltpu

D_MODEL = 4096
HEAD_DIM = 128
A_HEADS = 8
A_WIDTH = A_HEADS * HEAD_DIM
A_SUB = HEAD_DIM // 2
B_GROUPS = ((128, 1), (512, 4), (2048, 16))
B_HEADS_PER_GROUP = 3
B_GROUP_WIDTH = B_HEADS_PER_GROUP * HEAD_DIM
B_WIDTH = B_GROUP_WIDTH * len(B_GROUPS)
C_WINDOWS = (2, 4, 8, 16)
C_WIDTH = 1024
C_GROUP_DIM = C_WIDTH // len(C_WINDOWS)
D_WIDTH = D_MODEL - A_WIDTH - B_WIDTH - C_WIDTH
CONV_WIDTH = 3
D_FF = 4 * D_MODEL
ROPE_THETA = 10000.0
NORM_EPS = 1e-6
DIFF_EPS = 1e-5

LANES = 128
VMEM_LIMIT_BYTES = 56 * 1024 * 1024
MASKED_SCORE = -1e30

F32 = jnp.float32
BF16 = jnp.bfloat16


def _params(*semantics):
    return pltpu.CompilerParams(dimension_semantics=semantics, vmem_limit_bytes=VMEM_LIMIT_BYTES)


def _rmsnorm_kernel(x_ref, g_ref, o_ref, *, eps):
    x = x_ref[...]
    ms = jnp.mean(x * x, axis=-1, keepdims=True)
    o_ref[...] = (x * lax.rsqrt(ms + eps) * g_ref[...]).astype(o_ref.dtype)


def _rmsnorm(x, g, out_dtype, tm=256):
    m, d = x.shape
    return pl.pallas_call(
        functools.partial(_rmsnorm_kernel, eps=NORM_EPS),
        grid=(m // tm,),
        in_specs=[pl.BlockSpec((tm, d), lambda i: (i, 0)),
                  pl.BlockSpec((1, d), lambda i: (0, 0))],
        out_specs=pl.BlockSpec((tm, d), lambda i: (i, 0)),
        out_shape=jax.ShapeDtypeStruct((m, d), out_dtype),
        compiler_params=_params("parallel"),
        name="rmsnorm",
    )(x, g.reshape(1, d))


def _rotate(y, cos_ref, sin_refs, shifts):
    outs = []
    for c in range(y.shape[1] // LANES):
        yc = y[:, c * LANES:(c + 1) * LANES]
        oc = yc * cos_ref[...]
        for s_ref, shift in zip(sin_refs, shifts):
            oc = oc + pltpu.roll(yc, shift, 1) * s_ref[...]
        outs.append(oc)
    return jnp.concatenate(outs, axis=1)


def _mm_rope_kernel(a_ref, w_ref, cos_ref, *rest, shifts, n_rope_blocks, q_scale):
    sin_refs, o_ref = rest[:-1], rest[-1]
    j = pl.program_id(1)
    acc = jnp.dot(a_ref[...], w_ref[...], preferred_element_type=F32)

    @pl.when(j < n_rope_blocks)
    def _():
        scale = jnp.where(j == 0, q_scale, 1.0).astype(F32)
        o_ref[...] = (_rotate(acc, cos_ref, sin_refs, shifts) * scale).astype(o_ref.dtype)

    @pl.when(j >= n_rope_blocks)
    def _():
        o_ref[...] = acc.astype(o_ref.dtype)


def _mm_rope(a, w, tables, shifts, n_rope_blocks, q_scale, bn, seq, bm=1024):
    m, k = a.shape
    n = w.shape[1]
    tab_blocks = seq // bm
    tab_spec = pl.BlockSpec((bm, LANES), lambda i, j: (i % tab_blocks, 0))
    return pl.pallas_call(
        functools.partial(_mm_rope_kernel, shifts=shifts, n_rope_blocks=n_rope_blocks, q_scale=q_scale),
        grid=(m // bm, n // bn),
        in_specs=[pl.BlockSpec((bm, k), lambda i, j: (i, 0)),
                  pl.BlockSpec((k, bn), lambda i, j: (0, j))] + [tab_spec] * len(tables),
        out_specs=pl.BlockSpec((bm, bn), lambda i, j: (i, j)),
        out_shape=jax.ShapeDtypeStruct((m, n), BF16),
        compiler_params=_params("parallel", "arbitrary"),
        name="in_proj_rope",
    )(a, w, *tables)


def _mm_kernel(a_ref, w_ref, o_ref, *, relu_sq):
    acc = jnp.dot(a_ref[...], w_ref[...], preferred_element_type=F32)
    if relu_sq:
        acc = jnp.square(jnp.maximum(acc, 0.0))
    o_ref[...] = acc.astype(o_ref.dtype)


def _mm(a, w, out_dtype, bn, relu_sq=False, bm=1024, name="matmul"):
    m, k = a.shape
    n = w.shape[1]
    return pl.pallas_call(
        functools.partial(_mm_kernel, relu_sq=relu_sq),
        grid=(m // bm, n // bn),
        in_specs=[pl.BlockSpec((bm, k), lambda i, j: (i, 0)),
                  pl.BlockSpec((k, bn), lambda i, j: (0, j))],
        out_specs=pl.BlockSpec((bm, bn), lambda i, j: (i, j)),
        out_shape=jax.ShapeDtypeStruct((m, n), out_dtype),
        compiler_params=_params("parallel", "arbitrary"),
        name=name,
    )(a, w)


def _mm_residual_kernel(a_ref, w_ref, r_ref, o_ref, acc_ref):
    kk = pl.program_id(2)

    @pl.when(kk == 0)
    def _():
        acc_ref[...] = r_ref[...]

    acc_ref[...] += jnp.dot(a_ref[...], w_ref[...], preferred_element_type=F32)

    @pl.when(kk == pl.num_programs(2) - 1)
    def _():
        o_ref[...] = acc_ref[...]


def _mm_residual(a, w, r, bm=1024, bn=1024, bk=2048):
    m, k = a.shape
    n = w.shape[1]
    return pl.pallas_call(
        _mm_residual_kernel,
        grid=(m // bm, n // bn, k // bk),
        in_specs=[pl.BlockSpec((bm, bk), lambda i, j, kk: (i, kk)),
                  pl.BlockSpec((bk, bn), lambda i, j, kk: (kk, j)),
                  pl.BlockSpec((bm, bn), lambda i, j, kk: (i, j))],
        out_specs=pl.BlockSpec((bm, bn), lambda i, j, kk: (i, j)),
        out_shape=jax.ShapeDtypeStruct((m, n), F32),
        scratch_shapes=[pltpu.VMEM((bm, bn), F32)],
        compiler_params=_params("parallel", "parallel", "arbitrary"),
        name="mlp_down",
    )(a, w, r)


def _out_proj_kernel(a_ref, b_ref, c_ref, d_ref, w_ref, r_ref, o_ref):
    acc = r_ref[...] + jnp.dot(a_ref[...], w_ref[0:A_WIDTH, :], preferred_element_type=F32)
    off = A_WIDTH
    for g in range(len(B_GROUPS)):
        acc += jnp.dot(b_ref[g], w_ref[off:off + B_GROUP_WIDTH, :], preferred_element_type=F32)
        off += B_GROUP_WIDTH
    acc += jnp.dot(c_ref[...], w_ref[off:off + C_WIDTH, :], preferred_element_type=F32)
    off += C_WIDTH
    acc += jnp.dot(d_ref[...], w_ref[off:off + D_WIDTH, :], preferred_element_type=F32)
    o_ref[...] = acc


def _out_proj(out_a, out_b, out_c, out_d, w, r, bm=1024, bn=1024):
    m = out_a.shape[0]
    n = w.shape[1]
    n_groups = len(B_GROUPS)
    return pl.pallas_call(
        _out_proj_kernel,
        grid=(m // bm, n // bn),
        in_specs=[pl.BlockSpec((bm, A_WIDTH), lambda i, j: (i, 0)),
                  pl.BlockSpec((n_groups, bm, B_GROUP_WIDTH), lambda i, j: (0, i, 0)),
                  pl.BlockSpec((bm, C_WIDTH), lambda i, j: (i, 0)),
                  pl.BlockSpec((bm, D_WIDTH), lambda i, j: (i, 0)),
                  pl.BlockSpec((D_MODEL, bn), lambda i, j: (0, j)),
                  pl.BlockSpec((bm, bn), lambda i, j: (i, j))],
        out_specs=pl.BlockSpec((bm, bn), lambda i, j: (i, j)),
        out_shape=jax.ShapeDtypeStruct((m, n), F32),
        compiler_params=_params("parallel", "arbitrary"),
        name="out_proj",
    )(out_a, out_b, out_c, out_d, w, r)


def _online_softmax_step(s, v, m, l, acc):
    m_new = jnp.maximum(m, jnp.max(s, axis=-1, keepdims=True))
    alpha = jnp.exp(m - m_new)
    p = jnp.exp(s - m_new)
    l_new = alpha * l + jnp.sum(p, axis=-1, keepdims=True)
    acc_new = alpha * acc + jnp.dot(p.astype(BF16), v, preferred_element_type=F32)
    return m_new, l_new, acc_new


def _attn_a_kernel(q_ref, k_ref, v_ref, lam_ref, g_ref, o_ref, *, tq, tk, lam_init):
    qi = pl.program_id(2)
    q = q_ref[...]
    lane = lax.broadcasted_iota(jnp.int32, q.shape, 1)
    zero = jnp.zeros_like(q)
    qq = jnp.concatenate([jnp.where(lane < A_SUB, q, zero), jnp.where(lane >= A_SUB, q, zero)], axis=0)

    def scores(j):
        k = k_ref[pl.ds(pl.multiple_of(j * tk, tk), tk), :]
        v = v_ref[pl.ds(pl.multiple_of(j * tk, tk), tk), :]
        s = lax.dot_general(qq, k, (((1,), (1,)), ((), ())), preferred_element_type=F32)
        return s, v

    def body(j, carry):
        s, v = scores(j)
        return _online_softmax_step(s, v, *carry)

    init = (jnp.full((2 * tq, 1), MASKED_SCORE, F32), jnp.zeros((2 * tq, 1), F32),
            jnp.zeros((2 * tq, HEAD_DIM), F32))
    carry = lax.fori_loop(0, qi, body, init)
    s, v = scores(qi)
    row = lax.broadcasted_iota(jnp.int32, (2 * tq, tk), 0)
    col = lax.broadcasted_iota(jnp.int32, (2 * tq, tk), 1)
    row = jnp.where(row >= tq, row - tq, row)
    s = jnp.where(row >= col, s, MASKED_SCORE)
    _, l, acc = _online_softmax_step(s, v, *carry)

    o = acc / l
    lp = lam_ref[...]
    lam = (jnp.exp(jnp.sum(lp[0:1] * lp[1:2], axis=-1, keepdims=True))
           - jnp.exp(jnp.sum(lp[2:3] * lp[3:4], axis=-1, keepdims=True)) + lam_init)
    o = o[:tq] - lam * o[tq:]
    ms = jnp.mean(o * o, axis=-1, keepdims=True)
    o = o * lax.rsqrt(ms + DIFF_EPS) * g_ref[...]
    o_ref[...] = (o * (1.0 - lam_init)).astype(o_ref.dtype)


def _attn_a(qkv, lam_params, subln_g, lam_init, batch, seq, tq=256):
    nq = seq // tq
    return pl.pallas_call(
        functools.partial(_attn_a_kernel, tq=tq, tk=tq, lam_init=lam_init),
        grid=(batch, A_HEADS, nq),
        in_specs=[pl.BlockSpec((tq, HEAD_DIM), lambda b, h, i: (b * nq + i, h)),
                  pl.BlockSpec((seq, HEAD_DIM), lambda b, h, i: (b, A_HEADS + h)),
                  pl.BlockSpec((seq, HEAD_DIM), lambda b, h, i: (b, 2 * A_HEADS + h)),
                  pl.BlockSpec((4, A_SUB), lambda b, h, i: (0, 0)),
                  pl.BlockSpec((1, HEAD_DIM), lambda b, h, i: (0, 0))],
        out_specs=pl.BlockSpec((tq, HEAD_DIM), lambda b, h, i: (b * nq + i, h)),
        out_shape=jax.ShapeDtypeStruct((batch * seq, A_WIDTH), BF16),
        compiler_params=_params("parallel", "parallel", "arbitrary"),
        name="diff_attention",
    )(qkv, qkv, qkv, lam_params, subln_g.reshape(1, HEAD_DIM))


def _attn_b_kernel(*refs, tq):
    n_groups = len(B_GROUPS)
    q_refs = refs[0:n_groups]
    k_refs = refs[n_groups:2 * n_groups]
    v_refs = refs[2 * n_groups:3 * n_groups]
    o_ref = refs[3 * n_groups]
    qi = pl.program_id(2)
    tk = tq
    scale = HEAD_DIM ** -0.5
    base_delta = (lax.broadcasted_iota(jnp.int32, (tq, tk), 0)
                  - lax.broadcasted_iota(jnp.int32, (tq, tk), 1))

    outs, lses = [], []
    for g, (window, dilation) in enumerate(B_GROUPS):
        q = q_refs[g][...]
        k_ref, v_ref = k_refs[g], v_refs[g]
        back_blocks = -(-window // tk)
        dil_ok = (base_delta & (dilation - 1)) == 0

        def body(j, carry, k_ref=k_ref, v_ref=v_ref, q=q, dil_ok=dil_ok, window=window):
            k = k_ref[pl.ds(pl.multiple_of(j * tk, tk), tk), :]
            v = v_ref[pl.ds(pl.multiple_of(j * tk, tk), tk), :]
            s = lax.dot_general(q, k, (((1,), (1,)), ((), ())), preferred_element_type=F32) * scale
            delta = base_delta + (qi - j) * tk
            valid = dil_ok & (delta >= 0) & (delta <= window)
            s = jnp.where(valid, s, MASKED_SCORE)
            return _online_softmax_step(s, v, *carry)

        init = (jnp.full((tq, 1), MASKED_SCORE, F32), jnp.zeros((tq, 1), F32),
                jnp.zeros((tq, HEAD_DIM), F32))
        m, l, acc = lax.fori_loop(jnp.maximum(qi - back_blocks, 0), qi + 1, body, init)
        outs.append(acc / l)
        lses.append(m + jnp.log(l))

    lse_max = functools.reduce(jnp.maximum, lses)
    weights = [jnp.exp(x - lse_max) for x in lses]
    denom = functools.reduce(lambda a, b: a + b, weights)
    for g in range(n_groups):
        o_ref[g] = (outs[g] * (weights[g] / denom)).astype(o_ref.dtype)


def _attn_b(qkv, batch, seq, tq=256):
    nq = seq // tq
    n_groups = len(B_GROUPS)
    heads = B_HEADS_PER_GROUP * n_groups

    def q_spec(g):
        return pl.BlockSpec((tq, HEAD_DIM), lambda b, h, i: (b * nq + i, g * B_HEADS_PER_GROUP + h))

    def kv_spec(g, part):
        return pl.BlockSpec((seq, HEAD_DIM), lambda b, h, i: (b, part * heads + g * B_HEADS_PER_GROUP + h))

    in_specs = ([q_spec(g) for g in range(n_groups)] + [kv_spec(g, 1) for g in range(n_groups)]
                + [kv_spec(g, 2) for g in range(n_groups)])
    return pl.pallas_call(
        functools.partial(_attn_b_kernel, tq=tq),
        grid=(batch, B_HEADS_PER_GROUP, nq),
        in_specs=in_specs,
        out_specs=pl.BlockSpec((n_groups, tq, HEAD_DIM), lambda b, h, i: (0, b * nq + i, h)),
        out_shape=jax.ShapeDtypeStruct((n_groups, batch * seq, B_GROUP_WIDTH), BF16),
        compiler_params=_params("parallel", "parallel", "arbitrary"),
        name="dilated_attention",
    )(*([qkv] * (3 * n_groups)))


def _shift_rows(x, k, row):
    return jnp.where(row >= k, pltpu.roll(x, k, 0), 0.0)


def _pool_kernel(u_ref, w_ref, scale_ref, o_ref):
    g = pl.program_id(1)
    u = u_ref[...]
    row = lax.broadcasted_iota(jnp.int32, u.shape, 0)
    window = jnp.left_shift(2, g)
    total = u
    step = 1
    while step < max(C_WINDOWS):
        widened = total + _shift_rows(total, step, row)
        total = jnp.where(step < window, widened, total)
        step *= 2
    count = jnp.minimum(row + 1, window).astype(F32)
    pooled = total / count - u
    y = jnp.dot(pooled.astype(BF16), w_ref[0], preferred_element_type=F32)
    o_ref[...] = (y * scale_ref[...]).astype(o_ref.dtype)


def _pool(u, pool_w, pool_scale, batch, seq):
    n_groups = len(C_WINDOWS)
    return pl.pallas_call(
        _pool_kernel,
        grid=(batch, n_groups),
        in_specs=[pl.BlockSpec((seq, C_GROUP_DIM), lambda b, g: (b, g)),
                  pl.BlockSpec((1, C_GROUP_DIM, C_GROUP_DIM), lambda b, g: (g, 0, 0)),
                  pl.BlockSpec((1, C_GROUP_DIM), lambda b, g: (0, g))],
        out_specs=pl.BlockSpec((seq, C_GROUP_DIM), lambda b, g: (b, g)),
        out_shape=jax.ShapeDtypeStruct((batch * seq, C_WIDTH), BF16),
        compiler_params=_params("parallel", "parallel"),
        name="multiscale_pool",
    )(u, pool_w, pool_scale.reshape(1, C_WIDTH))


def _conv_kernel(gb_ref, gc_ref, h_ref, w_ref, o_ref):
    z = gc_ref[...] * h_ref[...]
    row = lax.broadcasted_iota(jnp.int32, z.shape, 0)
    w = w_ref[...]
    y = w[CONV_WIDTH - 1:CONV_WIDTH] * z
    for tap in range(1, CONV_WIDTH):
        y = y + w[CONV_WIDTH - 1 - tap:CONV_WIDTH - tap] * _shift_rows(z, tap, row)
    o_ref[...] = (gb_ref[...] * y).astype(o_ref.dtype)


def _short_conv(gates, conv_w, batch, seq):
    n_col = D_WIDTH // LANES

    def spec(part):
        return pl.BlockSpec((seq, LANES), lambda b, c: (b, part * n_col + c))

    return pl.pallas_call(
        _conv_kernel,
        grid=(batch, n_col),
        in_specs=[spec(0), spec(1), spec(2), pl.BlockSpec((CONV_WIDTH, LANES), lambda b, c: (0, c))],
        out_specs=pl.BlockSpec((seq, LANES), lambda b, c: (b, c)),
        out_shape=jax.ShapeDtypeStruct((batch * seq, D_WIDTH), BF16),
        compiler_params=_params("parallel", "parallel"),
        name="short_conv",
    )(gates, gates, gates, conv_w)


def _rope_tables(seq, dim):
    half = dim // 2
    inv = ROPE_THETA ** (-jnp.arange(0, dim, 2, dtype=F32) / dim)
    ang = jnp.arange(seq, dtype=F32)[:, None] * inv[None, :]
    cos, sin = jnp.cos(ang), jnp.sin(ang)
    reps = LANES // dim
    cos_t = jnp.tile(jnp.concatenate([cos, cos], axis=1), (1, reps))
    zeros = jnp.zeros_like(sin)
    if dim == LANES:
        return (cos_t, jnp.concatenate([-sin, sin], axis=1)), (half,)
    upper = jnp.tile(jnp.concatenate([zeros, sin], axis=1), (1, reps))
    lower = jnp.tile(jnp.concatenate([-sin, zeros], axis=1), (1, reps))
    return (cos_t, upper, lower), (half, LANES - half)


def kernel(x, w_in, w_out, norm_mix, norm_mlp, diff_lambda, diff_subln, pool_w, pool_scale, conv_w,
           w_up, w_down, norm_final):
    batch, seq, d_model = x.shape
    depth = w_in.shape[0]
    m = batch * seq
    x = x.reshape(m, d_model)

    tables_a, shifts_a = _rope_tables(seq, A_SUB)
    tables_b, shifts_b = _rope_tables(seq, HEAD_DIM)

    a_end = 3 * A_WIDTH
    b_end = a_end + 3 * B_WIDTH
    c_end = b_end + C_WIDTH

    for l in range(depth):
        wl = w_in[l]
        w_a = wl[:, :a_end].astype(BF16)
        w_b = wl[:, a_end:b_end].astype(BF16)
        w_c = wl[:, b_end:c_end].astype(BF16)
        w_d = wl[:, c_end:].astype(BF16)
        lam_init = 0.8 - 0.6 * math.exp(-0.3 * l)

        h = _rmsnorm(x, norm_mix[l], BF16)
        qkv_a = _mm_rope(h, w_a, tables_a, shifts_a, 2, A_SUB ** -0.5, A_WIDTH, seq)
        qkv_b = _mm_rope(h, w_b, tables_b, shifts_b, 2, 1.0, B_WIDTH, seq)
        u = _mm(h, w_c, F32, C_WIDTH, name="in_proj_pool")
        gates = _mm(h, w_d, F32, D_WIDTH, name="in_proj_conv")

        out_a = _attn_a(qkv_a, diff_lambda[l], diff_subln[l], lam_init, batch, seq)
        out_b = _attn_b(qkv_b, batch, seq)
        out_c = _pool(u, pool_w[l].astype(BF16), pool_scale[l], batch, seq)
        out_d = _short_conv(gates, conv_w[l], batch, seq)

        x = _out_proj(out_a, out_b, out_c, out_d, w_out[l].astype(BF16), x)

        h = _rmsnorm(x, norm_mlp[l], BF16)
        act = _mm(h, w_up[l].astype(BF16), BF16, 1024, relu_sq=True, name="mlp_up")
        x = _mm_residual(act, w_down[l].astype(BF16), x)

    out = _rmsnorm(x, norm_final, F32)
    return out.reshape(batch, seq, d_model)
```

```python
import functools
import math

import jax
import jax.numpy as jnp
from jax import lax
from jax.experimental import pallas as pl
from jax.experimental.pallas import tpu as pltpu

D_MODEL = 4096
HEAD_DIM = 128
A_HEADS = 8
A_WIDTH = A_HEADS * HEAD_DIM
A_SUB = HEAD_DIM // 2
B_GROUPS = ((128, 1), (512, 4), (2048, 16))
B_HEADS_PER_GROUP = 3
B_GROUP_WIDTH = B_HEADS_PER_GROUP * HEAD_DIM
B_WIDTH = B_GROUP_WIDTH * len(B_GROUPS)
C_WINDOWS = (2, 4, 8, 16)
C_WIDTH = 1024
C_GROUP_DIM = C_WIDTH // len(C_WINDOWS)
D_WIDTH = D_MODEL - A_WIDTH - B_WIDTH - C_WIDTH
CONV_WIDTH = 3
D_FF = 4 * D_MODEL
ROPE_THETA = 10000.0
NORM_EPS = 1e-6
DIFF_EPS = 1e-5

LANES = 128
VMEM_LIMIT_BYTES = 56 * 1024 * 1024
MASKED_SCORE = -1e30

F32 = jnp.float32
BF16 = jnp.bfloat16


def _params(*semantics):
    return pltpu.CompilerParams(dimension_semantics=semantics, vmem_limit_bytes=VMEM_LIMIT_BYTES)


def _rmsnorm_kernel(x_ref, g_ref, o_ref, *, eps):
    x = x_ref[...]
    ms = jnp.mean(x * x, axis=-1, keepdims=True)
    o_ref[...] = (x * lax.rsqrt(ms + eps) * g_ref[...]).astype(o_ref.dtype)


def _rmsnorm(x, g, out_dtype, tm=256):
    m, d = x.shape
    return pl.pallas_call(
        functools.partial(_rmsnorm_kernel, eps=NORM_EPS),
        grid=(m // tm,),
        in_specs=[pl.BlockSpec((tm, d), lambda i: (i, 0)),
                  pl.BlockSpec((1, d), lambda i: (0, 0))],
        out_specs=pl.BlockSpec((tm, d), lambda i: (i, 0)),
        out_shape=jax.ShapeDtypeStruct((m, d), out_dtype),
        compiler_params=_params("parallel"),
        name="rmsnorm",
    )(x, g.reshape(1, d))


def _rotate(y, cos_ref, sin_refs, shifts):
    outs = []
    for c in range(y.shape[1] // LANES):
        yc = y[:, c * LANES:(c + 1) * LANES]
        oc = yc * cos_ref[...]
        for s_ref, shift in zip(sin_refs, shifts):
            oc = oc + pltpu.roll(yc, shift, 1) * s_ref[...]
        outs.append(oc)
    return jnp.concatenate(outs, axis=1)


def _mm_rope_kernel(a_ref, w_ref, cos_ref, *rest, shifts, n_rope_blocks, q_scale):
    sin_refs, o_ref = rest[:-1], rest[-1]
    j = pl.program_id(1)
    acc = jnp.dot(a_ref[...], w_ref[...], preferred_element_type=F32)

    @pl.when(j < n_rope_blocks)
    def _():
        scale = jnp.where(j == 0, q_scale, 1.0).astype(F32)
        o_ref[...] = (_rotate(acc, cos_ref, sin_refs, shifts) * scale).astype(o_ref.dtype)

    @pl.when(j >= n_rope_blocks)
    def _():
        o_ref[...] = acc.astype(o_ref.dtype)


def _mm_rope(a, w, tables, shifts, n_rope_blocks, q_scale, bn, seq, bm=1024):
    m, k = a.shape
    n = w.shape[1]
    tab_blocks = seq // bm
    tab_spec = pl.BlockSpec((bm, LANES), lambda i, j: (i % tab_blocks, 0))
    return pl.pallas_call(
        functools.partial(_mm_rope_kernel, shifts=shifts, n_rope_blocks=n_rope_blocks, q_scale=q_scale),
        grid=(m // bm, n // bn),
        in_specs=[pl.BlockSpec((bm, k), lambda i, j: (i, 0)),
                  pl.BlockSpec((k, bn), lambda i, j: (0, j))] + [tab_spec] * len(tables),
        out_specs=pl.BlockSpec((bm, bn), lambda i, j: (i, j)),
        out_shape=jax.ShapeDtypeStruct((m, n), BF16),
        compiler_params=_params("parallel", "arbitrary"),
        name="in_proj_rope",
    )(a, w, *tables)


def _mm_kernel(a_ref, w_ref, o_ref, *, relu_sq):
    acc = jnp.dot(a_ref[...], w_ref[...], preferred_element_type=F32)
    if relu_sq:
        acc = jnp.square(jnp.maximum(acc, 0.0))
    o_ref[...] = acc.astype(o_ref.dtype)


def _mm(a, w, out_dtype, bn, relu_sq=False, bm=1024, name="matmul"):
    m, k = a.shape
    n = w.shape[1]
    return pl.pallas_call(
        functools.partial(_mm_kernel, relu_sq=relu_sq),
        grid=(m // bm, n // bn),
        in_specs=[pl.BlockSpec((bm, k), lambda i, j: (i, 0)),
                  pl.BlockSpec((k, bn), lambda i, j: (0, j))],
        out_specs=pl.BlockSpec((bm, bn), lambda i, j: (i, j)),
        out_shape=jax.ShapeDtypeStruct((m, n), out_dtype),
        compiler_params=_params("parallel", "arbitrary"),
        name=name,
    )(a, w)


def _mm_transposed_kernel(wt_ref, a_ref, o_ref, *, t):
    acc = lax.dot_general(wt_ref[...], a_ref[...], (((1,), (1,)), ((), ())), preferred_element_type=F32)
    for c in range(acc.shape[1] // t):
        o_ref[c] = acc[:, c * t:(c + 1) * t].astype(o_ref.dtype)


def _mm_transposed(a, wt, t, bm=1024):
    m, k = a.shape
    n = wt.shape[0]
    return pl.pallas_call(
        functools.partial(_mm_transposed_kernel, t=t),
        grid=(m // bm,),
        in_specs=[pl.BlockSpec((n, k), lambda i: (0, 0)),
                  pl.BlockSpec((bm, k), lambda i: (i, 0))],
        out_specs=pl.BlockSpec((bm // t, n, t), lambda i: (i, 0, 0)),
        out_shape=jax.ShapeDtypeStruct((m // t, n, t), BF16),
        compiler_params=_params("parallel"),
        name="in_proj_vt",
    )(wt, a)


def _mm_residual_kernel(a_ref, w_ref, r_ref, o_ref, acc_ref):
    kk = pl.program_id(2)

    @pl.when(kk == 0)
    def _():
        acc_ref[...] = r_ref[...]

    acc_ref[...] += jnp.dot(a_ref[...], w_ref[...], preferred_element_type=F32)

    @pl.when(kk == pl.num_programs(2) - 1)
    def _():
        o_ref[...] = acc_ref[...]


def _mm_residual(a, w, r, bm=1024, bn=1024, bk=2048):
    m, k = a.shape
    n = w.shape[1]
    return pl.pallas_call(
        _mm_residual_kernel,
        grid=(m // bm, n // bn, k // bk),
        in_specs=[pl.BlockSpec((bm, bk), lambda i, j, kk: (i, kk)),
                  pl.BlockSpec((bk, bn), lambda i, j, kk: (kk, j)),
                  pl.BlockSpec((bm, bn), lambda i, j, kk: (i, j))],
        out_specs=pl.BlockSpec((bm, bn), lambda i, j, kk: (i, j)),
        out_shape=jax.ShapeDtypeStruct((m, n), F32),
        scratch_shapes=[pltpu.VMEM((bm, bn), F32)],
        compiler_params=_params("parallel", "parallel", "arbitrary"),
        name="mlp_down",
    )(a, w, r)


def _out_proj_kernel(a_ref, b_ref, c_ref, d_ref, w_ref, r_ref, o_ref):
    acc = r_ref[...] + jnp.dot(a_ref[...], w_ref[0:A_WIDTH, :], preferred_element_type=F32)
    off = A_WIDTH
    for g in range(len(B_GROUPS)):
        acc += jnp.dot(b_ref[g], w_ref[off:off + B_GROUP_WIDTH, :], preferred_element_type=F32)
        off += B_GROUP_WIDTH
    acc += jnp.dot(c_ref[...], w_ref[off:off + C_WIDTH, :], preferred_element_type=F32)
    off += C_WIDTH
    acc += jnp.dot(d_ref[...], w_ref[off:off + D_WIDTH, :], preferred_element_type=F32)
    o_ref[...] = acc


def _out_proj(out_a, out_b, out_c, out_d, w, r, bm=1024, bn=1024):
    m = out_a.shape[0]
    n = w.shape[1]
    n_groups = len(B_GROUPS)
    return pl.pallas_call(
        _out_proj_kernel,
        grid=(m // bm, n // bn),
        in_specs=[pl.BlockSpec((bm, A_WIDTH), lambda i, j: (i, 0)),
                  pl.BlockSpec((n_groups, bm, B_GROUP_WIDTH), lambda i, j: (0, i, 0)),
                  pl.BlockSpec((bm, C_WIDTH), lambda i, j: (i, 0)),
                  pl.BlockSpec((bm, D_WIDTH), lambda i, j: (i, 0)),
                  pl.BlockSpec((D_MODEL, bn), lambda i, j: (0, j)),
                  pl.BlockSpec((bm, bn), lambda i, j: (i, j))],
        out_specs=pl.BlockSpec((bm, bn), lambda i, j: (i, j)),
        out_shape=jax.ShapeDtypeStruct((m, n), F32),
        compiler_params=_params("parallel", "arbitrary"),
        name="out_proj",
    )(out_a, out_b, out_c, out_d, w, r)


def _online_softmax_step(s, v, m, l, acc):
    m_new = jnp.maximum(m, jnp.max(s, axis=-1, keepdims=True))
    alpha = jnp.exp(m - m_new)
    p = jnp.exp(s - m_new)
    l_new = alpha * l + jnp.sum(p, axis=-1, keepdims=True)
    acc_new = alpha * acc + jnp.dot(p.astype(BF16), v, preferred_element_type=F32)
    return m_new, l_new, acc_new


def _attn_a_kernel(q_ref, k_ref, vt_ref, lam_ref, g_ref, o_ref, *, t, heads, lam_init):
    qi = pl.program_id(2)
    lane = lax.broadcasted_iota(jnp.int32, (t, HEAD_DIM), 1)
    qqs = []
    for h in range(heads):
        q = q_ref[:, h * HEAD_DIM:(h + 1) * HEAD_DIM]
        zero = jnp.zeros_like(q)
        qqs.append(jnp.concatenate([jnp.where(lane < A_SUB, q, zero), jnp.where(lane >= A_SUB, q, zero)], axis=0))

    def step(j, carries, masked):
        out = []
        for h in range(heads):
            m, l, acc = carries[h]
            k = k_ref[pl.ds(pl.multiple_of(j * t, t), t), h * HEAD_DIM:(h + 1) * HEAD_DIM]
            vt = vt_ref[j, h * HEAD_DIM:(h + 1) * HEAD_DIM, :]
            s = lax.dot_general(k, qqs[h], (((1,), (1,)), ((), ())), preferred_element_type=F32)
            if masked:
                key = lax.broadcasted_iota(jnp.int32, s.shape, 0)
                qry = lax.broadcasted_iota(jnp.int32, s.shape, 1)
                qry = jnp.where(qry >= t, qry - t, qry)
                s = jnp.where(key <= qry, s, MASKED_SCORE)
            m_new = jnp.maximum(m, jnp.max(s, axis=0, keepdims=True))
            alpha = jnp.exp(m - m_new)
            p = jnp.exp(s - m_new)
            l_new = alpha * l + jnp.sum(p, axis=0, keepdims=True)
            acc_new = alpha * acc + jnp.dot(vt, p.astype(BF16), preferred_element_type=F32)
            out.append((m_new, l_new, acc_new))
        return tuple(out)

    init = tuple((jnp.full((1, 2 * t), MASKED_SCORE, F32), jnp.zeros((1, 2 * t), F32),
                  jnp.zeros((HEAD_DIM, 2 * t), F32)) for _ in range(heads))
    carries = lax.fori_loop(0, qi, functools.partial(step, masked=False), init)
    carries = step(qi, carries, masked=True)

    lp = lam_ref[...]
    lam = (jnp.exp(jnp.sum(lp[0:1] * lp[1:2], axis=-1, keepdims=True))
           - jnp.exp(jnp.sum(lp[2:3] * lp[3:4], axis=-1, keepdims=True)) + lam_init)
    for h in range(heads):
        _, l, acc = carries[h]
        o = acc / l
        o = (o[:, :t] - lam * o[:, t:]).T
        ms = jnp.mean(o * o, axis=-1, keepdims=True)
        o = o * lax.rsqrt(ms + DIFF_EPS) * g_ref[...]
        o_ref[:, h * HEAD_DIM:(h + 1) * HEAD_DIM] = (o * (1.0 - lam_init)).astype(o_ref.dtype)


def _attn_a(qk, vt, lam_params, subln_g, lam_init, batch, seq, t, heads=2):
    nq = seq // t
    width = heads * HEAD_DIM
    n_pairs = A_HEADS // heads
    return pl.pallas_call(
        functools.partial(_attn_a_kernel, t=t, heads=heads, lam_init=lam_init),
        grid=(batch, n_pairs, nq),
        in_specs=[pl.BlockSpec((t, width), lambda b, h, i: (b * nq + i, h)),
                  pl.BlockSpec((seq, width), lambda b, h, i: (b, n_pairs + h)),
                  pl.BlockSpec((nq, width, t), lambda b, h, i: (b, h, 0)),
                  pl.BlockSpec((4, A_SUB), lambda b, h, i: (0, 0)),
                  pl.BlockSpec((1, HEAD_DIM), lambda b, h, i: (0, 0))],
        out_specs=pl.BlockSpec((t, width), lambda b, h, i: (b * nq + i, h)),
        out_shape=jax.ShapeDtypeStruct((batch * seq, A_WIDTH), BF16),
        compiler_params=_params("parallel", "parallel", "arbitrary"),
        name="diff_attention",
    )(qk, qk, vt, lam_params, subln_g.reshape(1, HEAD_DIM))


def _attn_b_kernel(*refs, tq):
    n_groups = len(B_GROUPS)
    q_refs = refs[0:n_groups]
    k_refs = refs[n_groups:2 * n_groups]
    v_refs = refs[2 * n_groups:3 * n_groups]
    o_ref = refs[3 * n_groups]
    qi = pl.program_id(2)
    tk = tq
    scale = HEAD_DIM ** -0.5
    base_delta = (lax.broadcasted_iota(jnp.int32, (tq, tk), 0)
                  - lax.broadcasted_iota(jnp.int32, (tq, tk), 1))

    outs, lses = [], []
    for g, (window, dilation) in enumerate(B_GROUPS):
        q = q_refs[g][...]
        k_ref, v_ref = k_refs[g], v_refs[g]
        back_blocks = -(-window // tk)
        dil_ok = (base_delta & (dilation - 1)) == 0

        def body(j, carry, k_ref=k_ref, v_ref=v_ref, q=q, dil_ok=dil_ok, window=window):
            k = k_ref[pl.ds(pl.multiple_of(j * tk, tk), tk), :]
            v = v_ref[pl.ds(pl.multiple_of(j * tk, tk), tk), :]
            s = lax.dot_general(q, k, (((1,), (1,)), ((), ())), preferred_element_type=F32) * scale
            delta = base_delta + (qi - j) * tk
            valid = dil_ok & (delta >= 0) & (delta <= window)
            s = jnp.where(valid, s, MASKED_SCORE)
            return _online_softmax_step(s, v, *carry)

        init = (jnp.full((tq, 1), MASKED_SCORE, F32), jnp.zeros((tq, 1), F32),
                jnp.zeros((tq, HEAD_DIM), F32))
        m, l, acc = lax.fori_loop(jnp.maximum(qi - back_blocks, 0), qi + 1, body, init)
        outs.append(acc / l)
        lses.append(m + jnp.log(l))

    lse_max = functools.reduce(jnp.maximum, lses)
    weights = [jnp.exp(x - lse_max) for x in lses]
    denom = functools.reduce(lambda a, b: a + b, weights)
    for g in range(n_groups):
        o_ref[g] = (outs[g] * (weights[g] / denom)).astype(o_ref.dtype)


def _attn_b(qkv, batch, seq, tq=256):
    nq = seq // tq
    n_groups = len(B_GROUPS)
    heads = B_HEADS_PER_GROUP * n_groups

    def q_spec(g):
        return pl.BlockSpec((tq, HEAD_DIM), lambda b, h, i: (b * nq + i, g * B_HEADS_PER_GROUP + h))

    def kv_spec(g, part):
        return pl.BlockSpec((seq, HEAD_DIM), lambda b, h, i: (b, part * heads + g * B_HEADS_PER_GROUP + h))

    in_specs = ([q_spec(g) for g in range(n_groups)] + [kv_spec(g, 1) for g in range(n_groups)]
                + [kv_spec(g, 2) for g in range(n_groups)])
    return pl.pallas_call(
        functools.partial(_attn_b_kernel, tq=tq),
        grid=(batch, B_HEADS_PER_GROUP, nq),
        in_specs=in_specs,
        out_specs=pl.BlockSpec((n_groups, tq, HEAD_DIM), lambda b, h, i: (0, b * nq + i, h)),
        out_shape=jax.ShapeDtypeStruct((n_groups, batch * seq, B_GROUP_WIDTH), BF16),
        compiler_params=_params("parallel", "parallel", "arbitrary"),
        name="dilated_attention",
    )(*([qkv] * (3 * n_groups)))


def _shift_rows(x, k, row):
    return jnp.where(row >= k, pltpu.roll(x, k, 0), 0.0)


def _pool_kernel(u_ref, w_ref, scale_ref, o_ref):
    g = pl.program_id(1)
    u = u_ref[...]
    row = lax.broadcasted_iota(jnp.int32, u.shape, 0)
    window = jnp.left_shift(2, g)
    total = u
    step = 1
    while step < max(C_WINDOWS):
        widened = total + _shift_rows(total, step, row)
        total = jnp.where(step < window, widened, total)
        step *= 2
    count = jnp.minimum(row + 1, window).astype(F32)
    pooled = total / count - u
    y = jnp.dot(pooled.astype(BF16), w_ref[0], preferred_element_type=F32)
    o_ref[...] = (y * scale_ref[...]).astype(o_ref.dtype)


def _pool(u, pool_w, pool_scale, batch, seq):
    n_groups = len(C_WINDOWS)
    return pl.pallas_call(
        _pool_kernel,
        grid=(batch, n_groups),
        in_specs=[pl.BlockSpec((seq, C_GROUP_DIM), lambda b, g: (b, g)),
                  pl.BlockSpec((1, C_GROUP_DIM, C_GROUP_DIM), lambda b, g: (g, 0, 0)),
                  pl.BlockSpec((1, C_GROUP_DIM), lambda b, g: (0, g))],
        out_specs=pl.BlockSpec((seq, C_GROUP_DIM), lambda b, g: (b, g)),
        out_shape=jax.ShapeDtypeStruct((batch * seq, C_WIDTH), BF16),
        compiler_params=_params("parallel", "parallel"),
        name="multiscale_pool",
    )(u, pool_w, pool_scale.reshape(1, C_WIDTH))


def _conv_kernel(gb_ref, gc_ref, h_ref, w_ref, o_ref):
    z = gc_ref[...] * h_ref[...]
    row = lax.broadcasted_iota(jnp.int32, z.shape, 0)
    w = w_ref[...]
    y = w[CONV_WIDTH - 1:CONV_WIDTH] * z
    for tap in range(1, CONV_WIDTH):
        y = y + w[CONV_WIDTH - 1 - tap:CONV_WIDTH - tap] * _shift_rows(z, tap, row)
    o_ref[...] = (gb_ref[...] * y).astype(o_ref.dtype)


def _short_conv(gates, conv_w, batch, seq):
    n_col = D_WIDTH // LANES

    def spec(part):
        return pl.BlockSpec((seq, LANES), lambda b, c: (b, part * n_col + c))

    return pl.pallas_call(
        _conv_kernel,
        grid=(batch, n_col),
        in_specs=[spec(0), spec(1), spec(2), pl.BlockSpec((CONV_WIDTH, LANES), lambda b, c: (0, c))],
        out_specs=pl.BlockSpec((seq, LANES), lambda b, c: (b, c)),
        out_shape=jax.ShapeDtypeStruct((batch * seq, D_WIDTH), BF16),
        compiler_params=_params("parallel", "parallel"),
        name="short_conv",
    )(gates, gates, gates, conv_w)


def _rope_tables(seq, dim):
    half = dim // 2
    inv = ROPE_THETA ** (-jnp.arange(0, dim, 2, dtype=F32) / dim)
    ang = jnp.arange(seq, dtype=F32)[:, None] * inv[None, :]
    cos, sin = jnp.cos(ang), jnp.sin(ang)
    reps = LANES // dim
    cos_t = jnp.tile(jnp.concatenate([cos, cos], axis=1), (1, reps))
    zeros = jnp.zeros_like(sin)
    if dim == LANES:
        return (cos_t, jnp.concatenate([-sin, sin], axis=1)), (half,)
    upper = jnp.tile(jnp.concatenate([zeros, sin], axis=1), (1, reps))
    lower = jnp.tile(jnp.concatenate([-sin, zeros], axis=1), (1, reps))
    return (cos_t, upper, lower), (half, LANES - half)


def kernel(x, w_in, w_out, norm_mix, norm_mlp, diff_lambda, diff_subln, pool_w, pool_scale, conv_w,
           w_up, w_down, norm_final):
    batch, seq, d_model = x.shape
    depth = w_in.shape[0]
    m = batch * seq
    x = x.reshape(m, d_model)

    tables_a, shifts_a = _rope_tables(seq, A_SUB)
    tables_b, shifts_b = _rope_tables(seq, HEAD_DIM)

    a_qk_end = 2 * A_WIDTH
    a_end = 3 * A_WIDTH
    b_end = a_end + 3 * B_WIDTH
    c_end = b_end + C_WIDTH
    t_a = 512

    for l in range(depth):
        wl = w_in[l]
        w_a = wl[:, :a_qk_end].astype(BF16)
        w_a_vt = wl[:, a_qk_end:a_end].T.astype(BF16)
        w_b = wl[:, a_end:b_end].astype(BF16)
        w_c = wl[:, b_end:c_end].astype(BF16)
        w_d = wl[:, c_end:].astype(BF16)
        lam_init = 0.8 - 0.6 * math.exp(-0.3 * l)

        h = _rmsnorm(x, norm_mix[l], BF16)
        qk_a = _mm_rope(h, w_a, tables_a, shifts_a, 2, A_SUB ** -0.5, A_WIDTH, seq)
        vt_a = _mm_transposed(h, w_a_vt, t_a)
        qkv_b = _mm_rope(h, w_b, tables_b, shifts_b, 2, 1.0, B_WIDTH, seq)
        u = _mm(h, w_c, F32, C_WIDTH, name="in_proj_pool")
        gates = _mm(h, w_d, F32, D_WIDTH, name="in_proj_conv")

        out_a = _attn_a(qk_a, vt_a, diff_lambda[l], diff_subln[l], lam_init, batch, seq, t_a)
        out_b = _attn_b(qkv_b, batch, seq)
        out_c = _pool(u, pool_w[l].astype(BF16), pool_scale[l], batch, seq)
        out_d = _short_conv(gates, conv_w[l], batch, seq)

        x = _out_proj(out_a, out_b, out_c, out_d, w_out[l].astype(BF16), x)

        h = _rmsnorm(x, norm_mlp[l], BF16)
        act = _mm(h, w_up[l].astype(BF16), BF16, 1024, relu_sq=True, name="mlp_up")
        x = _mm_residual(act, w_down[l].astype(BF16), x)

    out = _rmsnorm(x, norm_final, F32)
    return out.reshape(batch, seq, d_model)
```

```python
import functools
import math

import jax
import jax.numpy as jnp
from jax import lax
from jax.experimental import pallas as pl
from jax.experimental.pallas import tpu as pltpu

D_MODEL = 4096
HEAD_DIM = 128
A_HEADS = 8
A_WIDTH = A_HEADS * HEAD_DIM
A_SUB = HEAD_DIM // 2
B_GROUPS = ((128, 1), (512, 4), (2048, 16))
B_HEADS_PER_GROUP = 3
B_GROUP_WIDTH = B_HEADS_PER_GROUP * HEAD_DIM
B_WIDTH = B_GROUP_WIDTH * len(B_GROUPS)
C_WINDOWS = (2, 4, 8, 16)
C_WIDTH = 1024
C_GROUP_DIM = C_WIDTH // len(C_WINDOWS)
D_WIDTH = D_MODEL - A_WIDTH - B_WIDTH - C_WIDTH
CONV_WIDTH = 3
D_FF = 4 * D_MODEL
ROPE_THETA = 10000.0
NORM_EPS = 1e-6
DIFF_EPS = 1e-5

LANES = 128
VMEM_LIMIT_BYTES = 56 * 1024 * 1024
MASKED_SCORE = -1e30

F32 = jnp.float32
BF16 = jnp.bfloat16


def _params(*semantics):
    return pltpu.CompilerParams(dimension_semantics=semantics, vmem_limit_bytes=VMEM_LIMIT_BYTES)


def _rmsnorm_kernel(x_ref, g_ref, o_ref, *, eps):
    x = x_ref[...]
    ms = jnp.mean(x * x, axis=-1, keepdims=True)
    o_ref[...] = (x * lax.rsqrt(ms + eps) * g_ref[...]).astype(o_ref.dtype)


def _rmsnorm(x, g, out_dtype, tm=256):
    m, d = x.shape
    return pl.pallas_call(
        functools.partial(_rmsnorm_kernel, eps=NORM_EPS),
        grid=(m // tm,),
        in_specs=[pl.BlockSpec((tm, d), lambda i: (i, 0)),
                  pl.BlockSpec((1, d), lambda i: (0, 0))],
        out_specs=pl.BlockSpec((tm, d), lambda i: (i, 0)),
        out_shape=jax.ShapeDtypeStruct((m, d), out_dtype),
        compiler_params=_params("parallel"),
        name="rmsnorm",
    )(x, g.reshape(1, d))


def _rotate(y, cos_ref, sin_refs, shifts):
    outs = []
    for c in range(y.shape[1] // LANES):
        yc = y[:, c * LANES:(c + 1) * LANES]
        oc = yc * cos_ref[...]
        for s_ref, shift in zip(sin_refs, shifts):
            oc = oc + pltpu.roll(yc, shift, 1) * s_ref[...]
        outs.append(oc)
    return jnp.concatenate(outs, axis=1)


def _mm_rope_kernel(a_ref, w_ref, cos_ref, *rest, shifts, n_rope_blocks, q_scale):
    sin_refs, o_ref = rest[:-1], rest[-1]
    j = pl.program_id(1)
    acc = jnp.dot(a_ref[...], w_ref[...], preferred_element_type=F32)

    @pl.when(j < n_rope_blocks)
    def _():
        scale = jnp.where(j == 0, q_scale, 1.0).astype(F32)
        o_ref[...] = (_rotate(acc, cos_ref, sin_refs, shifts) * scale).astype(o_ref.dtype)

    @pl.when(j >= n_rope_blocks)
    def _():
        o_ref[...] = acc.astype(o_ref.dtype)


def _mm_rope(a, w, tables, shifts, n_rope_blocks, q_scale, bn, seq, bm=1024):
    m, k = a.shape
    n = w.shape[1]
    tab_blocks = seq // bm
    tab_spec = pl.BlockSpec((bm, LANES), lambda i, j: (i % tab_blocks, 0))
    return pl.pallas_call(
        functools.partial(_mm_rope_kernel, shifts=shifts, n_rope_blocks=n_rope_blocks, q_scale=q_scale),
        grid=(m // bm, n // bn),
        in_specs=[pl.BlockSpec((bm, k), lambda i, j: (i, 0)),
                  pl.BlockSpec((k, bn), lambda i, j: (0, j))] + [tab_spec] * len(tables),
        out_specs=pl.BlockSpec((bm, bn), lambda i, j: (i, j)),
        out_shape=jax.ShapeDtypeStruct((m, n), BF16),
        compiler_params=_params("parallel", "arbitrary"),
        name="in_proj_rope",
    )(a, w, *tables)


def _mm_kernel(a_ref, w_ref, o_ref, *, relu_sq):
    acc = jnp.dot(a_ref[...], w_ref[...].astype(BF16), preferred_element_type=F32)
    if relu_sq:
        acc = jnp.square(jnp.maximum(acc, 0.0))
    o_ref[...] = acc.astype(o_ref.dtype)


def _mm(a, w, out_dtype, bn, relu_sq=False, bm=1024, name="matmul"):
    m, k = a.shape
    n = w.shape[1]
    return pl.pallas_call(
        functools.partial(_mm_kernel, relu_sq=relu_sq),
        grid=(m // bm, n // bn),
        in_specs=[pl.BlockSpec((bm, k), lambda i, j: (i, 0)),
                  pl.BlockSpec((k, bn), lambda i, j: (0, j))],
        out_specs=pl.BlockSpec((bm, bn), lambda i, j: (i, j)),
        out_shape=jax.ShapeDtypeStruct((m, n), out_dtype),
        compiler_params=_params("parallel", "arbitrary"),
        name=name,
    )(a, w)


def _mm_layer(a, w, layer, out_dtype, bn, relu_sq=False, bm=2048, name="matmul"):
    m, k = a.shape
    n = w.shape[2]
    return pl.pallas_call(
        functools.partial(_mm_kernel, relu_sq=relu_sq),
        grid=(m // bm, n // bn),
        in_specs=[pl.BlockSpec((bm, k), lambda i, j: (i, 0), pipeline_mode=pl.Buffered(1)),
                  pl.BlockSpec((None, k, bn), lambda i, j: (layer, 0, j))],
        out_specs=pl.BlockSpec((bm, bn), lambda i, j: (i, j)),
        out_shape=jax.ShapeDtypeStruct((m, n), out_dtype),
        compiler_params=_params("parallel", "arbitrary"),
        name=name,
    )(a, w)


def _mm_transposed_kernel(wt_ref, a_ref, o_ref, *, t):
    acc = lax.dot_general(wt_ref[...], a_ref[...], (((1,), (1,)), ((), ())), preferred_element_type=F32)
    for c in range(acc.shape[1] // t):
        o_ref[c] = acc[:, c * t:(c + 1) * t].astype(o_ref.dtype)


def _mm_transposed(a, wt, t, bm=1024):
    m, k = a.shape
    n = wt.shape[0]
    return pl.pallas_call(
        functools.partial(_mm_transposed_kernel, t=t),
        grid=(m // bm,),
        in_specs=[pl.BlockSpec((n, k), lambda i: (0, 0)),
                  pl.BlockSpec((bm, k), lambda i: (i, 0))],
        out_specs=pl.BlockSpec((bm // t, n, t), lambda i: (i, 0, 0)),
        out_shape=jax.ShapeDtypeStruct((m // t, n, t), BF16),
        compiler_params=_params("parallel"),
        name="in_proj_vt",
    )(wt, a)


def _mm_residual_kernel(a_ref, w_ref, r_ref, o_ref):
    kk = pl.program_id(2)

    @pl.when(kk == 0)
    def _():
        o_ref[...] = r_ref[...]

    o_ref[...] += jnp.dot(a_ref[...], w_ref[...].astype(BF16), preferred_element_type=F32)


def _mm_residual(a, w, layer, r, bm=2048, bn=1024, bk=1024):
    m, k = a.shape
    n = w.shape[2]
    return pl.pallas_call(
        _mm_residual_kernel,
        grid=(m // bm, n // bn, k // bk),
        in_specs=[pl.BlockSpec((bm, bk), lambda i, j, kk: (i, kk)),
                  pl.BlockSpec((None, bk, bn), lambda i, j, kk: (layer, kk, j)),
                  pl.BlockSpec((bm, bn), lambda i, j, kk: (i, j), pipeline_mode=pl.Buffered(1))],
        out_specs=pl.BlockSpec((bm, bn), lambda i, j, kk: (i, j)),
        out_shape=jax.ShapeDtypeStruct((m, n), F32),
        compiler_params=_params("parallel", "parallel", "arbitrary"),
        name="mlp_down",
    )(a, w, r)


def _out_proj_kernel(a_ref, b_ref, c_ref, d_ref, w_ref, r_ref, o_ref):
    def w_rows(lo, size):
        return w_ref[lo:lo + size, :].astype(BF16)

    acc = r_ref[...] + jnp.dot(a_ref[...], w_rows(0, A_WIDTH), preferred_element_type=F32)
    off = A_WIDTH
    for g in range(len(B_GROUPS)):
        acc += jnp.dot(b_ref[g], w_rows(off, B_GROUP_WIDTH), preferred_element_type=F32)
        off += B_GROUP_WIDTH
    acc += jnp.dot(c_ref[...], w_rows(off, C_WIDTH), preferred_element_type=F32)
    off += C_WIDTH
    acc += jnp.dot(d_ref[...], w_rows(off, D_WIDTH), preferred_element_type=F32)
    o_ref[...] = acc


def _out_proj(out_a, out_b, out_c, out_d, w, layer, r, bm=1024, bn=512):
    m = out_a.shape[0]
    n = w.shape[2]
    n_groups = len(B_GROUPS)
    return pl.pallas_call(
        _out_proj_kernel,
        grid=(m // bm, n // bn),
        in_specs=[pl.BlockSpec((bm, A_WIDTH), lambda i, j: (i, 0)),
                  pl.BlockSpec((n_groups, bm, B_GROUP_WIDTH), lambda i, j: (0, i, 0)),
                  pl.BlockSpec((bm, C_WIDTH), lambda i, j: (i, 0)),
                  pl.BlockSpec((bm, D_WIDTH), lambda i, j: (i, 0)),
                  pl.BlockSpec((None, D_MODEL, bn), lambda i, j: (layer, 0, j)),
                  pl.BlockSpec((bm, bn), lambda i, j: (i, j))],
        out_specs=pl.BlockSpec((bm, bn), lambda i, j: (i, j)),
        out_shape=jax.ShapeDtypeStruct((m, n), F32),
        compiler_params=_params("parallel", "arbitrary"),
        name="out_proj",
    )(out_a, out_b, out_c, out_d, w, r)


def _online_softmax_step(s, v, m, l, acc):
    m_new = jnp.maximum(m, jnp.max(s, axis=-1, keepdims=True))
    alpha = jnp.exp(m - m_new)
    p = jnp.exp(s - m_new)
    l_new = alpha * l + jnp.sum(p, axis=-1, keepdims=True)
    acc_new = alpha * acc + jnp.dot(p.astype(BF16), v, preferred_element_type=F32)
    return m_new, l_new, acc_new


def _attn_a_kernel(q_ref, k_ref, vt_ref, lam_ref, g_ref, o_ref, *, t, heads, lam_init):
    qi = pl.program_id(2)
    lane = lax.broadcasted_iota(jnp.int32, (t, HEAD_DIM), 1)
    qqs = []
    for h in range(heads):
        q = q_ref[:, h * HEAD_DIM:(h + 1) * HEAD_DIM]
        zero = jnp.zeros_like(q)
        qqs.append(jnp.concatenate([jnp.where(lane < A_SUB, q, zero), jnp.where(lane >= A_SUB, q, zero)], axis=0))

    def step(j, carries, masked):
        out = []
        for h in range(heads):
            m, l, acc = carries[h]
            k = k_ref[pl.ds(pl.multiple_of(j * t, t), t), h * HEAD_DIM:(h + 1) * HEAD_DIM]
            vt = vt_ref[j, h * HEAD_DIM:(h + 1) * HEAD_DIM, :]
            s = lax.dot_general(k, qqs[h], (((1,), (1,)), ((), ())), preferred_element_type=F32)
            if masked:
                key = lax.broadcasted_iota(jnp.int32, s.shape, 0)
                qry = lax.broadcasted_iota(jnp.int32, s.shape, 1)
                qry = jnp.where(qry >= t, qry - t, qry)
                s = jnp.where(key <= qry, s, MASKED_SCORE)
            m_new = jnp.maximum(m, jnp.max(s, axis=0, keepdims=True))
            alpha = jnp.exp(m - m_new)
            p = jnp.exp(s - m_new)
            l_new = alpha * l + jnp.sum(p, axis=0, keepdims=True)
            acc_new = alpha * acc + jnp.dot(vt, p.astype(BF16), preferred_element_type=F32)
            out.append((m_new, l_new, acc_new))
        return tuple(out)

    init = tuple((jnp.full((1, 2 * t), MASKED_SCORE, F32), jnp.zeros((1, 2 * t), F32),
                  jnp.zeros((HEAD_DIM, 2 * t), F32)) for _ in range(heads))
    carries = lax.fori_loop(0, qi, functools.partial(step, masked=False), init)
    carries = step(qi, carries, masked=True)

    lp = lam_ref[...]
    lam = (jnp.exp(jnp.sum(lp[0:1] * lp[1:2], axis=-1, keepdims=True))
           - jnp.exp(jnp.sum(lp[2:3] * lp[3:4], axis=-1, keepdims=True)) + lam_init)
    for h in range(heads):
        _, l, acc = carries[h]
        o = acc / l
        o = (o[:, :t] - lam * o[:, t:]).T
        ms = jnp.mean(o * o, axis=-1, keepdims=True)
        o = o * lax.rsqrt(ms + DIFF_EPS) * g_ref[...]
        o_ref[:, h * HEAD_DIM:(h + 1) * HEAD_DIM] = (o * (1.0 - lam_init)).astype(o_ref.dtype)


def _attn_a(qk, vt, lam_params, subln_g, lam_init, batch, seq, t, heads=2):
    nq = seq // t
    width = heads * HEAD_DIM
    n_pairs = A_HEADS // heads
    return pl.pallas_call(
        functools.partial(_attn_a_kernel, t=t, heads=heads, lam_init=lam_init),
        grid=(batch, n_pairs, nq),
        in_specs=[pl.BlockSpec((t, width), lambda b, h, i: (b * nq + i, h)),
                  pl.BlockSpec((seq, width), lambda b, h, i: (b, n_pairs + h)),
                  pl.BlockSpec((nq, width, t), lambda b, h, i: (b, h, 0)),
                  pl.BlockSpec((4, A_SUB), lambda b, h, i: (0, 0)),
                  pl.BlockSpec((1, HEAD_DIM), lambda b, h, i: (0, 0))],
        out_specs=pl.BlockSpec((t, width), lambda b, h, i: (b * nq + i, h)),
        out_shape=jax.ShapeDtypeStruct((batch * seq, A_WIDTH), BF16),
        compiler_params=_params("parallel", "parallel", "arbitrary"),
        name="diff_attention",
    )(qk, qk, vt, lam_params, subln_g.reshape(1, HEAD_DIM))


def _attn_b_kernel(*refs, tq):
    n_groups = len(B_GROUPS)
    q_refs = refs[0:n_groups]
    k_refs = refs[n_groups:2 * n_groups]
    v_refs = refs[2 * n_groups:3 * n_groups]
    o_ref = refs[3 * n_groups]
    qi = pl.program_id(2)
    tk = tq
    scale = HEAD_DIM ** -0.5
    base_delta = (lax.broadcasted_iota(jnp.int32, (tq, tk), 0)
                  - lax.broadcasted_iota(jnp.int32, (tq, tk), 1))

    outs, lses = [], []
    for g, (window, dilation) in enumerate(B_GROUPS):
        q = q_refs[g][...]
        k_ref, v_ref = k_refs[g], v_refs[g]
        back_blocks = -(-window // tk)
        dil_ok = (base_delta & (dilation - 1)) == 0

        def body(j, carry, k_ref=k_ref, v_ref=v_ref, q=q, dil_ok=dil_ok, window=window):
            k = k_ref[pl.ds(pl.multiple_of(j * tk, tk), tk), :]
            v = v_ref[pl.ds(pl.multiple_of(j * tk, tk), tk), :]
            s = lax.dot_general(q, k, (((1,), (1,)), ((), ())), preferred_element_type=F32) * scale
            delta = base_delta + (qi - j) * tk
            valid = dil_ok & (delta >= 0) & (delta <= window)
            s = jnp.where(valid, s, MASKED_SCORE)
            return _online_softmax_step(s, v, *carry)

        init = (jnp.full((tq, 1), MASKED_SCORE, F32), jnp.zeros((tq, 1), F32),
                jnp.zeros((tq, HEAD_DIM), F32))
        m, l, acc = lax.fori_loop(jnp.maximum(qi - back_blocks, 0), qi + 1, body, init)
        outs.append(acc / l)
        lses.append(m + jnp.log(l))

    lse_max = functools.reduce(jnp.maximum, lses)
    weights = [jnp.exp(x - lse_max) for x in lses]
    denom = functools.reduce(lambda a, b: a + b, weights)
    for g in range(n_groups):
        o_ref[g] = (outs[g] * (weights[g] / denom)).astype(o_ref.dtype)


def _attn_b(qkv, batch, seq, tq=256):
    nq = seq // tq
    n_groups = len(B_GROUPS)
    heads = B_HEADS_PER_GROUP * n_groups

    def q_spec(g):
        return pl.BlockSpec((tq, HEAD_DIM), lambda b, h, i: (b * nq + i, g * B_HEADS_PER_GROUP + h))

    def kv_spec(g, part):
        return pl.BlockSpec((seq, HEAD_DIM), lambda b, h, i: (b, part * heads + g * B_HEADS_PER_GROUP + h))

    in_specs = ([q_spec(g) for g in range(n_groups)] + [kv_spec(g, 1) for g in range(n_groups)]
                + [kv_spec(g, 2) for g in range(n_groups)])
    return pl.pallas_call(
        functools.partial(_attn_b_kernel, tq=tq),
        grid=(batch, B_HEADS_PER_GROUP, nq),
        in_specs=in_specs,
        out_specs=pl.BlockSpec((n_groups, tq, HEAD_DIM), lambda b, h, i: (0, b * nq + i, h)),
        out_shape=jax.ShapeDtypeStruct((n_groups, batch * seq, B_GROUP_WIDTH), BF16),
        compiler_params=_params("parallel", "parallel", "arbitrary"),
        name="dilated_attention",
    )(*([qkv] * (3 * n_groups)))


def _shift_rows(x, k, row):
    return jnp.where(row >= k, pltpu.roll(x, k, 0), 0.0)


def _pool_kernel(u_ref, w_ref, scale_ref, o_ref):
    g = pl.program_id(1)
    u = u_ref[...]
    row = lax.broadcasted_iota(jnp.int32, u.shape, 0)
    window = jnp.left_shift(2, g)
    total = u
    step = 1
    while step < max(C_WINDOWS):
        widened = total + _shift_rows(total, step, row)
        total = jnp.where(step < window, widened, total)
        step *= 2
    count = jnp.minimum(row + 1, window).astype(F32)
    pooled = total / count - u
    y = jnp.dot(pooled.astype(BF16), w_ref[0], preferred_element_type=F32)
    o_ref[...] = (y * scale_ref[...]).astype(o_ref.dtype)


def _pool(u, pool_w, pool_scale, batch, seq):
    n_groups = len(C_WINDOWS)
    return pl.pallas_call(
        _pool_kernel,
        grid=(batch, n_groups),
        in_specs=[pl.BlockSpec((seq, C_GROUP_DIM), lambda b, g: (b, g)),
                  pl.BlockSpec((1, C_GROUP_DIM, C_GROUP_DIM), lambda b, g: (g, 0, 0)),
                  pl.BlockSpec((1, C_GROUP_DIM), lambda b, g: (0, g))],
        out_specs=pl.BlockSpec((seq, C_GROUP_DIM), lambda b, g: (b, g)),
        out_shape=jax.ShapeDtypeStruct((batch * seq, C_WIDTH), BF16),
        compiler_params=_params("parallel", "parallel"),
        name="multiscale_pool",
    )(u, pool_w, pool_scale.reshape(1, C_WIDTH))


def _conv_kernel(gb_ref, gc_ref, h_ref, w_ref, o_ref):
    z = gc_ref[...] * h_ref[...]
    row = lax.broadcasted_iota(jnp.int32, z.shape, 0)
    w = w_ref[...]
    y = w[CONV_WIDTH - 1:CONV_WIDTH] * z
    for tap in range(1, CONV_WIDTH):
        y = y + w[CONV_WIDTH - 1 - tap:CONV_WIDTH - tap] * _shift_rows(z, tap, row)
    o_ref[...] = (gb_ref[...] * y).astype(o_ref.dtype)


def _short_conv(gates, conv_w, batch, seq):
    n_col = D_WIDTH // LANES

    def spec(part):
        return pl.BlockSpec((seq, LANES), lambda b, c: (b, part * n_col + c))

    return pl.pallas_call(
        _conv_kernel,
        grid=(batch, n_col),
        in_specs=[spec(0), spec(1), spec(2), pl.BlockSpec((CONV_WIDTH, LANES), lambda b, c: (0, c))],
        out_specs=pl.BlockSpec((seq, LANES), lambda b, c: (b, c)),
        out_shape=jax.ShapeDtypeStruct((batch * seq, D_WIDTH), BF16),
        compiler_params=_params("parallel", "parallel"),
        name="short_conv",
    )(gates, gates, gates, conv_w)


def _rope_tables(seq, dim):
    half = dim // 2
    inv = ROPE_THETA ** (-jnp.arange(0, dim, 2, dtype=F32) / dim)
    ang = jnp.arange(seq, dtype=F32)[:, None] * inv[None, :]
    cos, sin = jnp.cos(ang), jnp.sin(ang)
    reps = LANES // dim
    cos_t = jnp.tile(jnp.concatenate([cos, cos], axis=1), (1, reps))
    zeros = jnp.zeros_like(sin)
    if dim == LANES:
        return (cos_t, jnp.concatenate([-sin, sin], axis=1)), (half,)
    upper = jnp.tile(jnp.concatenate([zeros, sin], axis=1), (1, reps))
    lower = jnp.tile(jnp.concatenate([-sin, zeros], axis=1), (1, reps))
    return (cos_t, upper, lower), (half, LANES - half)


def kernel(x, w_in, w_out, norm_mix, norm_mlp, diff_lambda, diff_subln, pool_w, pool_scale, conv_w,
           w_up, w_down, norm_final):
    batch, seq, d_model = x.shape
    depth = w_in.shape[0]
    m = batch * seq
    x = x.reshape(m, d_model)

    tables_a, shifts_a = _rope_tables(seq, A_SUB)
    tables_b, shifts_b = _rope_tables(seq, HEAD_DIM)

    a_qk_end = 2 * A_WIDTH
    a_end = 3 * A_WIDTH
    b_end = a_end + 3 * B_WIDTH
    c_end = b_end + C_WIDTH
    t_a = 512

    for l in range(depth):
        wl = w_in[l]
        w_a = wl[:, :a_qk_end].astype(BF16)
        w_a_vt = wl[:, a_qk_end:a_end].T.astype(BF16)
        w_b = wl[:, a_end:b_end].astype(BF16)
        w_c = wl[:, b_end:c_end].astype(BF16)
        w_d = wl[:, c_end:].astype(BF16)
        lam_init = 0.8 - 0.6 * math.exp(-0.3 * l)

        h = _rmsnorm(x, norm_mix[l], BF16)
        qk_a = _mm_rope(h, w_a, tables_a, shifts_a, 2, A_SUB ** -0.5, A_WIDTH, seq)
        vt_a = _mm_transposed(h, w_a_vt, t_a)
        qkv_b = _mm_rope(h, w_b, tables_b, shifts_b, 2, 1.0, B_WIDTH, seq)
        u = _mm(h, w_c, F32, C_WIDTH, name="in_proj_pool")
        gates = _mm(h, w_d, F32, D_WIDTH, name="in_proj_conv")

        out_a = _attn_a(qk_a, vt_a, diff_lambda[l], diff_subln[l], lam_init, batch, seq, t_a)
        out_b = _attn_b(qkv_b, batch, seq)
        out_c = _pool(u, pool_w[l].astype(BF16), pool_scale[l], batch, seq)
        out_d = _short_conv(gates, conv_w[l], batch, seq)

        x = _out_proj(out_a, out_b, out_c, out_d, w_out, l, x)

        h = _rmsnorm(x, norm_mlp[l], BF16)
        act = _mm_layer(h, w_up, l, BF16, 512, relu_sq=True, name="mlp_up")
        x = _mm_residual(act, w_down, l, x)

    out = _rmsnorm(x, norm_final, F32)
    return out.reshape(batch, seq, d_model)
```

```python
import functools
import math

import jax
import jax.numpy as jnp
from jax import lax
from jax.experimental import pallas as pl
from jax.experimental.pallas import tpu as pltpu

D_MODEL = 4096
HEAD_DIM = 128
A_HEADS = 8
A_WIDTH = A_HEADS * HEAD_DIM
A_SUB = HEAD_DIM // 2
B_GROUPS = ((128, 1), (512, 4), (2048, 16))
B_HEADS_PER_GROUP = 3
B_GROUP_WIDTH = B_HEADS_PER_GROUP * HEAD_DIM
B_WIDTH = B_GROUP_WIDTH * len(B_GROUPS)
B_TOKEN_TILE = 2048
B_QUERY_CHUNK = 256
C_WINDOWS = (2, 4, 8, 16)
C_WIDTH = 1024
C_GROUP_DIM = C_WIDTH // len(C_WINDOWS)
D_WIDTH = D_MODEL - A_WIDTH - B_WIDTH - C_WIDTH
CONV_WIDTH = 3
D_FF = 4 * D_MODEL
ROPE_THETA = 10000.0
NORM_EPS = 1e-6
DIFF_EPS = 1e-5

LANES = 128
VMEM_LIMIT_BYTES = 56 * 1024 * 1024
MASKED_SCORE = -1e30

F32 = jnp.float32
BF16 = jnp.bfloat16


def _params(*semantics):
    return pltpu.CompilerParams(dimension_semantics=semantics, vmem_limit_bytes=VMEM_LIMIT_BYTES)


def _rmsnorm_kernel(x_ref, g_ref, o_ref, *, eps):
    x = x_ref[...]
    ms = jnp.mean(x * x, axis=-1, keepdims=True)
    o_ref[...] = (x * lax.rsqrt(ms + eps) * g_ref[...]).astype(o_ref.dtype)


def _rmsnorm(x, g, out_dtype, tm=256):
    m, d = x.shape
    return pl.pallas_call(
        functools.partial(_rmsnorm_kernel, eps=NORM_EPS),
        grid=(m // tm,),
        in_specs=[pl.BlockSpec((tm, d), lambda i: (i, 0)),
                  pl.BlockSpec((1, d), lambda i: (0, 0))],
        out_specs=pl.BlockSpec((tm, d), lambda i: (i, 0)),
        out_shape=jax.ShapeDtypeStruct((m, d), out_dtype),
        compiler_params=_params("parallel"),
        name="rmsnorm",
    )(x, g.reshape(1, d))


def _rotate(y, cos_ref, sin_refs, shifts):
    outs = []
    for c in range(y.shape[1] // LANES):
        yc = y[:, c * LANES:(c + 1) * LANES]
        oc = yc * cos_ref[...]
        for s_ref, shift in zip(sin_refs, shifts):
            oc = oc + pltpu.roll(yc, shift, 1) * s_ref[...]
        outs.append(oc)
    return jnp.concatenate(outs, axis=1)


def _mm_rope_kernel(a_ref, w_ref, cos_ref, *rest, shifts, n_rope_blocks, q_scale):
    sin_refs, o_ref = rest[:-1], rest[-1]
    j = pl.program_id(1)
    acc = jnp.dot(a_ref[...], w_ref[...], preferred_element_type=F32)

    @pl.when(j < n_rope_blocks)
    def _():
        scale = jnp.where(j == 0, q_scale, 1.0).astype(F32)
        o_ref[...] = (_rotate(acc, cos_ref, sin_refs, shifts) * scale).astype(o_ref.dtype)

    @pl.when(j >= n_rope_blocks)
    def _():
        o_ref[...] = acc.astype(o_ref.dtype)


def _mm_rope(a, w, tables, shifts, n_rope_blocks, q_scale, bn, seq, bm=1024):
    m, k = a.shape
    n = w.shape[1]
    tab_blocks = seq // bm
    tab_spec = pl.BlockSpec((bm, LANES), lambda i, j: (i % tab_blocks, 0))
    return pl.pallas_call(
        functools.partial(_mm_rope_kernel, shifts=shifts, n_rope_blocks=n_rope_blocks, q_scale=q_scale),
        grid=(m // bm, n // bn),
        in_specs=[pl.BlockSpec((bm, k), lambda i, j: (i, 0)),
                  pl.BlockSpec((k, bn), lambda i, j: (0, j))] + [tab_spec] * len(tables),
        out_specs=pl.BlockSpec((bm, bn), lambda i, j: (i, j)),
        out_shape=jax.ShapeDtypeStruct((m, n), BF16),
        compiler_params=_params("parallel", "arbitrary"),
        name="in_proj_rope",
    )(a, w, *tables)


def _mm_rope_permute_kernel(a_ref, w_ref, cos_ref, sin_ref, o_ref, scr_ref, *, shift, n_rope_cols, dilation):
    acc = jnp.dot(a_ref[...], w_ref[...], preferred_element_type=F32)
    rot = _rotate(acc[:, :n_rope_cols], cos_ref, (sin_ref,), (shift,))
    if dilation == 1:
        o_ref[0, :, :n_rope_cols] = rot.astype(o_ref.dtype)
        o_ref[0, :, n_rope_cols:] = acc[:, n_rope_cols:].astype(o_ref.dtype)
    else:
        rows = acc.shape[0] // dilation
        for c in range(acc.shape[1] // LANES):
            cols = slice(c * LANES, (c + 1) * LANES)
            scr_ref[c] = rot[:, cols] if c * LANES < n_rope_cols else acc[:, cols]
            for r in range(dilation):
                o_ref[r, :, cols] = scr_ref[c, pl.ds(r, rows, stride=dilation), :].astype(o_ref.dtype)


def _mm_rope_permute(a, w, tables, shift, n_rope_cols, dilation, batch, seq, bm=1024):
    m, k = a.shape
    n = w.shape[1]
    tiles = seq // bm
    tab_spec = pl.BlockSpec((bm, LANES), lambda i: (i % tiles, 0))
    return pl.pallas_call(
        functools.partial(_mm_rope_permute_kernel, shift=shift, n_rope_cols=n_rope_cols, dilation=dilation),
        grid=(m // bm,),
        in_specs=[pl.BlockSpec((bm, k), lambda i: (i, 0)),
                  pl.BlockSpec((k, n), lambda i: (0, 0), pipeline_mode=pl.Buffered(1)),
                  tab_spec, tab_spec],
        out_specs=pl.BlockSpec((None, dilation, bm // dilation, n), lambda i: (i // tiles, 0, i % tiles, 0)),
        out_shape=jax.ShapeDtypeStruct((batch, dilation, seq // dilation, n), BF16),
        scratch_shapes=[pltpu.VMEM((n // LANES, bm, LANES), F32)],
        compiler_params=_params("parallel"),
        name="in_proj_dilated",
    )(a, w, *tables)


def _mm_kernel(a_ref, w_ref, o_ref, *, relu_sq):
    acc = jnp.dot(a_ref[...], w_ref[...].astype(BF16), preferred_element_type=F32)
    if relu_sq:
        acc = jnp.square(jnp.maximum(acc, 0.0))
    o_ref[...] = acc.astype(o_ref.dtype)


def _mm(a, w, out_dtype, bn, relu_sq=False, bm=1024, name="matmul"):
    m, k = a.shape
    n = w.shape[1]
    return pl.pallas_call(
        functools.partial(_mm_kernel, relu_sq=relu_sq),
        grid=(m // bm, n // bn),
        in_specs=[pl.BlockSpec((bm, k), lambda i, j: (i, 0)),
                  pl.BlockSpec((k, bn), lambda i, j: (0, j))],
        out_specs=pl.BlockSpec((bm, bn), lambda i, j: (i, j)),
        out_shape=jax.ShapeDtypeStruct((m, n), out_dtype),
        compiler_params=_params("parallel", "arbitrary"),
        name=name,
    )(a, w)


def _mm_layer(a, w, layer, out_dtype, bn, relu_sq=False, bm=2048, name="matmul"):
    m, k = a.shape
    n = w.shape[2]
    return pl.pallas_call(
        functools.partial(_mm_kernel, relu_sq=relu_sq),
        grid=(m // bm, n // bn),
        in_specs=[pl.BlockSpec((bm, k), lambda i, j: (i, 0), pipeline_mode=pl.Buffered(1)),
                  pl.BlockSpec((None, k, bn), lambda i, j: (layer, 0, j))],
        out_specs=pl.BlockSpec((bm, bn), lambda i, j: (i, j)),
        out_shape=jax.ShapeDtypeStruct((m, n), out_dtype),
        compiler_params=_params("parallel", "arbitrary"),
        name=name,
    )(a, w)


def _mm_transposed_kernel(wt_ref, a_ref, o_ref, *, t):
    acc = lax.dot_general(wt_ref[...], a_ref[...], (((1,), (1,)), ((), ())), preferred_element_type=F32)
    for c in range(acc.shape[1] // t):
        o_ref[c] = acc[:, c * t:(c + 1) * t].astype(o_ref.dtype)


def _mm_transposed(a, wt, t, bm=1024):
    m, k = a.shape
    n = wt.shape[0]
    return pl.pallas_call(
        functools.partial(_mm_transposed_kernel, t=t),
        grid=(m // bm,),
        in_specs=[pl.BlockSpec((n, k), lambda i: (0, 0)),
                  pl.BlockSpec((bm, k), lambda i: (i, 0))],
        out_specs=pl.BlockSpec((bm // t, n, t), lambda i: (i, 0, 0)),
        out_shape=jax.ShapeDtypeStruct((m // t, n, t), BF16),
        compiler_params=_params("parallel"),
        name="in_proj_vt",
    )(wt, a)


def _mm_residual_kernel(a_ref, w_ref, r_ref, o_ref):
    kk = pl.program_id(2)

    @pl.when(kk == 0)
    def _():
        o_ref[...] = r_ref[...]

    o_ref[...] += jnp.dot(a_ref[...], w_ref[...].astype(BF16), preferred_element_type=F32)


def _mm_residual(a, w, layer, r, bm=2048, bn=1024, bk=1024):
    m, k = a.shape
    n = w.shape[2]
    return pl.pallas_call(
        _mm_residual_kernel,
        grid=(m // bm, n // bn, k // bk),
        in_specs=[pl.BlockSpec((bm, bk), lambda i, j, kk: (i, kk)),
                  pl.BlockSpec((None, bk, bn), lambda i, j, kk: (layer, kk, j)),
                  pl.BlockSpec((bm, bn), lambda i, j, kk: (i, j), pipeline_mode=pl.Buffered(1))],
        out_specs=pl.BlockSpec((bm, bn), lambda i, j, kk: (i, j)),
        out_shape=jax.ShapeDtypeStruct((m, n), F32),
        compiler_params=_params("parallel", "parallel", "arbitrary"),
        name="mlp_down",
    )(a, w, r)


def _out_proj_kernel(a_ref, b_ref, c_ref, d_ref, w_ref, r_ref, o_ref):
    def w_rows(lo, size):
        return w_ref[lo:lo + size, :].astype(BF16)

    acc = r_ref[...] + jnp.dot(a_ref[...], w_rows(0, A_WIDTH), preferred_element_type=F32)
    off = A_WIDTH
    for g in range(len(B_GROUPS)):
        acc += jnp.dot(b_ref[g], w_rows(off, B_GROUP_WIDTH), preferred_element_type=F32)
        off += B_GROUP_WIDTH
    acc += jnp.dot(c_ref[...], w_rows(off, C_WIDTH), preferred_element_type=F32)
    off += C_WIDTH
    acc += jnp.dot(d_ref[...], w_rows(off, D_WIDTH), preferred_element_type=F32)
    o_ref[...] = acc


def _out_proj(out_a, out_b, out_c, out_d, w, layer, r, bm=1024, bn=512):
    m = out_a.shape[0]
    n = w.shape[2]
    n_groups = len(B_GROUPS)
    return pl.pallas_call(
        _out_proj_kernel,
        grid=(m // bm, n // bn),
        in_specs=[pl.BlockSpec((bm, A_WIDTH), lambda i, j: (i, 0)),
                  pl.BlockSpec((n_groups, bm, B_GROUP_WIDTH), lambda i, j: (0, i, 0)),
                  pl.BlockSpec((bm, C_WIDTH), lambda i, j: (i, 0)),
                  pl.BlockSpec((bm, D_WIDTH), lambda i, j: (i, 0)),
                  pl.BlockSpec((None, D_MODEL, bn), lambda i, j: (layer, 0, j)),
                  pl.BlockSpec((bm, bn), lambda i, j: (i, j))],
        out_specs=pl.BlockSpec((bm, bn), lambda i, j: (i, j)),
        out_shape=jax.ShapeDtypeStruct((m, n), F32),
        compiler_params=_params("parallel", "arbitrary"),
        name="out_proj",
    )(out_a, out_b, out_c, out_d, w, r)


def _online_softmax_step(s, v, m, l, acc):
    m_new = jnp.maximum(m, jnp.max(s, axis=-1, keepdims=True))
    alpha = jnp.exp(m - m_new)
    p = jnp.exp(s - m_new)
    l_new = alpha * l + jnp.sum(p, axis=-1, keepdims=True)
    acc_new = alpha * acc + jnp.dot(p.astype(BF16), v, preferred_element_type=F32)
    return m_new, l_new, acc_new


def _attn_a_kernel(q_ref, k_ref, vt_ref, lam_ref, g_ref, o_ref, *, t, heads, lam_init):
    qi = pl.program_id(2)
    lane = lax.broadcasted_iota(jnp.int32, (t, HEAD_DIM), 1)
    qqs = []
    for h in range(heads):
        q = q_ref[:, h * HEAD_DIM:(h + 1) * HEAD_DIM]
        zero = jnp.zeros_like(q)
        qqs.append(jnp.concatenate([jnp.where(lane < A_SUB, q, zero), jnp.where(lane >= A_SUB, q, zero)], axis=0))

    def step(j, carries, masked):
        out = []
        for h in range(heads):
            m, l, acc = carries[h]
            k = k_ref[pl.ds(pl.multiple_of(j * t, t), t), h * HEAD_DIM:(h + 1) * HEAD_DIM]
            vt = vt_ref[j, h * HEAD_DIM:(h + 1) * HEAD_DIM, :]
            s = lax.dot_general(k, qqs[h], (((1,), (1,)), ((), ())), preferred_element_type=F32)
            if masked:
                key = lax.broadcasted_iota(jnp.int32, s.shape, 0)
                qry = lax.broadcasted_iota(jnp.int32, s.shape, 1)
                qry = jnp.where(qry >= t, qry - t, qry)
                s = jnp.where(key <= qry, s, MASKED_SCORE)
            m_new = jnp.maximum(m, jnp.max(s, axis=0, keepdims=True))
            alpha = jnp.exp(m - m_new)
            p = jnp.exp(s - m_new)
            l_new = alpha * l + jnp.sum(p, axis=0, keepdims=True)
            acc_new = alpha * acc + jnp.dot(vt, p.astype(BF16), preferred_element_type=F32)
            out.append((m_new, l_new, acc_new))
        return tuple(out)

    init = tuple((jnp.full((1, 2 * t), MASKED_SCORE, F32), jnp.zeros((1, 2 * t), F32),
                  jnp.zeros((HEAD_DIM, 2 * t), F32)) for _ in range(heads))
    carries = lax.fori_loop(0, qi, functools.partial(step, masked=False), init)
    carries = step(qi, carries, masked=True)

    lp = lam_ref[...]
    lam = (jnp.exp(jnp.sum(lp[0:1] * lp[1:2], axis=-1, keepdims=True))
           - jnp.exp(jnp.sum(lp[2:3] * lp[3:4], axis=-1, keepdims=True)) + lam_init)
    for h in range(heads):
        _, l, acc = carries[h]
        o = acc / l
        o = (o[:, :t] - lam * o[:, t:]).T
        ms = jnp.mean(o * o, axis=-1, keepdims=True)
        o = o * lax.rsqrt(ms + DIFF_EPS) * g_ref[...]
        o_ref[:, h * HEAD_DIM:(h + 1) * HEAD_DIM] = (o * (1.0 - lam_init)).astype(o_ref.dtype)


def _attn_a(qk, vt, lam_params, subln_g, lam_init, batch, seq, t, heads=2):
    nq = seq // t
    width = heads * HEAD_DIM
    n_pairs = A_HEADS // heads
    return pl.pallas_call(
        functools.partial(_attn_a_kernel, t=t, heads=heads, lam_init=lam_init),
        grid=(batch, n_pairs, nq),
        in_specs=[pl.BlockSpec((t, width), lambda b, h, i: (b * nq + i, h)),
                  pl.BlockSpec((seq, width), lambda b, h, i: (b, n_pairs + h)),
                  pl.BlockSpec((nq, width, t), lambda b, h, i: (b, h, 0)),
                  pl.BlockSpec((4, A_SUB), lambda b, h, i: (0, 0)),
                  pl.BlockSpec((1, HEAD_DIM), lambda b, h, i: (0, 0))],
        out_specs=pl.BlockSpec((t, width), lambda b, h, i: (b * nq + i, h)),
        out_shape=jax.ShapeDtypeStruct((batch * seq, A_WIDTH), BF16),
        compiler_params=_params("parallel", "parallel", "arbitrary"),
        name="diff_attention",
    )(qk, qk, vt, lam_params, subln_g.reshape(1, HEAD_DIM))


def _attn_b_kernel(*refs, tile):
    n_groups = len(B_GROUPS)
    q_refs = refs[0:n_groups]
    k_refs = refs[n_groups:2 * n_groups]
    v_refs = refs[2 * n_groups:3 * n_groups]
    o_ref = refs[3 * n_groups]
    out_scr, lse_scr = refs[3 * n_groups + 1:]
    ti = pl.program_id(2)
    scale = HEAD_DIM ** -0.5
    base_delta = {}

    for g, (window, dilation) in enumerate(B_GROUPS):
        back = window // dilation
        seg = tile // dilation
        tq = min(B_QUERY_CHUNK, seg)
        tk = tq + back
        if tq not in base_delta:
            base_delta[tq] = (lax.broadcasted_iota(jnp.int32, (tq, tk), 0)
                              - lax.broadcasted_iota(jnp.int32, (tq, tk), 1))
        for r in range(dilation):
            for c in range(seg // tq):
                i_loc = c * tq
                i0 = ti * seg + i_loc
                ks = pl.multiple_of(jnp.maximum(i0 - back, 0), back)
                q = q_refs[g][r, i_loc:i_loc + tq, :]
                k = k_refs[g][r, pl.ds(ks, tk), :]
                v = v_refs[g][r, pl.ds(ks, tk), :]
                s = lax.dot_general(q, k, (((1,), (1,)), ((), ())), preferred_element_type=F32) * scale
                delta = base_delta[tq] + (i0 - ks)
                s = jnp.where((delta >= 0) & (delta <= back), s, MASKED_SCORE)
                m = jnp.max(s, axis=-1, keepdims=True)
                p = jnp.exp(s - m)
                l = jnp.sum(p, axis=-1, keepdims=True)
                o = jnp.dot(p.astype(BF16), v, preferred_element_type=F32) / l
                lse = jnp.broadcast_to(m + jnp.log(l), o.shape)
                rows = pl.ds(dilation * i_loc + r, tq, stride=dilation) if dilation > 1 else pl.ds(i_loc, tq)
                out_scr[g, rows, :] = o
                lse_scr[g, rows, :] = lse

    for c in range(tile // B_QUERY_CHUNK):
        rows = pl.ds(c * B_QUERY_CHUNK, B_QUERY_CHUNK)
        lses = [lse_scr[g, rows, :] for g in range(n_groups)]
        lse_max = functools.reduce(jnp.maximum, lses)
        weights = [jnp.exp(x - lse_max) for x in lses]
        denom = functools.reduce(lambda a, b: a + b, weights)
        for g in range(n_groups):
            o_ref[g, rows, :] = (out_scr[g, rows, :] * (weights[g] / denom)).astype(o_ref.dtype)


def _attn_b(qkvs, batch, seq, tile=B_TOKEN_TILE):
    n_tiles = seq // tile
    n_groups = len(B_GROUPS)

    def q_spec(g):
        d = B_GROUPS[g][1]
        return pl.BlockSpec((None, d, tile // d, HEAD_DIM), lambda b, h, i: (b, 0, i, h))

    def kv_spec(g, part):
        d = B_GROUPS[g][1]
        return pl.BlockSpec((None, d, seq // d, HEAD_DIM),
                            lambda b, h, i: (b, 0, 0, part * B_HEADS_PER_GROUP + h))

    in_specs = ([q_spec(g) for g in range(n_groups)] + [kv_spec(g, 1) for g in range(n_groups)]
                + [kv_spec(g, 2) for g in range(n_groups)])
    return pl.pallas_call(
        functools.partial(_attn_b_kernel, tile=tile),
        grid=(batch, B_HEADS_PER_GROUP, n_tiles),
        in_specs=in_specs,
        out_specs=pl.BlockSpec((n_groups, tile, HEAD_DIM), lambda b, h, i: (0, b * n_tiles + i, h)),
        out_shape=jax.ShapeDtypeStruct((n_groups, batch * seq, B_GROUP_WIDTH), BF16),
        scratch_shapes=[pltpu.VMEM((n_groups, tile, HEAD_DIM), F32),
                        pltpu.VMEM((n_groups, tile, HEAD_DIM), F32)],
        compiler_params=_params("parallel", "parallel", "arbitrary"),
        name="dilated_attention",
    )(*(list(qkvs) * 3))


def _shift_rows(x, k, row):
    return jnp.where(row >= k, pltpu.roll(x, k, 0), 0.0)


def _pool_kernel(u_ref, w_ref, scale_ref, o_ref):
    g = pl.program_id(1)
    u = u_ref[...]
    row = lax.broadcasted_iota(jnp.int32, u.shape, 0)
    window = jnp.left_shift(2, g)
    total = u
    step = 1
    while step < max(C_WINDOWS):
        widened = total + _shift_rows(total, step, row)
        total = jnp.where(step < window, widened, total)
        step *= 2
    count = jnp.minimum(row + 1, window).astype(F32)
    pooled = total / count - u
    y = jnp.dot(pooled.astype(BF16), w_ref[0], preferred_element_type=F32)
    o_ref[...] = (y * scale_ref[...]).astype(o_ref.dtype)


def _pool(u, pool_w, pool_scale, batch, seq):
    n_groups = len(C_WINDOWS)
    return pl.pallas_call(
        _pool_kernel,
        grid=(batch, n_groups),
        in_specs=[pl.BlockSpec((seq, C_GROUP_DIM), lambda b, g: (b, g)),
                  pl.BlockSpec((1, C_GROUP_DIM, C_GROUP_DIM), lambda b, g: (g, 0, 0)),
                  pl.BlockSpec((1, C_GROUP_DIM), lambda b, g: (0, g))],
        out_specs=pl.BlockSpec((seq, C_GROUP_DIM), lambda b, g: (b, g)),
        out_shape=jax.ShapeDtypeStruct((batch * seq, C_WIDTH), BF16),
        compiler_params=_params("parallel", "parallel"),
        name="multiscale_pool",
    )(u, pool_w, pool_scale.reshape(1, C_WIDTH))


def _conv_kernel(gb_ref, gc_ref, h_ref, w_ref, o_ref):
    z = gc_ref[...] * h_ref[...]
    row = lax.broadcasted_iota(jnp.int32, z.shape, 0)
    w = w_ref[...]
    y = w[CONV_WIDTH - 1:CONV_WIDTH] * z
    for tap in range(1, CONV_WIDTH):
        y = y + w[CONV_WIDTH - 1 - tap:CONV_WIDTH - tap] * _shift_rows(z, tap, row)
    o_ref[...] = (gb_ref[...] * y).astype(o_ref.dtype)


def _short_conv(gates, conv_w, batch, seq):
    n_col = D_WIDTH // LANES

    def spec(part):
        return pl.BlockSpec((seq, LANES), lambda b, c: (b, part * n_col + c))

    return pl.pallas_call(
        _conv_kernel,
        grid=(batch, n_col),
        in_specs=[spec(0), spec(1), spec(2), pl.BlockSpec((CONV_WIDTH, LANES), lambda b, c: (0, c))],
        out_specs=pl.BlockSpec((seq, LANES), lambda b, c: (b, c)),
        out_shape=jax.ShapeDtypeStruct((batch * seq, D_WIDTH), BF16),
        compiler_params=_params("parallel", "parallel"),
        name="short_conv",
    )(gates, gates, gates, conv_w)


def _rope_tables(seq, dim):
    half = dim // 2
    inv = ROPE_THETA ** (-jnp.arange(0, dim, 2, dtype=F32) / dim)
    ang = jnp.arange(seq, dtype=F32)[:, None] * inv[None, :]
    cos, sin = jnp.cos(ang), jnp.sin(ang)
    reps = LANES // dim
    cos_t = jnp.tile(jnp.concatenate([cos, cos], axis=1), (1, reps))
    zeros = jnp.zeros_like(sin)
    if dim == LANES:
        return (cos_t, jnp.concatenate([-sin, sin], axis=1)), (half,)
    upper = jnp.tile(jnp.concatenate([zeros, sin], axis=1), (1, reps))
    lower = jnp.tile(jnp.concatenate([-sin, zeros], axis=1), (1, reps))
    return (cos_t, upper, lower), (half, LANES - half)


def kernel(x, w_in, w_out, norm_mix, norm_mlp, diff_lambda, diff_subln, pool_w, pool_scale, conv_w,
           w_up, w_down, norm_final):
    batch, seq, d_model = x.shape
    depth = w_in.shape[0]
    m = batch * seq
    x = x.reshape(m, d_model)

    tables_a, shifts_a = _rope_tables(seq, A_SUB)
    tables_b, shifts_b = _rope_tables(seq, HEAD_DIM)

    a_qk_end = 2 * A_WIDTH
    a_end = 3 * A_WIDTH
    b_end = a_end + 3 * B_WIDTH
    c_end = b_end + C_WIDTH
    t_a = 512

    for l in range(depth):
        wl = w_in[l]
        w_a = wl[:, :a_qk_end].astype(BF16)
        w_a_vt = wl[:, a_qk_end:a_end].T.astype(BF16)
        w_b = [jnp.concatenate([wl[:, a_end + part * B_WIDTH + g * B_GROUP_WIDTH:
                                   a_end + part * B_WIDTH + (g + 1) * B_GROUP_WIDTH] for part in range(3)],
                               axis=1).astype(BF16) for g in range(len(B_GROUPS))]
        w_c = wl[:, b_end:c_end].astype(BF16)
        w_d = wl[:, c_end:].astype(BF16)
        lam_init = 0.8 - 0.6 * math.exp(-0.3 * l)

        h = _rmsnorm(x, norm_mix[l], BF16)
        qk_a = _mm_rope(h, w_a, tables_a, shifts_a, 2, A_SUB ** -0.5, A_WIDTH, seq)
        vt_a = _mm_transposed(h, w_a_vt, t_a)
        qkv_b = [_mm_rope_permute(h, w_b[g], tables_b, shifts_b[0], 2 * B_GROUP_WIDTH, B_GROUPS[g][1], batch, seq)
                 for g in range(len(B_GROUPS))]
        u = _mm(h, w_c, F32, C_WIDTH, name="in_proj_pool")
        gates = _mm(h, w_d, F32, D_WIDTH, name="in_proj_conv")

        out_a = _attn_a(qk_a, vt_a, diff_lambda[l], diff_subln[l], lam_init, batch, seq, t_a)
        out_b = _attn_b(qkv_b, batch, seq)
        out_c = _pool(u, pool_w[l].astype(BF16), pool_scale[l], batch, seq)
        out_d = _short_conv(gates, conv_w[l], batch, seq)

        x = _out_proj(out_a, out_b, out_c, out_d, w_out, l, x)

        h = _rmsnorm(x, norm_mlp[l], BF16)
        act = _mm_layer(h, w_up, l, BF16, 512, relu_sq=True, name="mlp_up")
        x = _mm_residual(act, w_down, l, x)

    out = _rmsnorm(x, norm_final, F32)
    return out.reshape(batch, seq, d_model)
```

```python
import functools
import math

import jax
import jax.numpy as jnp
from jax import lax
from jax.experimental import pallas as pl
from jax.experimental.pallas import tpu as pltpu

D_MODEL = 4096
HEAD_DIM = 128
A_HEADS = 8
A_WIDTH = A_HEADS * HEAD_DIM
A_SUB = HEAD_DIM // 2
A_SCORE_SCALE = A_SUB ** -0.5 * math.log2(math.e)
A_QUERY_CHUNK = 1024
A_ONES_ROWS = 16
B_GROUPS = ((128, 1), (512, 4), (2048, 16))
B_HEADS_PER_GROUP = 3
B_GROUP_WIDTH = B_HEADS_PER_GROUP * HEAD_DIM
B_WIDTH = B_GROUP_WIDTH * len(B_GROUPS)
B_TOKEN_TILE = 2048
B_QUERY_CHUNK = 256
C_WINDOWS = (2, 4, 8, 16)
C_WIDTH = 1024
C_GROUP_DIM = C_WIDTH // len(C_WINDOWS)
D_WIDTH = D_MODEL - A_WIDTH - B_WIDTH - C_WIDTH
CONV_WIDTH = 3
D_FF = 4 * D_MODEL
ROPE_THETA = 10000.0
NORM_EPS = 1e-6
DIFF_EPS = 1e-5

LANES = 128
MXU_COLS = 256
VMEM_LIMIT_BYTES = 56 * 1024 * 1024
MASKED_SCORE = -1e30

F32 = jnp.float32
BF16 = jnp.bfloat16


def _params(*semantics):
    return pltpu.CompilerParams(dimension_semantics=semantics, vmem_limit_bytes=VMEM_LIMIT_BYTES)


def _rmsnorm_kernel(x_ref, g_ref, o_ref, *, eps):
    x = x_ref[...]
    ms = jnp.mean(x * x, axis=-1, keepdims=True)
    o_ref[...] = (x * lax.rsqrt(ms + eps) * g_ref[...]).astype(o_ref.dtype)


def _rmsnorm(x, g, out_dtype, tm=256):
    m, d = x.shape
    return pl.pallas_call(
        functools.partial(_rmsnorm_kernel, eps=NORM_EPS),
        grid=(m // tm,),
        in_specs=[pl.BlockSpec((tm, d), lambda i: (i, 0)),
                  pl.BlockSpec((1, d), lambda i: (0, 0))],
        out_specs=pl.BlockSpec((tm, d), lambda i: (i, 0)),
        out_shape=jax.ShapeDtypeStruct((m, d), out_dtype),
        compiler_params=_params("parallel"),
        name="rmsnorm",
    )(x, g.reshape(1, d))


def _rotate(y, cos_ref, sin_refs, shifts):
    outs = []
    for c in range(y.shape[1] // LANES):
        yc = y[:, c * LANES:(c + 1) * LANES]
        oc = yc * cos_ref[...]
        for s_ref, shift in zip(sin_refs, shifts):
            oc = oc + pltpu.roll(yc, shift, 1) * s_ref[...]
        outs.append(oc)
    return jnp.concatenate(outs, axis=1)


def _mm_rope_kernel(a_ref, w_ref, cos_ref, *rest, shifts, q_scale):
    sin_refs, o_ref = rest[:-1], rest[-1]
    scale = jnp.where(pl.program_id(1) == 0, q_scale, 1.0).astype(F32)
    a = a_ref[...]
    for c in range(o_ref.shape[1] // MXU_COLS):
        cols = slice(c * MXU_COLS, (c + 1) * MXU_COLS)
        acc = jnp.dot(a, w_ref[:, cols], preferred_element_type=F32)
        o_ref[:, cols] = (_rotate(acc, cos_ref, sin_refs, shifts) * scale).astype(o_ref.dtype)


def _mm_rope(a, w, tables, shifts, q_scale, bn, seq, bm=1024):
    m, k = a.shape
    n = w.shape[1]
    tab_blocks = seq // bm
    tab_spec = pl.BlockSpec((bm, LANES), lambda i, j: (i % tab_blocks, 0))
    return pl.pallas_call(
        functools.partial(_mm_rope_kernel, shifts=shifts, q_scale=q_scale),
        grid=(m // bm, n // bn),
        in_specs=[pl.BlockSpec((bm, k), lambda i, j: (i, 0)),
                  pl.BlockSpec((k, bn), lambda i, j: (0, j))] + [tab_spec] * len(tables),
        out_specs=pl.BlockSpec((bm, bn), lambda i, j: (i, j)),
        out_shape=jax.ShapeDtypeStruct((m, n), BF16),
        compiler_params=_params("parallel", "arbitrary"),
        name="in_proj_rope",
    )(a, w, *tables)


def _mm_rope_permute_kernel(a_ref, w_ref, cos_ref, sin_ref, o_ref, scr_ref, *, shift, n_rope_cols, dilation):
    a = a_ref[...]
    n = o_ref.shape[2]
    rows = a.shape[0] // dilation
    for lo in range(0, n, 2 * MXU_COLS):
        hi = min(lo + 2 * MXU_COLS, n)
        acc = jnp.dot(a, w_ref[:, lo:hi], preferred_element_type=F32)
        rope_hi = min(hi, n_rope_cols)
        if lo < rope_hi:
            rot = _rotate(acc[:, :rope_hi - lo], cos_ref, (sin_ref,), (shift,))
            acc = rot if rope_hi == hi else jnp.concatenate([rot, acc[:, rope_hi - lo:]], axis=1)
        if dilation == 1:
            o_ref[0, :, lo:hi] = acc.astype(o_ref.dtype)
        else:
            for c in range(lo // LANES, hi // LANES):
                cols = slice(c * LANES, (c + 1) * LANES)
                scr_ref[c] = acc[:, c * LANES - lo:(c + 1) * LANES - lo]
                for r in range(dilation):
                    o_ref[r, :, cols] = scr_ref[c, pl.ds(r, rows, stride=dilation), :].astype(o_ref.dtype)


def _mm_rope_permute(a, w, tables, shift, n_rope_cols, dilation, batch, seq, bm=1024):
    m, k = a.shape
    n = w.shape[1]
    tiles = seq // bm
    tab_spec = pl.BlockSpec((bm, LANES), lambda i: (i % tiles, 0))
    return pl.pallas_call(
        functools.partial(_mm_rope_permute_kernel, shift=shift, n_rope_cols=n_rope_cols, dilation=dilation),
        grid=(m // bm,),
        in_specs=[pl.BlockSpec((bm, k), lambda i: (i, 0)),
                  pl.BlockSpec((k, n), lambda i: (0, 0), pipeline_mode=pl.Buffered(1)),
                  tab_spec, tab_spec],
        out_specs=pl.BlockSpec((None, dilation, bm // dilation, n), lambda i: (i // tiles, 0, i % tiles, 0)),
        out_shape=jax.ShapeDtypeStruct((batch, dilation, seq // dilation, n), BF16),
        scratch_shapes=[pltpu.VMEM((n // LANES, bm, LANES), F32)],
        compiler_params=_params("parallel"),
        name="in_proj_dilated",
    )(a, w, *tables)


def _mm_kernel(a_ref, w_ref, o_ref, *, relu_sq):
    acc = jnp.dot(a_ref[...], w_ref[...].astype(BF16), preferred_element_type=F32)
    if relu_sq:
        acc = jnp.square(jnp.maximum(acc, 0.0))
    o_ref[...] = acc.astype(o_ref.dtype)


def _mm(a, w, out_dtype, bn, relu_sq=False, bm=1024, name="matmul"):
    m, k = a.shape
    n = w.shape[1]
    return pl.pallas_call(
        functools.partial(_mm_kernel, relu_sq=relu_sq),
        grid=(m // bm, n // bn),
        in_specs=[pl.BlockSpec((bm, k), lambda i, j: (i, 0)),
                  pl.BlockSpec((k, bn), lambda i, j: (0, j))],
        out_specs=pl.BlockSpec((bm, bn), lambda i, j: (i, j)),
        out_shape=jax.ShapeDtypeStruct((m, n), out_dtype),
        compiler_params=_params("parallel", "arbitrary"),
        name=name,
    )(a, w)


def _mm_layer(a, w, layer, out_dtype, bn, relu_sq=False, bm=2048, name="matmul"):
    m, k = a.shape
    n = w.shape[2]
    return pl.pallas_call(
        functools.partial(_mm_kernel, relu_sq=relu_sq),
        grid=(m // bm, n // bn),
        in_specs=[pl.BlockSpec((bm, k), lambda i, j: (i, 0), pipeline_mode=pl.Buffered(1)),
                  pl.BlockSpec((None, k, bn), lambda i, j: (layer, 0, j))],
        out_specs=pl.BlockSpec((bm, bn), lambda i, j: (i, j)),
        out_shape=jax.ShapeDtypeStruct((m, n), out_dtype),
        compiler_params=_params("parallel", "arbitrary"),
        name=name,
    )(a, w)


def _mm_transposed_kernel(wt_ref, a_ref, o_ref, *, t):
    acc = lax.dot_general(wt_ref[...], a_ref[...], (((1,), (1,)), ((), ())), preferred_element_type=F32)
    for c in range(acc.shape[1] // t):
        o_ref[c] = acc[:, c * t:(c + 1) * t].astype(o_ref.dtype)


def _mm_transposed(a, wt, t, bm=1024):
    m, k = a.shape
    n = wt.shape[0]
    return pl.pallas_call(
        functools.partial(_mm_transposed_kernel, t=t),
        grid=(m // bm,),
        in_specs=[pl.BlockSpec((n, k), lambda i: (0, 0)),
                  pl.BlockSpec((bm, k), lambda i: (i, 0))],
        out_specs=pl.BlockSpec((bm // t, n, t), lambda i: (i, 0, 0)),
        out_shape=jax.ShapeDtypeStruct((m // t, n, t), BF16),
        compiler_params=_params("parallel"),
        name="in_proj_vt",
    )(wt, a)


def _mm_residual_kernel(a_ref, w_ref, r_ref, o_ref):
    kk = pl.program_id(2)

    @pl.when(kk == 0)
    def _():
        o_ref[...] = r_ref[...]

    o_ref[...] += jnp.dot(a_ref[...], w_ref[...].astype(BF16), preferred_element_type=F32)


def _mm_residual(a, w, layer, r, bm=2048, bn=1024, bk=1024):
    m, k = a.shape
    n = w.shape[2]
    return pl.pallas_call(
        _mm_residual_kernel,
        grid=(m // bm, n // bn, k // bk),
        in_specs=[pl.BlockSpec((bm, bk), lambda i, j, kk: (i, kk)),
                  pl.BlockSpec((None, bk, bn), lambda i, j, kk: (layer, kk, j)),
                  pl.BlockSpec((bm, bn), lambda i, j, kk: (i, j), pipeline_mode=pl.Buffered(1))],
        out_specs=pl.BlockSpec((bm, bn), lambda i, j, kk: (i, j)),
        out_shape=jax.ShapeDtypeStruct((m, n), F32),
        compiler_params=_params("parallel", "parallel", "arbitrary"),
        name="mlp_down",
    )(a, w, r)


def _out_proj_kernel(a_ref, b_ref, c_ref, d_ref, w_ref, r_ref, o_ref):
    def w_rows(lo, size):
        return w_ref[lo:lo + size, :].astype(BF16)

    acc = r_ref[...] + jnp.dot(a_ref[...], w_rows(0, A_WIDTH), preferred_element_type=F32)
    off = A_WIDTH
    for g in range(len(B_GROUPS)):
        acc += jnp.dot(b_ref[g], w_rows(off, B_GROUP_WIDTH), preferred_element_type=F32)
        off += B_GROUP_WIDTH
    acc += jnp.dot(c_ref[...], w_rows(off, C_WIDTH), preferred_element_type=F32)
    off += C_WIDTH
    acc += jnp.dot(d_ref[...], w_rows(off, D_WIDTH), preferred_element_type=F32)
    o_ref[...] = acc


def _out_proj(out_a, out_b, out_c, out_d, w, layer, r, bm=1024, bn=512):
    m = out_a.shape[0]
    n = w.shape[2]
    n_groups = len(B_GROUPS)
    return pl.pallas_call(
        _out_proj_kernel,
        grid=(m // bm, n // bn),
        in_specs=[pl.BlockSpec((bm, A_WIDTH), lambda i, j: (i, 0)),
                  pl.BlockSpec((n_groups, bm, B_GROUP_WIDTH), lambda i, j: (0, i, 0)),
                  pl.BlockSpec((bm, C_WIDTH), lambda i, j: (i, 0)),
                  pl.BlockSpec((bm, D_WIDTH), lambda i, j: (i, 0)),
                  pl.BlockSpec((None, D_MODEL, bn), lambda i, j: (layer, 0, j)),
                  pl.BlockSpec((bm, bn), lambda i, j: (i, j))],
        out_specs=pl.BlockSpec((bm, bn), lambda i, j: (i, j)),
        out_shape=jax.ShapeDtypeStruct((m, n), F32),
        compiler_params=_params("parallel", "arbitrary"),
        name="out_proj",
    )(out_a, out_b, out_c, out_d, w, r)


def _online_softmax_step(s, v, m, l, acc):
    m_new = jnp.maximum(m, jnp.max(s, axis=-1, keepdims=True))
    alpha = jnp.exp(m - m_new)
    p = jnp.exp(s - m_new)
    l_new = alpha * l + jnp.sum(p, axis=-1, keepdims=True)
    acc_new = alpha * acc + jnp.dot(p.astype(BF16), v, preferred_element_type=F32)
    return m_new, l_new, acc_new


def _attn_a_kernel(q_ref, k_ref, vt_ref, lam_ref, g_ref, o_ref, *, t, heads, lam_init):
    qi = pl.program_id(2)
    lane = lax.broadcasted_iota(jnp.int32, (t, HEAD_DIM), 1)
    qqs = []
    for h in range(heads):
        q = q_ref[:, h * HEAD_DIM:(h + 1) * HEAD_DIM]
        zero = jnp.zeros_like(q)
        qqs.append(jnp.concatenate([jnp.where(lane < A_SUB, q, zero), jnp.where(lane >= A_SUB, q, zero)], axis=0))

    n_chunks = 2 * t // A_QUERY_CHUNK
    ones = jnp.ones((A_ONES_ROWS, t), BF16)

    def step(j, carries, masked):
        out = []
        for h in range(heads):
            k = k_ref[pl.ds(pl.multiple_of(j * t, t), t), h * HEAD_DIM:(h + 1) * HEAD_DIM]
            vt = jnp.concatenate([vt_ref[j, h * HEAD_DIM:(h + 1) * HEAD_DIM, :], ones], axis=0)
            chains = []
            for c in range(n_chunks):
                m, acc = carries[h][c]
                qc = qqs[h][c * A_QUERY_CHUNK:(c + 1) * A_QUERY_CHUNK]
                s = lax.dot_general(k, qc, (((1,), (1,)), ((), ())), preferred_element_type=F32)
                if masked:
                    key = lax.broadcasted_iota(jnp.int32, s.shape, 0)
                    lane_q = lax.broadcasted_iota(jnp.int32, s.shape, 1) + c * A_QUERY_CHUNK
                    qry = jnp.where(lane_q >= t, lane_q - t, lane_q)
                    s = jnp.where(key <= qry, s, MASKED_SCORE)
                m_new = jnp.maximum(m, jnp.max(s, axis=0, keepdims=True))
                alpha = jnp.exp2(m - m_new)
                p = jnp.exp2(s - m_new)
                acc_new = alpha * acc + jnp.dot(vt, p.astype(BF16), preferred_element_type=F32)
                chains.append((m_new, acc_new))
            out.append(tuple(chains))
        return tuple(out)

    init = tuple(tuple((jnp.full((1, A_QUERY_CHUNK), MASKED_SCORE, F32),
                        jnp.zeros((HEAD_DIM + A_ONES_ROWS, A_QUERY_CHUNK), F32)) for _ in range(n_chunks))
                 for _ in range(heads))
    carries = lax.fori_loop(0, qi, functools.partial(step, masked=False), init)
    carries = step(qi, carries, masked=True)

    lp = lam_ref[...]
    lam = (jnp.exp(jnp.sum(lp[0:1] * lp[1:2], axis=-1, keepdims=True))
           - jnp.exp(jnp.sum(lp[2:3] * lp[3:4], axis=-1, keepdims=True)) + lam_init)
    for h in range(heads):
        acc = jnp.concatenate([carries[h][c][1] for c in range(n_chunks)], axis=1)
        o = acc[:HEAD_DIM] / acc[HEAD_DIM:HEAD_DIM + 1]
        o = (o[:, :t] - lam * o[:, t:]).T
        ms = jnp.mean(o * o, axis=-1, keepdims=True)
        o = o * lax.rsqrt(ms + DIFF_EPS) * g_ref[...]
        o_ref[:, h * HEAD_DIM:(h + 1) * HEAD_DIM] = (o * (1.0 - lam_init)).astype(o_ref.dtype)


def _attn_a(qk, vt, lam_params, subln_g, lam_init, batch, seq, t, heads=4):
    nq = seq // t
    width = heads * HEAD_DIM
    n_pairs = A_HEADS // heads
    return pl.pallas_call(
        functools.partial(_attn_a_kernel, t=t, heads=heads, lam_init=lam_init),
        grid=(batch, n_pairs, nq),
        in_specs=[pl.BlockSpec((t, width), lambda b, h, i: (b * nq + i, h)),
                  pl.BlockSpec((seq, width), lambda b, h, i: (b, n_pairs + h)),
                  pl.BlockSpec((nq, width, t), lambda b, h, i: (b, h, 0)),
                  pl.BlockSpec((4, A_SUB), lambda b, h, i: (0, 0)),
                  pl.BlockSpec((1, HEAD_DIM), lambda b, h, i: (0, 0))],
        out_specs=pl.BlockSpec((t, width), lambda b, h, i: (b * nq + i, h)),
        out_shape=jax.ShapeDtypeStruct((batch * seq, A_WIDTH), BF16),
        compiler_params=_params("parallel", "parallel", "arbitrary"),
        name="diff_attention",
    )(qk, qk, vt, lam_params, subln_g.reshape(1, HEAD_DIM))


def _attn_b_kernel(*refs, tile):
    n_groups = len(B_GROUPS)
    q_refs = refs[0:n_groups]
    k_refs = refs[n_groups:2 * n_groups]
    v_refs = refs[2 * n_groups:3 * n_groups]
    o_ref = refs[3 * n_groups]
    out_scr, lse_scr = refs[3 * n_groups + 1:]
    ti = pl.program_id(2)
    scale = HEAD_DIM ** -0.5
    base_delta = {}

    for g, (window, dilation) in enumerate(B_GROUPS):
        back = window // dilation
        seg = tile // dilation
        tq = min(B_QUERY_CHUNK, seg)
        tk = tq + back
        if tq not in base_delta:
            base_delta[tq] = (lax.broadcasted_iota(jnp.int32, (tq, tk), 0)
                              - lax.broadcasted_iota(jnp.int32, (tq, tk), 1))
        for r in range(dilation):
            for c in range(seg // tq):
                i_loc = c * tq
                i0 = ti * seg + i_loc
                ks = pl.multiple_of(jnp.maximum(i0 - back, 0), back)
                q = q_refs[g][r, i_loc:i_loc + tq, :]
                k = k_refs[g][r, pl.ds(ks, tk), :]
                v = v_refs[g][r, pl.ds(ks, tk), :]
                s = lax.dot_general(q, k, (((1,), (1,)), ((), ())), preferred_element_type=F32) * scale
                delta = base_delta[tq] + (i0 - ks)
                s = jnp.where((delta >= 0) & (delta <= back), s, MASKED_SCORE)
                m = jnp.max(s, axis=-1, keepdims=True)
                p = jnp.exp(s - m)
                l = jnp.sum(p, axis=-1, keepdims=True)
                o = jnp.dot(p.astype(BF16), v, preferred_element_type=F32) / l
                lse = jnp.broadcast_to(m + jnp.log(l), o.shape)
                rows = pl.ds(dilation * i_loc + r, tq, stride=dilation) if dilation > 1 else pl.ds(i_loc, tq)
                out_scr[g, rows, :] = o
                lse_scr[g, rows, :] = lse

    for c in range(tile // B_QUERY_CHUNK):
        rows = pl.ds(c * B_QUERY_CHUNK, B_QUERY_CHUNK)
        lses = [lse_scr[g, rows, :] for g in range(n_groups)]
        lse_max = functools.reduce(jnp.maximum, lses)
        weights = [jnp.exp(x - lse_max) for x in lses]
        denom = functools.reduce(lambda a, b: a + b, weights)
        for g in range(n_groups):
            o_ref[g, rows, :] = (out_scr[g, rows, :] * (weights[g] / denom)).astype(o_ref.dtype)


def _attn_b(qkvs, batch, seq, tile=B_TOKEN_TILE):
    n_tiles = seq // tile
    n_groups = len(B_GROUPS)

    def q_spec(g):
        d = B_GROUPS[g][1]
        return pl.BlockSpec((None, d, tile // d, HEAD_DIM), lambda b, h, i: (b, 0, i, h))

    def kv_spec(g, part):
        d = B_GROUPS[g][1]
        return pl.BlockSpec((None, d, seq // d, HEAD_DIM),
                            lambda b, h, i: (b, 0, 0, part * B_HEADS_PER_GROUP + h))

    in_specs = ([q_spec(g) for g in range(n_groups)] + [kv_spec(g, 1) for g in range(n_groups)]
                + [kv_spec(g, 2) for g in range(n_groups)])
    return pl.pallas_call(
        functools.partial(_attn_b_kernel, tile=tile),
        grid=(batch, B_HEADS_PER_GROUP, n_tiles),
        in_specs=in_specs,
        out_specs=pl.BlockSpec((n_groups, tile, HEAD_DIM), lambda b, h, i: (0, b * n_tiles + i, h)),
        out_shape=jax.ShapeDtypeStruct((n_groups, batch * seq, B_GROUP_WIDTH), BF16),
        scratch_shapes=[pltpu.VMEM((n_groups, tile, HEAD_DIM), F32),
                        pltpu.VMEM((n_groups, tile, HEAD_DIM), F32)],
        compiler_params=_params("parallel", "parallel", "arbitrary"),
        name="dilated_attention",
    )(*(list(qkvs) * 3))


def _shift_rows(x, k, row):
    return jnp.where(row >= k, pltpu.roll(x, k, 0), 0.0)


def _pool_kernel(u_ref, w_ref, scale_ref, o_ref):
    g = pl.program_id(1)
    u = u_ref[...]
    row = lax.broadcasted_iota(jnp.int32, u.shape, 0)
    window = jnp.left_shift(2, g)
    total = u
    step = 1
    while step < max(C_WINDOWS):
        widened = total + _shift_rows(total, step, row)
        total = jnp.where(step < window, widened, total)
        step *= 2
    count = jnp.minimum(row + 1, window).astype(F32)
    pooled = total / count - u
    y = jnp.dot(pooled.astype(BF16), w_ref[0], preferred_element_type=F32)
    o_ref[...] = (y * scale_ref[...]).astype(o_ref.dtype)


def _pool(u, pool_w, pool_scale, batch, seq):
    n_groups = len(C_WINDOWS)
    return pl.pallas_call(
        _pool_kernel,
        grid=(batch, n_groups),
        in_specs=[pl.BlockSpec((seq, C_GROUP_DIM), lambda b, g: (b, g)),
                  pl.BlockSpec((1, C_GROUP_DIM, C_GROUP_DIM), lambda b, g: (g, 0, 0)),
                  pl.BlockSpec((1, C_GROUP_DIM), lambda b, g: (0, g))],
        out_specs=pl.BlockSpec((seq, C_GROUP_DIM), lambda b, g: (b, g)),
        out_shape=jax.ShapeDtypeStruct((batch * seq, C_WIDTH), BF16),
        compiler_params=_params("parallel", "parallel"),
        name="multiscale_pool",
    )(u, pool_w, pool_scale.reshape(1, C_WIDTH))


def _conv_kernel(gb_ref, gc_ref, h_ref, w_ref, o_ref):
    z = gc_ref[...] * h_ref[...]
    row = lax.broadcasted_iota(jnp.int32, z.shape, 0)
    w = w_ref[...]
    y = w[CONV_WIDTH - 1:CONV_WIDTH] * z
    for tap in range(1, CONV_WIDTH):
        y = y + w[CONV_WIDTH - 1 - tap:CONV_WIDTH - tap] * _shift_rows(z, tap, row)
    o_ref[...] = (gb_ref[...] * y).astype(o_ref.dtype)


def _short_conv(gates, conv_w, batch, seq):
    n_col = D_WIDTH // LANES

    def spec(part):
        return pl.BlockSpec((seq, LANES), lambda b, c: (b, part * n_col + c))

    return pl.pallas_call(
        _conv_kernel,
        grid=(batch, n_col),
        in_specs=[spec(0), spec(1), spec(2), pl.BlockSpec((CONV_WIDTH, LANES), lambda b, c: (0, c))],
        out_specs=pl.BlockSpec((seq, LANES), lambda b, c: (b, c)),
        out_shape=jax.ShapeDtypeStruct((batch * seq, D_WIDTH), BF16),
        compiler_params=_params("parallel", "parallel"),
        name="short_conv",
    )(gates, gates, gates, conv_w)


def _rope_tables(seq, dim):
    half = dim // 2
    inv = ROPE_THETA ** (-jnp.arange(0, dim, 2, dtype=F32) / dim)
    ang = jnp.arange(seq, dtype=F32)[:, None] * inv[None, :]
    cos, sin = jnp.cos(ang), jnp.sin(ang)
    reps = LANES // dim
    cos_t = jnp.tile(jnp.concatenate([cos, cos], axis=1), (1, reps))
    zeros = jnp.zeros_like(sin)
    if dim == LANES:
        return (cos_t, jnp.concatenate([-sin, sin], axis=1)), (half,)
    upper = jnp.tile(jnp.concatenate([zeros, sin], axis=1), (1, reps))
    lower = jnp.tile(jnp.concatenate([-sin, zeros], axis=1), (1, reps))
    return (cos_t, upper, lower), (half, LANES - half)


def kernel(x, w_in, w_out, norm_mix, norm_mlp, diff_lambda, diff_subln, pool_w, pool_scale, conv_w,
           w_up, w_down, norm_final):
    batch, seq, d_model = x.shape
    depth = w_in.shape[0]
    m = batch * seq
    x = x.reshape(m, d_model)

    tables_a, shifts_a = _rope_tables(seq, A_SUB)
    tables_b, shifts_b = _rope_tables(seq, HEAD_DIM)

    a_qk_end = 2 * A_WIDTH
    a_end = 3 * A_WIDTH
    b_end = a_end + 3 * B_WIDTH
    c_end = b_end + C_WIDTH
    t_a = 512

    for l in range(depth):
        wl = w_in[l]
        w_a = wl[:, :a_qk_end].astype(BF16)
        w_a_vt = wl[:, a_qk_end:a_end].T.astype(BF16)
        w_b = [jnp.concatenate([wl[:, a_end + part * B_WIDTH + g * B_GROUP_WIDTH:
                                   a_end + part * B_WIDTH + (g + 1) * B_GROUP_WIDTH] for part in range(3)],
                               axis=1).astype(BF16) for g in range(len(B_GROUPS))]
        w_c = wl[:, b_end:c_end].astype(BF16)
        w_d = wl[:, c_end:].astype(BF16)
        lam_init = 0.8 - 0.6 * math.exp(-0.3 * l)

        h = _rmsnorm(x, norm_mix[l], BF16)
        qk_a = _mm_rope(h, w_a, tables_a, shifts_a, A_SCORE_SCALE, A_WIDTH, seq)
        vt_a = _mm_transposed(h, w_a_vt, t_a)
        qkv_b = [_mm_rope_permute(h, w_b[g], tables_b, shifts_b[0], 2 * B_GROUP_WIDTH, B_GROUPS[g][1], batch, seq)
                 for g in range(len(B_GROUPS))]
        u = _mm(h, w_c, F32, C_WIDTH, name="in_proj_pool")
        gates = _mm(h, w_d, F32, D_WIDTH, name="in_proj_conv")

        out_a = _attn_a(qk_a, vt_a, diff_lambda[l], diff_subln[l], lam_init, batch, seq, t_a)
        out_b = _attn_b(qkv_b, batch, seq)
        out_c = _pool(u, pool_w[l].astype(BF16), pool_scale[l], batch, seq)
        out_d = _short_conv(gates, conv_w[l], batch, seq)

        x = _out_proj(out_a, out_b, out_c, out_d, w_out, l, x)

        h = _rmsnorm(x, norm_mlp[l], BF16)
        act = _mm_layer(h, w_up, l, BF16, 512, relu_sq=True, name="mlp_up")
        x = _mm_residual(act, w_down, l, x)

    out = _rmsnorm(x, norm_final, F32)
    return out.reshape(batch, seq, d_model)
```

```python
import functools
import math

import jax
import jax.numpy as jnp
from jax import lax
from jax.experimental import pallas as pl
from jax.experimental.pallas import tpu as pltpu

D_MODEL = 4096
HEAD_DIM = 128
A_HEADS = 8
A_WIDTH = A_HEADS * HEAD_DIM
A_SUB = HEAD_DIM // 2
A_SCORE_SCALE = A_SUB ** -0.5 * math.log2(math.e)
A_ONES_ROWS = 16
B_GROUPS = ((128, 1), (512, 4), (2048, 16))
B_HEADS_PER_GROUP = 3
B_GROUP_WIDTH = B_HEADS_PER_GROUP * HEAD_DIM
B_WIDTH = B_GROUP_WIDTH * len(B_GROUPS)
B_TOKEN_TILE = 2048
B_QUERY_CHUNK = 256
C_WINDOWS = (2, 4, 8, 16)
C_WIDTH = 1024
C_GROUP_DIM = C_WIDTH // len(C_WINDOWS)
D_WIDTH = D_MODEL - A_WIDTH - B_WIDTH - C_WIDTH
CONV_WIDTH = 3
D_FF = 4 * D_MODEL
ROPE_THETA = 10000.0
NORM_EPS = 1e-6
DIFF_EPS = 1e-5

LANES = 128
MXU_COLS = 256
VMEM_LIMIT_BYTES = 56 * 1024 * 1024
MASKED_SCORE = -1e30

F32 = jnp.float32
BF16 = jnp.bfloat16


def _params(*semantics):
    return pltpu.CompilerParams(dimension_semantics=semantics, vmem_limit_bytes=VMEM_LIMIT_BYTES)


def _rmsnorm_kernel(x_ref, g_ref, o_ref, *, eps):
    x = x_ref[...]
    ms = jnp.mean(x * x, axis=-1, keepdims=True)
    o_ref[...] = (x * lax.rsqrt(ms + eps) * g_ref[...]).astype(o_ref.dtype)


def _rmsnorm(x, g, out_dtype, tm=256):
    m, d = x.shape
    return pl.pallas_call(
        functools.partial(_rmsnorm_kernel, eps=NORM_EPS),
        grid=(m // tm,),
        in_specs=[pl.BlockSpec((tm, d), lambda i: (i, 0)),
                  pl.BlockSpec((1, d), lambda i: (0, 0))],
        out_specs=pl.BlockSpec((tm, d), lambda i: (i, 0)),
        out_shape=jax.ShapeDtypeStruct((m, d), out_dtype),
        compiler_params=_params("parallel"),
        name="rmsnorm",
    )(x, g.reshape(1, d))


def _rotate(y, cos_ref, sin_refs, shifts):
    outs = []
    for c in range(y.shape[1] // LANES):
        yc = y[:, c * LANES:(c + 1) * LANES]
        oc = yc * cos_ref[...]
        for s_ref, shift in zip(sin_refs, shifts):
            oc = oc + pltpu.roll(yc, shift, 1) * s_ref[...]
        outs.append(oc)
    return jnp.concatenate(outs, axis=1)


def _mm_rope_kernel(a_ref, w_ref, cos_ref, *rest, shifts, q_scale):
    sin_refs, o_ref = rest[:-1], rest[-1]
    scale = jnp.where(pl.program_id(1) == 0, q_scale, 1.0).astype(F32)
    a = a_ref[...]
    for c in range(o_ref.shape[1] // MXU_COLS):
        cols = slice(c * MXU_COLS, (c + 1) * MXU_COLS)
        acc = jnp.dot(a, w_ref[:, cols], preferred_element_type=F32)
        o_ref[:, cols] = (_rotate(acc, cos_ref, sin_refs, shifts) * scale).astype(o_ref.dtype)


def _mm_rope(a, w, tables, shifts, q_scale, bn, seq, bm=1024):
    m, k = a.shape
    n = w.shape[1]
    tab_blocks = seq // bm
    tab_spec = pl.BlockSpec((bm, LANES), lambda i, j: (i % tab_blocks, 0))
    return pl.pallas_call(
        functools.partial(_mm_rope_kernel, shifts=shifts, q_scale=q_scale),
        grid=(m // bm, n // bn),
        in_specs=[pl.BlockSpec((bm, k), lambda i, j: (i, 0)),
                  pl.BlockSpec((k, bn), lambda i, j: (0, j))] + [tab_spec] * len(tables),
        out_specs=pl.BlockSpec((bm, bn), lambda i, j: (i, j)),
        out_shape=jax.ShapeDtypeStruct((m, n), BF16),
        compiler_params=_params("parallel", "arbitrary"),
        name="in_proj_rope",
    )(a, w, *tables)


def _mm_rope_permute_kernel(a_ref, w_ref, cos_ref, sin_ref, o_ref, scr_ref, *, shift, n_rope_cols, dilation):
    a = a_ref[...]
    n = o_ref.shape[2]
    rows = a.shape[0] // dilation
    for lo in range(0, n, 2 * MXU_COLS):
        hi = min(lo + 2 * MXU_COLS, n)
        acc = jnp.dot(a, w_ref[:, lo:hi], preferred_element_type=F32)
        rope_hi = min(hi, n_rope_cols)
        if lo < rope_hi:
            rot = _rotate(acc[:, :rope_hi - lo], cos_ref, (sin_ref,), (shift,))
            acc = rot if rope_hi == hi else jnp.concatenate([rot, acc[:, rope_hi - lo:]], axis=1)
        if dilation == 1:
            o_ref[0, :, lo:hi] = acc.astype(o_ref.dtype)
        else:
            for c in range(lo // LANES, hi // LANES):
                cols = slice(c * LANES, (c + 1) * LANES)
                scr_ref[c] = acc[:, c * LANES - lo:(c + 1) * LANES - lo]
                for r in range(dilation):
                    o_ref[r, :, cols] = scr_ref[c, pl.ds(r, rows, stride=dilation), :].astype(o_ref.dtype)


def _mm_rope_permute(a, w, tables, shift, n_rope_cols, dilation, batch, seq, bm=1024):
    m, k = a.shape
    n = w.shape[1]
    tiles = seq // bm
    tab_spec = pl.BlockSpec((bm, LANES), lambda i: (i % tiles, 0))
    return pl.pallas_call(
        functools.partial(_mm_rope_permute_kernel, shift=shift, n_rope_cols=n_rope_cols, dilation=dilation),
        grid=(m // bm,),
        in_specs=[pl.BlockSpec((bm, k), lambda i: (i, 0)),
                  pl.BlockSpec((k, n), lambda i: (0, 0), pipeline_mode=pl.Buffered(1)),
                  tab_spec, tab_spec],
        out_specs=pl.BlockSpec((None, dilation, bm // dilation, n), lambda i: (i // tiles, 0, i % tiles, 0)),
        out_shape=jax.ShapeDtypeStruct((batch, dilation, seq // dilation, n), BF16),
        scratch_shapes=[pltpu.VMEM((n // LANES, bm, LANES), F32)],
        compiler_params=_params("parallel"),
        name="in_proj_dilated",
    )(a, w, *tables)


def _mm_kernel(a_ref, w_ref, o_ref, *, relu_sq):
    acc = jnp.dot(a_ref[...], w_ref[...].astype(BF16), preferred_element_type=F32)
    if relu_sq:
        acc = jnp.square(jnp.maximum(acc, 0.0))
    o_ref[...] = acc.astype(o_ref.dtype)


def _mm(a, w, out_dtype, bn, relu_sq=False, bm=1024, name="matmul"):
    m, k = a.shape
    n = w.shape[1]
    return pl.pallas_call(
        functools.partial(_mm_kernel, relu_sq=relu_sq),
        grid=(m // bm, n // bn),
        in_specs=[pl.BlockSpec((bm, k), lambda i, j: (i, 0)),
                  pl.BlockSpec((k, bn), lambda i, j: (0, j))],
        out_specs=pl.BlockSpec((bm, bn), lambda i, j: (i, j)),
        out_shape=jax.ShapeDtypeStruct((m, n), out_dtype),
        compiler_params=_params("parallel", "arbitrary"),
        name=name,
    )(a, w)


def _mm_layer(a, w, layer, out_dtype, bn, relu_sq=False, bm=2048, name="matmul"):
    m, k = a.shape
    n = w.shape[2]
    return pl.pallas_call(
        functools.partial(_mm_kernel, relu_sq=relu_sq),
        grid=(m // bm, n // bn),
        in_specs=[pl.BlockSpec((bm, k), lambda i, j: (i, 0), pipeline_mode=pl.Buffered(1)),
                  pl.BlockSpec((None, k, bn), lambda i, j: (layer, 0, j))],
        out_specs=pl.BlockSpec((bm, bn), lambda i, j: (i, j)),
        out_shape=jax.ShapeDtypeStruct((m, n), out_dtype),
        compiler_params=_params("parallel", "arbitrary"),
        name=name,
    )(a, w)


def _mm_transposed_kernel(wt_ref, a_ref, o_ref, *, t):
    acc = lax.dot_general(wt_ref[...], a_ref[...], (((1,), (1,)), ((), ())), preferred_element_type=F32)
    for c in range(acc.shape[1] // t):
        o_ref[c] = acc[:, c * t:(c + 1) * t].astype(o_ref.dtype)


def _mm_transposed(a, wt, t, bm=1024):
    m, k = a.shape
    n = wt.shape[0]
    return pl.pallas_call(
        functools.partial(_mm_transposed_kernel, t=t),
        grid=(m // bm,),
        in_specs=[pl.BlockSpec((n, k), lambda i: (0, 0)),
                  pl.BlockSpec((bm, k), lambda i: (i, 0))],
        out_specs=pl.BlockSpec((bm // t, n, t), lambda i: (i, 0, 0)),
        out_shape=jax.ShapeDtypeStruct((m // t, n, t), BF16),
        compiler_params=_params("parallel"),
        name="in_proj_vt",
    )(wt, a)


def _mm_residual_kernel(a_ref, w_ref, r_ref, o_ref):
    kk = pl.program_id(2)

    @pl.when(kk == 0)
    def _():
        o_ref[...] = r_ref[...]

    o_ref[...] += jnp.dot(a_ref[...], w_ref[...].astype(BF16), preferred_element_type=F32)


def _mm_residual(a, w, layer, r, bm=2048, bn=1024, bk=1024):
    m, k = a.shape
    n = w.shape[2]
    return pl.pallas_call(
        _mm_residual_kernel,
        grid=(m // bm, n // bn, k // bk),
        in_specs=[pl.BlockSpec((bm, bk), lambda i, j, kk: (i, kk)),
                  pl.BlockSpec((None, bk, bn), lambda i, j, kk: (layer, kk, j)),
                  pl.BlockSpec((bm, bn), lambda i, j, kk: (i, j), pipeline_mode=pl.Buffered(1))],
        out_specs=pl.BlockSpec((bm, bn), lambda i, j, kk: (i, j)),
        out_shape=jax.ShapeDtypeStruct((m, n), F32),
        compiler_params=_params("parallel", "parallel", "arbitrary"),
        name="mlp_down",
    )(a, w, r)


def _out_proj_kernel(a_ref, b_ref, c_ref, d_ref, w_ref, r_ref, o_ref):
    mix = jnp.concatenate([a_ref[...]] + [b_ref[g] for g in range(len(B_GROUPS))] + [c_ref[...], d_ref[...]], axis=1)
    o_ref[...] = r_ref[...] + jnp.dot(mix, w_ref[...].astype(BF16), preferred_element_type=F32)


def _out_proj(out_a, out_b, out_c, out_d, w, layer, r, bm=1024, bn=512):
    m = out_a.shape[0]
    n = w.shape[2]
    n_groups = len(B_GROUPS)
    return pl.pallas_call(
        _out_proj_kernel,
        grid=(m // bm, n // bn),
        in_specs=[pl.BlockSpec((bm, A_WIDTH), lambda i, j: (i, 0)),
                  pl.BlockSpec((n_groups, bm, B_GROUP_WIDTH), lambda i, j: (0, i, 0)),
                  pl.BlockSpec((bm, C_WIDTH), lambda i, j: (i, 0)),
                  pl.BlockSpec((bm, D_WIDTH), lambda i, j: (i, 0)),
                  pl.BlockSpec((None, D_MODEL, bn), lambda i, j: (layer, 0, j)),
                  pl.BlockSpec((bm, bn), lambda i, j: (i, j))],
        out_specs=pl.BlockSpec((bm, bn), lambda i, j: (i, j)),
        out_shape=jax.ShapeDtypeStruct((m, n), F32),
        compiler_params=_params("parallel", "arbitrary"),
        name="out_proj",
    )(out_a, out_b, out_c, out_d, w, r)


def _online_softmax_step(s, v, m, l, acc):
    m_new = jnp.maximum(m, jnp.max(s, axis=-1, keepdims=True))
    alpha = jnp.exp(m - m_new)
    p = jnp.exp(s - m_new)
    l_new = alpha * l + jnp.sum(p, axis=-1, keepdims=True)
    acc_new = alpha * acc + jnp.dot(p.astype(BF16), v, preferred_element_type=F32)
    return m_new, l_new, acc_new


def _attn_a_kernel(q_ref, k_ref, vt_ref, lam_ref, g_ref, o_ref, *scratch, t, heads, lam_init):
    qi = pl.program_id(2)
    lane = lax.broadcasted_iota(jnp.int32, (t, HEAD_DIM), 1)
    qqs = []
    for h in range(heads):
        q = q_ref[:, h * HEAD_DIM:(h + 1) * HEAD_DIM]
        zero = jnp.zeros_like(q)
        qqs.append(jnp.concatenate([jnp.where(lane < A_SUB, q, zero), jnp.where(lane >= A_SUB, q, zero)], axis=0))

    ones = jnp.ones((A_ONES_ROWS, t), BF16)
    m_scr, acc_scr = scratch[:heads], scratch[heads:]
    for h in range(heads):
        m_scr[h][...] = jnp.full(m_scr[h].shape, MASKED_SCORE, F32)
        acc_scr[h][...] = jnp.zeros(acc_scr[h].shape, F32)

    def step(j, masked):
        def scores(h):
            k = k_ref[pl.ds(pl.multiple_of(j * t, t), t), h * HEAD_DIM:(h + 1) * HEAD_DIM]
            s = lax.dot_general(k, qqs[h], (((1,), (1,)), ((), ())), preferred_element_type=F32)
            if masked:
                key = lax.broadcasted_iota(jnp.int32, s.shape, 0)
                lane_q = lax.broadcasted_iota(jnp.int32, s.shape, 1)
                qry = jnp.where(lane_q >= t, lane_q - t, lane_q)
                s = jnp.where(key <= qry, s, MASKED_SCORE)
            return s

        s_next = scores(0)
        for h in range(heads):
            s = s_next
            if h + 1 < heads:
                s_next = scores(h + 1)
            vt = jnp.concatenate([vt_ref[j, h * HEAD_DIM:(h + 1) * HEAD_DIM, :], ones], axis=0)
            m = m_scr[h][...]
            m_new = jnp.maximum(m, jnp.max(s, axis=0, keepdims=True))
            alpha = jnp.exp2(m - m_new)
            p = jnp.exp2(s - m_new)
            m_scr[h][...] = m_new
            acc_scr[h][...] = alpha * acc_scr[h][...] + jnp.dot(vt, p.astype(BF16), preferred_element_type=F32)

    def body(j, carry):
        step(j, masked=False)
        return carry

    lax.fori_loop(0, qi, body, 0)
    step(qi, masked=True)

    lp = lam_ref[...]
    lam = (jnp.exp(jnp.sum(lp[0:1] * lp[1:2], axis=-1, keepdims=True))
           - jnp.exp(jnp.sum(lp[2:3] * lp[3:4], axis=-1, keepdims=True)) + lam_init)
    for h in range(heads):
        acc = acc_scr[h][...]
        o = acc[:HEAD_DIM] / acc[HEAD_DIM:HEAD_DIM + 1]
        o = (o[:, :t] - lam * o[:, t:]).T
        ms = jnp.mean(o * o, axis=-1, keepdims=True)
        o = o * lax.rsqrt(ms + DIFF_EPS) * g_ref[...]
        o_ref[:, h * HEAD_DIM:(h + 1) * HEAD_DIM] = (o * (1.0 - lam_init)).astype(o_ref.dtype)


def _attn_a(qk, vt, lam_params, subln_g, lam_init, batch, seq, t, heads=4):
    nq = seq // t
    width = heads * HEAD_DIM
    n_pairs = A_HEADS // heads
    return pl.pallas_call(
        functools.partial(_attn_a_kernel, t=t, heads=heads, lam_init=lam_init),
        grid=(batch, n_pairs, nq),
        in_specs=[pl.BlockSpec((t, width), lambda b, h, i: (b * nq + i, h)),
                  pl.BlockSpec((seq, width), lambda b, h, i: (b, n_pairs + h)),
                  pl.BlockSpec((nq, width, t), lambda b, h, i: (b, h, 0)),
                  pl.BlockSpec((4, A_SUB), lambda b, h, i: (0, 0)),
                  pl.BlockSpec((1, HEAD_DIM), lambda b, h, i: (0, 0))],
        out_specs=pl.BlockSpec((t, width), lambda b, h, i: (b * nq + i, h)),
        out_shape=jax.ShapeDtypeStruct((batch * seq, A_WIDTH), BF16),
        scratch_shapes=([pltpu.VMEM((1, 2 * t), F32)] * heads
                        + [pltpu.VMEM((HEAD_DIM + A_ONES_ROWS, 2 * t), F32)] * heads),
        compiler_params=_params("parallel", "parallel", "arbitrary"),
        name="diff_attention",
    )(qk, qk, vt, lam_params, subln_g.reshape(1, HEAD_DIM))


def _attn_b_kernel(*refs, tile):
    n_groups = len(B_GROUPS)
    q_refs = refs[0:n_groups]
    k_refs = refs[n_groups:2 * n_groups]
    v_refs = refs[2 * n_groups:3 * n_groups]
    o_ref = refs[3 * n_groups]
    out_scr, lse_scr = refs[3 * n_groups + 1:]
    ti = pl.program_id(2)
    scale = HEAD_DIM ** -0.5
    base_delta = {}

    for g, (window, dilation) in enumerate(B_GROUPS):
        back = window // dilation
        seg = tile // dilation
        tq = min(B_QUERY_CHUNK, seg)
        tk = tq + back
        if tq not in base_delta:
            base_delta[tq] = (lax.broadcasted_iota(jnp.int32, (tq, tk), 0)
                              - lax.broadcasted_iota(jnp.int32, (tq, tk), 1))
        for r in range(dilation):
            for c in range(seg // tq):
                i_loc = c * tq
                i0 = ti * seg + i_loc
                ks = pl.multiple_of(jnp.maximum(i0 - back, 0), back)
                q = q_refs[g][r, i_loc:i_loc + tq, :]
                k = k_refs[g][r, pl.ds(ks, tk), :]
                v = v_refs[g][r, pl.ds(ks, tk), :]
                s = lax.dot_general(q, k, (((1,), (1,)), ((), ())), preferred_element_type=F32) * scale
                delta = base_delta[tq] + (i0 - ks)
                s = jnp.where((delta >= 0) & (delta <= back), s, MASKED_SCORE)
                m = jnp.max(s, axis=-1, keepdims=True)
                p = jnp.exp(s - m)
                l = jnp.sum(p, axis=-1, keepdims=True)
                o = jnp.dot(p.astype(BF16), v, preferred_element_type=F32) / l
                lse = jnp.broadcast_to(m + jnp.log(l), o.shape)
                rows = pl.ds(dilation * i_loc + r, tq, stride=dilation) if dilation > 1 else pl.ds(i_loc, tq)
                out_scr[g, rows, :] = o
                lse_scr[g, rows, :] = lse

    for c in range(tile // B_QUERY_CHUNK):
        rows = pl.ds(c * B_QUERY_CHUNK, B_QUERY_CHUNK)
        lses = [lse_scr[g, rows, :] for g in range(n_groups)]
        lse_max = functools.reduce(jnp.maximum, lses)
        weights = [jnp.exp(x - lse_max) for x in lses]
        denom = functools.reduce(lambda a, b: a + b, weights)
        for g in range(n_groups):
            o_ref[g, rows, :] = (out_scr[g, rows, :] * (weights[g] / denom)).astype(o_ref.dtype)


def _attn_b(qkvs, batch, seq, tile=B_TOKEN_TILE):
    n_tiles = seq // tile
    n_groups = len(B_GROUPS)

    def q_spec(g):
        d = B_GROUPS[g][1]
        return pl.BlockSpec((None, d, tile // d, HEAD_DIM), lambda b, h, i: (b, 0, i, h))

    def kv_spec(g, part):
        d = B_GROUPS[g][1]
        return pl.BlockSpec((None, d, seq // d, HEAD_DIM),
                            lambda b, h, i: (b, 0, 0, part * B_HEADS_PER_GROUP + h))

    in_specs = ([q_spec(g) for g in range(n_groups)] + [kv_spec(g, 1) for g in range(n_groups)]
                + [kv_spec(g, 2) for g in range(n_groups)])
    return pl.pallas_call(
        functools.partial(_attn_b_kernel, tile=tile),
        grid=(batch, B_HEADS_PER_GROUP, n_tiles),
        in_specs=in_specs,
        out_specs=pl.BlockSpec((n_groups, tile, HEAD_DIM), lambda b, h, i: (0, b * n_tiles + i, h)),
        out_shape=jax.ShapeDtypeStruct((n_groups, batch * seq, B_GROUP_WIDTH), BF16),
        scratch_shapes=[pltpu.VMEM((n_groups, tile, HEAD_DIM), F32),
                        pltpu.VMEM((n_groups, tile, HEAD_DIM), F32)],
        compiler_params=_params("parallel", "parallel", "arbitrary"),
        name="dilated_attention",
    )(*(list(qkvs) * 3))


def _shift_rows(x, k, row):
    return jnp.where(row >= k, pltpu.roll(x, k, 0), 0.0)


def _pool_kernel(u_ref, w_ref, scale_ref, o_ref):
    g = pl.program_id(1)
    u = u_ref[...]
    row = lax.broadcasted_iota(jnp.int32, u.shape, 0)
    window = jnp.left_shift(2, g)
    total = u
    step = 1
    while step < max(C_WINDOWS):
        widened = total + _shift_rows(total, step, row)
        total = jnp.where(step < window, widened, total)
        step *= 2
    count = jnp.minimum(row + 1, window).astype(F32)
    pooled = total / count - u
    y = jnp.dot(pooled.astype(BF16), w_ref[0], preferred_element_type=F32)
    o_ref[...] = (y * scale_ref[...]).astype(o_ref.dtype)


def _pool(u, pool_w, pool_scale, batch, seq):
    n_groups = len(C_WINDOWS)
    return pl.pallas_call(
        _pool_kernel,
        grid=(batch, n_groups),
        in_specs=[pl.BlockSpec((seq, C_GROUP_DIM), lambda b, g: (b, g)),
                  pl.BlockSpec((1, C_GROUP_DIM, C_GROUP_DIM), lambda b, g: (g, 0, 0)),
                  pl.BlockSpec((1, C_GROUP_DIM), lambda b, g: (0, g))],
        out_specs=pl.BlockSpec((seq, C_GROUP_DIM), lambda b, g: (b, g)),
        out_shape=jax.ShapeDtypeStruct((batch * seq, C_WIDTH), BF16),
        compiler_params=_params("parallel", "parallel"),
        name="multiscale_pool",
    )(u, pool_w, pool_scale.reshape(1, C_WIDTH))


def _conv_kernel(gb_ref, gc_ref, h_ref, w_ref, o_ref):
    z = gc_ref[...] * h_ref[...]
    row = lax.broadcasted_iota(jnp.int32, z.shape, 0)
    w = w_ref[...]
    y = w[CONV_WIDTH - 1:CONV_WIDTH] * z
    for tap in range(1, CONV_WIDTH):
        y = y + w[CONV_WIDTH - 1 - tap:CONV_WIDTH - tap] * _shift_rows(z, tap, row)
    o_ref[...] = (gb_ref[...] * y).astype(o_ref.dtype)


def _short_conv(gates, conv_w, batch, seq):
    n_col = D_WIDTH // LANES

    def spec(part):
        return pl.BlockSpec((seq, LANES), lambda b, c: (b, part * n_col + c))

    return pl.pallas_call(
        _conv_kernel,
        grid=(batch, n_col),
        in_specs=[spec(0), spec(1), spec(2), pl.BlockSpec((CONV_WIDTH, LANES), lambda b, c: (0, c))],
        out_specs=pl.BlockSpec((seq, LANES), lambda b, c: (b, c)),
        out_shape=jax.ShapeDtypeStruct((batch * seq, D_WIDTH), BF16),
        compiler_params=_params("parallel", "parallel"),
        name="short_conv",
    )(gates, gates, gates, conv_w)


def _rope_tables(seq, dim):
    half = dim // 2
    inv = ROPE_THETA ** (-jnp.arange(0, dim, 2, dtype=F32) / dim)
    ang = jnp.arange(seq, dtype=F32)[:, None] * inv[None, :]
    cos, sin = jnp.cos(ang), jnp.sin(ang)
    reps = LANES // dim
    cos_t = jnp.tile(jnp.concatenate([cos, cos], axis=1), (1, reps))
    zeros = jnp.zeros_like(sin)
    if dim == LANES:
        return (cos_t, jnp.concatenate([-sin, sin], axis=1)), (half,)
    upper = jnp.tile(jnp.concatenate([zeros, sin], axis=1), (1, reps))
    lower = jnp.tile(jnp.concatenate([-sin, zeros], axis=1), (1, reps))
    return (cos_t, upper, lower), (half, LANES - half)


def kernel(x, w_in, w_out, norm_mix, norm_mlp, diff_lambda, diff_subln, pool_w, pool_scale, conv_w,
           w_up, w_down, norm_final):
    batch, seq, d_model = x.shape
    depth = w_in.shape[0]
    m = batch * seq
    x = x.reshape(m, d_model)

    tables_a, shifts_a = _rope_tables(seq, A_SUB)
    tables_b, shifts_b = _rope_tables(seq, HEAD_DIM)

    a_qk_end = 2 * A_WIDTH
    a_end = 3 * A_WIDTH
    b_end = a_end + 3 * B_WIDTH
    c_end = b_end + C_WIDTH
    t_a = 512

    for l in range(depth):
        wl = w_in[l]
        w_a = wl[:, :a_qk_end].astype(BF16)
        w_a_vt = wl[:, a_qk_end:a_end].T.astype(BF16)
        w_b = [jnp.concatenate([wl[:, a_end + part * B_WIDTH + g * B_GROUP_WIDTH:
                                   a_end + part * B_WIDTH + (g + 1) * B_GROUP_WIDTH] for part in range(3)],
                               axis=1).astype(BF16) for g in range(len(B_GROUPS))]
        w_c = wl[:, b_end:c_end].astype(BF16)
        w_d = wl[:, c_end:].astype(BF16)
        lam_init = 0.8 - 0.6 * math.exp(-0.3 * l)

        h = _rmsnorm(x, norm_mix[l], BF16)
        qk_a = _mm_rope(h, w_a, tables_a, shifts_a, A_SCORE_SCALE, A_WIDTH, seq)
        vt_a = _mm_transposed(h, w_a_vt, t_a)
        qkv_b = [_mm_rope_permute(h, w_b[g], tables_b, shifts_b[0], 2 * B_GROUP_WIDTH, B_GROUPS[g][1], batch, seq)
                 for g in range(len(B_GROUPS))]
        u = _mm(h, w_c, F32, C_WIDTH, name="in_proj_pool")
        gates = _mm(h, w_d, F32, D_WIDTH, name="in_proj_conv")

        out_a = _attn_a(qk_a, vt_a, diff_lambda[l], diff_subln[l], lam_init, batch, seq, t_a)
        out_b = _attn_b(qkv_b, batch, seq)
        out_c = _pool(u, pool_w[l].astype(BF16), pool_scale[l], batch, seq)
        out_d = _short_conv(gates, conv_w[l], batch, seq)

        x = _out_proj(out_a, out_b, out_c, out_d, w_out, l, x)

        h = _rmsnorm(x, norm_mlp[l], BF16)
        act = _mm_layer(h, w_up, l, BF16, 512, relu_sq=True, name="mlp_up")
        x = _mm_residual(act, w_down, l, x)

    out = _rmsnorm(x, norm_final, F32)
    return out.reshape(batch, seq, d_model)
```

```python
import functools
import math

import jax
import jax.numpy as jnp
from jax import lax
from jax.experimental import pallas as pl
from jax.experimental.pallas import tpu as pltpu

D_MODEL = 4096
HEAD_DIM = 128
A_HEADS = 8
A_WIDTH = A_HEADS * HEAD_DIM
A_SUB = HEAD_DIM // 2
A_SCORE_SCALE = A_SUB ** -0.5 * math.log2(math.e)
A_ONES_ROWS = 16
B_GROUPS = ((128, 1), (512, 4), (2048, 16))
B_HEADS_PER_GROUP = 3
B_GROUP_WIDTH = B_HEADS_PER_GROUP * HEAD_DIM
B_WIDTH = B_GROUP_WIDTH * len(B_GROUPS)
B_TOKEN_TILE = 2048
B_QUERY_CHUNK = 256
C_WINDOWS = (2, 4, 8, 16)
C_WIDTH = 1024
C_GROUP_DIM = C_WIDTH // len(C_WINDOWS)
D_WIDTH = D_MODEL - A_WIDTH - B_WIDTH - C_WIDTH
CONV_WIDTH = 3
D_FF = 4 * D_MODEL
ROPE_THETA = 10000.0
NORM_EPS = 1e-6
DIFF_EPS = 1e-5

LANES = 128
MXU_COLS = 256
MLP_ROW_TILE = 2048
VMEM_LIMIT_BYTES = 56 * 1024 * 1024
MASKED_SCORE = -1e30

F32 = jnp.float32
BF16 = jnp.bfloat16


def _params(*semantics):
    return pltpu.CompilerParams(dimension_semantics=semantics, vmem_limit_bytes=VMEM_LIMIT_BYTES)


def _rmsnorm_kernel(x_ref, g_ref, o_ref, *, eps):
    x = x_ref[...]
    ms = jnp.mean(x * x, axis=-1, keepdims=True)
    o_ref[...] = (x * lax.rsqrt(ms + eps) * g_ref[...]).astype(o_ref.dtype)


def _rmsnorm(x, g, out_dtype, tm=256):
    m, d = x.shape
    return pl.pallas_call(
        functools.partial(_rmsnorm_kernel, eps=NORM_EPS),
        grid=(m // tm,),
        in_specs=[pl.BlockSpec((tm, d), lambda i: (i, 0)),
                  pl.BlockSpec((1, d), lambda i: (0, 0))],
        out_specs=pl.BlockSpec((tm, d), lambda i: (i, 0)),
        out_shape=jax.ShapeDtypeStruct((m, d), out_dtype),
        compiler_params=_params("parallel"),
        name="rmsnorm",
    )(x, g.reshape(1, d))


def _rotate(y, cos_ref, sin_refs, shifts):
    outs = []
    for c in range(y.shape[1] // LANES):
        yc = y[:, c * LANES:(c + 1) * LANES]
        oc = yc * cos_ref[...]
        for s_ref, shift in zip(sin_refs, shifts):
            oc = oc + pltpu.roll(yc, shift, 1) * s_ref[...]
        outs.append(oc)
    return jnp.concatenate(outs, axis=1)


def _mm_rope_kernel(a_ref, w_ref, cos_ref, *rest, shifts, q_scale):
    sin_refs, o_ref = rest[:-1], rest[-1]
    scale = jnp.where(pl.program_id(1) == 0, q_scale, 1.0).astype(F32)
    a = a_ref[...]
    for c in range(o_ref.shape[1] // MXU_COLS):
        cols = slice(c * MXU_COLS, (c + 1) * MXU_COLS)
        acc = jnp.dot(a, w_ref[:, cols], preferred_element_type=F32)
        o_ref[:, cols] = (_rotate(acc, cos_ref, sin_refs, shifts) * scale).astype(o_ref.dtype)


def _mm_rope(a, w, tables, shifts, q_scale, bn, seq, bm=1024):
    m, k = a.shape
    n = w.shape[1]
    tab_blocks = seq // bm
    tab_spec = pl.BlockSpec((bm, LANES), lambda i, j: (i % tab_blocks, 0))
    return pl.pallas_call(
        functools.partial(_mm_rope_kernel, shifts=shifts, q_scale=q_scale),
        grid=(m // bm, n // bn),
        in_specs=[pl.BlockSpec((bm, k), lambda i, j: (i, 0)),
                  pl.BlockSpec((k, bn), lambda i, j: (0, j))] + [tab_spec] * len(tables),
        out_specs=pl.BlockSpec((bm, bn), lambda i, j: (i, j)),
        out_shape=jax.ShapeDtypeStruct((m, n), BF16),
        compiler_params=_params("parallel", "arbitrary"),
        name="in_proj_rope",
    )(a, w, *tables)


def _mm_rope_permute_kernel(a_ref, w_ref, cos_ref, sin_ref, o_ref, scr_ref, *, shift, n_rope_cols, dilation):
    a = a_ref[...]
    n = o_ref.shape[2]
    rows = a.shape[0] // dilation
    for lo in range(0, n, 2 * MXU_COLS):
        hi = min(lo + 2 * MXU_COLS, n)
        acc = jnp.dot(a, w_ref[:, lo:hi], preferred_element_type=F32)
        rope_hi = min(hi, n_rope_cols)
        if lo < rope_hi:
            rot = _rotate(acc[:, :rope_hi - lo], cos_ref, (sin_ref,), (shift,))
            acc = rot if rope_hi == hi else jnp.concatenate([rot, acc[:, rope_hi - lo:]], axis=1)
        if dilation == 1:
            o_ref[0, :, lo:hi] = acc.astype(o_ref.dtype)
        else:
            for c in range(lo // LANES, hi // LANES):
                cols = slice(c * LANES, (c + 1) * LANES)
                scr_ref[c] = acc[:, c * LANES - lo:(c + 1) * LANES - lo]
                for r in range(dilation):
                    o_ref[r, :, cols] = scr_ref[c, pl.ds(r, rows, stride=dilation), :].astype(o_ref.dtype)


def _mm_rope_permute(a, w, tables, shift, n_rope_cols, dilation, batch, seq, bm=1024):
    m, k = a.shape
    n = w.shape[1]
    tiles = seq // bm
    tab_spec = pl.BlockSpec((bm, LANES), lambda i: (i % tiles, 0))
    return pl.pallas_call(
        functools.partial(_mm_rope_permute_kernel, shift=shift, n_rope_cols=n_rope_cols, dilation=dilation),
        grid=(m // bm,),
        in_specs=[pl.BlockSpec((bm, k), lambda i: (i, 0)),
                  pl.BlockSpec((k, n), lambda i: (0, 0), pipeline_mode=pl.Buffered(1)),
                  tab_spec, tab_spec],
        out_specs=pl.BlockSpec((None, dilation, bm // dilation, n), lambda i: (i // tiles, 0, i % tiles, 0)),
        out_shape=jax.ShapeDtypeStruct((batch, dilation, seq // dilation, n), BF16),
        scratch_shapes=[pltpu.VMEM((n // LANES, bm, LANES), F32)],
        compiler_params=_params("parallel"),
        name="in_proj_dilated",
    )(a, w, *tables)


def _mm_kernel(a_ref, w_ref, o_ref, *, relu_sq):
    acc = jnp.dot(a_ref[...], w_ref[...].astype(BF16), preferred_element_type=F32)
    if relu_sq:
        acc = jnp.square(jnp.maximum(acc, 0.0))
    o_ref[...] = acc.astype(o_ref.dtype)


def _mm(a, w, out_dtype, bn, relu_sq=False, bm=1024, name="matmul"):
    m, k = a.shape
    n = w.shape[1]
    return pl.pallas_call(
        functools.partial(_mm_kernel, relu_sq=relu_sq),
        grid=(m // bm, n // bn),
        in_specs=[pl.BlockSpec((bm, k), lambda i, j: (i, 0)),
                  pl.BlockSpec((k, bn), lambda i, j: (0, j))],
        out_specs=pl.BlockSpec((bm, bn), lambda i, j: (i, j)),
        out_shape=jax.ShapeDtypeStruct((m, n), out_dtype),
        compiler_params=_params("parallel", "arbitrary"),
        name=name,
    )(a, w)


def _mm_layer_tiled(a, w, layer, out_dtype, bn, relu_sq=False, bm=MLP_ROW_TILE, name="matmul"):
    m, k = a.shape
    n = w.shape[2]
    return pl.pallas_call(
        functools.partial(_mm_kernel, relu_sq=relu_sq),
        grid=(m // bm, n // bn),
        in_specs=[pl.BlockSpec((bm, k), lambda i, j: (i, 0), pipeline_mode=pl.Buffered(1)),
                  pl.BlockSpec((None, k, bn), lambda i, j: (layer, 0, j))],
        out_specs=pl.BlockSpec((None, None, bm, bn), lambda i, j: (i, j, 0, 0)),
        out_shape=jax.ShapeDtypeStruct((m // bm, n // bn, bm, bn), out_dtype),
        compiler_params=_params("parallel", "arbitrary"),
        name=name,
    )(a, w)


def _mm_transposed_kernel(wt_ref, a_ref, o_ref, *, t):
    acc = lax.dot_general(wt_ref[...], a_ref[...], (((1,), (1,)), ((), ())), preferred_element_type=F32)
    for c in range(acc.shape[1] // t):
        o_ref[c] = acc[:, c * t:(c + 1) * t].astype(o_ref.dtype)


def _mm_transposed(a, wt, t, bm=1024):
    m, k = a.shape
    n = wt.shape[0]
    return pl.pallas_call(
        functools.partial(_mm_transposed_kernel, t=t),
        grid=(m // bm,),
        in_specs=[pl.BlockSpec((n, k), lambda i: (0, 0)),
                  pl.BlockSpec((bm, k), lambda i: (i, 0))],
        out_specs=pl.BlockSpec((bm // t, n, t), lambda i: (i, 0, 0)),
        out_shape=jax.ShapeDtypeStruct((m // t, n, t), BF16),
        compiler_params=_params("parallel"),
        name="in_proj_vt",
    )(wt, a)


def _mm_residual_kernel(a_ref, w_ref, r_ref, o_ref):
    kk = pl.program_id(2)

    @pl.when(kk == 0)
    def _():
        o_ref[...] = r_ref[...]

    a = jnp.concatenate([a_ref[c] for c in range(a_ref.shape[0])], axis=1)
    o_ref[...] += jnp.dot(a, w_ref[...].astype(BF16), preferred_element_type=F32)


def _mm_residual(a, w, layer, r, bn=1024, bk=1024):
    row_tiles, col_tiles, bm, tile_cols = a.shape
    m, k = row_tiles * bm, col_tiles * tile_cols
    n = w.shape[2]
    return pl.pallas_call(
        _mm_residual_kernel,
        grid=(m // bm, n // bn, k // bk),
        in_specs=[pl.BlockSpec((None, bk // tile_cols, bm, tile_cols), lambda i, j, kk: (i, kk, 0, 0)),
                  pl.BlockSpec((None, bk, bn), lambda i, j, kk: (layer, kk, j)),
                  pl.BlockSpec((bm, bn), lambda i, j, kk: (i, j), pipeline_mode=pl.Buffered(1))],
        out_specs=pl.BlockSpec((bm, bn), lambda i, j, kk: (i, j)),
        out_shape=jax.ShapeDtypeStruct((m, n), F32),
        compiler_params=_params("parallel", "parallel", "arbitrary"),
        name="mlp_down",
    )(a, w, r)


def _out_proj_kernel(a_ref, b_ref, c_ref, d_ref, w_ref, r_ref, o_ref):
    mix = jnp.concatenate([a_ref[...]] + [b_ref[g] for g in range(len(B_GROUPS))] + [c_ref[...], d_ref[...]], axis=1)
    o_ref[...] = r_ref[...] + jnp.dot(mix, w_ref[...].astype(BF16), preferred_element_type=F32)


def _out_proj(out_a, out_b, out_c, out_d, w, layer, r, bm=1024, bn=512):
    m = out_a.shape[0]
    n = w.shape[2]
    n_groups = len(B_GROUPS)
    return pl.pallas_call(
        _out_proj_kernel,
        grid=(m // bm, n // bn),
        in_specs=[pl.BlockSpec((bm, A_WIDTH), lambda i, j: (i, 0)),
                  pl.BlockSpec((n_groups, bm, B_GROUP_WIDTH), lambda i, j: (0, i, 0)),
                  pl.BlockSpec((bm, C_WIDTH), lambda i, j: (i, 0)),
                  pl.BlockSpec((bm, D_WIDTH), lambda i, j: (i, 0)),
                  pl.BlockSpec((None, D_MODEL, bn), lambda i, j: (layer, 0, j)),
                  pl.BlockSpec((bm, bn), lambda i, j: (i, j))],
        out_specs=pl.BlockSpec((bm, bn), lambda i, j: (i, j)),
        out_shape=jax.ShapeDtypeStruct((m, n), F32),
        compiler_params=_params("parallel", "arbitrary"),
        name="out_proj",
    )(out_a, out_b, out_c, out_d, w, r)


def _online_softmax_step(s, v, m, l, acc):
    m_new = jnp.maximum(m, jnp.max(s, axis=-1, keepdims=True))
    alpha = jnp.exp(m - m_new)
    p = jnp.exp(s - m_new)
    l_new = alpha * l + jnp.sum(p, axis=-1, keepdims=True)
    acc_new = alpha * acc + jnp.dot(p.astype(BF16), v, preferred_element_type=F32)
    return m_new, l_new, acc_new


def _attn_a_kernel(q_ref, k_ref, vt_ref, lam_ref, g_ref, o_ref, *scratch, t, heads, lam_init):
    qi = pl.program_id(2)
    lane = lax.broadcasted_iota(jnp.int32, (t, HEAD_DIM), 1)
    qqs = []
    for h in range(heads):
        q = q_ref[:, h * HEAD_DIM:(h + 1) * HEAD_DIM]
        zero = jnp.zeros_like(q)
        qqs.append(jnp.concatenate([jnp.where(lane < A_SUB, q, zero), jnp.where(lane >= A_SUB, q, zero)], axis=0))

    ones = jnp.ones((A_ONES_ROWS, t), BF16)
    m_scr, acc_scr = scratch[:heads], scratch[heads:]
    for h in range(heads):
        m_scr[h][...] = jnp.full(m_scr[h].shape, MASKED_SCORE, F32)
        acc_scr[h][...] = jnp.zeros(acc_scr[h].shape, F32)

    def step(j, masked):
        def scores(h):
            k = k_ref[pl.ds(pl.multiple_of(j * t, t), t), h * HEAD_DIM:(h + 1) * HEAD_DIM]
            s = lax.dot_general(k, qqs[h], (((1,), (1,)), ((), ())), preferred_element_type=F32)
            if masked:
                key = lax.broadcasted_iota(jnp.int32, s.shape, 0)
                lane_q = lax.broadcasted_iota(jnp.int32, s.shape, 1)
                qry = jnp.where(lane_q >= t, lane_q - t, lane_q)
                s = jnp.where(key <= qry, s, MASKED_SCORE)
            return s

        s_next = scores(0)
        for h in range(heads):
            s = s_next
            if h + 1 < heads:
                s_next = scores(h + 1)
            vt = jnp.concatenate([vt_ref[j, h * HEAD_DIM:(h + 1) * HEAD_DIM, :], ones], axis=0)
            m = m_scr[h][...]
            m_new = jnp.maximum(m, jnp.max(s, axis=0, keepdims=True))
            alpha = jnp.exp2(m - m_new)
            p = jnp.exp2(s - m_new)
            m_scr[h][...] = m_new
            acc_scr[h][...] = alpha * acc_scr[h][...] + jnp.dot(vt, p.astype(BF16), preferred_element_type=F32)

    def body(j, carry):
        step(j, masked=False)
        return carry

    lax.fori_loop(0, qi, body, 0)
    step(qi, masked=True)

    lp = lam_ref[...]
    lam = (jnp.exp(jnp.sum(lp[0:1] * lp[1:2], axis=-1, keepdims=True))
           - jnp.exp(jnp.sum(lp[2:3] * lp[3:4], axis=-1, keepdims=True)) + lam_init)
    for h in range(heads):
        acc = acc_scr[h][...]
        o = acc[:HEAD_DIM] / acc[HEAD_DIM:HEAD_DIM + 1]
        o = (o[:, :t] - lam * o[:, t:]).T
        ms = jnp.mean(o * o, axis=-1, keepdims=True)
        o = o * lax.rsqrt(ms + DIFF_EPS) * g_ref[...]
        o_ref[:, h * HEAD_DIM:(h + 1) * HEAD_DIM] = (o * (1.0 - lam_init)).astype(o_ref.dtype)


def _attn_a(qk, vt, lam_params, subln_g, lam_init, batch, seq, t, heads=4):
    nq = seq // t
    width = heads * HEAD_DIM
    n_pairs = A_HEADS // heads
    return pl.pallas_call(
        functools.partial(_attn_a_kernel, t=t, heads=heads, lam_init=lam_init),
        grid=(batch, n_pairs, nq),
        in_specs=[pl.BlockSpec((t, width), lambda b, h, i: (b * nq + i, h)),
                  pl.BlockSpec((seq, width), lambda b, h, i: (b, n_pairs + h)),
                  pl.BlockSpec((nq, width, t), lambda b, h, i: (b, h, 0)),
                  pl.BlockSpec((4, A_SUB), lambda b, h, i: (0, 0)),
                  pl.BlockSpec((1, HEAD_DIM), lambda b, h, i: (0, 0))],
        out_specs=pl.BlockSpec((t, width), lambda b, h, i: (b * nq + i, h)),
        out_shape=jax.ShapeDtypeStruct((batch * seq, A_WIDTH), BF16),
        scratch_shapes=([pltpu.VMEM((1, 2 * t), F32)] * heads
                        + [pltpu.VMEM((HEAD_DIM + A_ONES_ROWS, 2 * t), F32)] * heads),
        compiler_params=_params("parallel", "parallel", "arbitrary"),
        name="diff_attention",
    )(qk, qk, vt, lam_params, subln_g.reshape(1, HEAD_DIM))


def _attn_b_kernel(*refs, tile):
    n_groups = len(B_GROUPS)
    q_refs = refs[0:n_groups]
    k_refs = refs[n_groups:2 * n_groups]
    v_refs = refs[2 * n_groups:3 * n_groups]
    o_ref = refs[3 * n_groups]
    out_scr, lse_scr = refs[3 * n_groups + 1:]
    ti = pl.program_id(2)
    scale = HEAD_DIM ** -0.5
    base_delta = {}
    chunks = []

    for g, (window, dilation) in enumerate(B_GROUPS):
        back = window // dilation
        seg = tile // dilation
        tq = min(B_QUERY_CHUNK, seg)
        tk = tq + back
        if tq not in base_delta:
            base_delta[tq] = (lax.broadcasted_iota(jnp.int32, (tq, tk), 0)
                              - lax.broadcasted_iota(jnp.int32, (tq, tk), 1))
        for r in range(dilation):
            for c in range(seg // tq):
                chunks.append((g, dilation, back, tq, tk, r, c * tq, ti * seg + c * tq))

    def scores(chunk):
        g, dilation, back, tq, tk, r, i_loc, i0 = chunk
        ks = pl.multiple_of(jnp.maximum(i0 - back, 0), back)
        q = q_refs[g][r, i_loc:i_loc + tq, :]
        k = k_refs[g][r, pl.ds(ks, tk), :]
        s = lax.dot_general(q, k, (((1,), (1,)), ((), ())), preferred_element_type=F32) * scale
        delta = base_delta[tq] + (i0 - ks)
        return jnp.where((delta >= 0) & (delta <= back), s, MASKED_SCORE), ks

    nxt = scores(chunks[0])
    for idx, (g, dilation, back, tq, tk, r, i_loc, i0) in enumerate(chunks):
        s, ks = nxt
        if idx + 1 < len(chunks):
            nxt = scores(chunks[idx + 1])
        v = v_refs[g][r, pl.ds(ks, tk), :]
        m = jnp.max(s, axis=-1, keepdims=True)
        p = jnp.exp(s - m)
        l = jnp.sum(p, axis=-1, keepdims=True)
        o = jnp.dot(p.astype(BF16), v, preferred_element_type=F32) / l
        lse = jnp.broadcast_to(m + jnp.log(l), o.shape)
        rows = pl.ds(dilation * i_loc + r, tq, stride=dilation) if dilation > 1 else pl.ds(i_loc, tq)
        out_scr[g, rows, :] = o
        lse_scr[g, rows, :] = lse

    for c in range(tile // B_QUERY_CHUNK):
        rows = pl.ds(c * B_QUERY_CHUNK, B_QUERY_CHUNK)
        lses = [lse_scr[g, rows, :] for g in range(n_groups)]
        lse_max = functools.reduce(jnp.maximum, lses)
        weights = [jnp.exp(x - lse_max) for x in lses]
        denom = functools.reduce(lambda a, b: a + b, weights)
        for g in range(n_groups):
            o_ref[g, rows, :] = (out_scr[g, rows, :] * (weights[g] / denom)).astype(o_ref.dtype)


def _attn_b(qkvs, batch, seq, tile=B_TOKEN_TILE):
    n_tiles = seq // tile
    n_groups = len(B_GROUPS)

    def q_spec(g):
        d = B_GROUPS[g][1]
        return pl.BlockSpec((None, d, tile // d, HEAD_DIM), lambda b, h, i: (b, 0, i, h))

    def kv_spec(g, part):
        d = B_GROUPS[g][1]
        return pl.BlockSpec((None, d, seq // d, HEAD_DIM),
                            lambda b, h, i: (b, 0, 0, part * B_HEADS_PER_GROUP + h))

    in_specs = ([q_spec(g) for g in range(n_groups)] + [kv_spec(g, 1) for g in range(n_groups)]
                + [kv_spec(g, 2) for g in range(n_groups)])
    return pl.pallas_call(
        functools.partial(_attn_b_kernel, tile=tile),
        grid=(batch, B_HEADS_PER_GROUP, n_tiles),
        in_specs=in_specs,
        out_specs=pl.BlockSpec((n_groups, tile, HEAD_DIM), lambda b, h, i: (0, b * n_tiles + i, h)),
        out_shape=jax.ShapeDtypeStruct((n_groups, batch * seq, B_GROUP_WIDTH), BF16),
        scratch_shapes=[pltpu.VMEM((n_groups, tile, HEAD_DIM), F32),
                        pltpu.VMEM((n_groups, tile, HEAD_DIM), F32)],
        compiler_params=_params("parallel", "parallel", "arbitrary"),
        name="dilated_attention",
    )(*(list(qkvs) * 3))


def _shift_rows(x, k, row):
    return jnp.where(row >= k, pltpu.roll(x, k, 0), 0.0)


def _pool_kernel(u_ref, w_ref, scale_ref, o_ref):
    g = pl.program_id(1)
    u = u_ref[...]
    row = lax.broadcasted_iota(jnp.int32, u.shape, 0)
    window = jnp.left_shift(2, g)
    total = u
    step = 1
    while step < max(C_WINDOWS):
        widened = total + _shift_rows(total, step, row)
        total = jnp.where(step < window, widened, total)
        step *= 2
    count = jnp.minimum(row + 1, window).astype(F32)
    pooled = total / count - u
    y = jnp.dot(pooled.astype(BF16), w_ref[0], preferred_element_type=F32)
    o_ref[...] = (y * scale_ref[...]).astype(o_ref.dtype)


def _pool(u, pool_w, pool_scale, batch, seq):
    n_groups = len(C_WINDOWS)
    return pl.pallas_call(
        _pool_kernel,
        grid=(batch, n_groups),
        in_specs=[pl.BlockSpec((seq, C_GROUP_DIM), lambda b, g: (b, g)),
                  pl.BlockSpec((1, C_GROUP_DIM, C_GROUP_DIM), lambda b, g: (g, 0, 0)),
                  pl.BlockSpec((1, C_GROUP_DIM), lambda b, g: (0, g))],
        out_specs=pl.BlockSpec((seq, C_GROUP_DIM), lambda b, g: (b, g)),
        out_shape=jax.ShapeDtypeStruct((batch * seq, C_WIDTH), BF16),
        compiler_params=_params("parallel", "parallel"),
        name="multiscale_pool",
    )(u, pool_w, pool_scale.reshape(1, C_WIDTH))


def _conv_kernel(gb_ref, gc_ref, h_ref, w_ref, o_ref):
    z = gc_ref[...] * h_ref[...]
    row = lax.broadcasted_iota(jnp.int32, z.shape, 0)
    w = w_ref[...]
    y = w[CONV_WIDTH - 1:CONV_WIDTH] * z
    for tap in range(1, CONV_WIDTH):
        y = y + w[CONV_WIDTH - 1 - tap:CONV_WIDTH - tap] * _shift_rows(z, tap, row)
    o_ref[...] = (gb_ref[...] * y).astype(o_ref.dtype)


def _short_conv(gates, conv_w, batch, seq):
    n_col = D_WIDTH // LANES

    def spec(part):
        return pl.BlockSpec((seq, LANES), lambda b, c: (b, part * n_col + c))

    return pl.pallas_call(
        _conv_kernel,
        grid=(batch, n_col),
        in_specs=[spec(0), spec(1), spec(2), pl.BlockSpec((CONV_WIDTH, LANES), lambda b, c: (0, c))],
        out_specs=pl.BlockSpec((seq, LANES), lambda b, c: (b, c)),
        out_shape=jax.ShapeDtypeStruct((batch * seq, D_WIDTH), BF16),
        compiler_params=_params("parallel", "parallel"),
        name="short_conv",
    )(gates, gates, gates, conv_w)


def _rope_tables(seq, dim):
    half = dim // 2
    inv = ROPE_THETA ** (-jnp.arange(0, dim, 2, dtype=F32) / dim)
    ang = jnp.arange(seq, dtype=F32)[:, None] * inv[None, :]
    cos, sin = jnp.cos(ang), jnp.sin(ang)
    reps = LANES // dim
    cos_t = jnp.tile(jnp.concatenate([cos, cos], axis=1), (1, reps))
    zeros = jnp.zeros_like(sin)
    if dim == LANES:
        return (cos_t, jnp.concatenate([-sin, sin], axis=1)), (half,)
    upper = jnp.tile(jnp.concatenate([zeros, sin], axis=1), (1, reps))
    lower = jnp.tile(jnp.concatenate([-sin, zeros], axis=1), (1, reps))
    return (cos_t, upper, lower), (half, LANES - half)


def kernel(x, w_in, w_out, norm_mix, norm_mlp, diff_lambda, diff_subln, pool_w, pool_scale, conv_w,
           w_up, w_down, norm_final):
    batch, seq, d_model = x.shape
    depth = w_in.shape[0]
    m = batch * seq
    x = x.reshape(m, d_model)

    tables_a, shifts_a = _rope_tables(seq, A_SUB)
    tables_b, shifts_b = _rope_tables(seq, HEAD_DIM)

    a_qk_end = 2 * A_WIDTH
    a_end = 3 * A_WIDTH
    b_end = a_end + 3 * B_WIDTH
    c_end = b_end + C_WIDTH
    t_a = 512

    for l in range(depth):
        wl = w_in[l]
        w_a = wl[:, :a_qk_end].astype(BF16)
        w_a_vt = wl[:, a_qk_end:a_end].T.astype(BF16)
        w_b = [jnp.concatenate([wl[:, a_end + part * B_WIDTH + g * B_GROUP_WIDTH:
                                   a_end + part * B_WIDTH + (g + 1) * B_GROUP_WIDTH] for part in range(3)],
                               axis=1).astype(BF16) for g in range(len(B_GROUPS))]
        w_c = wl[:, b_end:c_end].astype(BF16)
        w_d = wl[:, c_end:].astype(BF16)
        lam_init = 0.8 - 0.6 * math.exp(-0.3 * l)

        h = _rmsnorm(x, norm_mix[l], BF16)
        qk_a = _mm_rope(h, w_a, tables_a, shifts_a, A_SCORE_SCALE, A_WIDTH, seq)
        vt_a = _mm_transposed(h, w_a_vt, t_a)
        qkv_b = [_mm_rope_permute(h, w_b[g], tables_b, shifts_b[0], 2 * B_GROUP_WIDTH, B_GROUPS[g][1], batch, seq)
                 for g in range(len(B_GROUPS))]
        u = _mm(h, w_c, F32, C_WIDTH, name="in_proj_pool")
        gates = _mm(h, w_d, F32, D_WIDTH, name="in_proj_conv")

        out_a = _attn_a(qk_a, vt_a, diff_lambda[l], diff_subln[l], lam_init, batch, seq, t_a)
        out_b = _attn_b(qkv_b, batch, seq)
        out_c = _pool(u, pool_w[l].astype(BF16), pool_scale[l], batch, seq)
        out_d = _short_conv(gates, conv_w[l], batch, seq)

        x = _out_proj(out_a, out_b, out_c, out_d, w_out, l, x)

        h = _rmsnorm(x, norm_mlp[l], BF16)
        act = _mm_layer_tiled(h, w_up, l, BF16, 512, relu_sq=True, name="mlp_up")
        x = _mm_residual(act, w_down, l, x)

    out = _rmsnorm(x, norm_final, F32)
    return out.reshape(batch, seq, d_model)
```

```python
import functools
import math

import jax
import jax.numpy as jnp
from jax import lax
from jax.experimental import pallas as pl
from jax.experimental.pallas import tpu as pltpu

D_MODEL = 4096
HEAD_DIM = 128
A_HEADS = 8
A_WIDTH = A_HEADS * HEAD_DIM
A_SUB = HEAD_DIM // 2
A_SCORE_SCALE = A_SUB ** -0.5 * math.log2(math.e)
A_ONES_ROWS = 16
B_GROUPS = ((128, 1), (512, 4), (2048, 16))
B_HEADS_PER_GROUP = 3
B_GROUP_WIDTH = B_HEADS_PER_GROUP * HEAD_DIM
B_WIDTH = B_GROUP_WIDTH * len(B_GROUPS)
B_TOKEN_TILE = 2048
B_QUERY_CHUNK = 256
C_WINDOWS = (2, 4, 8, 16)
C_WIDTH = 1024
C_GROUP_DIM = C_WIDTH // len(C_WINDOWS)
D_WIDTH = D_MODEL - A_WIDTH - B_WIDTH - C_WIDTH
CONV_WIDTH = 3
D_FF = 4 * D_MODEL
ROPE_THETA = 10000.0
NORM_EPS = 1e-6
DIFF_EPS = 1e-5

LANES = 128
MXU_COLS = 256
MLP_ROW_TILE = 2048
IN_PROJ_COL_TILE = 512
VMEM_LIMIT_BYTES = 56 * 1024 * 1024
MASKED_SCORE = -1e30

F32 = jnp.float32
BF16 = jnp.bfloat16


def _params(*semantics):
    return pltpu.CompilerParams(dimension_semantics=semantics, vmem_limit_bytes=VMEM_LIMIT_BYTES)


def _rmsnorm_kernel(x_ref, g_ref, o_ref, *, eps):
    x = x_ref[...]
    ms = jnp.mean(x * x, axis=-1, keepdims=True)
    o_ref[...] = (x * lax.rsqrt(ms + eps) * g_ref[...]).astype(o_ref.dtype)


def _rmsnorm(x, g, out_dtype, tm=256):
    m, d = x.shape
    return pl.pallas_call(
        functools.partial(_rmsnorm_kernel, eps=NORM_EPS),
        grid=(m // tm,),
        in_specs=[pl.BlockSpec((tm, d), lambda i: (i, 0)),
                  pl.BlockSpec((1, d), lambda i: (0, 0))],
        out_specs=pl.BlockSpec((tm, d), lambda i: (i, 0)),
        out_shape=jax.ShapeDtypeStruct((m, d), out_dtype),
        compiler_params=_params("parallel"),
        name="rmsnorm",
    )(x, g.reshape(1, d))


def _rotate(y, cos_ref, sin_refs, shifts):
    outs = []
    for c in range(y.shape[1] // LANES):
        yc = y[:, c * LANES:(c + 1) * LANES]
        oc = yc * cos_ref[...]
        for s_ref, shift in zip(sin_refs, shifts):
            oc = oc + pltpu.roll(yc, shift, 1) * s_ref[...]
        outs.append(oc)
    return jnp.concatenate(outs, axis=1)


def _mm_rope_kernel(a_ref, w_ref, cos_ref, *rest, shifts, q_scale, n_q_blocks):
    sin_refs, o_ref = rest[:-1], rest[-1]
    scale = jnp.where(pl.program_id(1) < n_q_blocks, q_scale, 1.0).astype(F32)
    a = a_ref[...]
    for c in range(o_ref.shape[1] // MXU_COLS):
        cols = slice(c * MXU_COLS, (c + 1) * MXU_COLS)
        acc = jnp.dot(a, w_ref[:, cols].astype(BF16), preferred_element_type=F32)
        o_ref[:, cols] = (_rotate(acc, cos_ref, sin_refs, shifts) * scale).astype(o_ref.dtype)


def _mm_rope(a, w, layer, n, tables, shifts, q_scale, n_q_cols, seq, bn=512, bm=1024):
    m, k = a.shape
    tab_blocks = seq // bm
    tab_spec = pl.BlockSpec((bm, LANES), lambda i, j: (i % tab_blocks, 0))
    return pl.pallas_call(
        functools.partial(_mm_rope_kernel, shifts=shifts, q_scale=q_scale, n_q_blocks=n_q_cols // bn),
        grid=(m // bm, n // bn),
        in_specs=[pl.BlockSpec((bm, k), lambda i, j: (i, 0)),
                  pl.BlockSpec((None, k, bn), lambda i, j: (layer, 0, j))] + [tab_spec] * len(tables),
        out_specs=pl.BlockSpec((bm, bn), lambda i, j: (i, j)),
        out_shape=jax.ShapeDtypeStruct((m, n), BF16),
        compiler_params=_params("parallel", "arbitrary"),
        name="in_proj_rope",
    )(a, w, *tables)


def _mm_rope_permute_kernel(a_ref, w_ref, cos_ref, sin_ref, o_ref, scr_ref, *, shift, n_rope_cols, dilation):
    a = a_ref[...]
    n = o_ref.shape[2]
    rows = a.shape[0] // dilation
    for lo in range(0, n, 2 * MXU_COLS):
        hi = min(lo + 2 * MXU_COLS, n)
        acc = jnp.dot(a, w_ref[:, lo:hi], preferred_element_type=F32)
        rope_hi = min(hi, n_rope_cols)
        if lo < rope_hi:
            rot = _rotate(acc[:, :rope_hi - lo], cos_ref, (sin_ref,), (shift,))
            acc = rot if rope_hi == hi else jnp.concatenate([rot, acc[:, rope_hi - lo:]], axis=1)
        if dilation == 1:
            o_ref[0, :, lo:hi] = acc.astype(o_ref.dtype)
        else:
            for c in range(lo // LANES, hi // LANES):
                cols = slice(c * LANES, (c + 1) * LANES)
                scr_ref[c] = acc[:, c * LANES - lo:(c + 1) * LANES - lo]
                for r in range(dilation):
                    o_ref[r, :, cols] = scr_ref[c, pl.ds(r, rows, stride=dilation), :].astype(o_ref.dtype)


def _mm_rope_permute(a, w, tables, shift, n_rope_cols, dilation, batch, seq, bm=1024):
    m, k = a.shape
    n = w.shape[1]
    tiles = seq // bm
    tab_spec = pl.BlockSpec((bm, LANES), lambda i: (i % tiles, 0))
    return pl.pallas_call(
        functools.partial(_mm_rope_permute_kernel, shift=shift, n_rope_cols=n_rope_cols, dilation=dilation),
        grid=(m // bm,),
        in_specs=[pl.BlockSpec((bm, k), lambda i: (i, 0)),
                  pl.BlockSpec((k, n), lambda i: (0, 0), pipeline_mode=pl.Buffered(1)),
                  tab_spec, tab_spec],
        out_specs=pl.BlockSpec((None, dilation, bm // dilation, n), lambda i: (i // tiles, 0, i % tiles, 0)),
        out_shape=jax.ShapeDtypeStruct((batch, dilation, seq // dilation, n), BF16),
        scratch_shapes=[pltpu.VMEM((n // LANES, bm, LANES), F32)],
        compiler_params=_params("parallel"),
        name="in_proj_dilated",
    )(a, w, *tables)


def _mm_kernel(a_ref, w_ref, o_ref, *, relu_sq):
    acc = jnp.dot(a_ref[...], w_ref[...].astype(BF16), preferred_element_type=F32)
    if relu_sq:
        acc = jnp.square(jnp.maximum(acc, 0.0))
    o_ref[...] = acc.astype(o_ref.dtype)


def _mm_cols(a, w, layer, col0, n, out_dtype, bn=512, bm=1024, name="matmul"):
    m, k = a.shape
    first = col0 // bn
    assert first * bn == col0 and n % bn == 0
    return pl.pallas_call(
        functools.partial(_mm_kernel, relu_sq=False),
        grid=(m // bm, n // bn),
        in_specs=[pl.BlockSpec((bm, k), lambda i, j: (i, 0)),
                  pl.BlockSpec((None, k, bn), lambda i, j: (layer, 0, first + j))],
        out_specs=pl.BlockSpec((bm, bn), lambda i, j: (i, j)),
        out_shape=jax.ShapeDtypeStruct((m, n), out_dtype),
        compiler_params=_params("parallel", "arbitrary"),
        name=name,
    )(a, w)


def _mm_layer_tiled(a, w, layer, out_dtype, bn, relu_sq=False, bm=MLP_ROW_TILE, name="matmul"):
    m, k = a.shape
    n = w.shape[2]
    return pl.pallas_call(
        functools.partial(_mm_kernel, relu_sq=relu_sq),
        grid=(m // bm, n // bn),
        in_specs=[pl.BlockSpec((bm, k), lambda i, j: (i, 0), pipeline_mode=pl.Buffered(1)),
                  pl.BlockSpec((None, k, bn), lambda i, j: (layer, 0, j))],
        out_specs=pl.BlockSpec((None, None, bm, bn), lambda i, j: (i, j, 0, 0)),
        out_shape=jax.ShapeDtypeStruct((m // bm, n // bn, bm, bn), out_dtype),
        compiler_params=_params("parallel", "arbitrary"),
        name=name,
    )(a, w)


def _mm_transposed_kernel(wt_ref, a_ref, o_ref, *, t):
    acc = lax.dot_general(wt_ref[...], a_ref[...], (((1,), (1,)), ((), ())), preferred_element_type=F32)
    for c in range(acc.shape[1] // t):
        o_ref[c] = acc[:, c * t:(c + 1) * t].astype(o_ref.dtype)


def _mm_transposed(a, wt, t, bm=1024):
    m, k = a.shape
    n = wt.shape[0]
    return pl.pallas_call(
        functools.partial(_mm_transposed_kernel, t=t),
        grid=(m // bm,),
        in_specs=[pl.BlockSpec((n, k), lambda i: (0, 0)),
                  pl.BlockSpec((bm, k), lambda i: (i, 0))],
        out_specs=pl.BlockSpec((bm // t, n, t), lambda i: (i, 0, 0)),
        out_shape=jax.ShapeDtypeStruct((m // t, n, t), BF16),
        compiler_params=_params("parallel"),
        name="in_proj_vt",
    )(wt, a)


def _mm_residual_kernel(a_ref, w_ref, r_ref, o_ref):
    kk = pl.program_id(2)

    @pl.when(kk == 0)
    def _():
        o_ref[...] = r_ref[...]

    a = jnp.concatenate([a_ref[c] for c in range(a_ref.shape[0])], axis=1)
    o_ref[...] += jnp.dot(a, w_ref[...].astype(BF16), preferred_element_type=F32)


def _mm_residual(a, w, layer, r, bn=1024, bk=1024):
    row_tiles, col_tiles, bm, tile_cols = a.shape
    m, k = row_tiles * bm, col_tiles * tile_cols
    n = w.shape[2]
    return pl.pallas_call(
        _mm_residual_kernel,
        grid=(m // bm, n // bn, k // bk),
        in_specs=[pl.BlockSpec((None, bk // tile_cols, bm, tile_cols), lambda i, j, kk: (i, kk, 0, 0)),
                  pl.BlockSpec((None, bk, bn), lambda i, j, kk: (layer, kk, j)),
                  pl.BlockSpec((bm, bn), lambda i, j, kk: (i, j), pipeline_mode=pl.Buffered(1))],
        out_specs=pl.BlockSpec((bm, bn), lambda i, j, kk: (i, j)),
        out_shape=jax.ShapeDtypeStruct((m, n), F32),
        compiler_params=_params("parallel", "parallel", "arbitrary"),
        name="mlp_down",
    )(a, w, r)


def _out_proj_kernel(a_ref, b_ref, c_ref, d_ref, w_ref, r_ref, o_ref):
    mix = jnp.concatenate([a_ref[...]] + [b_ref[g] for g in range(len(B_GROUPS))] + [c_ref[...], d_ref[...]], axis=1)
    o_ref[...] = r_ref[...] + jnp.dot(mix, w_ref[...].astype(BF16), preferred_element_type=F32)


def _out_proj(out_a, out_b, out_c, out_d, w, layer, r, bm=1024, bn=512):
    m = out_a.shape[0]
    n = w.shape[2]
    n_groups = len(B_GROUPS)
    return pl.pallas_call(
        _out_proj_kernel,
        grid=(m // bm, n // bn),
        in_specs=[pl.BlockSpec((bm, A_WIDTH), lambda i, j: (i, 0)),
                  pl.BlockSpec((n_groups, bm, B_GROUP_WIDTH), lambda i, j: (0, i, 0)),
                  pl.BlockSpec((bm, C_WIDTH), lambda i, j: (i, 0)),
                  pl.BlockSpec((bm, D_WIDTH), lambda i, j: (i, 0)),
                  pl.BlockSpec((None, D_MODEL, bn), lambda i, j: (layer, 0, j)),
                  pl.BlockSpec((bm, bn), lambda i, j: (i, j))],
        out_specs=pl.BlockSpec((bm, bn), lambda i, j: (i, j)),
        out_shape=jax.ShapeDtypeStruct((m, n), F32),
        compiler_params=_params("parallel", "arbitrary"),
        name="out_proj",
    )(out_a, out_b, out_c, out_d, w, r)


def _online_softmax_step(s, v, m, l, acc):
    m_new = jnp.maximum(m, jnp.max(s, axis=-1, keepdims=True))
    alpha = jnp.exp(m - m_new)
    p = jnp.exp(s - m_new)
    l_new = alpha * l + jnp.sum(p, axis=-1, keepdims=True)
    acc_new = alpha * acc + jnp.dot(p.astype(BF16), v, preferred_element_type=F32)
    return m_new, l_new, acc_new


def _attn_a_kernel(q_ref, k_ref, vt_ref, lam_ref, g_ref, o_ref, *scratch, t, heads, lam_init):
    qi = pl.program_id(2)
    lane = lax.broadcasted_iota(jnp.int32, (t, HEAD_DIM), 1)
    qqs = []
    for h in range(heads):
        q = q_ref[:, h * HEAD_DIM:(h + 1) * HEAD_DIM]
        zero = jnp.zeros_like(q)
        qqs.append(jnp.concatenate([jnp.where(lane < A_SUB, q, zero), jnp.where(lane >= A_SUB, q, zero)], axis=0))

    ones = jnp.ones((A_ONES_ROWS, t), BF16)
    m_scr, acc_scr = scratch[:heads], scratch[heads:]
    for h in range(heads):
        m_scr[h][...] = jnp.full(m_scr[h].shape, MASKED_SCORE, F32)
        acc_scr[h][...] = jnp.zeros(acc_scr[h].shape, F32)

    def step(j, masked):
        def scores(h):
            k = k_ref[pl.ds(pl.multiple_of(j * t, t), t), h * HEAD_DIM:(h + 1) * HEAD_DIM]
            s = lax.dot_general(k, qqs[h], (((1,), (1,)), ((), ())), preferred_element_type=F32)
            if masked:
                key = lax.broadcasted_iota(jnp.int32, s.shape, 0)
                lane_q = lax.broadcasted_iota(jnp.int32, s.shape, 1)
                qry = jnp.where(lane_q >= t, lane_q - t, lane_q)
                s = jnp.where(key <= qry, s, MASKED_SCORE)
            return s

        s_next = scores(0)
        for h in range(heads):
            s = s_next
            if h + 1 < heads:
                s_next = scores(h + 1)
            vt = jnp.concatenate([vt_ref[j, h * HEAD_DIM:(h + 1) * HEAD_DIM, :], ones], axis=0)
            m = m_scr[h][...]
            m_new = jnp.maximum(m, jnp.max(s, axis=0, keepdims=True))
            alpha = jnp.exp2(m - m_new)
            p = jnp.exp2(s - m_new)
            m_scr[h][...] = m_new
            acc_scr[h][...] = alpha * acc_scr[h][...] + jnp.dot(vt, p.astype(BF16), preferred_element_type=F32)

    def body(j, carry):
        step(j, masked=False)
        return carry

    lax.fori_loop(0, qi, body, 0)
    step(qi, masked=True)

    lp = lam_ref[...]
    lam = (jnp.exp(jnp.sum(lp[0:1] * lp[1:2], axis=-1, keepdims=True))
           - jnp.exp(jnp.sum(lp[2:3] * lp[3:4], axis=-1, keepdims=True)) + lam_init)
    for h in range(heads):
        acc = acc_scr[h][...]
        o = acc[:HEAD_DIM] / acc[HEAD_DIM:HEAD_DIM + 1]
        o = (o[:, :t] - lam * o[:, t:]).T
        ms = jnp.mean(o * o, axis=-1, keepdims=True)
        o = o * lax.rsqrt(ms + DIFF_EPS) * g_ref[...]
        o_ref[:, h * HEAD_DIM:(h + 1) * HEAD_DIM] = (o * (1.0 - lam_init)).astype(o_ref.dtype)


def _attn_a(qk, vt, lam_params, subln_g, lam_init, batch, seq, t, heads=8):
    nq = seq // t
    width = heads * HEAD_DIM
    n_pairs = A_HEADS // heads
    return pl.pallas_call(
        functools.partial(_attn_a_kernel, t=t, heads=heads, lam_init=lam_init),
        grid=(batch, n_pairs, nq),
        in_specs=[pl.BlockSpec((t, width), lambda b, h, i: (b * nq + i, h)),
                  pl.BlockSpec((seq, width), lambda b, h, i: (b, n_pairs + h)),
                  pl.BlockSpec((nq, width, t), lambda b, h, i: (b, h, 0)),
                  pl.BlockSpec((4, A_SUB), lambda b, h, i: (0, 0)),
                  pl.BlockSpec((1, HEAD_DIM), lambda b, h, i: (0, 0))],
        out_specs=pl.BlockSpec((t, width), lambda b, h, i: (b * nq + i, h)),
        out_shape=jax.ShapeDtypeStruct((batch * seq, A_WIDTH), BF16),
        scratch_shapes=([pltpu.VMEM((1, 2 * t), F32)] * heads
                        + [pltpu.VMEM((HEAD_DIM + A_ONES_ROWS, 2 * t), F32)] * heads),
        compiler_params=_params("parallel", "parallel", "arbitrary"),
        name="diff_attention",
    )(qk, qk, vt, lam_params, subln_g.reshape(1, HEAD_DIM))


def _attn_b_kernel(*refs, tile):
    n_groups = len(B_GROUPS)
    q_refs = refs[0:n_groups]
    k_refs = refs[n_groups:2 * n_groups]
    v_refs = refs[2 * n_groups:3 * n_groups]
    o_ref = refs[3 * n_groups]
    out_scr, lse_scr = refs[3 * n_groups + 1:]
    ti = pl.program_id(2)
    scale = HEAD_DIM ** -0.5
    base_delta = {}
    chunks = []

    for g, (window, dilation) in enumerate(B_GROUPS):
        back = window // dilation
        seg = tile // dilation
        tq = min(B_QUERY_CHUNK, seg)
        tk = tq + back
        if tq not in base_delta:
            base_delta[tq] = (lax.broadcasted_iota(jnp.int32, (tq, tk), 0)
                              - lax.broadcasted_iota(jnp.int32, (tq, tk), 1))
        for r in range(dilation):
            for c in range(seg // tq):
                chunks.append((g, dilation, back, tq, tk, r, c * tq, ti * seg + c * tq))

    def scores(chunk):
        g, dilation, back, tq, tk, r, i_loc, i0 = chunk
        ks = pl.multiple_of(jnp.maximum(i0 - back, 0), back)
        q = q_refs[g][r, i_loc:i_loc + tq, :]
        k = k_refs[g][r, pl.ds(ks, tk), :]
        s = lax.dot_general(q, k, (((1,), (1,)), ((), ())), preferred_element_type=F32) * scale
        delta = base_delta[tq] + (i0 - ks)
        return jnp.where((delta >= 0) & (delta <= back), s, MASKED_SCORE), ks

    nxt = scores(chunks[0])
    for idx, (g, dilation, back, tq, tk, r, i_loc, i0) in enumerate(chunks):
        s, ks = nxt
        if idx + 1 < len(chunks):
            nxt = scores(chunks[idx + 1])
        v = v_refs[g][r, pl.ds(ks, tk), :]
        m = jnp.max(s, axis=-1, keepdims=True)
        p = jnp.exp(s - m)
        l = jnp.sum(p, axis=-1, keepdims=True)
        o = jnp.dot(p.astype(BF16), v, preferred_element_type=F32) / l
        lse = jnp.broadcast_to(m + jnp.log(l), o.shape)
        rows = pl.ds(dilation * i_loc + r, tq, stride=dilation) if dilation > 1 else pl.ds(i_loc, tq)
        out_scr[g, rows, :] = o
        lse_scr[g, rows, :] = lse

    for c in range(tile // B_QUERY_CHUNK):
        rows = pl.ds(c * B_QUERY_CHUNK, B_QUERY_CHUNK)
        lses = [lse_scr[g, rows, :] for g in range(n_groups)]
        lse_max = functools.reduce(jnp.maximum, lses)
        weights = [jnp.exp(x - lse_max) for x in lses]
        denom = functools.reduce(lambda a, b: a + b, weights)
        for g in range(n_groups):
            o_ref[g, rows, :] = (out_scr[g, rows, :] * (weights[g] / denom)).astype(o_ref.dtype)


def _attn_b(qkvs, batch, seq, tile=B_TOKEN_TILE):
    n_tiles = seq // tile
    n_groups = len(B_GROUPS)

    def q_spec(g):
        d = B_GROUPS[g][1]
        return pl.BlockSpec((None, d, tile // d, HEAD_DIM), lambda b, h, i: (b, 0, i, h))

    def kv_spec(g, part):
        d = B_GROUPS[g][1]
        return pl.BlockSpec((None, d, seq // d, HEAD_DIM),
                            lambda b, h, i: (b, 0, 0, part * B_HEADS_PER_GROUP + h))

    in_specs = ([q_spec(g) for g in range(n_groups)] + [kv_spec(g, 1) for g in range(n_groups)]
                + [kv_spec(g, 2) for g in range(n_groups)])
    return pl.pallas_call(
        functools.partial(_attn_b_kernel, tile=tile),
        grid=(batch, B_HEADS_PER_GROUP, n_tiles),
        in_specs=in_specs,
        out_specs=pl.BlockSpec((n_groups, tile, HEAD_DIM), lambda b, h, i: (0, b * n_tiles + i, h)),
        out_shape=jax.ShapeDtypeStruct((n_groups, batch * seq, B_GROUP_WIDTH), BF16),
        scratch_shapes=[pltpu.VMEM((n_groups, tile, HEAD_DIM), F32),
                        pltpu.VMEM((n_groups, tile, HEAD_DIM), F32)],
        compiler_params=_params("parallel", "parallel", "arbitrary"),
        name="dilated_attention",
    )(*(list(qkvs) * 3))


def _shift_rows(x, k, row):
    return jnp.where(row >= k, pltpu.roll(x, k, 0), 0.0)


def _pool_kernel(u_lo_ref, u_hi_ref, w_ref, scale_ref, o_ref):
    g = pl.program_id(1)
    u = jnp.concatenate([u_lo_ref[...], u_hi_ref[...]], axis=1)
    row = lax.broadcasted_iota(jnp.int32, u.shape, 0)
    window = jnp.left_shift(2, g)
    total = u
    step = 1
    while step < max(C_WINDOWS):
        widened = total + _shift_rows(total, step, row)
        total = jnp.where(step < window, widened, total)
        step *= 2
    count = jnp.minimum(row + 1, window).astype(F32)
    pooled = total / count - u
    y = jnp.dot(pooled.astype(BF16), w_ref[0], preferred_element_type=F32)
    o_ref[...] = (y * scale_ref[...]).astype(o_ref.dtype)


def _pool(proj, col0, pool_w, pool_scale, batch, seq):
    n_groups = len(C_WINDOWS)
    tiles_per_group = C_GROUP_DIM // LANES
    assert tiles_per_group == 2 and col0 % LANES == 0
    first = col0 // LANES

    def u_spec(half):
        return pl.BlockSpec((seq, LANES), lambda b, g: (b, first + tiles_per_group * g + half))

    return pl.pallas_call(
        _pool_kernel,
        grid=(batch, n_groups),
        in_specs=[u_spec(0), u_spec(1),
                  pl.BlockSpec((1, C_GROUP_DIM, C_GROUP_DIM), lambda b, g: (g, 0, 0)),
                  pl.BlockSpec((1, C_GROUP_DIM), lambda b, g: (0, g))],
        out_specs=pl.BlockSpec((seq, C_GROUP_DIM), lambda b, g: (b, g)),
        out_shape=jax.ShapeDtypeStruct((batch * seq, C_WIDTH), BF16),
        compiler_params=_params("parallel", "parallel"),
        name="multiscale_pool",
    )(proj, proj, pool_w, pool_scale.reshape(1, C_WIDTH))


def _conv_kernel(gb_ref, gc_ref, h_ref, w_ref, o_ref):
    z = gc_ref[...] * h_ref[...]
    row = lax.broadcasted_iota(jnp.int32, z.shape, 0)
    w = w_ref[...]
    y = w[CONV_WIDTH - 1:CONV_WIDTH] * z
    for tap in range(1, CONV_WIDTH):
        y = y + w[CONV_WIDTH - 1 - tap:CONV_WIDTH - tap] * _shift_rows(z, tap, row)
    o_ref[...] = (gb_ref[...] * y).astype(o_ref.dtype)


def _short_conv(proj, col0, conv_w, batch, seq):
    n_col = D_WIDTH // LANES
    assert col0 % LANES == 0
    first = col0 // LANES

    def spec(part):
        return pl.BlockSpec((seq, LANES), lambda b, c: (b, first + part * n_col + c))

    return pl.pallas_call(
        _conv_kernel,
        grid=(batch, n_col),
        in_specs=[spec(0), spec(1), spec(2), pl.BlockSpec((CONV_WIDTH, LANES), lambda b, c: (0, c))],
        out_specs=pl.BlockSpec((seq, LANES), lambda b, c: (b, c)),
        out_shape=jax.ShapeDtypeStruct((batch * seq, D_WIDTH), BF16),
        compiler_params=_params("parallel", "parallel"),
        name="short_conv",
    )(proj, proj, proj, conv_w)


def _rope_tables(seq, dim):
    half = dim // 2
    inv = ROPE_THETA ** (-jnp.arange(0, dim, 2, dtype=F32) / dim)
    ang = jnp.arange(seq, dtype=F32)[:, None] * inv[None, :]
    cos, sin = jnp.cos(ang), jnp.sin(ang)
    reps = LANES // dim
    cos_t = jnp.tile(jnp.concatenate([cos, cos], axis=1), (1, reps))
    zeros = jnp.zeros_like(sin)
    if dim == LANES:
        return (cos_t, jnp.concatenate([-sin, sin], axis=1)), (half,)
    upper = jnp.tile(jnp.concatenate([zeros, sin], axis=1), (1, reps))
    lower = jnp.tile(jnp.concatenate([-sin, zeros], axis=1), (1, reps))
    return (cos_t, upper, lower), (half, LANES - half)


def kernel(x, w_in, w_out, norm_mix, norm_mlp, diff_lambda, diff_subln, pool_w, pool_scale, conv_w,
           w_up, w_down, norm_final):
    batch, seq, d_model = x.shape
    depth = w_in.shape[0]
    m = batch * seq
    x = x.reshape(m, d_model)

    tables_a, shifts_a = _rope_tables(seq, A_SUB)
    tables_b, shifts_b = _rope_tables(seq, HEAD_DIM)

    a_qk_end = 2 * A_WIDTH
    a_end = 3 * A_WIDTH
    b_end = a_end + 3 * B_WIDTH
    c_end = b_end + C_WIDTH
    t_a = 512
    tail0 = (b_end // IN_PROJ_COL_TILE) * IN_PROJ_COL_TILE
    in_width = w_in.shape[2]

    for l in range(depth):
        wl = w_in[l]
        w_a_vt = wl[:, a_qk_end:a_end].T.astype(BF16)
        w_b = [jnp.concatenate([wl[:, a_end + part * B_WIDTH + g * B_GROUP_WIDTH:
                                   a_end + part * B_WIDTH + (g + 1) * B_GROUP_WIDTH] for part in range(3)],
                               axis=1).astype(BF16) for g in range(len(B_GROUPS))]
        lam_init = 0.8 - 0.6 * math.exp(-0.3 * l)

        h = _rmsnorm(x, norm_mix[l], BF16)
        qk_a = _mm_rope(h, w_in, l, a_qk_end, tables_a, shifts_a, A_SCORE_SCALE, A_WIDTH, seq,
                        bn=IN_PROJ_COL_TILE)
        vt_a = _mm_transposed(h, w_a_vt, t_a)
        qkv_b = [_mm_rope_permute(h, w_b[g], tables_b, shifts_b[0], 2 * B_GROUP_WIDTH, B_GROUPS[g][1], batch, seq)
                 for g in range(len(B_GROUPS))]
        tail = _mm_cols(h, w_in, l, tail0, in_width - tail0, F32, bn=IN_PROJ_COL_TILE, name="in_proj_tail")

        out_a = _attn_a(qk_a, vt_a, diff_lambda[l], diff_subln[l], lam_init, batch, seq, t_a)
        out_b = _attn_b(qkv_b, batch, seq)
        out_c = _pool(tail, b_end - tail0, pool_w[l].astype(BF16), pool_scale[l], batch, seq)
        out_d = _short_conv(tail, c_end - tail0, conv_w[l], batch, seq)

        x = _out_proj(out_a, out_b, out_c, out_d, w_out, l, x)

        h = _rmsnorm(x, norm_mlp[l], BF16)
        act = _mm_layer_tiled(h, w_up, l, BF16, 512, relu_sq=True, name="mlp_up")
        x = _mm_residual(act, w_down, l, x)

    out = _rmsnorm(x, norm_final, F32)
    return out.reshape(batch, seq, d_model)
```

```python
import functools
import math

import jax
import jax.numpy as jnp
from jax import lax
from jax.experimental import pallas as pl
from jax.experimental.pallas import tpu as pltpu

D_MODEL = 4096
HEAD_DIM = 128
A_HEADS = 8
A_WIDTH = A_HEADS * HEAD_DIM
A_SUB = HEAD_DIM // 2
A_SCORE_SCALE = A_SUB ** -0.5 * math.log2(math.e)
A_ONES_ROWS = 16
B_GROUPS = ((128, 1), (512, 4), (2048, 16))
B_HEADS_PER_GROUP = 3
B_GROUP_WIDTH = B_HEADS_PER_GROUP * HEAD_DIM
B_WIDTH = B_GROUP_WIDTH * len(B_GROUPS)
B_TOKEN_TILE = 2048
B_QUERY_CHUNK = 256
C_WINDOWS = (2, 4, 8, 16)
C_WIDTH = 1024
C_GROUP_DIM = C_WIDTH // len(C_WINDOWS)
D_WIDTH = D_MODEL - A_WIDTH - B_WIDTH - C_WIDTH
CONV_WIDTH = 3
D_FF = 4 * D_MODEL
ROPE_THETA = 10000.0
NORM_EPS = 1e-6
DIFF_EPS = 1e-5

LANES = 128
MXU_COLS = 256
MLP_ROW_TILE = 2048
IN_PROJ_COL_TILE = 512
IN_PROJ_ROW_TILE = 2048
VMEM_LIMIT_BYTES = 56 * 1024 * 1024
MASKED_SCORE = -1e30

F32 = jnp.float32
BF16 = jnp.bfloat16


def _params(*semantics):
    return pltpu.CompilerParams(dimension_semantics=semantics, vmem_limit_bytes=VMEM_LIMIT_BYTES)


def _rmsnorm_kernel(x_ref, g_ref, o_ref, *, eps):
    x = x_ref[...]
    ms = jnp.mean(x * x, axis=-1, keepdims=True)
    o_ref[...] = (x * lax.rsqrt(ms + eps) * g_ref[...]).astype(o_ref.dtype)


def _rmsnorm(x, g, out_dtype, tm=256):
    m, d = x.shape
    return pl.pallas_call(
        functools.partial(_rmsnorm_kernel, eps=NORM_EPS),
        grid=(m // tm,),
        in_specs=[pl.BlockSpec((tm, d), lambda i: (i, 0)),
                  pl.BlockSpec((1, d), lambda i: (0, 0))],
        out_specs=pl.BlockSpec((tm, d), lambda i: (i, 0)),
        out_shape=jax.ShapeDtypeStruct((m, d), out_dtype),
        compiler_params=_params("parallel"),
        name="rmsnorm",
    )(x, g.reshape(1, d))


def _rotate(y, cos_ref, sin_refs, shifts):
    outs = []
    for c in range(y.shape[1] // LANES):
        yc = y[:, c * LANES:(c + 1) * LANES]
        oc = yc * cos_ref[...]
        for s_ref, shift in zip(sin_refs, shifts):
            oc = oc + pltpu.roll(yc, shift, 1) * s_ref[...]
        outs.append(oc)
    return jnp.concatenate(outs, axis=1)


def _mm_rope_kernel(a_ref, w_ref, cos_ref, *rest, shifts, q_scale, n_q_blocks):
    sin_refs, o_ref = rest[:-1], rest[-1]
    scale = jnp.where(pl.program_id(1) < n_q_blocks, q_scale, 1.0).astype(F32)
    a = a_ref[...]
    for c in range(o_ref.shape[1] // MXU_COLS):
        cols = slice(c * MXU_COLS, (c + 1) * MXU_COLS)
        acc = jnp.dot(a, w_ref[:, cols].astype(BF16), preferred_element_type=F32)
        o_ref[:, cols] = (_rotate(acc, cos_ref, sin_refs, shifts) * scale).astype(o_ref.dtype)


def _mm_rope(a, w, layer, n, tables, shifts, q_scale, n_q_cols, seq, bn=512, bm=IN_PROJ_ROW_TILE):
    m, k = a.shape
    tab_blocks = seq // bm
    tab_spec = pl.BlockSpec((bm, LANES), lambda i, j: (i % tab_blocks, 0))
    return pl.pallas_call(
        functools.partial(_mm_rope_kernel, shifts=shifts, q_scale=q_scale, n_q_blocks=n_q_cols // bn),
        grid=(m // bm, n // bn),
        in_specs=[pl.BlockSpec((bm, k), lambda i, j: (i, 0), pipeline_mode=pl.Buffered(1)),
                  pl.BlockSpec((None, k, bn), lambda i, j: (layer, 0, j))] + [tab_spec] * len(tables),
        out_specs=pl.BlockSpec((bm, bn), lambda i, j: (i, j)),
        out_shape=jax.ShapeDtypeStruct((m, n), BF16),
        compiler_params=_params("parallel", "arbitrary"),
        name="in_proj_rope",
    )(a, w, *tables)


def _mm_rope_permute_kernel(a_ref, wq_ref, wk_ref, wv_ref, cos_ref, sin_ref, o_ref, scr_ref, *,
                            shift, n_rope_cols, dilation):
    a = a_ref[...]
    w = jnp.concatenate([wq_ref[...], wk_ref[...], wv_ref[...]], axis=1)
    n = o_ref.shape[2]
    rows = a.shape[0] // dilation
    for lo in range(0, n, 2 * MXU_COLS):
        hi = min(lo + 2 * MXU_COLS, n)
        acc = jnp.dot(a, w[:, lo:hi], preferred_element_type=F32)
        rope_hi = min(hi, n_rope_cols)
        if lo < rope_hi:
            rot = _rotate(acc[:, :rope_hi - lo], cos_ref, (sin_ref,), (shift,))
            acc = rot if rope_hi == hi else jnp.concatenate([rot, acc[:, rope_hi - lo:]], axis=1)
        if dilation == 1:
            o_ref[0, :, lo:hi] = acc.astype(o_ref.dtype)
        else:
            for c in range(lo // LANES, hi // LANES):
                cols = slice(c * LANES, (c + 1) * LANES)
                scr_ref[c] = acc[:, c * LANES - lo:(c + 1) * LANES - lo]
                for r in range(dilation):
                    o_ref[r, :, cols] = scr_ref[c, pl.ds(r, rows, stride=dilation), :].astype(o_ref.dtype)


def _mm_rope_permute(a, w, layer, group, tables, shift, dilation, batch, seq, bm=1024):
    m, k = a.shape
    n = 3 * B_GROUP_WIDTH
    n_rope_cols = 2 * B_GROUP_WIDTH
    tiles = seq // bm
    tab_spec = pl.BlockSpec((bm, LANES), lambda i: (i % tiles, 0))
    n_groups = len(B_GROUPS)

    def w_spec(part):
        return pl.BlockSpec((None, k, B_GROUP_WIDTH), lambda i: (layer, 0, part * n_groups + group),
                            pipeline_mode=pl.Buffered(1))

    return pl.pallas_call(
        functools.partial(_mm_rope_permute_kernel, shift=shift, n_rope_cols=n_rope_cols, dilation=dilation),
        grid=(m // bm,),
        in_specs=[pl.BlockSpec((bm, k), lambda i: (i, 0)), w_spec(0), w_spec(1), w_spec(2),
                  tab_spec, tab_spec],
        out_specs=pl.BlockSpec((None, dilation, bm // dilation, n), lambda i: (i // tiles, 0, i % tiles, 0)),
        out_shape=jax.ShapeDtypeStruct((batch, dilation, seq // dilation, n), BF16),
        scratch_shapes=[pltpu.VMEM((n // LANES, bm, LANES), F32)],
        compiler_params=_params("parallel"),
        name="in_proj_dilated",
    )(a, w, w, w, *tables)


def _mm_kernel(a_ref, w_ref, o_ref, *, relu_sq):
    acc = jnp.dot(a_ref[...], w_ref[...].astype(BF16), preferred_element_type=F32)
    if relu_sq:
        acc = jnp.square(jnp.maximum(acc, 0.0))
    o_ref[...] = acc.astype(o_ref.dtype)


def _mm_cols(a, w, layer, col0, n, out_dtype, bn=512, bm=IN_PROJ_ROW_TILE, name="matmul"):
    m, k = a.shape
    first = col0 // bn
    assert first * bn == col0 and n % bn == 0
    return pl.pallas_call(
        functools.partial(_mm_kernel, relu_sq=False),
        grid=(m // bm, n // bn),
        in_specs=[pl.BlockSpec((bm, k), lambda i, j: (i, 0), pipeline_mode=pl.Buffered(1)),
                  pl.BlockSpec((None, k, bn), lambda i, j: (layer, 0, first + j))],
        out_specs=pl.BlockSpec((bm, bn), lambda i, j: (i, j)),
        out_shape=jax.ShapeDtypeStruct((m, n), out_dtype),
        compiler_params=_params("parallel", "arbitrary"),
        name=name,
    )(a, w)


def _mm_layer_tiled(a, w, layer, out_dtype, bn, relu_sq=False, bm=MLP_ROW_TILE, name="matmul"):
    m, k = a.shape
    n = w.shape[2]
    return pl.pallas_call(
        functools.partial(_mm_kernel, relu_sq=relu_sq),
        grid=(m // bm, n // bn),
        in_specs=[pl.BlockSpec((bm, k), lambda i, j: (i, 0), pipeline_mode=pl.Buffered(1)),
                  pl.BlockSpec((None, k, bn), lambda i, j: (layer, 0, j))],
        out_specs=pl.BlockSpec((None, None, bm, bn), lambda i, j: (i, j, 0, 0)),
        out_shape=jax.ShapeDtypeStruct((m // bm, n // bn, bm, bn), out_dtype),
        compiler_params=_params("parallel", "arbitrary"),
        name=name,
    )(a, w)


def _mm_transposed_kernel(wt_ref, a_ref, o_ref, *, t):
    acc = lax.dot_general(wt_ref[...], a_ref[...], (((1,), (1,)), ((), ())), preferred_element_type=F32)
    for c in range(acc.shape[1] // t):
        o_ref[c] = acc[:, c * t:(c + 1) * t].astype(o_ref.dtype)


def _mm_transposed(a, wt, layer, t, bm=1024):
    m, k = a.shape
    n = wt.shape[1]
    return pl.pallas_call(
        functools.partial(_mm_transposed_kernel, t=t),
        grid=(m // bm,),
        in_specs=[pl.BlockSpec((None, n, k), lambda i: (layer, 0, 0), pipeline_mode=pl.Buffered(1)),
                  pl.BlockSpec((bm, k), lambda i: (i, 0))],
        out_specs=pl.BlockSpec((bm // t, n, t), lambda i: (i, 0, 0)),
        out_shape=jax.ShapeDtypeStruct((m // t, n, t), BF16),
        compiler_params=_params("parallel"),
        name="in_proj_vt",
    )(wt, a)


def _mm_residual_kernel(a_ref, w_ref, r_ref, o_ref):
    kk = pl.program_id(2)

    @pl.when(kk == 0)
    def _():
        o_ref[...] = r_ref[...]

    a = jnp.concatenate([a_ref[c] for c in range(a_ref.shape[0])], axis=1)
    o_ref[...] += jnp.dot(a, w_ref[...].astype(BF16), preferred_element_type=F32)


def _mm_residual(a, w, layer, r, bn=1024, bk=1024):
    row_tiles, col_tiles, bm, tile_cols = a.shape
    m, k = row_tiles * bm, col_tiles * tile_cols
    n = w.shape[2]
    return pl.pallas_call(
        _mm_residual_kernel,
        grid=(m // bm, n // bn, k // bk),
        in_specs=[pl.BlockSpec((None, bk // tile_cols, bm, tile_cols), lambda i, j, kk: (i, kk, 0, 0)),
                  pl.BlockSpec((None, bk, bn), lambda i, j, kk: (layer, kk, j)),
                  pl.BlockSpec((bm, bn), lambda i, j, kk: (i, j), pipeline_mode=pl.Buffered(1))],
        out_specs=pl.BlockSpec((bm, bn), lambda i, j, kk: (i, j)),
        out_shape=jax.ShapeDtypeStruct((m, n), F32),
        compiler_params=_params("parallel", "parallel", "arbitrary"),
        name="mlp_down",
    )(a, w, r)


def _out_proj_kernel(a_ref, b_ref, c_ref, d_ref, w_ref, r_ref, o_ref):
    mix = jnp.concatenate([a_ref[...]] + [b_ref[g] for g in range(len(B_GROUPS))] + [c_ref[...], d_ref[...]], axis=1)
    o_ref[...] = r_ref[...] + jnp.dot(mix, w_ref[...].astype(BF16), preferred_element_type=F32)


def _out_proj(out_a, out_b, out_c, out_d, w, layer, r, bm=1024, bn=512):
    m = out_a.shape[0]
    n = w.shape[2]
    n_groups = len(B_GROUPS)
    return pl.pallas_call(
        _out_proj_kernel,
        grid=(m // bm, n // bn),
        in_specs=[pl.BlockSpec((bm, A_WIDTH), lambda i, j: (i, 0)),
                  pl.BlockSpec((n_groups, bm, B_GROUP_WIDTH), lambda i, j: (0, i, 0)),
                  pl.BlockSpec((bm, C_WIDTH), lambda i, j: (i, 0)),
                  pl.BlockSpec((bm, D_WIDTH), lambda i, j: (i, 0)),
                  pl.BlockSpec((None, D_MODEL, bn), lambda i, j: (layer, 0, j)),
                  pl.BlockSpec((bm, bn), lambda i, j: (i, j))],
        out_specs=pl.BlockSpec((bm, bn), lambda i, j: (i, j)),
        out_shape=jax.ShapeDtypeStruct((m, n), F32),
        compiler_params=_params("parallel", "arbitrary"),
        name="out_proj",
    )(out_a, out_b, out_c, out_d, w, r)


def _online_softmax_step(s, v, m, l, acc):
    m_new = jnp.maximum(m, jnp.max(s, axis=-1, keepdims=True))
    alpha = jnp.exp(m - m_new)
    p = jnp.exp(s - m_new)
    l_new = alpha * l + jnp.sum(p, axis=-1, keepdims=True)
    acc_new = alpha * acc + jnp.dot(p.astype(BF16), v, preferred_element_type=F32)
    return m_new, l_new, acc_new


def _attn_a_kernel(q_ref, k_ref, vt_ref, lam_ref, g_ref, o_ref, *scratch, t, heads, lam_init):
    qi = pl.program_id(2)
    lane = lax.broadcasted_iota(jnp.int32, (t, HEAD_DIM), 1)
    qqs = []
    for h in range(heads):
        q = q_ref[:, h * HEAD_DIM:(h + 1) * HEAD_DIM]
        zero = jnp.zeros_like(q)
        qqs.append(jnp.concatenate([jnp.where(lane < A_SUB, q, zero), jnp.where(lane >= A_SUB, q, zero)], axis=0))

    ones = jnp.ones((A_ONES_ROWS, t), BF16)
    m_scr, acc_scr = scratch[:heads], scratch[heads:]
    for h in range(heads):
        m_scr[h][...] = jnp.full(m_scr[h].shape, MASKED_SCORE, F32)
        acc_scr[h][...] = jnp.zeros(acc_scr[h].shape, F32)

    def step(j, masked):
        def scores(h):
            k = k_ref[pl.ds(pl.multiple_of(j * t, t), t), h * HEAD_DIM:(h + 1) * HEAD_DIM]
            s = lax.dot_general(k, qqs[h], (((1,), (1,)), ((), ())), preferred_element_type=F32)
            if masked:
                key = lax.broadcasted_iota(jnp.int32, s.shape, 0)
                lane_q = lax.broadcasted_iota(jnp.int32, s.shape, 1)
                qry = jnp.where(lane_q >= t, lane_q - t, lane_q)
                s = jnp.where(key <= qry, s, MASKED_SCORE)
            return s

        s_next = scores(0)
        for h in range(heads):
            s = s_next
            if h + 1 < heads:
                s_next = scores(h + 1)
            vt = jnp.concatenate([vt_ref[j, h * HEAD_DIM:(h + 1) * HEAD_DIM, :], ones], axis=0)
            m = m_scr[h][...]
            m_new = jnp.maximum(m, jnp.max(s, axis=0, keepdims=True))
            alpha = jnp.exp2(m - m_new)
            p = jnp.exp2(s - m_new)
            m_scr[h][...] = m_new
            acc_scr[h][...] = alpha * acc_scr[h][...] + jnp.dot(vt, p.astype(BF16), preferred_element_type=F32)

    def body(j, carry):
        step(j, masked=False)
        return carry

    lax.fori_loop(0, qi, body, 0)
    step(qi, masked=True)

    lp = lam_ref[...]
    lam = (jnp.exp(jnp.sum(lp[0:1] * lp[1:2], axis=-1, keepdims=True))
           - jnp.exp(jnp.sum(lp[2:3] * lp[3:4], axis=-1, keepdims=True)) + lam_init)
    for h in range(heads):
        acc = acc_scr[h][...]
        o = acc[:HEAD_DIM] / acc[HEAD_DIM:HEAD_DIM + 1]
        o = (o[:, :t] - lam * o[:, t:]).T
        ms = jnp.mean(o * o, axis=-1, keepdims=True)
        o = o * lax.rsqrt(ms + DIFF_EPS) * g_ref[...]
        o_ref[:, h * HEAD_DIM:(h + 1) * HEAD_DIM] = (o * (1.0 - lam_init)).astype(o_ref.dtype)


def _attn_a(qk, vt, lam_params, subln_g, lam_init, batch, seq, t, heads=8):
    nq = seq // t
    width = heads * HEAD_DIM
    n_pairs = A_HEADS // heads
    return pl.pallas_call(
        functools.partial(_attn_a_kernel, t=t, heads=heads, lam_init=lam_init),
        grid=(batch, n_pairs, nq),
        in_specs=[pl.BlockSpec((t, width), lambda b, h, i: (b * nq + i, h)),
                  pl.BlockSpec((seq, width), lambda b, h, i: (b, n_pairs + h)),
                  pl.BlockSpec((nq, width, t), lambda b, h, i: (b, h, 0)),
                  pl.BlockSpec((4, A_SUB), lambda b, h, i: (0, 0)),
                  pl.BlockSpec((1, HEAD_DIM), lambda b, h, i: (0, 0))],
        out_specs=pl.BlockSpec((t, width), lambda b, h, i: (b * nq + i, h)),
        out_shape=jax.ShapeDtypeStruct((batch * seq, A_WIDTH), BF16),
        scratch_shapes=([pltpu.VMEM((1, 2 * t), F32)] * heads
                        + [pltpu.VMEM((HEAD_DIM + A_ONES_ROWS, 2 * t), F32)] * heads),
        compiler_params=_params("parallel", "parallel", "arbitrary"),
        name="diff_attention",
    )(qk, qk, vt, lam_params, subln_g.reshape(1, HEAD_DIM))


def _attn_b_kernel(*refs, tile):
    n_groups = len(B_GROUPS)
    q_refs = refs[0:n_groups]
    k_refs = refs[n_groups:2 * n_groups]
    v_refs = refs[2 * n_groups:3 * n_groups]
    o_ref = refs[3 * n_groups]
    out_scr, lse_scr = refs[3 * n_groups + 1:]
    ti = pl.program_id(2)
    scale = HEAD_DIM ** -0.5
    base_delta = {}
    chunks = []

    for g, (window, dilation) in enumerate(B_GROUPS):
        back = window // dilation
        seg = tile // dilation
        tq = min(B_QUERY_CHUNK, seg)
        tk = tq + back
        if tq not in base_delta:
            base_delta[tq] = (lax.broadcasted_iota(jnp.int32, (tq, tk), 0)
                              - lax.broadcasted_iota(jnp.int32, (tq, tk), 1))
        for r in range(dilation):
            for c in range(seg // tq):
                chunks.append((g, dilation, back, tq, tk, r, c * tq, ti * seg + c * tq))

    def scores(chunk):
        g, dilation, back, tq, tk, r, i_loc, i0 = chunk
        ks = pl.multiple_of(jnp.maximum(i0 - back, 0), back)
        q = q_refs[g][r, i_loc:i_loc + tq, :]
        k = k_refs[g][r, pl.ds(ks, tk), :]
        s = lax.dot_general(q, k, (((1,), (1,)), ((), ())), preferred_element_type=F32) * scale
        delta = base_delta[tq] + (i0 - ks)
        return jnp.where((delta >= 0) & (delta <= back), s, MASKED_SCORE), ks

    nxt = scores(chunks[0])
    for idx, (g, dilation, back, tq, tk, r, i_loc, i0) in enumerate(chunks):
        s, ks = nxt
        if idx + 1 < len(chunks):
            nxt = scores(chunks[idx + 1])
        v = v_refs[g][r, pl.ds(ks, tk), :]
        m = jnp.max(s, axis=-1, keepdims=True)
        p = jnp.exp(s - m)
        l = jnp.sum(p, axis=-1, keepdims=True)
        o = jnp.dot(p.astype(BF16), v, preferred_element_type=F32) / l
        lse = jnp.broadcast_to(m + jnp.log(l), o.shape)
        rows = pl.ds(dilation * i_loc + r, tq, stride=dilation) if dilation > 1 else pl.ds(i_loc, tq)
        out_scr[g, rows, :] = o
        lse_scr[g, rows, :] = lse

    for c in range(tile // B_QUERY_CHUNK):
        rows = pl.ds(c * B_QUERY_CHUNK, B_QUERY_CHUNK)
        lses = [lse_scr[g, rows, :] for g in range(n_groups)]
        lse_max = functools.reduce(jnp.maximum, lses)
        weights = [jnp.exp(x - lse_max) for x in lses]
        denom = functools.reduce(lambda a, b: a + b, weights)
        for g in range(n_groups):
            o_ref[g, rows, :] = (out_scr[g, rows, :] * (weights[g] / denom)).astype(o_ref.dtype)


def _attn_b(qkvs, batch, seq, tile=B_TOKEN_TILE):
    n_tiles = seq // tile
    n_groups = len(B_GROUPS)

    def q_spec(g):
        d = B_GROUPS[g][1]
        return pl.BlockSpec((None, d, tile // d, HEAD_DIM), lambda b, h, i: (b, 0, i, h))

    def kv_spec(g, part):
        d = B_GROUPS[g][1]
        return pl.BlockSpec((None, d, seq // d, HEAD_DIM),
                            lambda b, h, i: (b, 0, 0, part * B_HEADS_PER_GROUP + h))

    in_specs = ([q_spec(g) for g in range(n_groups)] + [kv_spec(g, 1) for g in range(n_groups)]
                + [kv_spec(g, 2) for g in range(n_groups)])
    return pl.pallas_call(
        functools.partial(_attn_b_kernel, tile=tile),
        grid=(batch, B_HEADS_PER_GROUP, n_tiles),
        in_specs=in_specs,
        out_specs=pl.BlockSpec((n_groups, tile, HEAD_DIM), lambda b, h, i: (0, b * n_tiles + i, h)),
        out_shape=jax.ShapeDtypeStruct((n_groups, batch * seq, B_GROUP_WIDTH), BF16),
        scratch_shapes=[pltpu.VMEM((n_groups, tile, HEAD_DIM), F32),
                        pltpu.VMEM((n_groups, tile, HEAD_DIM), F32)],
        compiler_params=_params("parallel", "parallel", "arbitrary"),
        name="dilated_attention",
    )(*(list(qkvs) * 3))


def _shift_rows(x, k, row):
    return jnp.where(row >= k, pltpu.roll(x, k, 0), 0.0)


def _pool_kernel(u_lo_ref, u_hi_ref, w_ref, scale_ref, o_ref):
    g = pl.program_id(1)
    u = jnp.concatenate([u_lo_ref[...], u_hi_ref[...]], axis=1)
    row = lax.broadcasted_iota(jnp.int32, u.shape, 0)
    window = jnp.left_shift(2, g)
    total = u
    step = 1
    while step < max(C_WINDOWS):
        widened = total + _shift_rows(total, step, row)
        total = jnp.where(step < window, widened, total)
        step *= 2
    count = jnp.minimum(row + 1, window).astype(F32)
    pooled = total / count - u
    y = jnp.dot(pooled.astype(BF16), w_ref[0], preferred_element_type=F32)
    o_ref[...] = (y * scale_ref[...]).astype(o_ref.dtype)


def _pool(proj, col0, pool_w, pool_scale, batch, seq):
    n_groups = len(C_WINDOWS)
    tiles_per_group = C_GROUP_DIM // LANES
    assert tiles_per_group == 2 and col0 % LANES == 0
    first = col0 // LANES

    def u_spec(half):
        return pl.BlockSpec((seq, LANES), lambda b, g: (b, first + tiles_per_group * g + half))

    return pl.pallas_call(
        _pool_kernel,
        grid=(batch, n_groups),
        in_specs=[u_spec(0), u_spec(1),
                  pl.BlockSpec((1, C_GROUP_DIM, C_GROUP_DIM), lambda b, g: (g, 0, 0)),
                  pl.BlockSpec((1, C_GROUP_DIM), lambda b, g: (0, g))],
        out_specs=pl.BlockSpec((seq, C_GROUP_DIM), lambda b, g: (b, g)),
        out_shape=jax.ShapeDtypeStruct((batch * seq, C_WIDTH), BF16),
        compiler_params=_params("parallel", "parallel"),
        name="multiscale_pool",
    )(proj, proj, pool_w, pool_scale.reshape(1, C_WIDTH))


def _conv_kernel(gb_ref, gc_ref, h_ref, w_ref, o_ref):
    z = gc_ref[...] * h_ref[...]
    row = lax.broadcasted_iota(jnp.int32, z.shape, 0)
    w = w_ref[...]
    y = w[CONV_WIDTH - 1:CONV_WIDTH] * z
    for tap in range(1, CONV_WIDTH):
        y = y + w[CONV_WIDTH - 1 - tap:CONV_WIDTH - tap] * _shift_rows(z, tap, row)
    o_ref[...] = (gb_ref[...] * y).astype(o_ref.dtype)


def _short_conv(proj, col0, conv_w, batch, seq):
    n_col = D_WIDTH // LANES
    assert col0 % LANES == 0
    first = col0 // LANES

    def spec(part):
        return pl.BlockSpec((seq, LANES), lambda b, c: (b, first + part * n_col + c))

    return pl.pallas_call(
        _conv_kernel,
        grid=(batch, n_col),
        in_specs=[spec(0), spec(1), spec(2), pl.BlockSpec((CONV_WIDTH, LANES), lambda b, c: (0, c))],
        out_specs=pl.BlockSpec((seq, LANES), lambda b, c: (b, c)),
        out_shape=jax.ShapeDtypeStruct((batch * seq, D_WIDTH), BF16),
        compiler_params=_params("parallel", "parallel"),
        name="short_conv",
    )(proj, proj, proj, conv_w)


def _rope_tables(seq, dim):
    half = dim // 2
    inv = ROPE_THETA ** (-jnp.arange(0, dim, 2, dtype=F32) / dim)
    ang = jnp.arange(seq, dtype=F32)[:, None] * inv[None, :]
    cos, sin = jnp.cos(ang), jnp.sin(ang)
    reps = LANES // dim
    cos_t = jnp.tile(jnp.concatenate([cos, cos], axis=1), (1, reps))
    zeros = jnp.zeros_like(sin)
    if dim == LANES:
        return (cos_t, jnp.concatenate([-sin, sin], axis=1)), (half,)
    upper = jnp.tile(jnp.concatenate([zeros, sin], axis=1), (1, reps))
    lower = jnp.tile(jnp.concatenate([-sin, zeros], axis=1), (1, reps))
    return (cos_t, upper, lower), (half, LANES - half)


def kernel(x, w_in, w_out, norm_mix, norm_mlp, diff_lambda, diff_subln, pool_w, pool_scale, conv_w,
           w_up, w_down, norm_final):
    batch, seq, d_model = x.shape
    depth = w_in.shape[0]
    m = batch * seq
    x = x.reshape(m, d_model)

    tables_a, shifts_a = _rope_tables(seq, A_SUB)
    tables_b, shifts_b = _rope_tables(seq, HEAD_DIM)

    a_qk_end = 2 * A_WIDTH
    a_end = 3 * A_WIDTH
    b_end = a_end + 3 * B_WIDTH
    c_end = b_end + C_WIDTH
    t_a = 512
    tail0 = (b_end // IN_PROJ_COL_TILE) * IN_PROJ_COL_TILE
    in_width = w_in.shape[2]

    w_a_vt = jnp.swapaxes(w_in[:, :, a_qk_end:a_end], 1, 2).astype(BF16)
    w_b = w_in[:, :, a_end:b_end].astype(BF16)

    for l in range(depth):
        lam_init = 0.8 - 0.6 * math.exp(-0.3 * l)

        h = _rmsnorm(x, norm_mix[l], BF16)
        qk_a = _mm_rope(h, w_in, l, a_qk_end, tables_a, shifts_a, A_SCORE_SCALE, A_WIDTH, seq,
                        bn=IN_PROJ_COL_TILE)
        vt_a = _mm_transposed(h, w_a_vt, l, t_a)
        qkv_b = [_mm_rope_permute(h, w_b, l, g, tables_b, shifts_b[0], B_GROUPS[g][1], batch, seq)
                 for g in range(len(B_GROUPS))]
        tail = _mm_cols(h, w_in, l, tail0, in_width - tail0, F32, bn=IN_PROJ_COL_TILE, name="in_proj_tail")

        out_a = _attn_a(qk_a, vt_a, diff_lambda[l], diff_subln[l], lam_init, batch, seq, t_a)
        out_b = _attn_b(qkv_b, batch, seq)
        out_c = _pool(tail, b_end - tail0, pool_w[l].astype(BF16), pool_scale[l], batch, seq)
        out_d = _short_conv(tail, c_end - tail0, conv_w[l], batch, seq)

        x = _out_proj(out_a, out_b, out_c, out_d, w_out, l, x)

        h = _rmsnorm(x, norm_mlp[l], BF16)
        act = _mm_layer_tiled(h, w_up, l, BF16, 512, relu_sq=True, name="mlp_up")
        x = _mm_residual(act, w_down, l, x)

    out = _rmsnorm(x, norm_final, F32)
    return out.reshape(batch, seq, d_model)
```

```python
import functools
import math

import jax
import jax.numpy as jnp
from jax import lax
from jax.experimental import pallas as pl
from jax.experimental.pallas import tpu as pltpu

D_MODEL = 4096
HEAD_DIM = 128
A_HEADS = 8
A_WIDTH = A_HEADS * HEAD_DIM
A_SUB = HEAD_DIM // 2
A_SCORE_SCALE = A_SUB ** -0.5 * math.log2(math.e)
A_ONES_ROWS = 16
B_GROUPS = ((128, 1), (512, 4), (2048, 16))
B_HEADS_PER_GROUP = 3
B_GROUP_WIDTH = B_HEADS_PER_GROUP * HEAD_DIM
B_WIDTH = B_GROUP_WIDTH * len(B_GROUPS)
B_TOKEN_TILE = 2048
B_QUERY_CHUNK = 256
C_WINDOWS = (2, 4, 8, 16)
C_WIDTH = 1024
C_GROUP_DIM = C_WIDTH // len(C_WINDOWS)
D_WIDTH = D_MODEL - A_WIDTH - B_WIDTH - C_WIDTH
CONV_WIDTH = 3
D_FF = 4 * D_MODEL
ROPE_THETA = 10000.0
NORM_EPS = 1e-6
DIFF_EPS = 1e-5

LANES = 128
MXU_COLS = 256
MLP_ROW_TILE = 2048
IN_PROJ_COL_TILE = 1024
IN_PROJ_ROW_TILE = 1024
VMEM_LIMIT_BYTES = 56 * 1024 * 1024
MASKED_SCORE = -1e30

F32 = jnp.float32
BF16 = jnp.bfloat16


def _params(*semantics):
    return pltpu.CompilerParams(dimension_semantics=semantics, vmem_limit_bytes=VMEM_LIMIT_BYTES)


def _rmsnorm_kernel(x_ref, g_ref, o_ref, *, eps):
    x = x_ref[...]
    ms = jnp.mean(x * x, axis=-1, keepdims=True)
    o_ref[...] = (x * lax.rsqrt(ms + eps) * g_ref[...]).astype(o_ref.dtype)


def _rmsnorm(x, g, out_dtype, tm=256):
    m, d = x.shape
    return pl.pallas_call(
        functools.partial(_rmsnorm_kernel, eps=NORM_EPS),
        grid=(m // tm,),
        in_specs=[pl.BlockSpec((tm, d), lambda i: (i, 0)),
                  pl.BlockSpec((1, d), lambda i: (0, 0))],
        out_specs=pl.BlockSpec((tm, d), lambda i: (i, 0)),
        out_shape=jax.ShapeDtypeStruct((m, d), out_dtype),
        compiler_params=_params("parallel"),
        name="rmsnorm",
    )(x, g.reshape(1, d))


def _rotate(y, cos_ref, sin_refs, shifts):
    outs = []
    for c in range(y.shape[1] // LANES):
        yc = y[:, c * LANES:(c + 1) * LANES]
        oc = yc * cos_ref[...]
        for s_ref, shift in zip(sin_refs, shifts):
            oc = oc + pltpu.roll(yc, shift, 1) * s_ref[...]
        outs.append(oc)
    return jnp.concatenate(outs, axis=1)


def _mm_rope_kernel(a_ref, w_ref, cos_ref, *rest, shifts, q_scale, n_q_blocks):
    sin_refs, o_ref = rest[:-1], rest[-1]
    scale = jnp.where(pl.program_id(1) < n_q_blocks, q_scale, 1.0).astype(F32)
    a = a_ref[...]
    for c in range(o_ref.shape[1] // MXU_COLS):
        cols = slice(c * MXU_COLS, (c + 1) * MXU_COLS)
        acc = jnp.dot(a, w_ref[:, cols].astype(BF16), preferred_element_type=F32)
        o_ref[:, cols] = (_rotate(acc, cos_ref, sin_refs, shifts) * scale).astype(o_ref.dtype)


def _mm_rope(a, w, layer, n, tables, shifts, q_scale, n_q_cols, seq, bn=IN_PROJ_COL_TILE, bm=IN_PROJ_ROW_TILE):
    m, k = a.shape
    tab_blocks = seq // bm
    tab_spec = pl.BlockSpec((bm, LANES), lambda i, j: (i % tab_blocks, 0))
    return pl.pallas_call(
        functools.partial(_mm_rope_kernel, shifts=shifts, q_scale=q_scale, n_q_blocks=n_q_cols // bn),
        grid=(m // bm, n // bn),
        in_specs=[pl.BlockSpec((bm, k), lambda i, j: (i, 0)),
                  pl.BlockSpec((None, k, bn), lambda i, j: (layer, 0, j))] + [tab_spec] * len(tables),
        out_specs=pl.BlockSpec((bm, bn), lambda i, j: (i, j)),
        out_shape=jax.ShapeDtypeStruct((m, n), BF16),
        compiler_params=_params("parallel", "arbitrary"),
        name="in_proj_rope",
    )(a, w, *tables)


def _mm_rope_permute_kernel(a_ref, wq_ref, wk_ref, wv_ref, cos_ref, sin_ref, o_ref, scr_ref, *,
                            shift, n_rope_cols, dilation):
    a = a_ref[...]
    w = jnp.concatenate([wq_ref[...], wk_ref[...], wv_ref[...]], axis=1)
    n = o_ref.shape[2]
    rows = a.shape[0] // dilation
    for lo in range(0, n, 2 * MXU_COLS):
        hi = min(lo + 2 * MXU_COLS, n)
        acc = jnp.dot(a, w[:, lo:hi], preferred_element_type=F32)
        rope_hi = min(hi, n_rope_cols)
        if lo < rope_hi:
            rot = _rotate(acc[:, :rope_hi - lo], cos_ref, (sin_ref,), (shift,))
            acc = rot if rope_hi == hi else jnp.concatenate([rot, acc[:, rope_hi - lo:]], axis=1)
        if dilation == 1:
            o_ref[0, :, lo:hi] = acc.astype(o_ref.dtype)
        else:
            for c in range(lo // LANES, hi // LANES):
                cols = slice(c * LANES, (c + 1) * LANES)
                scr_ref[c] = acc[:, c * LANES - lo:(c + 1) * LANES - lo]
                for r in range(dilation):
                    o_ref[r, :, cols] = scr_ref[c, pl.ds(r, rows, stride=dilation), :].astype(o_ref.dtype)


def _mm_rope_permute(a, w, layer, col0, group, tables, shift, dilation, batch, seq, bm=IN_PROJ_ROW_TILE):
    m, k = a.shape
    n = 3 * B_GROUP_WIDTH
    n_rope_cols = 2 * B_GROUP_WIDTH
    tiles = seq // bm
    tab_spec = pl.BlockSpec((bm, LANES), lambda i: (i % tiles, 0))
    n_groups = len(B_GROUPS)
    first = col0 // B_GROUP_WIDTH
    assert first * B_GROUP_WIDTH == col0

    def w_spec(part):
        return pl.BlockSpec((None, k, B_GROUP_WIDTH), lambda i: (layer, 0, first + part * n_groups + group),
                            pipeline_mode=pl.Buffered(1))

    return pl.pallas_call(
        functools.partial(_mm_rope_permute_kernel, shift=shift, n_rope_cols=n_rope_cols, dilation=dilation),
        grid=(m // bm,),
        in_specs=[pl.BlockSpec((bm, k), lambda i: (i, 0)), w_spec(0), w_spec(1), w_spec(2),
                  tab_spec, tab_spec],
        out_specs=pl.BlockSpec((None, dilation, bm // dilation, n), lambda i: (i // tiles, 0, i % tiles, 0)),
        out_shape=jax.ShapeDtypeStruct((batch, dilation, seq // dilation, n), BF16),
        scratch_shapes=[pltpu.VMEM((n // LANES, bm, LANES), F32)],
        compiler_params=_params("parallel"),
        name="in_proj_dilated",
    )(a, w, w, w, *tables)


def _mm_kernel(a_ref, w_ref, o_ref, *, relu_sq):
    acc = jnp.dot(a_ref[...], w_ref[...].astype(BF16), preferred_element_type=F32)
    if relu_sq:
        acc = jnp.square(jnp.maximum(acc, 0.0))
    o_ref[...] = acc.astype(o_ref.dtype)


def _mm_cols(a, w, layer, col0, n, out_dtype, bn=IN_PROJ_COL_TILE, bm=IN_PROJ_ROW_TILE, name="matmul"):
    m, k = a.shape
    first = col0 // bn
    assert first * bn == col0 and n % bn == 0
    return pl.pallas_call(
        functools.partial(_mm_kernel, relu_sq=False),
        grid=(m // bm, n // bn),
        in_specs=[pl.BlockSpec((bm, k), lambda i, j: (i, 0)),
                  pl.BlockSpec((None, k, bn), lambda i, j: (layer, 0, first + j))],
        out_specs=pl.BlockSpec((bm, bn), lambda i, j: (i, j)),
        out_shape=jax.ShapeDtypeStruct((m, n), out_dtype),
        compiler_params=_params("parallel", "arbitrary"),
        name=name,
    )(a, w)


def _mm_layer_tiled(a, w, layer, out_dtype, bn, relu_sq=False, bm=MLP_ROW_TILE, name="matmul"):
    m, k = a.shape
    n = w.shape[2]
    return pl.pallas_call(
        functools.partial(_mm_kernel, relu_sq=relu_sq),
        grid=(m // bm, n // bn),
        in_specs=[pl.BlockSpec((bm, k), lambda i, j: (i, 0), pipeline_mode=pl.Buffered(1)),
                  pl.BlockSpec((None, k, bn), lambda i, j: (layer, 0, j))],
        out_specs=pl.BlockSpec((None, None, bm, bn), lambda i, j: (i, j, 0, 0)),
        out_shape=jax.ShapeDtypeStruct((m // bm, n // bn, bm, bn), out_dtype),
        compiler_params=_params("parallel", "arbitrary"),
        name=name,
    )(a, w)


def _mm_transposed_kernel(wt_ref, a_ref, o_ref, *, t):
    acc = lax.dot_general(wt_ref[...], a_ref[...], (((1,), (1,)), ((), ())), preferred_element_type=F32)
    for c in range(acc.shape[1] // t):
        o_ref[c] = acc[:, c * t:(c + 1) * t].astype(o_ref.dtype)


def _mm_transposed(a, wt, layer, t, bm=1024):
    m, k = a.shape
    n = wt.shape[1]
    return pl.pallas_call(
        functools.partial(_mm_transposed_kernel, t=t),
        grid=(m // bm,),
        in_specs=[pl.BlockSpec((None, n, k), lambda i: (layer, 0, 0), pipeline_mode=pl.Buffered(1)),
                  pl.BlockSpec((bm, k), lambda i: (i, 0))],
        out_specs=pl.BlockSpec((bm // t, n, t), lambda i: (i, 0, 0)),
        out_shape=jax.ShapeDtypeStruct((m // t, n, t), BF16),
        compiler_params=_params("parallel"),
        name="in_proj_vt",
    )(wt, a)


def _mm_residual_kernel(a_ref, w_ref, r_ref, o_ref):
    kk = pl.program_id(2)

    @pl.when(kk == 0)
    def _():
        o_ref[...] = r_ref[...]

    a = jnp.concatenate([a_ref[c] for c in range(a_ref.shape[0])], axis=1)
    o_ref[...] += jnp.dot(a, w_ref[...].astype(BF16), preferred_element_type=F32)


def _mm_residual(a, w, layer, r, bn=1024, bk=1024):
    row_tiles, col_tiles, bm, tile_cols = a.shape
    m, k = row_tiles * bm, col_tiles * tile_cols
    n = w.shape[2]
    return pl.pallas_call(
        _mm_residual_kernel,
        grid=(m // bm, n // bn, k // bk),
        in_specs=[pl.BlockSpec((None, bk // tile_cols, bm, tile_cols), lambda i, j, kk: (i, kk, 0, 0)),
                  pl.BlockSpec((None, bk, bn), lambda i, j, kk: (layer, kk, j)),
                  pl.BlockSpec((bm, bn), lambda i, j, kk: (i, j), pipeline_mode=pl.Buffered(1))],
        out_specs=pl.BlockSpec((bm, bn), lambda i, j, kk: (i, j)),
        out_shape=jax.ShapeDtypeStruct((m, n), F32),
        compiler_params=_params("parallel", "parallel", "arbitrary"),
        name="mlp_down",
    )(a, w, r)


def _out_proj_kernel(a_ref, b_ref, c_ref, d_ref, w_ref, r_ref, o_ref):
    mix = jnp.concatenate([a_ref[...]] + [b_ref[g] for g in range(len(B_GROUPS))] + [c_ref[...], d_ref[...]], axis=1)
    o_ref[...] = r_ref[...] + jnp.dot(mix, w_ref[...].astype(BF16), preferred_element_type=F32)


def _out_proj(out_a, out_b, out_c, out_d, w, layer, r, bm=1024, bn=512):
    m = out_a.shape[0]
    n = w.shape[2]
    n_groups = len(B_GROUPS)
    return pl.pallas_call(
        _out_proj_kernel,
        grid=(m // bm, n // bn),
        in_specs=[pl.BlockSpec((bm, A_WIDTH), lambda i, j: (i, 0)),
                  pl.BlockSpec((n_groups, bm, B_GROUP_WIDTH), lambda i, j: (0, i, 0)),
                  pl.BlockSpec((bm, C_WIDTH), lambda i, j: (i, 0)),
                  pl.BlockSpec((bm, D_WIDTH), lambda i, j: (i, 0)),
                  pl.BlockSpec((None, D_MODEL, bn), lambda i, j: (layer, 0, j)),
                  pl.BlockSpec((bm, bn), lambda i, j: (i, j))],
        out_specs=pl.BlockSpec((bm, bn), lambda i, j: (i, j)),
        out_shape=jax.ShapeDtypeStruct((m, n), F32),
        compiler_params=_params("parallel", "arbitrary"),
        name="out_proj",
    )(out_a, out_b, out_c, out_d, w, r)


def _online_softmax_step(s, v, m, l, acc):
    m_new = jnp.maximum(m, jnp.max(s, axis=-1, keepdims=True))
    alpha = jnp.exp(m - m_new)
    p = jnp.exp(s - m_new)
    l_new = alpha * l + jnp.sum(p, axis=-1, keepdims=True)
    acc_new = alpha * acc + jnp.dot(p.astype(BF16), v, preferred_element_type=F32)
    return m_new, l_new, acc_new


def _attn_a_kernel(q_ref, k_ref, vt_ref, lam_ref, g_ref, o_ref, *scratch, t, heads, lam_init):
    qi = pl.program_id(2)
    lane = lax.broadcasted_iota(jnp.int32, (t, HEAD_DIM), 1)
    qqs = []
    for h in range(heads):
        q = q_ref[:, h * HEAD_DIM:(h + 1) * HEAD_DIM]
        zero = jnp.zeros_like(q)
        qqs.append(jnp.concatenate([jnp.where(lane < A_SUB, q, zero), jnp.where(lane >= A_SUB, q, zero)], axis=0))

    ones = jnp.ones((A_ONES_ROWS, t), BF16)
    m_scr, acc_scr = scratch[:heads], scratch[heads:]
    for h in range(heads):
        m_scr[h][...] = jnp.full(m_scr[h].shape, MASKED_SCORE, F32)
        acc_scr[h][...] = jnp.zeros(acc_scr[h].shape, F32)

    def step(j, masked):
        def scores(h):
            k = k_ref[pl.ds(pl.multiple_of(j * t, t), t), h * HEAD_DIM:(h + 1) * HEAD_DIM]
            s = lax.dot_general(k, qqs[h], (((1,), (1,)), ((), ())), preferred_element_type=F32)
            if masked:
                key = lax.broadcasted_iota(jnp.int32, s.shape, 0)
                lane_q = lax.broadcasted_iota(jnp.int32, s.shape, 1)
                qry = jnp.where(lane_q >= t, lane_q - t, lane_q)
                s = jnp.where(key <= qry, s, MASKED_SCORE)
            return s

        s_next = scores(0)
        for h in range(heads):
            s = s_next
            if h + 1 < heads:
                s_next = scores(h + 1)
            vt = jnp.concatenate([vt_ref[j, h * HEAD_DIM:(h + 1) * HEAD_DIM, :], ones], axis=0)
            m = m_scr[h][...]
            m_new = jnp.maximum(m, jnp.max(s, axis=0, keepdims=True))
            alpha = jnp.exp2(m - m_new)
            p = jnp.exp2(s - m_new)
            m_scr[h][...] = m_new
            acc_scr[h][...] = alpha * acc_scr[h][...] + jnp.dot(vt, p.astype(BF16), preferred_element_type=F32)

    def body(j, carry):
        step(j, masked=False)
        return carry

    lax.fori_loop(0, qi, body, 0)
    step(qi, masked=True)

    lp = lam_ref[...]
    lam = (jnp.exp(jnp.sum(lp[0:1] * lp[1:2], axis=-1, keepdims=True))
           - jnp.exp(jnp.sum(lp[2:3] * lp[3:4], axis=-1, keepdims=True)) + lam_init)
    for h in range(heads):
        acc = acc_scr[h][...]
        o = acc[:HEAD_DIM] / acc[HEAD_DIM:HEAD_DIM + 1]
        o = (o[:, :t] - lam * o[:, t:]).T
        ms = jnp.mean(o * o, axis=-1, keepdims=True)
        o = o * lax.rsqrt(ms + DIFF_EPS) * g_ref[...]
        o_ref[:, h * HEAD_DIM:(h + 1) * HEAD_DIM] = (o * (1.0 - lam_init)).astype(o_ref.dtype)


def _attn_a(qk, vt, lam_params, subln_g, lam_init, batch, seq, t, heads=8):
    nq = seq // t
    width = heads * HEAD_DIM
    n_pairs = A_HEADS // heads
    return pl.pallas_call(
        functools.partial(_attn_a_kernel, t=t, heads=heads, lam_init=lam_init),
        grid=(batch, n_pairs, nq),
        in_specs=[pl.BlockSpec((t, width), lambda b, h, i: (b * nq + i, h)),
                  pl.BlockSpec((seq, width), lambda b, h, i: (b, n_pairs + h)),
                  pl.BlockSpec((nq, width, t), lambda b, h, i: (b, h, 0)),
                  pl.BlockSpec((4, A_SUB), lambda b, h, i: (0, 0)),
                  pl.BlockSpec((1, HEAD_DIM), lambda b, h, i: (0, 0))],
        out_specs=pl.BlockSpec((t, width), lambda b, h, i: (b * nq + i, h)),
        out_shape=jax.ShapeDtypeStruct((batch * seq, A_WIDTH), BF16),
        scratch_shapes=([pltpu.VMEM((1, 2 * t), F32)] * heads
                        + [pltpu.VMEM((HEAD_DIM + A_ONES_ROWS, 2 * t), F32)] * heads),
        compiler_params=_params("parallel", "parallel", "arbitrary"),
        name="diff_attention",
    )(qk, qk, vt, lam_params, subln_g.reshape(1, HEAD_DIM))


def _attn_b_kernel(*refs, tile):
    n_groups = len(B_GROUPS)
    q_refs = refs[0:n_groups]
    k_refs = refs[n_groups:2 * n_groups]
    v_refs = refs[2 * n_groups:3 * n_groups]
    o_ref = refs[3 * n_groups]
    out_scr, lse_scr = refs[3 * n_groups + 1:]
    ti = pl.program_id(2)
    scale = HEAD_DIM ** -0.5
    base_delta = {}
    chunks = []

    for g, (window, dilation) in enumerate(B_GROUPS):
        back = window // dilation
        seg = tile // dilation
        tq = min(B_QUERY_CHUNK, seg)
        tk = tq + back
        if tq not in base_delta:
            base_delta[tq] = (lax.broadcasted_iota(jnp.int32, (tq, tk), 0)
                              - lax.broadcasted_iota(jnp.int32, (tq, tk), 1))
        for r in range(dilation):
            for c in range(seg // tq):
                chunks.append((g, dilation, back, tq, tk, r, c * tq, ti * seg + c * tq))

    def scores(chunk):
        g, dilation, back, tq, tk, r, i_loc, i0 = chunk
        ks = pl.multiple_of(jnp.maximum(i0 - back, 0), back)
        q = q_refs[g][r, i_loc:i_loc + tq, :]
        k = k_refs[g][r, pl.ds(ks, tk), :]
        s = lax.dot_general(q, k, (((1,), (1,)), ((), ())), preferred_element_type=F32) * scale
        delta = base_delta[tq] + (i0 - ks)
        return jnp.where((delta >= 0) & (delta <= back), s, MASKED_SCORE), ks

    nxt = scores(chunks[0])
    for idx, (g, dilation, back, tq, tk, r, i_loc, i0) in enumerate(chunks):
        s, ks = nxt
        if idx + 1 < len(chunks):
            nxt = scores(chunks[idx + 1])
        v = v_refs[g][r, pl.ds(ks, tk), :]
        m = jnp.max(s, axis=-1, keepdims=True)
        p = jnp.exp(s - m)
        l = jnp.sum(p, axis=-1, keepdims=True)
        o = jnp.dot(p.astype(BF16), v, preferred_element_type=F32) / l
        lse = jnp.broadcast_to(m + jnp.log(l), o.shape)
        rows = pl.ds(dilation * i_loc + r, tq, stride=dilation) if dilation > 1 else pl.ds(i_loc, tq)
        out_scr[g, rows, :] = o
        lse_scr[g, rows, :] = lse

    for c in range(tile // B_QUERY_CHUNK):
        rows = pl.ds(c * B_QUERY_CHUNK, B_QUERY_CHUNK)
        lses = [lse_scr[g, rows, :] for g in range(n_groups)]
        lse_max = functools.reduce(jnp.maximum, lses)
        weights = [jnp.exp(x - lse_max) for x in lses]
        denom = functools.reduce(lambda a, b: a + b, weights)
        for g in range(n_groups):
            o_ref[g, rows, :] = (out_scr[g, rows, :] * (weights[g] / denom)).astype(o_ref.dtype)


def _attn_b(qkvs, batch, seq, tile=B_TOKEN_TILE):
    n_tiles = seq // tile
    n_groups = len(B_GROUPS)

    def q_spec(g):
        d = B_GROUPS[g][1]
        return pl.BlockSpec((None, d, tile // d, HEAD_DIM), lambda b, h, i: (b, 0, i, h))

    def kv_spec(g, part):
        d = B_GROUPS[g][1]
        return pl.BlockSpec((None, d, seq // d, HEAD_DIM),
                            lambda b, h, i: (b, 0, 0, part * B_HEADS_PER_GROUP + h))

    in_specs = ([q_spec(g) for g in range(n_groups)] + [kv_spec(g, 1) for g in range(n_groups)]
                + [kv_spec(g, 2) for g in range(n_groups)])
    return pl.pallas_call(
        functools.partial(_attn_b_kernel, tile=tile),
        grid=(batch, B_HEADS_PER_GROUP, n_tiles),
        in_specs=in_specs,
        out_specs=pl.BlockSpec((n_groups, tile, HEAD_DIM), lambda b, h, i: (0, b * n_tiles + i, h)),
        out_shape=jax.ShapeDtypeStruct((n_groups, batch * seq, B_GROUP_WIDTH), BF16),
        scratch_shapes=[pltpu.VMEM((n_groups, tile, HEAD_DIM), F32),
                        pltpu.VMEM((n_groups, tile, HEAD_DIM), F32)],
        compiler_params=_params("parallel", "parallel", "arbitrary"),
        name="dilated_attention",
    )(*(list(qkvs) * 3))


def _shift_rows(x, k, row):
    return jnp.where(row >= k, pltpu.roll(x, k, 0), 0.0)


def _pool_kernel(u_lo_ref, u_hi_ref, w_ref, scale_ref, o_ref):
    g = pl.program_id(1)
    u = jnp.concatenate([u_lo_ref[...], u_hi_ref[...]], axis=1)
    row = lax.broadcasted_iota(jnp.int32, u.shape, 0)
    window = jnp.left_shift(2, g)
    total = u
    step = 1
    while step < max(C_WINDOWS):
        widened = total + _shift_rows(total, step, row)
        total = jnp.where(step < window, widened, total)
        step *= 2
    count = jnp.minimum(row + 1, window).astype(F32)
    pooled = total / count - u
    y = jnp.dot(pooled.astype(BF16), w_ref[0], preferred_element_type=F32)
    o_ref[...] = (y * scale_ref[...]).astype(o_ref.dtype)


def _pool(proj, col0, pool_w, pool_scale, batch, seq):
    n_groups = len(C_WINDOWS)
    tiles_per_group = C_GROUP_DIM // LANES
    assert tiles_per_group == 2 and col0 % LANES == 0
    first = col0 // LANES

    def u_spec(half):
        return pl.BlockSpec((seq, LANES), lambda b, g: (b, first + tiles_per_group * g + half))

    return pl.pallas_call(
        _pool_kernel,
        grid=(batch, n_groups),
        in_specs=[u_spec(0), u_spec(1),
                  pl.BlockSpec((1, C_GROUP_DIM, C_GROUP_DIM), lambda b, g: (g, 0, 0)),
                  pl.BlockSpec((1, C_GROUP_DIM), lambda b, g: (0, g))],
        out_specs=pl.BlockSpec((seq, C_GROUP_DIM), lambda b, g: (b, g)),
        out_shape=jax.ShapeDtypeStruct((batch * seq, C_WIDTH), BF16),
        compiler_params=_params("parallel", "parallel"),
        name="multiscale_pool",
    )(proj, proj, pool_w, pool_scale.reshape(1, C_WIDTH))


def _conv_kernel(gb_ref, gc_ref, h_ref, w_ref, o_ref):
    z = gc_ref[...] * h_ref[...]
    row = lax.broadcasted_iota(jnp.int32, z.shape, 0)
    w = w_ref[...]
    y = w[CONV_WIDTH - 1:CONV_WIDTH] * z
    for tap in range(1, CONV_WIDTH):
        y = y + w[CONV_WIDTH - 1 - tap:CONV_WIDTH - tap] * _shift_rows(z, tap, row)
    o_ref[...] = (gb_ref[...] * y).astype(o_ref.dtype)


def _short_conv(proj, col0, conv_w, batch, seq):
    n_col = D_WIDTH // LANES
    assert col0 % LANES == 0
    first = col0 // LANES

    def spec(part):
        return pl.BlockSpec((seq, LANES), lambda b, c: (b, first + part * n_col + c))

    return pl.pallas_call(
        _conv_kernel,
        grid=(batch, n_col),
        in_specs=[spec(0), spec(1), spec(2), pl.BlockSpec((CONV_WIDTH, LANES), lambda b, c: (0, c))],
        out_specs=pl.BlockSpec((seq, LANES), lambda b, c: (b, c)),
        out_shape=jax.ShapeDtypeStruct((batch * seq, D_WIDTH), BF16),
        compiler_params=_params("parallel", "parallel"),
        name="short_conv",
    )(proj, proj, proj, conv_w)


def _rope_tables(seq, dim):
    half = dim // 2
    inv = ROPE_THETA ** (-jnp.arange(0, dim, 2, dtype=F32) / dim)
    ang = jnp.arange(seq, dtype=F32)[:, None] * inv[None, :]
    cos, sin = jnp.cos(ang), jnp.sin(ang)
    reps = LANES // dim
    cos_t = jnp.tile(jnp.concatenate([cos, cos], axis=1), (1, reps))
    zeros = jnp.zeros_like(sin)
    if dim == LANES:
        return (cos_t, jnp.concatenate([-sin, sin], axis=1)), (half,)
    upper = jnp.tile(jnp.concatenate([zeros, sin], axis=1), (1, reps))
    lower = jnp.tile(jnp.concatenate([-sin, zeros], axis=1), (1, reps))
    return (cos_t, upper, lower), (half, LANES - half)


def kernel(x, w_in, w_out, norm_mix, norm_mlp, diff_lambda, diff_subln, pool_w, pool_scale, conv_w,
           w_up, w_down, norm_final):
    batch, seq, d_model = x.shape
    depth = w_in.shape[0]
    m = batch * seq
    x = x.reshape(m, d_model)

    tables_a, shifts_a = _rope_tables(seq, A_SUB)
    tables_b, shifts_b = _rope_tables(seq, HEAD_DIM)

    a_qk_end = 2 * A_WIDTH
    a_end = 3 * A_WIDTH
    b_end = a_end + 3 * B_WIDTH
    c_end = b_end + C_WIDTH
    t_a = 512
    tail0 = (b_end // IN_PROJ_COL_TILE) * IN_PROJ_COL_TILE
    in_width = w_in.shape[2]

    w_in = w_in.astype(BF16)
    w_a_vt = jnp.swapaxes(w_in[:, :, a_qk_end:a_end], 1, 2)

    for l in range(depth):
        lam_init = 0.8 - 0.6 * math.exp(-0.3 * l)

        h = _rmsnorm(x, norm_mix[l], BF16)
        qk_a = _mm_rope(h, w_in, l, a_qk_end, tables_a, shifts_a, A_SCORE_SCALE, A_WIDTH, seq)
        vt_a = _mm_transposed(h, w_a_vt, l, t_a)
        qkv_b = [_mm_rope_permute(h, w_in, l, a_end, g, tables_b, shifts_b[0], B_GROUPS[g][1], batch, seq)
                 for g in range(len(B_GROUPS))]
        tail = _mm_cols(h, w_in, l, tail0, in_width - tail0, F32, name="in_proj_tail")

        out_a = _attn_a(qk_a, vt_a, diff_lambda[l], diff_subln[l], lam_init, batch, seq, t_a)
        out_b = _attn_b(qkv_b, batch, seq)
        out_c = _pool(tail, b_end - tail0, pool_w[l].astype(BF16), pool_scale[l], batch, seq)
        out_d = _short_conv(tail, c_end - tail0, conv_w[l], batch, seq)

        x = _out_proj(out_a, out_b, out_c, out_d, w_out, l, x)

        h = _rmsnorm(x, norm_mlp[l], BF16)
        act = _mm_layer_tiled(h, w_up, l, BF16, 512, relu_sq=True, name="mlp_up")
        x = _mm_residual(act, w_down, l, x)

    out = _rmsnorm(x, norm_final, F32)
    return out.reshape(batch, seq, d_model)
```

```python
import functools
import math

import jax
import jax.numpy as jnp
from jax import lax
from jax.experimental import pallas as pl
from jax.experimental.pallas import tpu as pltpu

D_MODEL = 4096
HEAD_DIM = 128
A_HEADS = 8
A_WIDTH = A_HEADS * HEAD_DIM
A_SUB = HEAD_DIM // 2
A_SCORE_SCALE = A_SUB ** -0.5 * math.log2(math.e)
A_ONES_ROWS = 16
B_GROUPS = ((128, 1), (512, 4), (2048, 16))
B_HEADS_PER_GROUP = 3
B_GROUP_WIDTH = B_HEADS_PER_GROUP * HEAD_DIM
B_WIDTH = B_GROUP_WIDTH * len(B_GROUPS)
B_TOKEN_TILE = 2048
B_QUERY_CHUNK = 256
C_WINDOWS = (2, 4, 8, 16)
C_WIDTH = 1024
C_GROUP_DIM = C_WIDTH // len(C_WINDOWS)
D_WIDTH = D_MODEL - A_WIDTH - B_WIDTH - C_WIDTH
CONV_WIDTH = 3
D_FF = 4 * D_MODEL
ROPE_THETA = 10000.0
NORM_EPS = 1e-6
DIFF_EPS = 1e-5

LANES = 128
MXU_COLS = 256
MLP_ROW_TILE = 2048
IN_PROJ_COL_TILE = 1024
IN_PROJ_ROW_TILE = 1024
VMEM_LIMIT_BYTES = 56 * 1024 * 1024
MASKED_SCORE = -1e30

F32 = jnp.float32
BF16 = jnp.bfloat16


def _params(*semantics):
    return pltpu.CompilerParams(dimension_semantics=semantics, vmem_limit_bytes=VMEM_LIMIT_BYTES)


def _rmsnorm_kernel(x_ref, g_ref, o_ref, *, eps):
    x = x_ref[...]
    ms = jnp.mean(x * x, axis=-1, keepdims=True)
    o_ref[...] = (x * lax.rsqrt(ms + eps) * g_ref[...]).astype(o_ref.dtype)


def _rmsnorm(x, g, out_dtype, tm=256):
    m, d = x.shape
    return pl.pallas_call(
        functools.partial(_rmsnorm_kernel, eps=NORM_EPS),
        grid=(m // tm,),
        in_specs=[pl.BlockSpec((tm, d), lambda i: (i, 0)),
                  pl.BlockSpec((1, d), lambda i: (0, 0))],
        out_specs=pl.BlockSpec((tm, d), lambda i: (i, 0)),
        out_shape=jax.ShapeDtypeStruct((m, d), out_dtype),
        compiler_params=_params("parallel"),
        name="rmsnorm",
    )(x, g.reshape(1, d))


def _prep_kernel(x_ref, g_ref, xb_ref, ss_ref):
    x = x_ref[...]
    xb_ref[...] = (x * g_ref[...]).astype(xb_ref.dtype)
    ss_ref[...] = jnp.broadcast_to(jnp.sum(x * x, axis=-1, keepdims=True), ss_ref.shape)


def _prep(x, g, tm=256):
    m, d = x.shape
    return pl.pallas_call(
        _prep_kernel,
        grid=(m // tm,),
        in_specs=[pl.BlockSpec((tm, d), lambda i: (i, 0)),
                  pl.BlockSpec((1, d), lambda i: (0, 0))],
        out_specs=[pl.BlockSpec((tm, d), lambda i: (i, 0)),
                   pl.BlockSpec((tm, LANES), lambda i: (i, 0))],
        out_shape=[jax.ShapeDtypeStruct((m, d), BF16), jax.ShapeDtypeStruct((m, LANES), F32)],
        compiler_params=_params("parallel"),
        name="norm_prep",
    )(x, g.reshape(1, d))


def _row_scale(ss_ref):
    return lax.rsqrt(ss_ref[...] * (1.0 / D_MODEL) + NORM_EPS)


def _scale_rows(y, rs):
    return y * jnp.concatenate([rs] * (y.shape[1] // LANES), axis=1)


def _store_norm_inputs(x_new, g_ref, xb_ref, ss_ref, first):
    xb_ref[...] = (x_new * g_ref[...]).astype(xb_ref.dtype)
    part = jnp.broadcast_to(jnp.sum(x_new * x_new, axis=-1, keepdims=True), ss_ref.shape)

    @pl.when(first)
    def _():
        ss_ref[...] = part

    @pl.when(jnp.logical_not(first))
    def _():
        ss_ref[...] += part


def _rotate(y, cos_ref, sin_refs, shifts):
    outs = []
    for c in range(y.shape[1] // LANES):
        yc = y[:, c * LANES:(c + 1) * LANES]
        oc = yc * cos_ref[...]
        for s_ref, shift in zip(sin_refs, shifts):
            oc = oc + pltpu.roll(yc, shift, 1) * s_ref[...]
        outs.append(oc)
    return jnp.concatenate(outs, axis=1)


def _mm_rope_kernel(a_ref, ss_ref, w_ref, cos_ref, *rest, shifts, q_scale, n_q_blocks):
    sin_refs, o_ref = rest[:-1], rest[-1]
    scale = jnp.where(pl.program_id(1) < n_q_blocks, q_scale, 1.0).astype(F32)
    a = a_ref[...]
    rs = _row_scale(ss_ref)
    for c in range(o_ref.shape[1] // MXU_COLS):
        cols = slice(c * MXU_COLS, (c + 1) * MXU_COLS)
        acc = _scale_rows(jnp.dot(a, w_ref[:, cols].astype(BF16), preferred_element_type=F32), rs)
        o_ref[:, cols] = (_rotate(acc, cos_ref, sin_refs, shifts) * scale).astype(o_ref.dtype)


def _mm_rope(a, ss, w, layer, n, tables, shifts, q_scale, n_q_cols, seq, bn=IN_PROJ_COL_TILE,
             bm=IN_PROJ_ROW_TILE):
    m, k = a.shape
    tab_blocks = seq // bm
    tab_spec = pl.BlockSpec((bm, LANES), lambda i, j: (i % tab_blocks, 0))
    return pl.pallas_call(
        functools.partial(_mm_rope_kernel, shifts=shifts, q_scale=q_scale, n_q_blocks=n_q_cols // bn),
        grid=(m // bm, n // bn),
        in_specs=[pl.BlockSpec((bm, k), lambda i, j: (i, 0)),
                  pl.BlockSpec((bm, LANES), lambda i, j: (i, 0)),
                  pl.BlockSpec((None, k, bn), lambda i, j: (layer, 0, j))] + [tab_spec] * len(tables),
        out_specs=pl.BlockSpec((bm, bn), lambda i, j: (i, j)),
        out_shape=jax.ShapeDtypeStruct((m, n), BF16),
        compiler_params=_params("parallel", "arbitrary"),
        name="in_proj_rope",
    )(a, ss, w, *tables)


def _mm_rope_permute_kernel(a_ref, ss_ref, wq_ref, wk_ref, wv_ref, cos_ref, sin_ref, o_ref, scr_ref, *,
                            shift, n_rope_cols, dilation):
    a = a_ref[...]
    w = jnp.concatenate([wq_ref[...], wk_ref[...], wv_ref[...]], axis=1)
    rs = _row_scale(ss_ref)
    n = o_ref.shape[2]
    rows = a.shape[0] // dilation
    for lo in range(0, n, 2 * MXU_COLS):
        hi = min(lo + 2 * MXU_COLS, n)
        acc = _scale_rows(jnp.dot(a, w[:, lo:hi], preferred_element_type=F32), rs)
        rope_hi = min(hi, n_rope_cols)
        if lo < rope_hi:
            rot = _rotate(acc[:, :rope_hi - lo], cos_ref, (sin_ref,), (shift,))
            acc = rot if rope_hi == hi else jnp.concatenate([rot, acc[:, rope_hi - lo:]], axis=1)
        if dilation == 1:
            o_ref[0, :, lo:hi] = acc.astype(o_ref.dtype)
        else:
            for c in range(lo // LANES, hi // LANES):
                cols = slice(c * LANES, (c + 1) * LANES)
                scr_ref[c] = acc[:, c * LANES - lo:(c + 1) * LANES - lo]
                for r in range(dilation):
                    o_ref[r, :, cols] = scr_ref[c, pl.ds(r, rows, stride=dilation), :].astype(o_ref.dtype)


def _mm_rope_permute(a, ss, w, layer, col0, group, tables, shift, dilation, batch, seq, bm=IN_PROJ_ROW_TILE):
    m, k = a.shape
    n = 3 * B_GROUP_WIDTH
    n_rope_cols = 2 * B_GROUP_WIDTH
    tiles = seq // bm
    tab_spec = pl.BlockSpec((bm, LANES), lambda i: (i % tiles, 0))
    n_groups = len(B_GROUPS)
    first = col0 // B_GROUP_WIDTH
    assert first * B_GROUP_WIDTH == col0

    def w_spec(part):
        return pl.BlockSpec((None, k, B_GROUP_WIDTH), lambda i: (layer, 0, first + part * n_groups + group),
                            pipeline_mode=pl.Buffered(1))

    return pl.pallas_call(
        functools.partial(_mm_rope_permute_kernel, shift=shift, n_rope_cols=n_rope_cols, dilation=dilation),
        grid=(m // bm,),
        in_specs=[pl.BlockSpec((bm, k), lambda i: (i, 0)), pl.BlockSpec((bm, LANES), lambda i: (i, 0)),
                  w_spec(0), w_spec(1), w_spec(2), tab_spec, tab_spec],
        out_specs=pl.BlockSpec((None, dilation, bm // dilation, n), lambda i: (i // tiles, 0, i % tiles, 0)),
        out_shape=jax.ShapeDtypeStruct((batch, dilation, seq // dilation, n), BF16),
        scratch_shapes=[pltpu.VMEM((n // LANES, bm, LANES), F32)],
        compiler_params=_params("parallel"),
        name="in_proj_dilated",
    )(a, ss, w, w, w, *tables)


def _mm_kernel(a_ref, ss_ref, w_ref, o_ref, *, relu_sq):
    acc = jnp.dot(a_ref[...], w_ref[...].astype(BF16), preferred_element_type=F32)
    acc = _scale_rows(acc, _row_scale(ss_ref))
    if relu_sq:
        acc = jnp.square(jnp.maximum(acc, 0.0))
    o_ref[...] = acc.astype(o_ref.dtype)


def _mm_cols(a, ss, w, layer, col0, n, out_dtype, bn=IN_PROJ_COL_TILE, bm=IN_PROJ_ROW_TILE, name="matmul"):
    m, k = a.shape
    first = col0 // bn
    assert first * bn == col0 and n % bn == 0
    return pl.pallas_call(
        functools.partial(_mm_kernel, relu_sq=False),
        grid=(m // bm, n // bn),
        in_specs=[pl.BlockSpec((bm, k), lambda i, j: (i, 0)),
                  pl.BlockSpec((bm, LANES), lambda i, j: (i, 0)),
                  pl.BlockSpec((None, k, bn), lambda i, j: (layer, 0, first + j))],
        out_specs=pl.BlockSpec((bm, bn), lambda i, j: (i, j)),
        out_shape=jax.ShapeDtypeStruct((m, n), out_dtype),
        compiler_params=_params("parallel", "arbitrary"),
        name=name,
    )(a, ss, w)


def _mm_layer_tiled(a, ss, w, layer, out_dtype, bn, relu_sq=False, bm=MLP_ROW_TILE, name="matmul"):
    m, k = a.shape
    n = w.shape[2]
    return pl.pallas_call(
        functools.partial(_mm_kernel, relu_sq=relu_sq),
        grid=(m // bm, n // bn),
        in_specs=[pl.BlockSpec((bm, k), lambda i, j: (i, 0), pipeline_mode=pl.Buffered(1)),
                  pl.BlockSpec((bm, LANES), lambda i, j: (i, 0)),
                  pl.BlockSpec((None, k, bn), lambda i, j: (layer, 0, j))],
        out_specs=pl.BlockSpec((None, None, bm, bn), lambda i, j: (i, j, 0, 0)),
        out_shape=jax.ShapeDtypeStruct((m // bm, n // bn, bm, bn), out_dtype),
        compiler_params=_params("parallel", "arbitrary"),
        name=name,
    )(a, ss, w)


def _mm_transposed_kernel(wt_ref, a_ref, ss_ref, o_ref, *, t):
    acc = lax.dot_general(wt_ref[...], a_ref[...], (((1,), (1,)), ((), ())), preferred_element_type=F32)
    acc = acc * jnp.transpose(_row_scale(ss_ref))[0:1, :]
    for c in range(acc.shape[1] // t):
        o_ref[c] = acc[:, c * t:(c + 1) * t].astype(o_ref.dtype)


def _mm_transposed(a, ss, wt, layer, t, bm=1024):
    m, k = a.shape
    n = wt.shape[1]
    return pl.pallas_call(
        functools.partial(_mm_transposed_kernel, t=t),
        grid=(m // bm,),
        in_specs=[pl.BlockSpec((None, n, k), lambda i: (layer, 0, 0), pipeline_mode=pl.Buffered(1)),
                  pl.BlockSpec((bm, k), lambda i: (i, 0)),
                  pl.BlockSpec((bm, LANES), lambda i: (i, 0))],
        out_specs=pl.BlockSpec((bm // t, n, t), lambda i: (i, 0, 0)),
        out_shape=jax.ShapeDtypeStruct((m // t, n, t), BF16),
        compiler_params=_params("parallel"),
        name="in_proj_vt",
    )(wt, a, ss)


def _mm_residual_kernel(a_ref, w_ref, r_ref, *rest):
    o_ref = rest[0] if len(rest) == 1 else rest[1]
    kk = pl.program_id(2)

    @pl.when(kk == 0)
    def _():
        o_ref[...] = r_ref[...]

    a = jnp.concatenate([a_ref[c] for c in range(a_ref.shape[0])], axis=1)
    o_ref[...] += jnp.dot(a, w_ref[...].astype(BF16), preferred_element_type=F32)

    if len(rest) > 1:
        g_ref, _, xb_ref, ss_ref = rest

        @pl.when(kk == pl.num_programs(2) - 1)
        def _():
            _store_norm_inputs(o_ref[...], g_ref, xb_ref, ss_ref, pl.program_id(1) == 0)


def _mm_residual(a, w, layer, r, g=None, bn=1024, bk=1024):
    row_tiles, col_tiles, bm, tile_cols = a.shape
    m, k = row_tiles * bm, col_tiles * tile_cols
    n = w.shape[2]
    tile_spec = pl.BlockSpec((bm, bn), lambda i, j, kk: (i, j))
    in_specs = [pl.BlockSpec((None, bk // tile_cols, bm, tile_cols), lambda i, j, kk: (i, kk, 0, 0)),
                pl.BlockSpec((None, bk, bn), lambda i, j, kk: (layer, kk, j)),
                pl.BlockSpec((bm, bn), lambda i, j, kk: (i, j), pipeline_mode=pl.Buffered(1))]
    out_specs = [tile_spec]
    out_shape = [jax.ShapeDtypeStruct((m, n), F32)]
    operands = [a, w, r]
    if g is not None:
        in_specs.append(pl.BlockSpec((1, bn), lambda i, j, kk: (0, j)))
        out_specs += [tile_spec, pl.BlockSpec((bm, LANES), lambda i, j, kk: (i, 0))]
        out_shape += [jax.ShapeDtypeStruct((m, n), BF16), jax.ShapeDtypeStruct((m, LANES), F32)]
        operands.append(g.reshape(1, n))
    return pl.pallas_call(
        _mm_residual_kernel,
        grid=(m // bm, n // bn, k // bk),
        in_specs=in_specs,
        out_specs=out_specs,
        out_shape=out_shape,
        compiler_params=_params("parallel", "arbitrary", "arbitrary"),
        name="mlp_down",
    )(*operands)


def _out_proj_kernel(a_ref, b_ref, c_ref, d_ref, w_ref, r_ref, g_ref, o_ref, xb_ref, ss_ref):
    mix = jnp.concatenate([a_ref[...]] + [b_ref[g] for g in range(len(B_GROUPS))] + [c_ref[...], d_ref[...]], axis=1)
    x_new = r_ref[...] + jnp.dot(mix, w_ref[...].astype(BF16), preferred_element_type=F32)
    o_ref[...] = x_new
    _store_norm_inputs(x_new, g_ref, xb_ref, ss_ref, pl.program_id(1) == 0)


def _out_proj(out_a, out_b, out_c, out_d, w, layer, r, g, bm=1024, bn=512):
    m = out_a.shape[0]
    n = w.shape[2]
    n_groups = len(B_GROUPS)
    return pl.pallas_call(
        _out_proj_kernel,
        grid=(m // bm, n // bn),
        in_specs=[pl.BlockSpec((bm, A_WIDTH), lambda i, j: (i, 0)),
                  pl.BlockSpec((n_groups, bm, B_GROUP_WIDTH), lambda i, j: (0, i, 0)),
                  pl.BlockSpec((bm, C_WIDTH), lambda i, j: (i, 0)),
                  pl.BlockSpec((bm, D_WIDTH), lambda i, j: (i, 0)),
                  pl.BlockSpec((None, D_MODEL, bn), lambda i, j: (layer, 0, j)),
                  pl.BlockSpec((bm, bn), lambda i, j: (i, j)),
                  pl.BlockSpec((1, bn), lambda i, j: (0, j))],
        out_specs=[pl.BlockSpec((bm, bn), lambda i, j: (i, j)),
                   pl.BlockSpec((bm, bn), lambda i, j: (i, j)),
                   pl.BlockSpec((bm, LANES), lambda i, j: (i, 0))],
        out_shape=[jax.ShapeDtypeStruct((m, n), F32), jax.ShapeDtypeStruct((m, n), BF16),
                   jax.ShapeDtypeStruct((m, LANES), F32)],
        compiler_params=_params("parallel", "arbitrary"),
        name="out_proj",
    )(out_a, out_b, out_c, out_d, w, r, g.reshape(1, n))


def _attn_a_kernel(q_ref, k_ref, vt_ref, lam_ref, g_ref, o_ref, *scratch, t, heads, lam_init):
    qi = pl.program_id(2)
    lane = lax.broadcasted_iota(jnp.int32, (t, HEAD_DIM), 1)
    qqs = []
    for h in range(heads):
        q = q_ref[:, h * HEAD_DIM:(h + 1) * HEAD_DIM]
        zero = jnp.zeros_like(q)
        qqs.append(jnp.concatenate([jnp.where(lane < A_SUB, q, zero), jnp.where(lane >= A_SUB, q, zero)], axis=0))

    ones = jnp.ones((A_ONES_ROWS, t), BF16)
    m_scr, acc_scr = scratch[:heads], scratch[heads:]
    for h in range(heads):
        m_scr[h][...] = jnp.full(m_scr[h].shape, MASKED_SCORE, F32)
        acc_scr[h][...] = jnp.zeros(acc_scr[h].shape, F32)

    def step(j, masked):
        def scores(h):
            k = k_ref[pl.ds(pl.multiple_of(j * t, t), t), h * HEAD_DIM:(h + 1) * HEAD_DIM]
            s = lax.dot_general(k, qqs[h], (((1,), (1,)), ((), ())), preferred_element_type=F32)
            if masked:
                key = lax.broadcasted_iota(jnp.int32, s.shape, 0)
                lane_q = lax.broadcasted_iota(jnp.int32, s.shape, 1)
                qry = jnp.where(lane_q >= t, lane_q - t, lane_q)
                s = jnp.where(key <= qry, s, MASKED_SCORE)
            return s

        s_next = scores(0)
        for h in range(heads):
            s = s_next
            if h + 1 < heads:
                s_next = scores(h + 1)
            vt = jnp.concatenate([vt_ref[j, h * HEAD_DIM:(h + 1) * HEAD_DIM, :], ones], axis=0)
            m = m_scr[h][...]
            m_new = jnp.maximum(m, jnp.max(s, axis=0, keepdims=True))
            alpha = jnp.exp2(m - m_new)
            p = jnp.exp2(s - m_new)
            m_scr[h][...] = m_new
            acc_scr[h][...] = alpha * acc_scr[h][...] + jnp.dot(vt, p.astype(BF16), preferred_element_type=F32)

    def body(j, carry):
        step(j, masked=False)
        return carry

    lax.fori_loop(0, qi, body, 0)
    step(qi, masked=True)

    lp = lam_ref[...]
    lam = (jnp.exp(jnp.sum(lp[0:1] * lp[1:2], axis=-1, keepdims=True))
           - jnp.exp(jnp.sum(lp[2:3] * lp[3:4], axis=-1, keepdims=True)) + lam_init)
    for h in range(heads):
        acc = acc_scr[h][...]
        o = acc[:HEAD_DIM] / acc[HEAD_DIM:HEAD_DIM + 1]
        o = (o[:, :t] - lam * o[:, t:]).T
        ms = jnp.mean(o * o, axis=-1, keepdims=True)
        o = o * lax.rsqrt(ms + DIFF_EPS) * g_ref[...]
        o_ref[:, h * HEAD_DIM:(h + 1) * HEAD_DIM] = (o * (1.0 - lam_init)).astype(o_ref.dtype)


def _attn_a(qk, vt, lam_params, subln_g, lam_init, batch, seq, t, heads=8):
    nq = seq // t
    width = heads * HEAD_DIM
    n_pairs = A_HEADS // heads
    return pl.pallas_call(
        functools.partial(_attn_a_kernel, t=t, heads=heads, lam_init=lam_init),
        grid=(batch, n_pairs, nq),
        in_specs=[pl.BlockSpec((t, width), lambda b, h, i: (b * nq + i, h)),
                  pl.BlockSpec((seq, width), lambda b, h, i: (b, n_pairs + h)),
                  pl.BlockSpec((nq, width, t), lambda b, h, i: (b, h, 0)),
                  pl.BlockSpec((4, A_SUB), lambda b, h, i: (0, 0)),
                  pl.BlockSpec((1, HEAD_DIM), lambda b, h, i: (0, 0))],
        out_specs=pl.BlockSpec((t, width), lambda b, h, i: (b * nq + i, h)),
        out_shape=jax.ShapeDtypeStruct((batch * seq, A_WIDTH), BF16),
        scratch_shapes=([pltpu.VMEM((1, 2 * t), F32)] * heads
                        + [pltpu.VMEM((HEAD_DIM + A_ONES_ROWS, 2 * t), F32)] * heads),
        compiler_params=_params("parallel", "parallel", "arbitrary"),
        name="diff_attention",
    )(qk, qk, vt, lam_params, subln_g.reshape(1, HEAD_DIM))


def _attn_b_kernel(*refs, tile):
    n_groups = len(B_GROUPS)
    q_refs = refs[0:n_groups]
    k_refs = refs[n_groups:2 * n_groups]
    v_refs = refs[2 * n_groups:3 * n_groups]
    o_ref = refs[3 * n_groups]
    out_scr, lse_scr = refs[3 * n_groups + 1:]
    ti = pl.program_id(2)
    scale = HEAD_DIM ** -0.5
    base_delta = {}
    chunks = []

    for g, (window, dilation) in enumerate(B_GROUPS):
        back = window // dilation
        seg = tile // dilation
        tq = min(B_QUERY_CHUNK, seg)
        tk = tq + back
        if tq not in base_delta:
            base_delta[tq] = (lax.broadcasted_iota(jnp.int32, (tq, tk), 0)
                              - lax.broadcasted_iota(jnp.int32, (tq, tk), 1))
        for r in range(dilation):
            for c in range(seg // tq):
                chunks.append((g, dilation, back, tq, tk, r, c * tq, ti * seg + c * tq))

    def scores(chunk):
        g, dilation, back, tq, tk, r, i_loc, i0 = chunk
        ks = pl.multiple_of(jnp.maximum(i0 - back, 0), back)
        q = q_refs[g][r, i_loc:i_loc + tq, :]
        k = k_refs[g][r, pl.ds(ks, tk), :]
        s = lax.dot_general(q, k, (((1,), (1,)), ((), ())), preferred_element_type=F32) * scale
        delta = base_delta[tq] + (i0 - ks)
        return jnp.where((delta >= 0) & (delta <= back), s, MASKED_SCORE), ks

    nxt = scores(chunks[0])
    for idx, (g, dilation, back, tq, tk, r, i_loc, i0) in enumerate(chunks):
        s, ks = nxt
        if idx + 1 < len(chunks):
            nxt = scores(chunks[idx + 1])
        v = v_refs[g][r, pl.ds(ks, tk), :]
        m = jnp.max(s, axis=-1, keepdims=True)
        p = jnp.exp(s - m)
        l = jnp.sum(p, axis=-1, keepdims=True)
        o = jnp.dot(p.astype(BF16), v, preferred_element_type=F32) / l
        lse = jnp.broadcast_to(m + jnp.log(l), o.shape)
        rows = pl.ds(dilation * i_loc + r, tq, stride=dilation) if dilation > 1 else pl.ds(i_loc, tq)
        out_scr[g, rows, :] = o
        lse_scr[g, rows, :] = lse

    for c in range(tile // B_QUERY_CHUNK):
        rows = pl.ds(c * B_QUERY_CHUNK, B_QUERY_CHUNK)
        lses = [lse_scr[g, rows, :] for g in range(n_groups)]
        lse_max = functools.reduce(jnp.maximum, lses)
        weights = [jnp.exp(x - lse_max) for x in lses]
        denom = functools.reduce(lambda a, b: a + b, weights)
        for g in range(n_groups):
            o_ref[g, rows, :] = (out_scr[g, rows, :] * (weights[g] / denom)).astype(o_ref.dtype)


def _attn_b(qkvs, batch, seq, tile=B_TOKEN_TILE):
    n_tiles = seq // tile
    n_groups = len(B_GROUPS)

    def q_spec(g):
        d = B_GROUPS[g][1]
        return pl.BlockSpec((None, d, tile // d, HEAD_DIM), lambda b, h, i: (b, 0, i, h))

    def kv_spec(g, part):
        d = B_GROUPS[g][1]
        return pl.BlockSpec((None, d, seq // d, HEAD_DIM),
                            lambda b, h, i: (b, 0, 0, part * B_HEADS_PER_GROUP + h))

    in_specs = ([q_spec(g) for g in range(n_groups)] + [kv_spec(g, 1) for g in range(n_groups)]
                + [kv_spec(g, 2) for g in range(n_groups)])
    return pl.pallas_call(
        functools.partial(_attn_b_kernel, tile=tile),
        grid=(batch, B_HEADS_PER_GROUP, n_tiles),
        in_specs=in_specs,
        out_specs=pl.BlockSpec((n_groups, tile, HEAD_DIM), lambda b, h, i: (0, b * n_tiles + i, h)),
        out_shape=jax.ShapeDtypeStruct((n_groups, batch * seq, B_GROUP_WIDTH), BF16),
        scratch_shapes=[pltpu.VMEM((n_groups, tile, HEAD_DIM), F32),
                        pltpu.VMEM((n_groups, tile, HEAD_DIM), F32)],
        compiler_params=_params("parallel", "parallel", "arbitrary"),
        name="dilated_attention",
    )(*(list(qkvs) * 3))


def _shift_rows(x, k, row):
    return jnp.where(row >= k, pltpu.roll(x, k, 0), 0.0)


def _pool_kernel(u_lo_ref, u_hi_ref, w_ref, scale_ref, o_ref):
    g = pl.program_id(1)
    u = jnp.concatenate([u_lo_ref[...], u_hi_ref[...]], axis=1)
    row = lax.broadcasted_iota(jnp.int32, u.shape, 0)
    window = jnp.left_shift(2, g)
    total = u
    step = 1
    while step < max(C_WINDOWS):
        widened = total + _shift_rows(total, step, row)
        total = jnp.where(step < window, widened, total)
        step *= 2
    count = jnp.minimum(row + 1, window).astype(F32)
    pooled = total / count - u
    y = jnp.dot(pooled.astype(BF16), w_ref[0], preferred_element_type=F32)
    o_ref[...] = (y * scale_ref[...]).astype(o_ref.dtype)


def _pool(proj, col0, pool_w, pool_scale, batch, seq):
    n_groups = len(C_WINDOWS)
    tiles_per_group = C_GROUP_DIM // LANES
    assert tiles_per_group == 2 and col0 % LANES == 0
    first = col0 // LANES

    def u_spec(half):
        return pl.BlockSpec((seq, LANES), lambda b, g: (b, first + tiles_per_group * g + half))

    return pl.pallas_call(
        _pool_kernel,
        grid=(batch, n_groups),
        in_specs=[u_spec(0), u_spec(1),
                  pl.BlockSpec((1, C_GROUP_DIM, C_GROUP_DIM), lambda b, g: (g, 0, 0)),
                  pl.BlockSpec((1, C_GROUP_DIM), lambda b, g: (0, g))],
        out_specs=pl.BlockSpec((seq, C_GROUP_DIM), lambda b, g: (b, g)),
        out_shape=jax.ShapeDtypeStruct((batch * seq, C_WIDTH), BF16),
        compiler_params=_params("parallel", "parallel"),
        name="multiscale_pool",
    )(proj, proj, pool_w, pool_scale.reshape(1, C_WIDTH))


def _conv_kernel(gb_ref, gc_ref, h_ref, w_ref, o_ref):
    z = gc_ref[...] * h_ref[...]
    row = lax.broadcasted_iota(jnp.int32, z.shape, 0)
    w = w_ref[...]
    y = w[CONV_WIDTH - 1:CONV_WIDTH] * z
    for tap in range(1, CONV_WIDTH):
        y = y + w[CONV_WIDTH - 1 - tap:CONV_WIDTH - tap] * _shift_rows(z, tap, row)
    o_ref[...] = (gb_ref[...] * y).astype(o_ref.dtype)


def _short_conv(proj, col0, conv_w, batch, seq):
    n_col = D_WIDTH // LANES
    assert col0 % LANES == 0
    first = col0 // LANES

    def spec(part):
        return pl.BlockSpec((seq, LANES), lambda b, c: (b, first + part * n_col + c))

    return pl.pallas_call(
        _conv_kernel,
        grid=(batch, n_col),
        in_specs=[spec(0), spec(1), spec(2), pl.BlockSpec((CONV_WIDTH, LANES), lambda b, c: (0, c))],
        out_specs=pl.BlockSpec((seq, LANES), lambda b, c: (b, c)),
        out_shape=jax.ShapeDtypeStruct((batch * seq, D_WIDTH), BF16),
        compiler_params=_params("parallel", "parallel"),
        name="short_conv",
    )(proj, proj, proj, conv_w)


def _rope_tables(seq, dim):
    half = dim // 2
    inv = ROPE_THETA ** (-jnp.arange(0, dim, 2, dtype=F32) / dim)
    ang = jnp.arange(seq, dtype=F32)[:, None] * inv[None, :]
    cos, sin = jnp.cos(ang), jnp.sin(ang)
    reps = LANES // dim
    cos_t = jnp.tile(jnp.concatenate([cos, cos], axis=1), (1, reps))
    zeros = jnp.zeros_like(sin)
    if dim == LANES:
        return (cos_t, jnp.concatenate([-sin, sin], axis=1)), (half,)
    upper = jnp.tile(jnp.concatenate([zeros, sin], axis=1), (1, reps))
    lower = jnp.tile(jnp.concatenate([-sin, zeros], axis=1), (1, reps))
    return (cos_t, upper, lower), (half, LANES - half)


def kernel(x, w_in, w_out, norm_mix, norm_mlp, diff_lambda, diff_subln, pool_w, pool_scale, conv_w,
           w_up, w_down, norm_final):
    batch, seq, d_model = x.shape
    depth = w_in.shape[0]
    m = batch * seq
    x = x.reshape(m, d_model)

    tables_a, shifts_a = _rope_tables(seq, A_SUB)
    tables_b, shifts_b = _rope_tables(seq, HEAD_DIM)

    a_qk_end = 2 * A_WIDTH
    a_end = 3 * A_WIDTH
    b_end = a_end + 3 * B_WIDTH
    c_end = b_end + C_WIDTH
    t_a = 512
    tail0 = (b_end // IN_PROJ_COL_TILE) * IN_PROJ_COL_TILE
    in_width = w_in.shape[2]

    w_in = w_in.astype(BF16)
    w_a_vt = jnp.swapaxes(w_in[:, :, a_qk_end:a_end], 1, 2)

    h, ss = _prep(x, norm_mix[0])
    for l in range(depth):
        lam_init = 0.8 - 0.6 * math.exp(-0.3 * l)

        qk_a = _mm_rope(h, ss, w_in, l, a_qk_end, tables_a, shifts_a, A_SCORE_SCALE, A_WIDTH, seq)
        vt_a = _mm_transposed(h, ss, w_a_vt, l, t_a)
        qkv_b = [_mm_rope_permute(h, ss, w_in, l, a_end, g, tables_b, shifts_b[0], B_GROUPS[g][1], batch, seq)
                 for g in range(len(B_GROUPS))]
        tail = _mm_cols(h, ss, w_in, l, tail0, in_width - tail0, F32, name="in_proj_tail")

        out_a = _attn_a(qk_a, vt_a, diff_lambda[l], diff_subln[l], lam_init, batch, seq, t_a)
        out_b = _attn_b(qkv_b, batch, seq)
        out_c = _pool(tail, b_end - tail0, pool_w[l].astype(BF16), pool_scale[l], batch, seq)
        out_d = _short_conv(tail, c_end - tail0, conv_w[l], batch, seq)

        x, h, ss = _out_proj(out_a, out_b, out_c, out_d, w_out, l, x, norm_mlp[l])
        act = _mm_layer_tiled(h, ss, w_up, l, BF16, 512, relu_sq=True, name="mlp_up")
        if l + 1 < depth:
            x, h, ss = _mm_residual(act, w_down, l, x, norm_mix[l + 1])
        else:
            x, = _mm_residual(act, w_down, l, x)

    out = _rmsnorm(x, norm_final, F32)
    return out.reshape(batch, seq, d_model)
```

```python
import functools
import math

import jax
import jax.numpy as jnp
from jax import lax
from jax.experimental import pallas as pl
from jax.experimental.pallas import tpu as pltpu

D_MODEL = 4096
HEAD_DIM = 128
A_HEADS = 8
A_WIDTH = A_HEADS * HEAD_DIM
A_SUB = HEAD_DIM // 2
A_SCORE_SCALE = A_SUB ** -0.5 * math.log2(math.e)
A_ONES_ROWS = 16
B_GROUPS = ((128, 1), (512, 4), (2048, 16))
B_HEADS_PER_GROUP = 3
B_GROUP_WIDTH = B_HEADS_PER_GROUP * HEAD_DIM
B_WIDTH = B_GROUP_WIDTH * len(B_GROUPS)
B_TOKEN_TILE = 2048
B_QUERY_CHUNK = 256
C_WINDOWS = (2, 4, 8, 16)
C_WIDTH = 1024
C_GROUP_DIM = C_WIDTH // len(C_WINDOWS)
D_WIDTH = D_MODEL - A_WIDTH - B_WIDTH - C_WIDTH
CONV_WIDTH = 3
D_FF = 4 * D_MODEL
ROPE_THETA = 10000.0
NORM_EPS = 1e-6
DIFF_EPS = 1e-5

LANES = 128
MXU_COLS = 256
MLP_ROW_TILE = 2048
IN_PROJ_COL_TILE = 1024
IN_PROJ_ROW_TILE = 1024
VMEM_LIMIT_BYTES = 56 * 1024 * 1024
MASKED_SCORE = -1e30

F32 = jnp.float32
BF16 = jnp.bfloat16


def _params(*semantics):
    return pltpu.CompilerParams(dimension_semantics=semantics, vmem_limit_bytes=VMEM_LIMIT_BYTES)


def _rmsnorm_kernel(x_ref, g_ref, o_ref, *, eps):
    x = x_ref[...]
    ms = jnp.mean(x * x, axis=-1, keepdims=True)
    o_ref[...] = (x * lax.rsqrt(ms + eps) * g_ref[...]).astype(o_ref.dtype)


def _rmsnorm(x, g, out_dtype, tm=256):
    m, d = x.shape
    return pl.pallas_call(
        functools.partial(_rmsnorm_kernel, eps=NORM_EPS),
        grid=(m // tm,),
        in_specs=[pl.BlockSpec((tm, d), lambda i: (i, 0)),
                  pl.BlockSpec((1, d), lambda i: (0, 0))],
        out_specs=pl.BlockSpec((tm, d), lambda i: (i, 0)),
        out_shape=jax.ShapeDtypeStruct((m, d), out_dtype),
        compiler_params=_params("parallel"),
        name="rmsnorm",
    )(x, g.reshape(1, d))


def _prep_kernel(x_ref, g_ref, xb_ref, ss_ref):
    x = x_ref[...]
    xb_ref[...] = (x * g_ref[...]).astype(xb_ref.dtype)
    ss_ref[...] = jnp.broadcast_to(jnp.sum(x * x, axis=-1, keepdims=True), ss_ref.shape)


def _prep(x, g, tm=256):
    m, d = x.shape
    return pl.pallas_call(
        _prep_kernel,
        grid=(m // tm,),
        in_specs=[pl.BlockSpec((tm, d), lambda i: (i, 0)),
                  pl.BlockSpec((1, d), lambda i: (0, 0))],
        out_specs=[pl.BlockSpec((tm, d), lambda i: (i, 0)),
                   pl.BlockSpec((tm, LANES), lambda i: (i, 0))],
        out_shape=[jax.ShapeDtypeStruct((m, d), BF16), jax.ShapeDtypeStruct((m, LANES), F32)],
        compiler_params=_params("parallel"),
        name="norm_prep",
    )(x, g.reshape(1, d))


def _row_scale(ss_ref):
    return lax.rsqrt(ss_ref[...] * (1.0 / D_MODEL) + NORM_EPS)


def _scale_rows(y, rs):
    return y * jnp.concatenate([rs] * (y.shape[1] // LANES), axis=1)


def _store_norm_inputs(x_new, g_ref, xb_ref, ss_ref, first):
    xb_ref[...] = (x_new * g_ref[...]).astype(xb_ref.dtype)
    part = jnp.broadcast_to(jnp.sum(x_new * x_new, axis=-1, keepdims=True), ss_ref.shape)

    @pl.when(first)
    def _():
        ss_ref[...] = part

    @pl.when(jnp.logical_not(first))
    def _():
        ss_ref[...] += part


def _rotate(y, cos_ref, sin_refs, shifts):
    outs = []
    for c in range(y.shape[1] // LANES):
        yc = y[:, c * LANES:(c + 1) * LANES]
        oc = yc * cos_ref[...]
        for s_ref, shift in zip(sin_refs, shifts):
            oc = oc + pltpu.roll(yc, shift, 1) * s_ref[...]
        outs.append(oc)
    return jnp.concatenate(outs, axis=1)


def _mm_rope_kernel(a_ref, ss_ref, w_ref, cos_ref, *rest, shifts, q_scale, n_q_blocks):
    sin_refs, o_ref = rest[:-1], rest[-1]
    scale = jnp.where(pl.program_id(1) < n_q_blocks, q_scale, 1.0).astype(F32)
    a = a_ref[...]
    rs = _row_scale(ss_ref)
    for c in range(o_ref.shape[1] // MXU_COLS):
        cols = slice(c * MXU_COLS, (c + 1) * MXU_COLS)
        acc = _scale_rows(jnp.dot(a, w_ref[:, cols].astype(BF16), preferred_element_type=F32), rs)
        o_ref[:, cols] = (_rotate(acc, cos_ref, sin_refs, shifts) * scale).astype(o_ref.dtype)


def _mm_rope(a, ss, w, layer, n, tables, shifts, q_scale, n_q_cols, seq, bn=IN_PROJ_COL_TILE,
             bm=IN_PROJ_ROW_TILE):
    m, k = a.shape
    tab_blocks = seq // bm
    tab_spec = pl.BlockSpec((bm, LANES), lambda i, j: (i % tab_blocks, 0))
    return pl.pallas_call(
        functools.partial(_mm_rope_kernel, shifts=shifts, q_scale=q_scale, n_q_blocks=n_q_cols // bn),
        grid=(m // bm, n // bn),
        in_specs=[pl.BlockSpec((bm, k), lambda i, j: (i, 0)),
                  pl.BlockSpec((bm, LANES), lambda i, j: (i, 0)),
                  pl.BlockSpec((None, k, bn), lambda i, j: (layer, 0, j))] + [tab_spec] * len(tables),
        out_specs=pl.BlockSpec((bm, bn), lambda i, j: (i, j)),
        out_shape=jax.ShapeDtypeStruct((m, n), BF16),
        compiler_params=_params("parallel", "arbitrary"),
        name="in_proj_rope",
    )(a, ss, w, *tables)


def _mm_rope_permute_kernel(a_ref, ss_ref, wq_ref, wk_ref, wv_ref, cos_ref, sin_ref, o_ref, scr_ref, *,
                            shift, n_rope_cols, dilation):
    a = a_ref[...]
    w = jnp.concatenate([wq_ref[...], wk_ref[...], wv_ref[...]], axis=1)
    rs = _row_scale(ss_ref)
    n = o_ref.shape[2]
    rows = a.shape[0] // dilation
    for lo in range(0, n, 2 * MXU_COLS):
        hi = min(lo + 2 * MXU_COLS, n)
        acc = _scale_rows(jnp.dot(a, w[:, lo:hi], preferred_element_type=F32), rs)
        rope_hi = min(hi, n_rope_cols)
        if lo < rope_hi:
            rot = _rotate(acc[:, :rope_hi - lo], cos_ref, (sin_ref,), (shift,))
            acc = rot if rope_hi == hi else jnp.concatenate([rot, acc[:, rope_hi - lo:]], axis=1)
        if dilation == 1:
            o_ref[0, :, lo:hi] = acc.astype(o_ref.dtype)
        else:
            for c in range(lo // LANES, hi // LANES):
                cols = slice(c * LANES, (c + 1) * LANES)
                scr_ref[c] = acc[:, c * LANES - lo:(c + 1) * LANES - lo]
                for r in range(dilation):
                    o_ref[r, :, cols] = scr_ref[c, pl.ds(r, rows, stride=dilation), :].astype(o_ref.dtype)


def _mm_rope_permute(a, ss, w, layer, col0, group, tables, shift, dilation, batch, seq, bm=IN_PROJ_ROW_TILE):
    m, k = a.shape
    n = 3 * B_GROUP_WIDTH
    n_rope_cols = 2 * B_GROUP_WIDTH
    tiles = seq // bm
    tab_spec = pl.BlockSpec((bm, LANES), lambda i: (i % tiles, 0))
    n_groups = len(B_GROUPS)
    first = col0 // B_GROUP_WIDTH
    assert first * B_GROUP_WIDTH == col0

    def w_spec(part):
        return pl.BlockSpec((None, k, B_GROUP_WIDTH), lambda i: (layer, 0, first + part * n_groups + group),
                            pipeline_mode=pl.Buffered(1))

    return pl.pallas_call(
        functools.partial(_mm_rope_permute_kernel, shift=shift, n_rope_cols=n_rope_cols, dilation=dilation),
        grid=(m // bm,),
        in_specs=[pl.BlockSpec((bm, k), lambda i: (i, 0)), pl.BlockSpec((bm, LANES), lambda i: (i, 0)),
                  w_spec(0), w_spec(1), w_spec(2), tab_spec, tab_spec],
        out_specs=pl.BlockSpec((None, dilation, bm // dilation, n), lambda i: (i // tiles, 0, i % tiles, 0)),
        out_shape=jax.ShapeDtypeStruct((batch, dilation, seq // dilation, n), BF16),
        scratch_shapes=[pltpu.VMEM((n // LANES, bm, LANES), F32)],
        compiler_params=_params("parallel"),
        name="in_proj_dilated",
    )(a, ss, w, w, w, *tables)


def _mm_kernel(a_ref, ss_ref, w_ref, o_ref, *, relu_sq):
    acc = jnp.dot(a_ref[...], w_ref[...].astype(BF16), preferred_element_type=F32)
    acc = _scale_rows(acc, _row_scale(ss_ref))
    if relu_sq:
        acc = jnp.square(jnp.maximum(acc, 0.0))
    o_ref[...] = acc.astype(o_ref.dtype)


def _mm_cols(a, ss, w, layer, col0, n, out_dtype, bn=IN_PROJ_COL_TILE, bm=IN_PROJ_ROW_TILE, name="matmul"):
    m, k = a.shape
    first = col0 // bn
    assert first * bn == col0 and n % bn == 0
    return pl.pallas_call(
        functools.partial(_mm_kernel, relu_sq=False),
        grid=(m // bm, n // bn),
        in_specs=[pl.BlockSpec((bm, k), lambda i, j: (i, 0)),
                  pl.BlockSpec((bm, LANES), lambda i, j: (i, 0)),
                  pl.BlockSpec((None, k, bn), lambda i, j: (layer, 0, first + j))],
        out_specs=pl.BlockSpec((bm, bn), lambda i, j: (i, j)),
        out_shape=jax.ShapeDtypeStruct((m, n), out_dtype),
        compiler_params=_params("parallel", "arbitrary"),
        name=name,
    )(a, ss, w)


def _mlp_up_kernel(a_ref, ss_ref, w_ref, wd_ref, o_ref, wd_bf16_ref):
    _mm_kernel(a_ref, ss_ref, w_ref, o_ref, relu_sq=True)
    wd_bf16_ref[...] = wd_ref[...].astype(BF16)


def _mlp_up(a, ss, w, w_down, layer, bn, bm=MLP_ROW_TILE):
    m, k = a.shape
    n = w.shape[2]
    n_j = n // bn
    steps = (m // bm) * n_j
    slab = n // steps
    assert slab * steps == n
    return pl.pallas_call(
        _mlp_up_kernel,
        grid=(m // bm, n_j),
        in_specs=[pl.BlockSpec((bm, k), lambda i, j: (i, 0), pipeline_mode=pl.Buffered(1)),
                  pl.BlockSpec((bm, LANES), lambda i, j: (i, 0)),
                  pl.BlockSpec((None, k, bn), lambda i, j: (layer, 0, j)),
                  pl.BlockSpec((None, slab, k), lambda i, j: (layer, i * n_j + j, 0))],
        out_specs=[pl.BlockSpec((None, None, bm, bn), lambda i, j: (i, j, 0, 0)),
                   pl.BlockSpec((slab, k), lambda i, j: (i * n_j + j, 0))],
        out_shape=[jax.ShapeDtypeStruct((m // bm, n_j, bm, bn), BF16), jax.ShapeDtypeStruct((n, k), BF16)],
        compiler_params=_params("parallel", "arbitrary"),
        name="mlp_up",
    )(a, ss, w, w_down)


def _mm_transposed_kernel(wt_ref, a_ref, ss_ref, o_ref, *, t):
    acc = lax.dot_general(wt_ref[...], a_ref[...], (((1,), (1,)), ((), ())), preferred_element_type=F32)
    acc = acc * jnp.transpose(_row_scale(ss_ref))[0:1, :]
    for c in range(acc.shape[1] // t):
        o_ref[c] = acc[:, c * t:(c + 1) * t].astype(o_ref.dtype)


def _mm_transposed(a, ss, wt, layer, t, bm=1024):
    m, k = a.shape
    n = wt.shape[1]
    return pl.pallas_call(
        functools.partial(_mm_transposed_kernel, t=t),
        grid=(m // bm,),
        in_specs=[pl.BlockSpec((None, n, k), lambda i: (layer, 0, 0), pipeline_mode=pl.Buffered(1)),
                  pl.BlockSpec((bm, k), lambda i: (i, 0)),
                  pl.BlockSpec((bm, LANES), lambda i: (i, 0))],
        out_specs=pl.BlockSpec((bm // t, n, t), lambda i: (i, 0, 0)),
        out_shape=jax.ShapeDtypeStruct((m // t, n, t), BF16),
        compiler_params=_params("parallel"),
        name="in_proj_vt",
    )(wt, a, ss)


def _mm_residual_kernel(a_ref, w_ref, r_ref, *rest):
    o_ref = rest[0] if len(rest) == 1 else rest[1]
    kk = pl.program_id(2)

    @pl.when(kk == 0)
    def _():
        o_ref[...] = r_ref[...]

    a = jnp.concatenate([a_ref[c] for c in range(a_ref.shape[0])], axis=1)
    o_ref[...] += jnp.dot(a, w_ref[...], preferred_element_type=F32)

    if len(rest) > 1:
        g_ref, _, xb_ref, ss_ref = rest

        @pl.when(kk == pl.num_programs(2) - 1)
        def _():
            _store_norm_inputs(o_ref[...], g_ref, xb_ref, ss_ref, pl.program_id(1) == 0)


def _mm_residual(a, w, r, g=None, bn=1024, bk=1024):
    row_tiles, col_tiles, bm, tile_cols = a.shape
    m, k = row_tiles * bm, col_tiles * tile_cols
    n = w.shape[1]
    tile_spec = pl.BlockSpec((bm, bn), lambda i, j, kk: (i, j))
    in_specs = [pl.BlockSpec((None, bk // tile_cols, bm, tile_cols), lambda i, j, kk: (i, kk, 0, 0)),
                pl.BlockSpec((bk, bn), lambda i, j, kk: (kk, j)),
                pl.BlockSpec((bm, bn), lambda i, j, kk: (i, j), pipeline_mode=pl.Buffered(1))]
    out_specs = [tile_spec]
    out_shape = [jax.ShapeDtypeStruct((m, n), F32)]
    operands = [a, w, r]
    if g is not None:
        in_specs.append(pl.BlockSpec((1, bn), lambda i, j, kk: (0, j)))
        out_specs += [tile_spec, pl.BlockSpec((bm, LANES), lambda i, j, kk: (i, 0))]
        out_shape += [jax.ShapeDtypeStruct((m, n), BF16), jax.ShapeDtypeStruct((m, LANES), F32)]
        operands.append(g.reshape(1, n))
    return pl.pallas_call(
        _mm_residual_kernel,
        grid=(m // bm, n // bn, k // bk),
        in_specs=in_specs,
        out_specs=out_specs,
        out_shape=out_shape,
        compiler_params=_params("parallel", "arbitrary", "arbitrary"),
        name="mlp_down",
    )(*operands)


def _out_proj_kernel(a_ref, b_ref, c_ref, d_ref, w_ref, r_ref, g_ref, o_ref, xb_ref, ss_ref):
    mix = jnp.concatenate([a_ref[...]] + [b_ref[g] for g in range(len(B_GROUPS))] + [c_ref[...], d_ref[...]], axis=1)
    x_new = r_ref[...] + jnp.dot(mix, w_ref[...].astype(BF16), preferred_element_type=F32)
    o_ref[...] = x_new
    _store_norm_inputs(x_new, g_ref, xb_ref, ss_ref, pl.program_id(1) == 0)


def _out_proj(out_a, out_b, out_c, out_d, w, layer, r, g, bm=1024, bn=512):
    m = out_a.shape[0]
    n = w.shape[2]
    n_groups = len(B_GROUPS)
    return pl.pallas_call(
        _out_proj_kernel,
        grid=(m // bm, n // bn),
        in_specs=[pl.BlockSpec((bm, A_WIDTH), lambda i, j: (i, 0)),
                  pl.BlockSpec((n_groups, bm, B_GROUP_WIDTH), lambda i, j: (0, i, 0)),
                  pl.BlockSpec((bm, C_WIDTH), lambda i, j: (i, 0)),
                  pl.BlockSpec((bm, D_WIDTH), lambda i, j: (i, 0)),
                  pl.BlockSpec((None, D_MODEL, bn), lambda i, j: (layer, 0, j)),
                  pl.BlockSpec((bm, bn), lambda i, j: (i, j)),
                  pl.BlockSpec((1, bn), lambda i, j: (0, j))],
        out_specs=[pl.BlockSpec((bm, bn), lambda i, j: (i, j)),
                   pl.BlockSpec((bm, bn), lambda i, j: (i, j)),
                   pl.BlockSpec((bm, LANES), lambda i, j: (i, 0))],
        out_shape=[jax.ShapeDtypeStruct((m, n), F32), jax.ShapeDtypeStruct((m, n), BF16),
                   jax.ShapeDtypeStruct((m, LANES), F32)],
        compiler_params=_params("parallel", "arbitrary"),
        name="out_proj",
    )(out_a, out_b, out_c, out_d, w, r, g.reshape(1, n))


def _attn_a_kernel(q_ref, k_ref, vt_ref, lam_ref, g_ref, o_ref, *scratch, t, heads, lam_init):
    qi = pl.program_id(2)
    lane = lax.broadcasted_iota(jnp.int32, (t, HEAD_DIM), 1)
    qqs = []
    for h in range(heads):
        q = q_ref[:, h * HEAD_DIM:(h + 1) * HEAD_DIM]
        zero = jnp.zeros_like(q)
        qqs.append(jnp.concatenate([jnp.where(lane < A_SUB, q, zero), jnp.where(lane >= A_SUB, q, zero)], axis=0))

    ones = jnp.ones((A_ONES_ROWS, t), BF16)
    m_scr, acc_scr = scratch[:heads], scratch[heads:]
    for h in range(heads):
        m_scr[h][...] = jnp.full(m_scr[h].shape, MASKED_SCORE, F32)
        acc_scr[h][...] = jnp.zeros(acc_scr[h].shape, F32)

    def step(j, masked):
        def scores(h):
            k = k_ref[pl.ds(pl.multiple_of(j * t, t), t), h * HEAD_DIM:(h + 1) * HEAD_DIM]
            s = lax.dot_general(k, qqs[h], (((1,), (1,)), ((), ())), preferred_element_type=F32)
            if masked:
                key = lax.broadcasted_iota(jnp.int32, s.shape, 0)
                lane_q = lax.broadcasted_iota(jnp.int32, s.shape, 1)
                qry = jnp.where(lane_q >= t, lane_q - t, lane_q)
                s = jnp.where(key <= qry, s, MASKED_SCORE)
            return s

        s_next = scores(0)
        for h in range(heads):
            s = s_next
            if h + 1 < heads:
                s_next = scores(h + 1)
            vt = jnp.concatenate([vt_ref[j, h * HEAD_DIM:(h + 1) * HEAD_DIM, :], ones], axis=0)
            m = m_scr[h][...]
            m_new = jnp.maximum(m, jnp.max(s, axis=0, keepdims=True))
            alpha = jnp.exp2(m - m_new)
            p = jnp.exp2(s - m_new)
            m_scr[h][...] = m_new
            acc_scr[h][...] = alpha * acc_scr[h][...] + jnp.dot(vt, p.astype(BF16), preferred_element_type=F32)

    def body(j, carry):
        step(j, masked=False)
        return carry

    lax.fori_loop(0, qi, body, 0)
    step(qi, masked=True)

    lp = lam_ref[...]
    lam = (jnp.exp(jnp.sum(lp[0:1] * lp[1:2], axis=-1, keepdims=True))
           - jnp.exp(jnp.sum(lp[2:3] * lp[3:4], axis=-1, keepdims=True)) + lam_init)
    for h in range(heads):
        acc = acc_scr[h][...]
        o = acc[:HEAD_DIM] / acc[HEAD_DIM:HEAD_DIM + 1]
        o = (o[:, :t] - lam * o[:, t:]).T
        ms = jnp.mean(o * o, axis=-1, keepdims=True)
        o = o * lax.rsqrt(ms + DIFF_EPS) * g_ref[...]
        o_ref[:, h * HEAD_DIM:(h + 1) * HEAD_DIM] = (o * (1.0 - lam_init)).astype(o_ref.dtype)


def _attn_a(qk, vt, lam_params, subln_g, lam_init, batch, seq, t, heads=8):
    nq = seq // t
    width = heads * HEAD_DIM
    n_pairs = A_HEADS // heads
    return pl.pallas_call(
        functools.partial(_attn_a_kernel, t=t, heads=heads, lam_init=lam_init),
        grid=(batch, n_pairs, nq),
        in_specs=[pl.BlockSpec((t, width), lambda b, h, i: (b * nq + i, h)),
                  pl.BlockSpec((seq, width), lambda b, h, i: (b, n_pairs + h)),
                  pl.BlockSpec((nq, width, t), lambda b, h, i: (b, h, 0)),
                  pl.BlockSpec((4, A_SUB), lambda b, h, i: (0, 0)),
                  pl.BlockSpec((1, HEAD_DIM), lambda b, h, i: (0, 0))],
        out_specs=pl.BlockSpec((t, width), lambda b, h, i: (b * nq + i, h)),
        out_shape=jax.ShapeDtypeStruct((batch * seq, A_WIDTH), BF16),
        scratch_shapes=([pltpu.VMEM((1, 2 * t), F32)] * heads
                        + [pltpu.VMEM((HEAD_DIM + A_ONES_ROWS, 2 * t), F32)] * heads),
        compiler_params=_params("parallel", "parallel", "arbitrary"),
        name="diff_attention",
    )(qk, qk, vt, lam_params, subln_g.reshape(1, HEAD_DIM))


def _attn_b_kernel(*refs, tile):
    n_groups = len(B_GROUPS)
    q_refs = refs[0:n_groups]
    k_refs = refs[n_groups:2 * n_groups]
    v_refs = refs[2 * n_groups:3 * n_groups]
    o_ref = refs[3 * n_groups]
    out_scr, lse_scr = refs[3 * n_groups + 1:]
    ti = pl.program_id(2)
    scale = HEAD_DIM ** -0.5
    base_delta = {}
    chunks = []

    for g, (window, dilation) in enumerate(B_GROUPS):
        back = window // dilation
        seg = tile // dilation
        tq = min(B_QUERY_CHUNK, seg)
        tk = tq + back
        if tq not in base_delta:
            base_delta[tq] = (lax.broadcasted_iota(jnp.int32, (tq, tk), 0)
                              - lax.broadcasted_iota(jnp.int32, (tq, tk), 1))
        for r in range(dilation):
            for c in range(seg // tq):
                chunks.append((g, dilation, back, tq, tk, r, c * tq, ti * seg + c * tq))

    def scores(chunk):
        g, dilation, back, tq, tk, r, i_loc, i0 = chunk
        ks = pl.multiple_of(jnp.maximum(i0 - back, 0), back)
        q = q_refs[g][r, i_loc:i_loc + tq, :]
        k = k_refs[g][r, pl.ds(ks, tk), :]
        s = lax.dot_general(q, k, (((1,), (1,)), ((), ())), preferred_element_type=F32) * scale
        delta = base_delta[tq] + (i0 - ks)
        return jnp.where((delta >= 0) & (delta <= back), s, MASKED_SCORE), ks

    nxt = scores(chunks[0])
    for idx, (g, dilation, back, tq, tk, r, i_loc, i0) in enumerate(chunks):
        s, ks = nxt
        if idx + 1 < len(chunks):
            nxt = scores(chunks[idx + 1])
        v = v_refs[g][r, pl.ds(ks, tk), :]
        m = jnp.max(s, axis=-1, keepdims=True)
        p = jnp.exp(s - m)
        l = jnp.sum(p, axis=-1, keepdims=True)
        o = jnp.dot(p.astype(BF16), v, preferred_element_type=F32) / l
        lse = jnp.broadcast_to(m + jnp.log(l), o.shape)
        rows = pl.ds(dilation * i_loc + r, tq, stride=dilation) if dilation > 1 else pl.ds(i_loc, tq)
        out_scr[g, rows, :] = o
        lse_scr[g, rows, :] = lse

    for c in range(tile // B_QUERY_CHUNK):
        rows = pl.ds(c * B_QUERY_CHUNK, B_QUERY_CHUNK)
        lses = [lse_scr[g, rows, :] for g in range(n_groups)]
        lse_max = functools.reduce(jnp.maximum, lses)
        weights = [jnp.exp(x - lse_max) for x in lses]
        denom = functools.reduce(lambda a, b: a + b, weights)
        for g in range(n_groups):
            o_ref[g, rows, :] = (out_scr[g, rows, :] * (weights[g] / denom)).astype(o_ref.dtype)


def _attn_b(qkvs, batch, seq, tile=B_TOKEN_TILE):
    n_tiles = seq // tile
    n_groups = len(B_GROUPS)

    def q_spec(g):
        d = B_GROUPS[g][1]
        return pl.BlockSpec((None, d, tile // d, HEAD_DIM), lambda b, h, i: (b, 0, i, h))

    def kv_spec(g, part):
        d = B_GROUPS[g][1]
        return pl.BlockSpec((None, d, seq // d, HEAD_DIM),
                            lambda b, h, i: (b, 0, 0, part * B_HEADS_PER_GROUP + h))

    in_specs = ([q_spec(g) for g in range(n_groups)] + [kv_spec(g, 1) for g in range(n_groups)]
                + [kv_spec(g, 2) for g in range(n_groups)])
    return pl.pallas_call(
        functools.partial(_attn_b_kernel, tile=tile),
        grid=(batch, B_HEADS_PER_GROUP, n_tiles),
        in_specs=in_specs,
        out_specs=pl.BlockSpec((n_groups, tile, HEAD_DIM), lambda b, h, i: (0, b * n_tiles + i, h)),
        out_shape=jax.ShapeDtypeStruct((n_groups, batch * seq, B_GROUP_WIDTH), BF16),
        scratch_shapes=[pltpu.VMEM((n_groups, tile, HEAD_DIM), F32),
                        pltpu.VMEM((n_groups, tile, HEAD_DIM), F32)],
        compiler_params=_params("parallel", "parallel", "arbitrary"),
        name="dilated_attention",
    )(*(list(qkvs) * 3))


def _shift_rows(x, k, row):
    return jnp.where(row >= k, pltpu.roll(x, k, 0), 0.0)


def _pool_kernel(u_lo_ref, u_hi_ref, w_ref, scale_ref, o_ref):
    g = pl.program_id(1)
    u = jnp.concatenate([u_lo_ref[...], u_hi_ref[...]], axis=1)
    row = lax.broadcasted_iota(jnp.int32, u.shape, 0)
    window = jnp.left_shift(2, g)
    total = u
    step = 1
    while step < max(C_WINDOWS):
        widened = total + _shift_rows(total, step, row)
        total = jnp.where(step < window, widened, total)
        step *= 2
    count = jnp.minimum(row + 1, window).astype(F32)
    pooled = total / count - u
    y = jnp.dot(pooled.astype(BF16), w_ref[0], preferred_element_type=F32)
    o_ref[...] = (y * scale_ref[...]).astype(o_ref.dtype)


def _pool(proj, col0, pool_w, pool_scale, batch, seq):
    n_groups = len(C_WINDOWS)
    tiles_per_group = C_GROUP_DIM // LANES
    assert tiles_per_group == 2 and col0 % LANES == 0
    first = col0 // LANES

    def u_spec(half):
        return pl.BlockSpec((seq, LANES), lambda b, g: (b, first + tiles_per_group * g + half))

    return pl.pallas_call(
        _pool_kernel,
        grid=(batch, n_groups),
        in_specs=[u_spec(0), u_spec(1),
                  pl.BlockSpec((1, C_GROUP_DIM, C_GROUP_DIM), lambda b, g: (g, 0, 0)),
                  pl.BlockSpec((1, C_GROUP_DIM), lambda b, g: (0, g))],
        out_specs=pl.BlockSpec((seq, C_GROUP_DIM), lambda b, g: (b, g)),
        out_shape=jax.ShapeDtypeStruct((batch * seq, C_WIDTH), BF16),
        compiler_params=_params("parallel", "parallel"),
        name="multiscale_pool",
    )(proj, proj, pool_w, pool_scale.reshape(1, C_WIDTH))


def _conv_kernel(gb_ref, gc_ref, h_ref, w_ref, o_ref):
    z = gc_ref[...] * h_ref[...]
    row = lax.broadcasted_iota(jnp.int32, z.shape, 0)
    w = w_ref[...]
    y = w[CONV_WIDTH - 1:CONV_WIDTH] * z
    for tap in range(1, CONV_WIDTH):
        y = y + w[CONV_WIDTH - 1 - tap:CONV_WIDTH - tap] * _shift_rows(z, tap, row)
    o_ref[...] = (gb_ref[...] * y).astype(o_ref.dtype)


def _short_conv(proj, col0, conv_w, batch, seq):
    n_col = D_WIDTH // LANES
    assert col0 % LANES == 0
    first = col0 // LANES

    def spec(part):
        return pl.BlockSpec((seq, LANES), lambda b, c: (b, first + part * n_col + c))

    return pl.pallas_call(
        _conv_kernel,
        grid=(batch, n_col),
        in_specs=[spec(0), spec(1), spec(2), pl.BlockSpec((CONV_WIDTH, LANES), lambda b, c: (0, c))],
        out_specs=pl.BlockSpec((seq, LANES), lambda b, c: (b, c)),
        out_shape=jax.ShapeDtypeStruct((batch * seq, D_WIDTH), BF16),
        compiler_params=_params("parallel", "parallel"),
        name="short_conv",
    )(proj, proj, proj, conv_w)


def _rope_tables(seq, dim):
    half = dim // 2
    inv = ROPE_THETA ** (-jnp.arange(0, dim, 2, dtype=F32) / dim)
    ang = jnp.arange(seq, dtype=F32)[:, None] * inv[None, :]
    cos, sin = jnp.cos(ang), jnp.sin(ang)
    reps = LANES // dim
    cos_t = jnp.tile(jnp.concatenate([cos, cos], axis=1), (1, reps))
    zeros = jnp.zeros_like(sin)
    if dim == LANES:
        return (cos_t, jnp.concatenate([-sin, sin], axis=1)), (half,)
    upper = jnp.tile(jnp.concatenate([zeros, sin], axis=1), (1, reps))
    lower = jnp.tile(jnp.concatenate([-sin, zeros], axis=1), (1, reps))
    return (cos_t, upper, lower), (half, LANES - half)


def kernel(x, w_in, w_out, norm_mix, norm_mlp, diff_lambda, diff_subln, pool_w, pool_scale, conv_w,
           w_up, w_down, norm_final):
    batch, seq, d_model = x.shape
    depth = w_in.shape[0]
    m = batch * seq
    x = x.reshape(m, d_model)

    tables_a, shifts_a = _rope_tables(seq, A_SUB)
    tables_b, shifts_b = _rope_tables(seq, HEAD_DIM)

    a_qk_end = 2 * A_WIDTH
    a_end = 3 * A_WIDTH
    b_end = a_end + 3 * B_WIDTH
    c_end = b_end + C_WIDTH
    t_a = 512
    tail0 = (b_end // IN_PROJ_COL_TILE) * IN_PROJ_COL_TILE
    in_width = w_in.shape[2]

    w_in = w_in.astype(BF16)
    w_a_vt = jnp.swapaxes(w_in[:, :, a_qk_end:a_end], 1, 2)

    h, ss = _prep(x, norm_mix[0])
    for l in range(depth):
        lam_init = 0.8 - 0.6 * math.exp(-0.3 * l)

        qk_a = _mm_rope(h, ss, w_in, l, a_qk_end, tables_a, shifts_a, A_SCORE_SCALE, A_WIDTH, seq)
        vt_a = _mm_transposed(h, ss, w_a_vt, l, t_a)
        qkv_b = [_mm_rope_permute(h, ss, w_in, l, a_end, g, tables_b, shifts_b[0], B_GROUPS[g][1], batch, seq)
                 for g in range(len(B_GROUPS))]
        tail = _mm_cols(h, ss, w_in, l, tail0, in_width - tail0, F32, name="in_proj_tail")

        out_a = _attn_a(qk_a, vt_a, diff_lambda[l], diff_subln[l], lam_init, batch, seq, t_a)
        out_b = _attn_b(qkv_b, batch, seq)
        out_c = _pool(tail, b_end - tail0, pool_w[l].astype(BF16), pool_scale[l], batch, seq)
        out_d = _short_conv(tail, c_end - tail0, conv_w[l], batch, seq)

        x, h, ss = _out_proj(out_a, out_b, out_c, out_d, w_out, l, x, norm_mlp[l])
        act, w_down_l = _mlp_up(h, ss, w_up, w_down, l, 512)
        if l + 1 < depth:
            x, h, ss = _mm_residual(act, w_down_l, x, norm_mix[l + 1])
        else:
            x, = _mm_residual(act, w_down_l, x)

    out = _rmsnorm(x, norm_final, F32)
    return out.reshape(batch, seq, d_model)
```

```python
import functools
import math

import jax
import jax.numpy as jnp
from jax import lax
from jax.experimental import pallas as pl
from jax.experimental.pallas import tpu as pltpu

D_MODEL = 4096
HEAD_DIM = 128
A_HEADS = 8
A_WIDTH = A_HEADS * HEAD_DIM
A_SUB = HEAD_DIM // 2
A_SCORE_SCALE = A_SUB ** -0.5 * math.log2(math.e)
A_ONES_ROWS = 16
B_GROUPS = ((128, 1), (512, 4), (2048, 16))
B_HEADS_PER_GROUP = 3
B_GROUP_WIDTH = B_HEADS_PER_GROUP * HEAD_DIM
B_WIDTH = B_GROUP_WIDTH * len(B_GROUPS)
B_TOKEN_TILE = 2048
B_QUERY_CHUNK = 256
C_WINDOWS = (2, 4, 8, 16)
C_WIDTH = 1024
C_GROUP_DIM = C_WIDTH // len(C_WINDOWS)
D_WIDTH = D_MODEL - A_WIDTH - B_WIDTH - C_WIDTH
CONV_WIDTH = 3
D_FF = 4 * D_MODEL
ROPE_THETA = 10000.0
NORM_EPS = 1e-6
DIFF_EPS = 1e-5

LANES = 128
MXU_COLS = 256
MLP_ROW_TILE = 2048
IN_PROJ_COL_TILE = 1024
IN_PROJ_ROW_TILE = 1024
VMEM_LIMIT_BYTES = 56 * 1024 * 1024
MASKED_SCORE = -1e30

F32 = jnp.float32
BF16 = jnp.bfloat16


def _params(*semantics):
    return pltpu.CompilerParams(dimension_semantics=semantics, vmem_limit_bytes=VMEM_LIMIT_BYTES)


def _rmsnorm_kernel(x_ref, g_ref, o_ref, *, eps):
    x = x_ref[...]
    ms = jnp.mean(x * x, axis=-1, keepdims=True)
    o_ref[...] = (x * lax.rsqrt(ms + eps) * g_ref[...]).astype(o_ref.dtype)


def _rmsnorm(x, g, out_dtype, tm=256):
    m, d = x.shape
    return pl.pallas_call(
        functools.partial(_rmsnorm_kernel, eps=NORM_EPS),
        grid=(m // tm,),
        in_specs=[pl.BlockSpec((tm, d), lambda i: (i, 0)),
                  pl.BlockSpec((1, d), lambda i: (0, 0))],
        out_specs=pl.BlockSpec((tm, d), lambda i: (i, 0)),
        out_shape=jax.ShapeDtypeStruct((m, d), out_dtype),
        compiler_params=_params("parallel"),
        name="rmsnorm",
    )(x, g.reshape(1, d))


def _prep_kernel(x_ref, g_ref, xb_ref, ss_ref):
    x = x_ref[...]
    xb_ref[...] = (x * g_ref[...]).astype(xb_ref.dtype)
    ss_ref[...] = jnp.broadcast_to(jnp.sum(x * x, axis=-1, keepdims=True), ss_ref.shape)


def _prep(x, g, tm=256):
    m, d = x.shape
    return pl.pallas_call(
        _prep_kernel,
        grid=(m // tm,),
        in_specs=[pl.BlockSpec((tm, d), lambda i: (i, 0)),
                  pl.BlockSpec((1, d), lambda i: (0, 0))],
        out_specs=[pl.BlockSpec((tm, d), lambda i: (i, 0)),
                   pl.BlockSpec((tm, LANES), lambda i: (i, 0))],
        out_shape=[jax.ShapeDtypeStruct((m, d), BF16), jax.ShapeDtypeStruct((m, LANES), F32)],
        compiler_params=_params("parallel"),
        name="norm_prep",
    )(x, g.reshape(1, d))


def _row_scale(ss_ref):
    return lax.rsqrt(ss_ref[...] * (1.0 / D_MODEL) + NORM_EPS)


def _scale_rows(y, rs):
    return y * jnp.concatenate([rs] * (y.shape[1] // LANES), axis=1)


def _store_norm_inputs(x_new, g_ref, xb_ref, ss_ref, first):
    xb_ref[...] = (x_new * g_ref[...]).astype(xb_ref.dtype)
    part = jnp.broadcast_to(jnp.sum(x_new * x_new, axis=-1, keepdims=True), ss_ref.shape)

    @pl.when(first)
    def _():
        ss_ref[...] = part

    @pl.when(jnp.logical_not(first))
    def _():
        ss_ref[...] += part


def _rotate(y, cos_ref, sin_refs, shifts):
    outs = []
    for c in range(y.shape[1] // LANES):
        yc = y[:, c * LANES:(c + 1) * LANES]
        oc = yc * cos_ref[...]
        for s_ref, shift in zip(sin_refs, shifts):
            oc = oc + pltpu.roll(yc, shift, 1) * s_ref[...]
        outs.append(oc)
    return jnp.concatenate(outs, axis=1)


def _mm_rope_kernel(a_ref, ss_ref, w_ref, cos_ref, *rest, shifts, q_scale, n_q_blocks):
    sin_refs, o_ref = rest[:-1], rest[-1]
    scale = jnp.where(pl.program_id(1) < n_q_blocks, q_scale, 1.0).astype(F32)
    a = a_ref[...]
    rs = _row_scale(ss_ref)
    for c in range(o_ref.shape[1] // MXU_COLS):
        cols = slice(c * MXU_COLS, (c + 1) * MXU_COLS)
        acc = _scale_rows(jnp.dot(a, w_ref[:, cols].astype(BF16), preferred_element_type=F32), rs)
        o_ref[:, cols] = (_rotate(acc, cos_ref, sin_refs, shifts) * scale).astype(o_ref.dtype)


def _mm_rope(a, ss, w, layer, n, tables, shifts, q_scale, n_q_cols, seq, bn=IN_PROJ_COL_TILE,
             bm=IN_PROJ_ROW_TILE):
    m, k = a.shape
    tab_blocks = seq // bm
    tab_spec = pl.BlockSpec((bm, LANES), lambda i, j: (i % tab_blocks, 0))
    return pl.pallas_call(
        functools.partial(_mm_rope_kernel, shifts=shifts, q_scale=q_scale, n_q_blocks=n_q_cols // bn),
        grid=(m // bm, n // bn),
        in_specs=[pl.BlockSpec((bm, k), lambda i, j: (i, 0)),
                  pl.BlockSpec((bm, LANES), lambda i, j: (i, 0)),
                  pl.BlockSpec((None, k, bn), lambda i, j: (layer, 0, j))] + [tab_spec] * len(tables),
        out_specs=pl.BlockSpec((bm, bn), lambda i, j: (i, j)),
        out_shape=jax.ShapeDtypeStruct((m, n), BF16),
        compiler_params=_params("parallel", "arbitrary"),
        name="in_proj_rope",
    )(a, ss, w, *tables)


def _mm_rope_permute_kernel(a_ref, ss_ref, wq_ref, wk_ref, wv_ref, cos_ref, sin_ref, o_ref, scr_ref, *,
                            shift, n_rope_cols, dilation):
    a = a_ref[...]
    w = jnp.concatenate([wq_ref[...], wk_ref[...], wv_ref[...]], axis=1)
    rs = _row_scale(ss_ref)
    n = o_ref.shape[2]
    rows = a.shape[0] // dilation
    for lo in range(0, n, 2 * MXU_COLS):
        hi = min(lo + 2 * MXU_COLS, n)
        acc = _scale_rows(jnp.dot(a, w[:, lo:hi], preferred_element_type=F32), rs)
        rope_hi = min(hi, n_rope_cols)
        if lo < rope_hi:
            rot = _rotate(acc[:, :rope_hi - lo], cos_ref, (sin_ref,), (shift,))
            acc = rot if rope_hi == hi else jnp.concatenate([rot, acc[:, rope_hi - lo:]], axis=1)
        if dilation == 1:
            o_ref[0, :, lo:hi] = acc.astype(o_ref.dtype)
        else:
            for c in range(lo // LANES, hi // LANES):
                cols = slice(c * LANES, (c + 1) * LANES)
                scr_ref[c] = acc[:, c * LANES - lo:(c + 1) * LANES - lo]
                for r in range(dilation):
                    o_ref[r, :, cols] = scr_ref[c, pl.ds(r, rows, stride=dilation), :].astype(o_ref.dtype)


def _mm_rope_permute(a, ss, w, layer, col0, group, tables, shift, dilation, batch, seq, bm=IN_PROJ_ROW_TILE):
    m, k = a.shape
    n = 3 * B_GROUP_WIDTH
    n_rope_cols = 2 * B_GROUP_WIDTH
    tiles = seq // bm
    tab_spec = pl.BlockSpec((bm, LANES), lambda i: (i % tiles, 0))
    n_groups = len(B_GROUPS)
    first = col0 // B_GROUP_WIDTH
    assert first * B_GROUP_WIDTH == col0

    def w_spec(part):
        return pl.BlockSpec((None, k, B_GROUP_WIDTH), lambda i: (layer, 0, first + part * n_groups + group),
                            pipeline_mode=pl.Buffered(1))

    return pl.pallas_call(
        functools.partial(_mm_rope_permute_kernel, shift=shift, n_rope_cols=n_rope_cols, dilation=dilation),
        grid=(m // bm,),
        in_specs=[pl.BlockSpec((bm, k), lambda i: (i, 0)), pl.BlockSpec((bm, LANES), lambda i: (i, 0)),
                  w_spec(0), w_spec(1), w_spec(2), tab_spec, tab_spec],
        out_specs=pl.BlockSpec((None, dilation, bm // dilation, n), lambda i: (i // tiles, 0, i % tiles, 0)),
        out_shape=jax.ShapeDtypeStruct((batch, dilation, seq // dilation, n), BF16),
        scratch_shapes=[pltpu.VMEM((n // LANES, bm, LANES), F32)],
        compiler_params=_params("parallel"),
        name="in_proj_dilated",
    )(a, ss, w, w, w, *tables)


def _mm_kernel(a_ref, ss_ref, w_ref, o_ref, *, relu_sq):
    acc = jnp.dot(a_ref[...], w_ref[...].astype(BF16), preferred_element_type=F32)
    acc = _scale_rows(acc, _row_scale(ss_ref))
    if relu_sq:
        acc = jnp.square(jnp.maximum(acc, 0.0))
    o_ref[...] = acc.astype(o_ref.dtype)


def _mm_cols(a, ss, w, layer, col0, n, out_dtype, bn=IN_PROJ_COL_TILE, bm=IN_PROJ_ROW_TILE, name="matmul"):
    m, k = a.shape
    first = col0 // bn
    assert first * bn == col0 and n % bn == 0
    return pl.pallas_call(
        functools.partial(_mm_kernel, relu_sq=False),
        grid=(m // bm, n // bn),
        in_specs=[pl.BlockSpec((bm, k), lambda i, j: (i, 0)),
                  pl.BlockSpec((bm, LANES), lambda i, j: (i, 0)),
                  pl.BlockSpec((None, k, bn), lambda i, j: (layer, 0, first + j))],
        out_specs=pl.BlockSpec((bm, bn), lambda i, j: (i, j)),
        out_shape=jax.ShapeDtypeStruct((m, n), out_dtype),
        compiler_params=_params("parallel", "arbitrary"),
        name=name,
    )(a, ss, w)


def _mlp_up_kernel(a_ref, ss_ref, w_ref, wd_ref, o_ref, wd_bf16_ref):
    _mm_kernel(a_ref, ss_ref, w_ref, o_ref, relu_sq=True)
    wd_bf16_ref[...] = wd_ref[...].astype(BF16)


def _mlp_up(a, ss, w, w_down, layer, bn, bm=MLP_ROW_TILE):
    m, k = a.shape
    n = w.shape[2]
    n_j = n // bn
    steps = (m // bm) * n_j
    slab = n // steps
    assert slab * steps == n
    return pl.pallas_call(
        _mlp_up_kernel,
        grid=(m // bm, n_j),
        in_specs=[pl.BlockSpec((bm, k), lambda i, j: (i, 0), pipeline_mode=pl.Buffered(1)),
                  pl.BlockSpec((bm, LANES), lambda i, j: (i, 0)),
                  pl.BlockSpec((None, k, bn), lambda i, j: (layer, 0, j)),
                  pl.BlockSpec((None, slab, k), lambda i, j: (layer, i * n_j + j, 0))],
        out_specs=[pl.BlockSpec((None, None, bm, bn), lambda i, j: (i, j, 0, 0)),
                   pl.BlockSpec((slab, k), lambda i, j: (i * n_j + j, 0))],
        out_shape=[jax.ShapeDtypeStruct((m // bm, n_j, bm, bn), BF16), jax.ShapeDtypeStruct((n, k), BF16)],
        compiler_params=_params("parallel", "arbitrary"),
        name="mlp_up",
    )(a, ss, w, w_down)


def _mm_transposed_kernel(wt_ref, a_ref, ss_ref, o_ref, *, t):
    acc = lax.dot_general(wt_ref[...], a_ref[...], (((1,), (1,)), ((), ())), preferred_element_type=F32)
    acc = acc * jnp.transpose(_row_scale(ss_ref))[0:1, :]
    for c in range(acc.shape[1] // t):
        o_ref[c] = acc[:, c * t:(c + 1) * t].astype(o_ref.dtype)


def _mm_transposed(a, ss, wt, layer, t, bm=1024):
    m, k = a.shape
    n = wt.shape[1]
    return pl.pallas_call(
        functools.partial(_mm_transposed_kernel, t=t),
        grid=(m // bm,),
        in_specs=[pl.BlockSpec((None, n, k), lambda i: (layer, 0, 0), pipeline_mode=pl.Buffered(1)),
                  pl.BlockSpec((bm, k), lambda i: (i, 0)),
                  pl.BlockSpec((bm, LANES), lambda i: (i, 0))],
        out_specs=pl.BlockSpec((bm // t, n, t), lambda i: (i, 0, 0)),
        out_shape=jax.ShapeDtypeStruct((m // t, n, t), BF16),
        compiler_params=_params("parallel"),
        name="in_proj_vt",
    )(wt, a, ss)


def _mm_residual_kernel(a_ref, w_ref, r_ref, *rest):
    o_ref = rest[0] if len(rest) == 1 else rest[1]
    kk = pl.program_id(2)

    @pl.when(kk == 0)
    def _():
        o_ref[...] = r_ref[...]

    a = jnp.concatenate([a_ref[c] for c in range(a_ref.shape[0])], axis=1)
    o_ref[...] += jnp.dot(a, w_ref[...], preferred_element_type=F32)

    if len(rest) > 1:
        g_ref, _, xb_ref, ss_ref = rest

        @pl.when(kk == pl.num_programs(2) - 1)
        def _():
            _store_norm_inputs(o_ref[...], g_ref, xb_ref, ss_ref, pl.program_id(1) == 0)


def _mm_residual(a, w, r, g=None, bm=1024, bn=1024, bk=4096):
    row_tiles, col_tiles, tile_rows, tile_cols = a.shape
    m, k = row_tiles * tile_rows, col_tiles * tile_cols
    n = w.shape[1]
    sub = tile_rows // bm
    assert sub * bm == tile_rows
    tile_spec = pl.BlockSpec((bm, bn), lambda i, j, kk: (i, j))
    in_specs = [pl.BlockSpec((None, bk // tile_cols, bm, tile_cols), lambda i, j, kk: (i // sub, kk, i % sub, 0)),
                pl.BlockSpec((bk, bn), lambda i, j, kk: (kk, j)),
                pl.BlockSpec((bm, bn), lambda i, j, kk: (i, j), pipeline_mode=pl.Buffered(1))]
    out_specs = [tile_spec]
    out_shape = [jax.ShapeDtypeStruct((m, n), F32)]
    operands = [a, w, r]
    if g is not None:
        in_specs.append(pl.BlockSpec((1, bn), lambda i, j, kk: (0, j)))
        out_specs += [tile_spec, pl.BlockSpec((bm, LANES), lambda i, j, kk: (i, 0))]
        out_shape += [jax.ShapeDtypeStruct((m, n), BF16), jax.ShapeDtypeStruct((m, LANES), F32)]
        operands.append(g.reshape(1, n))
    return pl.pallas_call(
        _mm_residual_kernel,
        grid=(m // bm, n // bn, k // bk),
        in_specs=in_specs,
        out_specs=out_specs,
        out_shape=out_shape,
        compiler_params=_params("parallel", "arbitrary", "arbitrary"),
        name="mlp_down",
    )(*operands)


def _out_proj_kernel(a_ref, b_ref, c_ref, d_ref, w_ref, r_ref, g_ref, o_ref, xb_ref, ss_ref):
    mix = jnp.concatenate([a_ref[...]] + [b_ref[g] for g in range(len(B_GROUPS))] + [c_ref[...], d_ref[...]], axis=1)
    x_new = r_ref[...] + jnp.dot(mix, w_ref[...].astype(BF16), preferred_element_type=F32)
    o_ref[...] = x_new
    _store_norm_inputs(x_new, g_ref, xb_ref, ss_ref, pl.program_id(1) == 0)


def _out_proj(out_a, out_b, out_c, out_d, w, layer, r, g, bm=1024, bn=512):
    m = out_a.shape[0]
    n = w.shape[2]
    n_groups = len(B_GROUPS)
    return pl.pallas_call(
        _out_proj_kernel,
        grid=(m // bm, n // bn),
        in_specs=[pl.BlockSpec((bm, A_WIDTH), lambda i, j: (i, 0)),
                  pl.BlockSpec((n_groups, bm, B_GROUP_WIDTH), lambda i, j: (0, i, 0)),
                  pl.BlockSpec((bm, C_WIDTH), lambda i, j: (i, 0)),
                  pl.BlockSpec((bm, D_WIDTH), lambda i, j: (i, 0)),
                  pl.BlockSpec((None, D_MODEL, bn), lambda i, j: (layer, 0, j)),
                  pl.BlockSpec((bm, bn), lambda i, j: (i, j)),
                  pl.BlockSpec((1, bn), lambda i, j: (0, j))],
        out_specs=[pl.BlockSpec((bm, bn), lambda i, j: (i, j)),
                   pl.BlockSpec((bm, bn), lambda i, j: (i, j)),
                   pl.BlockSpec((bm, LANES), lambda i, j: (i, 0))],
        out_shape=[jax.ShapeDtypeStruct((m, n), F32), jax.ShapeDtypeStruct((m, n), BF16),
                   jax.ShapeDtypeStruct((m, LANES), F32)],
        compiler_params=_params("parallel", "arbitrary"),
        name="out_proj",
    )(out_a, out_b, out_c, out_d, w, r, g.reshape(1, n))


def _attn_a_kernel(q_ref, k_ref, vt_ref, lam_ref, g_ref, o_ref, *scratch, t, heads, lam_init):
    qi = pl.program_id(2)
    lane = lax.broadcasted_iota(jnp.int32, (t, HEAD_DIM), 1)
    qqs = []
    for h in range(heads):
        q = q_ref[:, h * HEAD_DIM:(h + 1) * HEAD_DIM]
        zero = jnp.zeros_like(q)
        qqs.append(jnp.concatenate([jnp.where(lane < A_SUB, q, zero), jnp.where(lane >= A_SUB, q, zero)], axis=0))

    ones = jnp.ones((A_ONES_ROWS, t), BF16)
    m_scr, acc_scr = scratch[:heads], scratch[heads:]
    for h in range(heads):
        m_scr[h][...] = jnp.full(m_scr[h].shape, MASKED_SCORE, F32)
        acc_scr[h][...] = jnp.zeros(acc_scr[h].shape, F32)

    def step(j, masked):
        def scores(h):
            k = k_ref[pl.ds(pl.multiple_of(j * t, t), t), h * HEAD_DIM:(h + 1) * HEAD_DIM]
            s = lax.dot_general(k, qqs[h], (((1,), (1,)), ((), ())), preferred_element_type=F32)
            if masked:
                key = lax.broadcasted_iota(jnp.int32, s.shape, 0)
                lane_q = lax.broadcasted_iota(jnp.int32, s.shape, 1)
                qry = jnp.where(lane_q >= t, lane_q - t, lane_q)
                s = jnp.where(key <= qry, s, MASKED_SCORE)
            return s

        s_next = scores(0)
        for h in range(heads):
            s = s_next
            if h + 1 < heads:
                s_next = scores(h + 1)
            vt = jnp.concatenate([vt_ref[j, h * HEAD_DIM:(h + 1) * HEAD_DIM, :], ones], axis=0)
            m = m_scr[h][...]
            m_new = jnp.maximum(m, jnp.max(s, axis=0, keepdims=True))
            alpha = jnp.exp2(m - m_new)
            p = jnp.exp2(s - m_new)
            m_scr[h][...] = m_new
            acc_scr[h][...] = alpha * acc_scr[h][...] + jnp.dot(vt, p.astype(BF16), preferred_element_type=F32)

    def body(j, carry):
        step(j, masked=False)
        return carry

    lax.fori_loop(0, qi, body, 0)
    step(qi, masked=True)

    lp = lam_ref[...]
    lam = (jnp.exp(jnp.sum(lp[0:1] * lp[1:2], axis=-1, keepdims=True))
           - jnp.exp(jnp.sum(lp[2:3] * lp[3:4], axis=-1, keepdims=True)) + lam_init)
    for h in range(heads):
        acc = acc_scr[h][...]
        o = acc[:HEAD_DIM] / acc[HEAD_DIM:HEAD_DIM + 1]
        o = (o[:, :t] - lam * o[:, t:]).T
        ms = jnp.mean(o * o, axis=-1, keepdims=True)
        o = o * lax.rsqrt(ms + DIFF_EPS) * g_ref[...]
        o_ref[:, h * HEAD_DIM:(h + 1) * HEAD_DIM] = (o * (1.0 - lam_init)).astype(o_ref.dtype)


def _attn_a(qk, vt, lam_params, subln_g, lam_init, batch, seq, t, heads=8):
    nq = seq // t
    width = heads * HEAD_DIM
    n_pairs = A_HEADS // heads
    return pl.pallas_call(
        functools.partial(_attn_a_kernel, t=t, heads=heads, lam_init=lam_init),
        grid=(batch, n_pairs, nq),
        in_specs=[pl.BlockSpec((t, width), lambda b, h, i: (b * nq + i, h)),
                  pl.BlockSpec((seq, width), lambda b, h, i: (b, n_pairs + h)),
                  pl.BlockSpec((nq, width, t), lambda b, h, i: (b, h, 0)),
                  pl.BlockSpec((4, A_SUB), lambda b, h, i: (0, 0)),
                  pl.BlockSpec((1, HEAD_DIM), lambda b, h, i: (0, 0))],
        out_specs=pl.BlockSpec((t, width), lambda b, h, i: (b * nq + i, h)),
        out_shape=jax.ShapeDtypeStruct((batch * seq, A_WIDTH), BF16),
        scratch_shapes=([pltpu.VMEM((1, 2 * t), F32)] * heads
                        + [pltpu.VMEM((HEAD_DIM + A_ONES_ROWS, 2 * t), F32)] * heads),
        compiler_params=_params("parallel", "parallel", "arbitrary"),
        name="diff_attention",
    )(qk, qk, vt, lam_params, subln_g.reshape(1, HEAD_DIM))


def _attn_b_kernel(*refs, tile):
    n_groups = len(B_GROUPS)
    q_refs = refs[0:n_groups]
    k_refs = refs[n_groups:2 * n_groups]
    v_refs = refs[2 * n_groups:3 * n_groups]
    o_ref = refs[3 * n_groups]
    out_scr, lse_scr = refs[3 * n_groups + 1:]
    ti = pl.program_id(2)
    scale = HEAD_DIM ** -0.5
    base_delta = {}
    chunks = []

    for g, (window, dilation) in enumerate(B_GROUPS):
        back = window // dilation
        seg = tile // dilation
        tq = min(B_QUERY_CHUNK, seg)
        tk = tq + back
        if tq not in base_delta:
            base_delta[tq] = (lax.broadcasted_iota(jnp.int32, (tq, tk), 0)
                              - lax.broadcasted_iota(jnp.int32, (tq, tk), 1))
        for r in range(dilation):
            for c in range(seg // tq):
                chunks.append((g, dilation, back, tq, tk, r, c * tq, ti * seg + c * tq))

    def scores(chunk):
        g, dilation, back, tq, tk, r, i_loc, i0 = chunk
        ks = pl.multiple_of(jnp.maximum(i0 - back, 0), back)
        q = q_refs[g][r, i_loc:i_loc + tq, :]
        k = k_refs[g][r, pl.ds(ks, tk), :]
        s = lax.dot_general(q, k, (((1,), (1,)), ((), ())), preferred_element_type=F32) * scale
        delta = base_delta[tq] + (i0 - ks)
        return jnp.where((delta >= 0) & (delta <= back), s, MASKED_SCORE), ks

    nxt = scores(chunks[0])
    for idx, (g, dilation, back, tq, tk, r, i_loc, i0) in enumerate(chunks):
        s, ks = nxt
        if idx + 1 < len(chunks):
            nxt = scores(chunks[idx + 1])
        v = v_refs[g][r, pl.ds(ks, tk), :]
        m = jnp.max(s, axis=-1, keepdims=True)
        p = jnp.exp(s - m)
        l = jnp.sum(p, axis=-1, keepdims=True)
        o = jnp.dot(p.astype(BF16), v, preferred_element_type=F32) / l
        lse = jnp.broadcast_to(m + jnp.log(l), o.shape)
        rows = pl.ds(dilation * i_loc + r, tq, stride=dilation) if dilation > 1 else pl.ds(i_loc, tq)
        out_scr[g, rows, :] = o
        lse_scr[g, rows, :] = lse

    for c in range(tile // B_QUERY_CHUNK):
        rows = pl.ds(c * B_QUERY_CHUNK, B_QUERY_CHUNK)
        lses = [lse_scr[g, rows, :] for g in range(n_groups)]
        lse_max = functools.reduce(jnp.maximum, lses)
        weights = [jnp.exp(x - lse_max) for x in lses]
        denom = functools.reduce(lambda a, b: a + b, weights)
        for g in range(n_groups):
            o_ref[g, rows, :] = (out_scr[g, rows, :] * (weights[g] / denom)).astype(o_ref.dtype)


def _attn_b(qkvs, batch, seq, tile=B_TOKEN_TILE):
    n_tiles = seq // tile
    n_groups = len(B_GROUPS)

    def q_spec(g):
        d = B_GROUPS[g][1]
        return pl.BlockSpec((None, d, tile // d, HEAD_DIM), lambda b, h, i: (b, 0, i, h))

    def kv_spec(g, part):
        d = B_GROUPS[g][1]
        return pl.BlockSpec((None, d, seq // d, HEAD_DIM),
                            lambda b, h, i: (b, 0, 0, part * B_HEADS_PER_GROUP + h))

    in_specs = ([q_spec(g) for g in range(n_groups)] + [kv_spec(g, 1) for g in range(n_groups)]
                + [kv_spec(g, 2) for g in range(n_groups)])
    return pl.pallas_call(
        functools.partial(_attn_b_kernel, tile=tile),
        grid=(batch, B_HEADS_PER_GROUP, n_tiles),
        in_specs=in_specs,
        out_specs=pl.BlockSpec((n_groups, tile, HEAD_DIM), lambda b, h, i: (0, b * n_tiles + i, h)),
        out_shape=jax.ShapeDtypeStruct((n_groups, batch * seq, B_GROUP_WIDTH), BF16),
        scratch_shapes=[pltpu.VMEM((n_groups, tile, HEAD_DIM), F32),
                        pltpu.VMEM((n_groups, tile, HEAD_DIM), F32)],
        compiler_params=_params("parallel", "parallel", "arbitrary"),
        name="dilated_attention",
    )(*(list(qkvs) * 3))


def _shift_rows(x, k, row):
    return jnp.where(row >= k, pltpu.roll(x, k, 0), 0.0)


def _pool_kernel(u_lo_ref, u_hi_ref, w_ref, scale_ref, o_ref):
    g = pl.program_id(1)
    u = jnp.concatenate([u_lo_ref[...], u_hi_ref[...]], axis=1)
    row = lax.broadcasted_iota(jnp.int32, u.shape, 0)
    window = jnp.left_shift(2, g)
    total = u
    step = 1
    while step < max(C_WINDOWS):
        widened = total + _shift_rows(total, step, row)
        total = jnp.where(step < window, widened, total)
        step *= 2
    count = jnp.minimum(row + 1, window).astype(F32)
    pooled = total / count - u
    y = jnp.dot(pooled.astype(BF16), w_ref[0], preferred_element_type=F32)
    o_ref[...] = (y * scale_ref[...]).astype(o_ref.dtype)


def _pool(proj, col0, pool_w, pool_scale, batch, seq):
    n_groups = len(C_WINDOWS)
    tiles_per_group = C_GROUP_DIM // LANES
    assert tiles_per_group == 2 and col0 % LANES == 0
    first = col0 // LANES

    def u_spec(half):
        return pl.BlockSpec((seq, LANES), lambda b, g: (b, first + tiles_per_group * g + half))

    return pl.pallas_call(
        _pool_kernel,
        grid=(batch, n_groups),
        in_specs=[u_spec(0), u_spec(1),
                  pl.BlockSpec((1, C_GROUP_DIM, C_GROUP_DIM), lambda b, g: (g, 0, 0)),
                  pl.BlockSpec((1, C_GROUP_DIM), lambda b, g: (0, g))],
        out_specs=pl.BlockSpec((seq, C_GROUP_DIM), lambda b, g: (b, g)),
        out_shape=jax.ShapeDtypeStruct((batch * seq, C_WIDTH), BF16),
        compiler_params=_params("parallel", "parallel"),
        name="multiscale_pool",
    )(proj, proj, pool_w, pool_scale.reshape(1, C_WIDTH))


def _conv_kernel(gb_ref, gc_ref, h_ref, w_ref, o_ref):
    z = gc_ref[...] * h_ref[...]
    row = lax.broadcasted_iota(jnp.int32, z.shape, 0)
    w = w_ref[...]
    y = w[CONV_WIDTH - 1:CONV_WIDTH] * z
    for tap in range(1, CONV_WIDTH):
        y = y + w[CONV_WIDTH - 1 - tap:CONV_WIDTH - tap] * _shift_rows(z, tap, row)
    o_ref[...] = (gb_ref[...] * y).astype(o_ref.dtype)


def _short_conv(proj, col0, conv_w, batch, seq):
    n_col = D_WIDTH // LANES
    assert col0 % LANES == 0
    first = col0 // LANES

    def spec(part):
        return pl.BlockSpec((seq, LANES), lambda b, c: (b, first + part * n_col + c))

    return pl.pallas_call(
        _conv_kernel,
        grid=(batch, n_col),
        in_specs=[spec(0), spec(1), spec(2), pl.BlockSpec((CONV_WIDTH, LANES), lambda b, c: (0, c))],
        out_specs=pl.BlockSpec((seq, LANES), lambda b, c: (b, c)),
        out_shape=jax.ShapeDtypeStruct((batch * seq, D_WIDTH), BF16),
        compiler_params=_params("parallel", "parallel"),
        name="short_conv",
    )(proj, proj, proj, conv_w)


def _rope_tables(seq, dim):
    half = dim // 2
    inv = ROPE_THETA ** (-jnp.arange(0, dim, 2, dtype=F32) / dim)
    ang = jnp.arange(seq, dtype=F32)[:, None] * inv[None, :]
    cos, sin = jnp.cos(ang), jnp.sin(ang)
    reps = LANES // dim
    cos_t = jnp.tile(jnp.concatenate([cos, cos], axis=1), (1, reps))
    zeros = jnp.zeros_like(sin)
    if dim == LANES:
        return (cos_t, jnp.concatenate([-sin, sin], axis=1)), (half,)
    upper = jnp.tile(jnp.concatenate([zeros, sin], axis=1), (1, reps))
    lower = jnp.tile(jnp.concatenate([-sin, zeros], axis=1), (1, reps))
    return (cos_t, upper, lower), (half, LANES - half)


def kernel(x, w_in, w_out, norm_mix, norm_mlp, diff_lambda, diff_subln, pool_w, pool_scale, conv_w,
           w_up, w_down, norm_final):
    batch, seq, d_model = x.shape
    depth = w_in.shape[0]
    m = batch * seq
    x = x.reshape(m, d_model)

    tables_a, shifts_a = _rope_tables(seq, A_SUB)
    tables_b, shifts_b = _rope_tables(seq, HEAD_DIM)

    a_qk_end = 2 * A_WIDTH
    a_end = 3 * A_WIDTH
    b_end = a_end + 3 * B_WIDTH
    c_end = b_end + C_WIDTH
    t_a = 512
    tail0 = (b_end // IN_PROJ_COL_TILE) * IN_PROJ_COL_TILE
    in_width = w_in.shape[2]

    w_in = w_in.astype(BF16)
    w_a_vt = jnp.swapaxes(w_in[:, :, a_qk_end:a_end], 1, 2)

    h, ss = _prep(x, norm_mix[0])
    for l in range(depth):
        lam_init = 0.8 - 0.6 * math.exp(-0.3 * l)

        qk_a = _mm_rope(h, ss, w_in, l, a_qk_end, tables_a, shifts_a, A_SCORE_SCALE, A_WIDTH, seq)
        vt_a = _mm_transposed(h, ss, w_a_vt, l, t_a)
        qkv_b = [_mm_rope_permute(h, ss, w_in, l, a_end, g, tables_b, shifts_b[0], B_GROUPS[g][1], batch, seq)
                 for g in range(len(B_GROUPS))]
        tail = _mm_cols(h, ss, w_in, l, tail0, in_width - tail0, F32, name="in_proj_tail")

        out_a = _attn_a(qk_a, vt_a, diff_lambda[l], diff_subln[l], lam_init, batch, seq, t_a)
        out_b = _attn_b(qkv_b, batch, seq)
        out_c = _pool(tail, b_end - tail0, pool_w[l].astype(BF16), pool_scale[l], batch, seq)
        out_d = _short_conv(tail, c_end - tail0, conv_w[l], batch, seq)

        x, h, ss = _out_proj(out_a, out_b, out_c, out_d, w_out, l, x, norm_mlp[l])
        act, w_down_l = _mlp_up(h, ss, w_up, w_down, l, 512)
        if l + 1 < depth:
            x, h, ss = _mm_residual(act, w_down_l, x, norm_mix[l + 1])
        else:
            x, = _mm_residual(act, w_down_l, x)

    out = _rmsnorm(x, norm_final, F32)
    return out.reshape(batch, seq, d_model)
```

```python
import functools
import math

import jax
import jax.numpy as jnp
from jax import lax
from jax.experimental import pallas as pl
from jax.experimental.pallas import tpu as pltpu

D_MODEL = 4096
HEAD_DIM = 128
A_HEADS = 8
A_WIDTH = A_HEADS * HEAD_DIM
A_SUB = HEAD_DIM // 2
A_SCORE_SCALE = A_SUB ** -0.5 * math.log2(math.e)
A_ONES_ROWS = 16
B_GROUPS = ((128, 1), (512, 4), (2048, 16))
B_HEADS_PER_GROUP = 3
B_GROUP_WIDTH = B_HEADS_PER_GROUP * HEAD_DIM
B_WIDTH = B_GROUP_WIDTH * len(B_GROUPS)
B_TOKEN_TILE = 2048
B_QUERY_CHUNK = 256
C_WINDOWS = (2, 4, 8, 16)
C_WIDTH = 1024
C_GROUP_DIM = C_WIDTH // len(C_WINDOWS)
D_WIDTH = D_MODEL - A_WIDTH - B_WIDTH - C_WIDTH
CONV_WIDTH = 3
D_FF = 4 * D_MODEL
ROPE_THETA = 10000.0
NORM_EPS = 1e-6
DIFF_EPS = 1e-5

LANES = 128
MXU_COLS = 256
MLP_ROW_TILE = 2048
IN_PROJ_COL_TILE = 1024
IN_PROJ_ROW_TILE = 1024
VMEM_LIMIT_BYTES = 56 * 1024 * 1024
MASKED_SCORE = -1e30

F32 = jnp.float32
BF16 = jnp.bfloat16


def _params(*semantics):
    return pltpu.CompilerParams(dimension_semantics=semantics, vmem_limit_bytes=VMEM_LIMIT_BYTES)


def _rmsnorm_kernel(x_ref, g_ref, o_ref, *, eps):
    x = x_ref[...]
    ms = jnp.mean(x * x, axis=-1, keepdims=True)
    o_ref[...] = (x * lax.rsqrt(ms + eps) * g_ref[...]).astype(o_ref.dtype)


def _rmsnorm(x, g, out_dtype, tm=256):
    m, d = x.shape
    return pl.pallas_call(
        functools.partial(_rmsnorm_kernel, eps=NORM_EPS),
        grid=(m // tm,),
        in_specs=[pl.BlockSpec((tm, d), lambda i: (i, 0)),
                  pl.BlockSpec((1, d), lambda i: (0, 0))],
        out_specs=pl.BlockSpec((tm, d), lambda i: (i, 0)),
        out_shape=jax.ShapeDtypeStruct((m, d), out_dtype),
        compiler_params=_params("parallel"),
        name="rmsnorm",
    )(x, g.reshape(1, d))


def _prep_kernel(x_ref, g_ref, xb_ref, ss_ref):
    x = x_ref[...]
    xb_ref[...] = (x * g_ref[...]).astype(xb_ref.dtype)
    ss_ref[...] = jnp.broadcast_to(jnp.sum(x * x, axis=-1, keepdims=True), ss_ref.shape)


def _prep(x, g, tm=256):
    m, d = x.shape
    return pl.pallas_call(
        _prep_kernel,
        grid=(m // tm,),
        in_specs=[pl.BlockSpec((tm, d), lambda i: (i, 0)),
                  pl.BlockSpec((1, d), lambda i: (0, 0))],
        out_specs=[pl.BlockSpec((tm, d), lambda i: (i, 0)),
                   pl.BlockSpec((tm, LANES), lambda i: (i, 0))],
        out_shape=[jax.ShapeDtypeStruct((m, d), BF16), jax.ShapeDtypeStruct((m, LANES), F32)],
        compiler_params=_params("parallel"),
        name="norm_prep",
    )(x, g.reshape(1, d))


def _row_scale(ss_ref):
    return lax.rsqrt(ss_ref[...] * (1.0 / D_MODEL) + NORM_EPS)


def _scale_rows(y, rs):
    return y * jnp.concatenate([rs] * (y.shape[1] // LANES), axis=1)


def _store_norm_inputs(x_new, g_ref, xb_ref, ss_ref, first):
    xb_ref[...] = (x_new * g_ref[...]).astype(xb_ref.dtype)
    part = jnp.broadcast_to(jnp.sum(x_new * x_new, axis=-1, keepdims=True), ss_ref.shape)

    @pl.when(first)
    def _():
        ss_ref[...] = part

    @pl.when(jnp.logical_not(first))
    def _():
        ss_ref[...] += part


def _rotate(y, cos_ref, sin_refs, shifts):
    outs = []
    for c in range(y.shape[1] // LANES):
        yc = y[:, c * LANES:(c + 1) * LANES]
        oc = yc * cos_ref[...]
        for s_ref, shift in zip(sin_refs, shifts):
            oc = oc + pltpu.roll(yc, shift, 1) * s_ref[...]
        outs.append(oc)
    return jnp.concatenate(outs, axis=1)


def _mm_rope_kernel(a_ref, ss_ref, w_ref, cos_ref, *rest, shifts, q_scale, n_q_blocks):
    sin_refs, o_ref = rest[:-1], rest[-1]
    scale = jnp.where(pl.program_id(1) < n_q_blocks, q_scale, 1.0).astype(F32)
    a = a_ref[...]
    rs = _row_scale(ss_ref)
    for c in range(o_ref.shape[1] // MXU_COLS):
        cols = slice(c * MXU_COLS, (c + 1) * MXU_COLS)
        acc = _scale_rows(jnp.dot(a, w_ref[:, cols].astype(BF16), preferred_element_type=F32), rs)
        o_ref[:, cols] = (_rotate(acc, cos_ref, sin_refs, shifts) * scale).astype(o_ref.dtype)


def _mm_rope(a, ss, w, layer, n, tables, shifts, q_scale, n_q_cols, seq, bn=IN_PROJ_COL_TILE,
             bm=IN_PROJ_ROW_TILE):
    m, k = a.shape
    tab_blocks = seq // bm
    tab_spec = pl.BlockSpec((bm, LANES), lambda i, j: (i % tab_blocks, 0))
    return pl.pallas_call(
        functools.partial(_mm_rope_kernel, shifts=shifts, q_scale=q_scale, n_q_blocks=n_q_cols // bn),
        grid=(m // bm, n // bn),
        in_specs=[pl.BlockSpec((bm, k), lambda i, j: (i, 0)),
                  pl.BlockSpec((bm, LANES), lambda i, j: (i, 0)),
                  pl.BlockSpec((None, k, bn), lambda i, j: (layer, 0, j))] + [tab_spec] * len(tables),
        out_specs=pl.BlockSpec((bm, bn), lambda i, j: (i, j)),
        out_shape=jax.ShapeDtypeStruct((m, n), BF16),
        compiler_params=_params("parallel", "arbitrary"),
        name="in_proj_rope",
    )(a, ss, w, *tables)


def _mm_rope_permute_kernel(a_ref, ss_ref, wq_ref, wk_ref, wv_ref, cos_ref, sin_ref, o_ref, scr_ref, *,
                            shift, n_rope_cols, dilation):
    a = a_ref[...]
    w = jnp.concatenate([wq_ref[...], wk_ref[...], wv_ref[...]], axis=1)
    rs = _row_scale(ss_ref)
    n = o_ref.shape[2]
    rows = a.shape[0] // dilation
    for lo in range(0, n, 2 * MXU_COLS):
        hi = min(lo + 2 * MXU_COLS, n)
        acc = _scale_rows(jnp.dot(a, w[:, lo:hi], preferred_element_type=F32), rs)
        rope_hi = min(hi, n_rope_cols)
        if lo < rope_hi:
            rot = _rotate(acc[:, :rope_hi - lo], cos_ref, (sin_ref,), (shift,))
            acc = rot if rope_hi == hi else jnp.concatenate([rot, acc[:, rope_hi - lo:]], axis=1)
        if dilation == 1:
            o_ref[0, :, lo:hi] = acc.astype(o_ref.dtype)
        else:
            for c in range(lo // LANES, hi // LANES):
                cols = slice(c * LANES, (c + 1) * LANES)
                scr_ref[c] = acc[:, c * LANES - lo:(c + 1) * LANES - lo]
                for r in range(dilation):
                    o_ref[r, :, cols] = scr_ref[c, pl.ds(r, rows, stride=dilation), :].astype(o_ref.dtype)


def _mm_rope_permute(a, ss, w, layer, col0, group, tables, shift, dilation, batch, seq, bm=IN_PROJ_ROW_TILE):
    m, k = a.shape
    n = 3 * B_GROUP_WIDTH
    n_rope_cols = 2 * B_GROUP_WIDTH
    tiles = seq // bm
    tab_spec = pl.BlockSpec((bm, LANES), lambda i: (i % tiles, 0))
    n_groups = len(B_GROUPS)
    first = col0 // B_GROUP_WIDTH
    assert first * B_GROUP_WIDTH == col0

    def w_spec(part):
        return pl.BlockSpec((None, k, B_GROUP_WIDTH), lambda i: (layer, 0, first + part * n_groups + group),
                            pipeline_mode=pl.Buffered(1))

    return pl.pallas_call(
        functools.partial(_mm_rope_permute_kernel, shift=shift, n_rope_cols=n_rope_cols, dilation=dilation),
        grid=(m // bm,),
        in_specs=[pl.BlockSpec((bm, k), lambda i: (i, 0)), pl.BlockSpec((bm, LANES), lambda i: (i, 0)),
                  w_spec(0), w_spec(1), w_spec(2), tab_spec, tab_spec],
        out_specs=pl.BlockSpec((None, dilation, bm // dilation, n), lambda i: (i // tiles, 0, i % tiles, 0)),
        out_shape=jax.ShapeDtypeStruct((batch, dilation, seq // dilation, n), BF16),
        scratch_shapes=[pltpu.VMEM((n // LANES, bm, LANES), F32)],
        compiler_params=_params("parallel"),
        name="in_proj_dilated",
    )(a, ss, w, w, w, *tables)


def _mm_kernel(a_ref, ss_ref, w_ref, o_ref, *, relu_sq):
    acc = jnp.dot(a_ref[...], w_ref[...].astype(BF16), preferred_element_type=F32)
    acc = _scale_rows(acc, _row_scale(ss_ref))
    if relu_sq:
        acc = jnp.square(jnp.maximum(acc, 0.0))
    o_ref[...] = acc.astype(o_ref.dtype)


def _mm_cols(a, ss, w, layer, col0, n, out_dtype, bn=IN_PROJ_COL_TILE, bm=IN_PROJ_ROW_TILE, name="matmul"):
    m, k = a.shape
    first = col0 // bn
    assert first * bn == col0 and n % bn == 0
    return pl.pallas_call(
        functools.partial(_mm_kernel, relu_sq=False),
        grid=(m // bm, n // bn),
        in_specs=[pl.BlockSpec((bm, k), lambda i, j: (i, 0)),
                  pl.BlockSpec((bm, LANES), lambda i, j: (i, 0)),
                  pl.BlockSpec((None, k, bn), lambda i, j: (layer, 0, first + j))],
        out_specs=pl.BlockSpec((bm, bn), lambda i, j: (i, j)),
        out_shape=jax.ShapeDtypeStruct((m, n), out_dtype),
        compiler_params=_params("parallel", "arbitrary"),
        name=name,
    )(a, ss, w)


def _mlp_up_kernel(a_ref, ss_ref, w_ref, wd_ref, o_ref, wd_bf16_ref):
    _mm_kernel(a_ref, ss_ref, w_ref, o_ref, relu_sq=True)
    wd_bf16_ref[...] = wd_ref[...].astype(BF16)


def _mlp_up(a, ss, w, w_down, layer, bn, bm=MLP_ROW_TILE):
    m, k = a.shape
    n = w.shape[2]
    n_j = n // bn
    steps = (m // bm) * n_j
    slab = n // steps
    assert slab * steps == n
    return pl.pallas_call(
        _mlp_up_kernel,
        grid=(m // bm, n_j),
        in_specs=[pl.BlockSpec((bm, k), lambda i, j: (i, 0), pipeline_mode=pl.Buffered(1)),
                  pl.BlockSpec((bm, LANES), lambda i, j: (i, 0)),
                  pl.BlockSpec((None, k, bn), lambda i, j: (layer, 0, j)),
                  pl.BlockSpec((None, slab, k), lambda i, j: (layer, i * n_j + j, 0))],
        out_specs=[pl.BlockSpec((None, None, bm, bn), lambda i, j: (i, j, 0, 0)),
                   pl.BlockSpec((slab, k), lambda i, j: (i * n_j + j, 0))],
        out_shape=[jax.ShapeDtypeStruct((m // bm, n_j, bm, bn), BF16), jax.ShapeDtypeStruct((n, k), BF16)],
        compiler_params=_params("parallel", "arbitrary"),
        name="mlp_up",
    )(a, ss, w, w_down)


def _mm_transposed_kernel(wt_ref, a_ref, ss_ref, o_ref, *, t):
    acc = lax.dot_general(wt_ref[...], a_ref[...], (((1,), (1,)), ((), ())), preferred_element_type=F32)
    acc = acc * jnp.transpose(_row_scale(ss_ref))[0:1, :]
    for c in range(acc.shape[1] // t):
        o_ref[c] = acc[:, c * t:(c + 1) * t].astype(o_ref.dtype)


def _mm_transposed(a, ss, wt, layer, t, bm=1024):
    m, k = a.shape
    n = wt.shape[1]
    return pl.pallas_call(
        functools.partial(_mm_transposed_kernel, t=t),
        grid=(m // bm,),
        in_specs=[pl.BlockSpec((None, n, k), lambda i: (layer, 0, 0), pipeline_mode=pl.Buffered(1)),
                  pl.BlockSpec((bm, k), lambda i: (i, 0)),
                  pl.BlockSpec((bm, LANES), lambda i: (i, 0))],
        out_specs=pl.BlockSpec((bm // t, n, t), lambda i: (i, 0, 0)),
        out_shape=jax.ShapeDtypeStruct((m // t, n, t), BF16),
        compiler_params=_params("parallel"),
        name="in_proj_vt",
    )(wt, a, ss)


def _mm_residual_kernel(a_ref, w_ref, r_ref, *rest):
    o_ref = rest[0] if len(rest) == 1 else rest[2]
    kk = pl.program_id(2)

    @pl.when(kk == 0)
    def _():
        o_ref[...] = r_ref[...]

    a = jnp.concatenate([a_ref[c] for c in range(a_ref.shape[0])], axis=1)
    o_ref[...] += jnp.dot(a, w_ref[...], preferred_element_type=F32)

    if len(rest) > 1:
        g_ref, w_next_ref, _, xb_ref, ss_ref, w_next_bf16_ref = rest
        w_next_bf16_ref[...] = w_next_ref[...].astype(BF16)

        @pl.when(kk == pl.num_programs(2) - 1)
        def _():
            _store_norm_inputs(o_ref[...], g_ref, xb_ref, ss_ref, pl.program_id(1) == 0)


def _mm_residual(a, w, r, g=None, w_next=None, next_layer=None, bm=1024, bn=1024, bk=4096):
    row_tiles, col_tiles, tile_rows, tile_cols = a.shape
    m, k = row_tiles * tile_rows, col_tiles * tile_cols
    n = w.shape[1]
    sub = tile_rows // bm
    assert sub * bm == tile_rows
    tile_spec = pl.BlockSpec((bm, bn), lambda i, j, kk: (i, j))
    in_specs = [pl.BlockSpec((None, bk // tile_cols, bm, tile_cols), lambda i, j, kk: (i // sub, kk, i % sub, 0)),
                pl.BlockSpec((bk, bn), lambda i, j, kk: (kk, j)),
                pl.BlockSpec((bm, bn), lambda i, j, kk: (i, j), pipeline_mode=pl.Buffered(1))]
    out_specs = [tile_spec]
    out_shape = [jax.ShapeDtypeStruct((m, n), F32)]
    operands = [a, w, r]
    if g is not None:
        n_j, n_k = n // bn, k // bk
        steps = (m // bm) * n_j * n_k
        rows_next, cols_next = w_next.shape[1:]
        slab = rows_next // steps
        assert slab * steps == rows_next

        def slab_index(i, j, kk):
            return (i * n_j + j) * n_k + kk

        in_specs += [pl.BlockSpec((1, bn), lambda i, j, kk: (0, j)),
                     pl.BlockSpec((None, slab, cols_next), lambda i, j, kk: (next_layer, slab_index(i, j, kk), 0))]
        out_specs += [tile_spec, pl.BlockSpec((bm, LANES), lambda i, j, kk: (i, 0)),
                      pl.BlockSpec((slab, cols_next), lambda i, j, kk: (slab_index(i, j, kk), 0))]
        out_shape += [jax.ShapeDtypeStruct((m, n), BF16), jax.ShapeDtypeStruct((m, LANES), F32),
                      jax.ShapeDtypeStruct((rows_next, cols_next), BF16)]
        operands += [g.reshape(1, n), w_next]
    return pl.pallas_call(
        _mm_residual_kernel,
        grid=(m // bm, n // bn, k // bk),
        in_specs=in_specs,
        out_specs=out_specs,
        out_shape=out_shape,
        compiler_params=_params("parallel", "arbitrary", "arbitrary"),
        name="mlp_down",
    )(*operands)


def _out_proj_kernel(a_ref, b_ref, c_ref, d_ref, w_ref, r_ref, g_ref, o_ref, xb_ref, ss_ref):
    mix = jnp.concatenate([a_ref[...]] + [b_ref[g] for g in range(len(B_GROUPS))] + [c_ref[...], d_ref[...]], axis=1)
    x_new = r_ref[...] + jnp.dot(mix, w_ref[...].astype(BF16), preferred_element_type=F32)
    o_ref[...] = x_new
    _store_norm_inputs(x_new, g_ref, xb_ref, ss_ref, pl.program_id(1) == 0)


def _out_proj(out_a, out_b, out_c, out_d, w, layer, r, g, bm=1024, bn=512):
    m = out_a.shape[0]
    n = w.shape[2]
    n_groups = len(B_GROUPS)
    return pl.pallas_call(
        _out_proj_kernel,
        grid=(m // bm, n // bn),
        in_specs=[pl.BlockSpec((bm, A_WIDTH), lambda i, j: (i, 0)),
                  pl.BlockSpec((n_groups, bm, B_GROUP_WIDTH), lambda i, j: (0, i, 0)),
                  pl.BlockSpec((bm, C_WIDTH), lambda i, j: (i, 0)),
                  pl.BlockSpec((bm, D_WIDTH), lambda i, j: (i, 0)),
                  pl.BlockSpec((None, D_MODEL, bn), lambda i, j: (layer, 0, j)),
                  pl.BlockSpec((bm, bn), lambda i, j: (i, j)),
                  pl.BlockSpec((1, bn), lambda i, j: (0, j))],
        out_specs=[pl.BlockSpec((bm, bn), lambda i, j: (i, j)),
                   pl.BlockSpec((bm, bn), lambda i, j: (i, j)),
                   pl.BlockSpec((bm, LANES), lambda i, j: (i, 0))],
        out_shape=[jax.ShapeDtypeStruct((m, n), F32), jax.ShapeDtypeStruct((m, n), BF16),
                   jax.ShapeDtypeStruct((m, LANES), F32)],
        compiler_params=_params("parallel", "arbitrary"),
        name="out_proj",
    )(out_a, out_b, out_c, out_d, w, r, g.reshape(1, n))


def _attn_a_kernel(q_ref, k_ref, vt_ref, lam_ref, g_ref, o_ref, *scratch, t, heads, lam_init):
    qi = pl.program_id(2)
    lane = lax.broadcasted_iota(jnp.int32, (t, HEAD_DIM), 1)
    qqs = []
    for h in range(heads):
        q = q_ref[:, h * HEAD_DIM:(h + 1) * HEAD_DIM]
        zero = jnp.zeros_like(q)
        qqs.append(jnp.concatenate([jnp.where(lane < A_SUB, q, zero), jnp.where(lane >= A_SUB, q, zero)], axis=0))

    ones = jnp.ones((A_ONES_ROWS, t), BF16)
    m_scr, acc_scr = scratch[:heads], scratch[heads:]
    for h in range(heads):
        m_scr[h][...] = jnp.full(m_scr[h].shape, MASKED_SCORE, F32)
        acc_scr[h][...] = jnp.zeros(acc_scr[h].shape, F32)

    def step(j, masked):
        def scores(h):
            k = k_ref[pl.ds(pl.multiple_of(j * t, t), t), h * HEAD_DIM:(h + 1) * HEAD_DIM]
            s = lax.dot_general(k, qqs[h], (((1,), (1,)), ((), ())), preferred_element_type=F32)
            if masked:
                key = lax.broadcasted_iota(jnp.int32, s.shape, 0)
                lane_q = lax.broadcasted_iota(jnp.int32, s.shape, 1)
                qry = jnp.where(lane_q >= t, lane_q - t, lane_q)
                s = jnp.where(key <= qry, s, MASKED_SCORE)
            return s

        s_next = scores(0)
        for h in range(heads):
            s = s_next
            if h + 1 < heads:
                s_next = scores(h + 1)
            vt = jnp.concatenate([vt_ref[j, h * HEAD_DIM:(h + 1) * HEAD_DIM, :], ones], axis=0)
            m = m_scr[h][...]
            m_new = jnp.maximum(m, jnp.max(s, axis=0, keepdims=True))
            alpha = jnp.exp2(m - m_new)
            p = jnp.exp2(s - m_new)
            m_scr[h][...] = m_new
            acc_scr[h][...] = alpha * acc_scr[h][...] + jnp.dot(vt, p.astype(BF16), preferred_element_type=F32)

    def body(j, carry):
        step(j, masked=False)
        return carry

    lax.fori_loop(0, qi, body, 0)
    step(qi, masked=True)

    lp = lam_ref[...]
    lam = (jnp.exp(jnp.sum(lp[0:1] * lp[1:2], axis=-1, keepdims=True))
           - jnp.exp(jnp.sum(lp[2:3] * lp[3:4], axis=-1, keepdims=True)) + lam_init)
    for h in range(heads):
        acc = acc_scr[h][...]
        o = acc[:HEAD_DIM] / acc[HEAD_DIM:HEAD_DIM + 1]
        o = (o[:, :t] - lam * o[:, t:]).T
        ms = jnp.mean(o * o, axis=-1, keepdims=True)
        o = o * lax.rsqrt(ms + DIFF_EPS) * g_ref[...]
        o_ref[:, h * HEAD_DIM:(h + 1) * HEAD_DIM] = (o * (1.0 - lam_init)).astype(o_ref.dtype)


def _attn_a(qk, vt, lam_params, subln_g, lam_init, batch, seq, t, heads=8):
    nq = seq // t
    width = heads * HEAD_DIM
    n_pairs = A_HEADS // heads
    return pl.pallas_call(
        functools.partial(_attn_a_kernel, t=t, heads=heads, lam_init=lam_init),
        grid=(batch, n_pairs, nq),
        in_specs=[pl.BlockSpec((t, width), lambda b, h, i: (b * nq + i, h)),
                  pl.BlockSpec((seq, width), lambda b, h, i: (b, n_pairs + h)),
                  pl.BlockSpec((nq, width, t), lambda b, h, i: (b, h, 0)),
                  pl.BlockSpec((4, A_SUB), lambda b, h, i: (0, 0)),
                  pl.BlockSpec((1, HEAD_DIM), lambda b, h, i: (0, 0))],
        out_specs=pl.BlockSpec((t, width), lambda b, h, i: (b * nq + i, h)),
        out_shape=jax.ShapeDtypeStruct((batch * seq, A_WIDTH), BF16),
        scratch_shapes=([pltpu.VMEM((1, 2 * t), F32)] * heads
                        + [pltpu.VMEM((HEAD_DIM + A_ONES_ROWS, 2 * t), F32)] * heads),
        compiler_params=_params("parallel", "parallel", "arbitrary"),
        name="diff_attention",
    )(qk, qk, vt, lam_params, subln_g.reshape(1, HEAD_DIM))


def _attn_b_kernel(*refs, tile):
    n_groups = len(B_GROUPS)
    q_refs = refs[0:n_groups]
    k_refs = refs[n_groups:2 * n_groups]
    v_refs = refs[2 * n_groups:3 * n_groups]
    o_ref = refs[3 * n_groups]
    out_scr, lse_scr = refs[3 * n_groups + 1:]
    ti = pl.program_id(2)
    scale = HEAD_DIM ** -0.5
    base_delta = {}
    chunks = []

    for g, (window, dilation) in enumerate(B_GROUPS):
        back = window // dilation
        seg = tile // dilation
        tq = min(B_QUERY_CHUNK, seg)
        tk = tq + back
        if tq not in base_delta:
            base_delta[tq] = (lax.broadcasted_iota(jnp.int32, (tq, tk), 0)
                              - lax.broadcasted_iota(jnp.int32, (tq, tk), 1))
        for r in range(dilation):
            for c in range(seg // tq):
                chunks.append((g, dilation, back, tq, tk, r, c * tq, ti * seg + c * tq))

    def scores(chunk):
        g, dilation, back, tq, tk, r, i_loc, i0 = chunk
        ks = pl.multiple_of(jnp.maximum(i0 - back, 0), back)
        q = q_refs[g][r, i_loc:i_loc + tq, :]
        k = k_refs[g][r, pl.ds(ks, tk), :]
        s = lax.dot_general(q, k, (((1,), (1,)), ((), ())), preferred_element_type=F32) * scale
        delta = base_delta[tq] + (i0 - ks)
        return jnp.where((delta >= 0) & (delta <= back), s, MASKED_SCORE), ks

    nxt = scores(chunks[0])
    for idx, (g, dilation, back, tq, tk, r, i_loc, i0) in enumerate(chunks):
        s, ks = nxt
        if idx + 1 < len(chunks):
            nxt = scores(chunks[idx + 1])
        v = v_refs[g][r, pl.ds(ks, tk), :]
        m = jnp.max(s, axis=-1, keepdims=True)
        p = jnp.exp(s - m)
        l = jnp.sum(p, axis=-1, keepdims=True)
        o = jnp.dot(p.astype(BF16), v, preferred_element_type=F32) / l
        lse = jnp.broadcast_to(m + jnp.log(l), o.shape)
        rows = pl.ds(dilation * i_loc + r, tq, stride=dilation) if dilation > 1 else pl.ds(i_loc, tq)
        out_scr[g, rows, :] = o
        lse_scr[g, rows, :] = lse

    for c in range(tile // B_QUERY_CHUNK):
        rows = pl.ds(c * B_QUERY_CHUNK, B_QUERY_CHUNK)
        lses = [lse_scr[g, rows, :] for g in range(n_groups)]
        lse_max = functools.reduce(jnp.maximum, lses)
        weights = [jnp.exp(x - lse_max) for x in lses]
        denom = functools.reduce(lambda a, b: a + b, weights)
        for g in range(n_groups):
            o_ref[g, rows, :] = (out_scr[g, rows, :] * (weights[g] / denom)).astype(o_ref.dtype)


def _attn_b(qkvs, batch, seq, tile=B_TOKEN_TILE):
    n_tiles = seq // tile
    n_groups = len(B_GROUPS)

    def q_spec(g):
        d = B_GROUPS[g][1]
        return pl.BlockSpec((None, d, tile // d, HEAD_DIM), lambda b, h, i: (b, 0, i, h))

    def kv_spec(g, part):
        d = B_GROUPS[g][1]
        return pl.BlockSpec((None, d, seq // d, HEAD_DIM),
                            lambda b, h, i: (b, 0, 0, part * B_HEADS_PER_GROUP + h))

    in_specs = ([q_spec(g) for g in range(n_groups)] + [kv_spec(g, 1) for g in range(n_groups)]
                + [kv_spec(g, 2) for g in range(n_groups)])
    return pl.pallas_call(
        functools.partial(_attn_b_kernel, tile=tile),
        grid=(batch, B_HEADS_PER_GROUP, n_tiles),
        in_specs=in_specs,
        out_specs=pl.BlockSpec((n_groups, tile, HEAD_DIM), lambda b, h, i: (0, b * n_tiles + i, h)),
        out_shape=jax.ShapeDtypeStruct((n_groups, batch * seq, B_GROUP_WIDTH), BF16),
        scratch_shapes=[pltpu.VMEM((n_groups, tile, HEAD_DIM), F32),
                        pltpu.VMEM((n_groups, tile, HEAD_DIM), F32)],
        compiler_params=_params("parallel", "parallel", "arbitrary"),
        name="dilated_attention",
    )(*(list(qkvs) * 3))


def _shift_rows(x, k, row):
    return jnp.where(row >= k, pltpu.roll(x, k, 0), 0.0)


def _pool_kernel(u_lo_ref, u_hi_ref, w_ref, scale_ref, o_ref):
    g = pl.program_id(1)
    u = jnp.concatenate([u_lo_ref[...], u_hi_ref[...]], axis=1)
    row = lax.broadcasted_iota(jnp.int32, u.shape, 0)
    window = jnp.left_shift(2, g)
    total = u
    step = 1
    while step < max(C_WINDOWS):
        widened = total + _shift_rows(total, step, row)
        total = jnp.where(step < window, widened, total)
        step *= 2
    count = jnp.minimum(row + 1, window).astype(F32)
    pooled = total / count - u
    y = jnp.dot(pooled.astype(BF16), w_ref[0], preferred_element_type=F32)
    o_ref[...] = (y * scale_ref[...]).astype(o_ref.dtype)


def _pool(proj, col0, pool_w, pool_scale, batch, seq):
    n_groups = len(C_WINDOWS)
    tiles_per_group = C_GROUP_DIM // LANES
    assert tiles_per_group == 2 and col0 % LANES == 0
    first = col0 // LANES

    def u_spec(half):
        return pl.BlockSpec((seq, LANES), lambda b, g: (b, first + tiles_per_group * g + half))

    return pl.pallas_call(
        _pool_kernel,
        grid=(batch, n_groups),
        in_specs=[u_spec(0), u_spec(1),
                  pl.BlockSpec((1, C_GROUP_DIM, C_GROUP_DIM), lambda b, g: (g, 0, 0)),
                  pl.BlockSpec((1, C_GROUP_DIM), lambda b, g: (0, g))],
        out_specs=pl.BlockSpec((seq, C_GROUP_DIM), lambda b, g: (b, g)),
        out_shape=jax.ShapeDtypeStruct((batch * seq, C_WIDTH), BF16),
        compiler_params=_params("parallel", "parallel"),
        name="multiscale_pool",
    )(proj, proj, pool_w, pool_scale.reshape(1, C_WIDTH))


def _conv_kernel(gb_ref, gc_ref, h_ref, w_ref, o_ref):
    z = gc_ref[...] * h_ref[...]
    row = lax.broadcasted_iota(jnp.int32, z.shape, 0)
    w = w_ref[...]
    y = w[CONV_WIDTH - 1:CONV_WIDTH] * z
    for tap in range(1, CONV_WIDTH):
        y = y + w[CONV_WIDTH - 1 - tap:CONV_WIDTH - tap] * _shift_rows(z, tap, row)
    o_ref[...] = (gb_ref[...] * y).astype(o_ref.dtype)


def _short_conv(proj, col0, conv_w, batch, seq):
    n_col = D_WIDTH // LANES
    assert col0 % LANES == 0
    first = col0 // LANES

    def spec(part):
        return pl.BlockSpec((seq, LANES), lambda b, c: (b, first + part * n_col + c))

    return pl.pallas_call(
        _conv_kernel,
        grid=(batch, n_col),
        in_specs=[spec(0), spec(1), spec(2), pl.BlockSpec((CONV_WIDTH, LANES), lambda b, c: (0, c))],
        out_specs=pl.BlockSpec((seq, LANES), lambda b, c: (b, c)),
        out_shape=jax.ShapeDtypeStruct((batch * seq, D_WIDTH), BF16),
        compiler_params=_params("parallel", "parallel"),
        name="short_conv",
    )(proj, proj, proj, conv_w)


def _rope_tables(seq, dim):
    half = dim // 2
    inv = ROPE_THETA ** (-jnp.arange(0, dim, 2, dtype=F32) / dim)
    ang = jnp.arange(seq, dtype=F32)[:, None] * inv[None, :]
    cos, sin = jnp.cos(ang), jnp.sin(ang)
    reps = LANES // dim
    cos_t = jnp.tile(jnp.concatenate([cos, cos], axis=1), (1, reps))
    zeros = jnp.zeros_like(sin)
    if dim == LANES:
        return (cos_t, jnp.concatenate([-sin, sin], axis=1)), (half,)
    upper = jnp.tile(jnp.concatenate([zeros, sin], axis=1), (1, reps))
    lower = jnp.tile(jnp.concatenate([-sin, zeros], axis=1), (1, reps))
    return (cos_t, upper, lower), (half, LANES - half)


def kernel(x, w_in, w_out, norm_mix, norm_mlp, diff_lambda, diff_subln, pool_w, pool_scale, conv_w,
           w_up, w_down, norm_final):
    batch, seq, d_model = x.shape
    depth = w_in.shape[0]
    m = batch * seq
    x = x.reshape(m, d_model)

    tables_a, shifts_a = _rope_tables(seq, A_SUB)
    tables_b, shifts_b = _rope_tables(seq, HEAD_DIM)

    a_qk_end = 2 * A_WIDTH
    a_end = 3 * A_WIDTH
    b_end = a_end + 3 * B_WIDTH
    c_end = b_end + C_WIDTH
    t_a = 512
    tail0 = (b_end // IN_PROJ_COL_TILE) * IN_PROJ_COL_TILE
    in_width = w_in.shape[2]

    w_in_l = w_in[0].astype(BF16)[None]

    h, ss = _prep(x, norm_mix[0])
    for l in range(depth):
        lam_init = 0.8 - 0.6 * math.exp(-0.3 * l)
        w_a_vt = jnp.swapaxes(w_in_l[:, :, a_qk_end:a_end], 1, 2)

        qk_a = _mm_rope(h, ss, w_in_l, 0, a_qk_end, tables_a, shifts_a, A_SCORE_SCALE, A_WIDTH, seq)
        vt_a = _mm_transposed(h, ss, w_a_vt, 0, t_a)
        qkv_b = [_mm_rope_permute(h, ss, w_in_l, 0, a_end, g, tables_b, shifts_b[0], B_GROUPS[g][1], batch, seq)
                 for g in range(len(B_GROUPS))]
        tail = _mm_cols(h, ss, w_in_l, 0, tail0, in_width - tail0, F32, name="in_proj_tail")

        out_a = _attn_a(qk_a, vt_a, diff_lambda[l], diff_subln[l], lam_init, batch, seq, t_a)
        out_b = _attn_b(qkv_b, batch, seq)
        out_c = _pool(tail, b_end - tail0, pool_w[l].astype(BF16), pool_scale[l], batch, seq)
        out_d = _short_conv(tail, c_end - tail0, conv_w[l], batch, seq)

        x, h, ss = _out_proj(out_a, out_b, out_c, out_d, w_out, l, x, norm_mlp[l])
        act, w_down_l = _mlp_up(h, ss, w_up, w_down, l, 512)
        if l + 1 < depth:
            x, h, ss, w_next = _mm_residual(act, w_down_l, x, norm_mix[l + 1], w_in, l + 1)
            w_in_l = w_next[None]
        else:
            x, = _mm_residual(act, w_down_l, x)

    out = _rmsnorm(x, norm_final, F32)
    return out.reshape(batch, seq, d_model)
```

```python
import functools
import math

import jax
import jax.numpy as jnp
from jax import lax
from jax.experimental import pallas as pl
from jax.experimental.pallas import tpu as pltpu

D_MODEL = 4096
HEAD_DIM = 128
A_HEADS = 8
A_WIDTH = A_HEADS * HEAD_DIM
A_SUB = HEAD_DIM // 2
A_SCORE_SCALE = A_SUB ** -0.5 * math.log2(math.e)
A_ONES_ROWS = 16
B_GROUPS = ((128, 1), (512, 4), (2048, 16))
B_HEADS_PER_GROUP = 3
B_GROUP_WIDTH = B_HEADS_PER_GROUP * HEAD_DIM
B_WIDTH = B_GROUP_WIDTH * len(B_GROUPS)
B_TOKEN_TILE = 2048
B_QUERY_CHUNK = 256
C_WINDOWS = (2, 4, 8, 16)
C_WIDTH = 1024
C_GROUP_DIM = C_WIDTH // len(C_WINDOWS)
D_WIDTH = D_MODEL - A_WIDTH - B_WIDTH - C_WIDTH
CONV_WIDTH = 3
D_FF = 4 * D_MODEL
ROPE_THETA = 10000.0
NORM_EPS = 1e-6
DIFF_EPS = 1e-5

LANES = 128
MXU_COLS = 256
MLP_ROW_TILE = 2048
IN_PROJ_COL_TILE = 1024
IN_PROJ_ROW_TILE = 1024
VMEM_LIMIT_BYTES = 56 * 1024 * 1024
MASKED_SCORE = -1e30

F32 = jnp.float32
BF16 = jnp.bfloat16


def _params(*semantics):
    return pltpu.CompilerParams(dimension_semantics=semantics, vmem_limit_bytes=VMEM_LIMIT_BYTES)


def _rmsnorm_kernel(x_ref, g_ref, o_ref, *, eps):
    x = x_ref[...]
    ms = jnp.mean(x * x, axis=-1, keepdims=True)
    o_ref[...] = (x * lax.rsqrt(ms + eps) * g_ref[...]).astype(o_ref.dtype)


def _rmsnorm(x, g, out_dtype, tm=256):
    m, d = x.shape
    return pl.pallas_call(
        functools.partial(_rmsnorm_kernel, eps=NORM_EPS),
        grid=(m // tm,),
        in_specs=[pl.BlockSpec((tm, d), lambda i: (i, 0)),
                  pl.BlockSpec((1, d), lambda i: (0, 0))],
        out_specs=pl.BlockSpec((tm, d), lambda i: (i, 0)),
        out_shape=jax.ShapeDtypeStruct((m, d), out_dtype),
        compiler_params=_params("parallel"),
        name="rmsnorm",
    )(x, g.reshape(1, d))


def _prep_kernel(x_ref, g_ref, xb_ref, ss_ref):
    x = x_ref[...]
    xb_ref[...] = (x * g_ref[...]).astype(xb_ref.dtype)
    ss_ref[...] = jnp.broadcast_to(jnp.sum(x * x, axis=-1, keepdims=True), ss_ref.shape)


def _prep(x, g, tm=256):
    m, d = x.shape
    return pl.pallas_call(
        _prep_kernel,
        grid=(m // tm,),
        in_specs=[pl.BlockSpec((tm, d), lambda i: (i, 0)),
                  pl.BlockSpec((1, d), lambda i: (0, 0))],
        out_specs=[pl.BlockSpec((tm, d), lambda i: (i, 0)),
                   pl.BlockSpec((tm, LANES), lambda i: (i, 0))],
        out_shape=[jax.ShapeDtypeStruct((m, d), BF16), jax.ShapeDtypeStruct((m, LANES), F32)],
        compiler_params=_params("parallel"),
        name="norm_prep",
    )(x, g.reshape(1, d))


def _cast_kernel(w_ref, o_ref):
    o_ref[...] = w_ref[...].astype(o_ref.dtype)


def _cast_layer(w, layer, rows=256):
    _, k, n = w.shape
    return pl.pallas_call(
        _cast_kernel,
        grid=(k // rows,),
        in_specs=[pl.BlockSpec((None, rows, n), lambda i: (layer, i, 0))],
        out_specs=pl.BlockSpec((rows, n), lambda i: (i, 0)),
        out_shape=jax.ShapeDtypeStruct((k, n), BF16),
        compiler_params=_params("parallel"),
        name="cast_weights",
    )(w)


def _row_scale(ss_ref):
    return lax.rsqrt(ss_ref[...] * (1.0 / D_MODEL) + NORM_EPS)


def _scale_rows(y, rs):
    return y * jnp.concatenate([rs] * (y.shape[1] // LANES), axis=1)


def _store_norm_inputs(x_new, g_ref, xb_ref, ss_ref, first):
    xb_ref[...] = (x_new * g_ref[...]).astype(xb_ref.dtype)
    part = jnp.broadcast_to(jnp.sum(x_new * x_new, axis=-1, keepdims=True), ss_ref.shape)

    @pl.when(first)
    def _():
        ss_ref[...] = part

    @pl.when(jnp.logical_not(first))
    def _():
        ss_ref[...] += part


def _rotate(y, cos_ref, sin_refs, shifts):
    outs = []
    for c in range(y.shape[1] // LANES):
        yc = y[:, c * LANES:(c + 1) * LANES]
        oc = yc * cos_ref[...]
        for s_ref, shift in zip(sin_refs, shifts):
            oc = oc + pltpu.roll(yc, shift, 1) * s_ref[...]
        outs.append(oc)
    return jnp.concatenate(outs, axis=1)


def _mm_rope_kernel(a_ref, ss_ref, w_ref, cos_ref, *rest, shifts, q_scale, n_q_blocks):
    sin_refs, o_ref = rest[:-1], rest[-1]
    scale = jnp.where(pl.program_id(1) < n_q_blocks, q_scale, 1.0).astype(F32)
    a = a_ref[...]
    rs = _row_scale(ss_ref)
    for c in range(o_ref.shape[1] // MXU_COLS):
        cols = slice(c * MXU_COLS, (c + 1) * MXU_COLS)
        acc = _scale_rows(jnp.dot(a, w_ref[:, cols].astype(BF16), preferred_element_type=F32), rs)
        o_ref[:, cols] = (_rotate(acc, cos_ref, sin_refs, shifts) * scale).astype(o_ref.dtype)


def _mm_rope(a, ss, w, layer, n, tables, shifts, q_scale, n_q_cols, seq, bn=IN_PROJ_COL_TILE,
             bm=IN_PROJ_ROW_TILE):
    m, k = a.shape
    tab_blocks = seq // bm
    tab_spec = pl.BlockSpec((bm, LANES), lambda i, j: (i % tab_blocks, 0))
    return pl.pallas_call(
        functools.partial(_mm_rope_kernel, shifts=shifts, q_scale=q_scale, n_q_blocks=n_q_cols // bn),
        grid=(m // bm, n // bn),
        in_specs=[pl.BlockSpec((bm, k), lambda i, j: (i, 0)),
                  pl.BlockSpec((bm, LANES), lambda i, j: (i, 0)),
                  pl.BlockSpec((None, k, bn), lambda i, j: (layer, 0, j))] + [tab_spec] * len(tables),
        out_specs=pl.BlockSpec((bm, bn), lambda i, j: (i, j)),
        out_shape=jax.ShapeDtypeStruct((m, n), BF16),
        compiler_params=_params("parallel", "arbitrary"),
        name="in_proj_rope",
    )(a, ss, w, *tables)


def _mm_rope_permute_kernel(a_ref, ss_ref, wq_ref, wk_ref, wv_ref, cos_ref, sin_ref, o_ref, scr_ref, *,
                            shift, n_rope_cols, dilation):
    a = a_ref[...]
    w = jnp.concatenate([wq_ref[...], wk_ref[...], wv_ref[...]], axis=1)
    rs = _row_scale(ss_ref)
    n = o_ref.shape[2]
    rows = a.shape[0] // dilation
    for lo in range(0, n, 2 * MXU_COLS):
        hi = min(lo + 2 * MXU_COLS, n)
        acc = _scale_rows(jnp.dot(a, w[:, lo:hi], preferred_element_type=F32), rs)
        rope_hi = min(hi, n_rope_cols)
        if lo < rope_hi:
            rot = _rotate(acc[:, :rope_hi - lo], cos_ref, (sin_ref,), (shift,))
            acc = rot if rope_hi == hi else jnp.concatenate([rot, acc[:, rope_hi - lo:]], axis=1)
        if dilation == 1:
            o_ref[0, :, lo:hi] = acc.astype(o_ref.dtype)
        else:
            for c in range(lo // LANES, hi // LANES):
                cols = slice(c * LANES, (c + 1) * LANES)
                scr_ref[c] = acc[:, c * LANES - lo:(c + 1) * LANES - lo]
                for r in range(dilation):
                    o_ref[r, :, cols] = scr_ref[c, pl.ds(r, rows, stride=dilation), :].astype(o_ref.dtype)


def _mm_rope_permute(a, ss, w, layer, col0, group, tables, shift, dilation, batch, seq, bm=IN_PROJ_ROW_TILE):
    m, k = a.shape
    n = 3 * B_GROUP_WIDTH
    n_rope_cols = 2 * B_GROUP_WIDTH
    tiles = seq // bm
    tab_spec = pl.BlockSpec((bm, LANES), lambda i: (i % tiles, 0))
    n_groups = len(B_GROUPS)
    first = col0 // B_GROUP_WIDTH
    assert first * B_GROUP_WIDTH == col0

    def w_spec(part):
        return pl.BlockSpec((None, k, B_GROUP_WIDTH), lambda i: (layer, 0, first + part * n_groups + group),
                            pipeline_mode=pl.Buffered(1))

    return pl.pallas_call(
        functools.partial(_mm_rope_permute_kernel, shift=shift, n_rope_cols=n_rope_cols, dilation=dilation),
        grid=(m // bm,),
        in_specs=[pl.BlockSpec((bm, k), lambda i: (i, 0)), pl.BlockSpec((bm, LANES), lambda i: (i, 0)),
                  w_spec(0), w_spec(1), w_spec(2), tab_spec, tab_spec],
        out_specs=pl.BlockSpec((None, dilation, bm // dilation, n), lambda i: (i // tiles, 0, i % tiles, 0)),
        out_shape=jax.ShapeDtypeStruct((batch, dilation, seq // dilation, n), BF16),
        scratch_shapes=[pltpu.VMEM((n // LANES, bm, LANES), F32)],
        compiler_params=_params("parallel"),
        name="in_proj_dilated",
    )(a, ss, w, w, w, *tables)


def _mm_kernel(a_ref, ss_ref, w_ref, o_ref, *, relu_sq):
    acc = jnp.dot(a_ref[...], w_ref[...].astype(BF16), preferred_element_type=F32)
    acc = _scale_rows(acc, _row_scale(ss_ref))
    if relu_sq:
        acc = jnp.square(jnp.maximum(acc, 0.0))
    o_ref[...] = acc.astype(o_ref.dtype)


def _mm_cols(a, ss, w, layer, col0, n, out_dtype, bn=IN_PROJ_COL_TILE, bm=IN_PROJ_ROW_TILE, name="matmul"):
    m, k = a.shape
    first = col0 // bn
    assert first * bn == col0 and n % bn == 0
    return pl.pallas_call(
        functools.partial(_mm_kernel, relu_sq=False),
        grid=(m // bm, n // bn),
        in_specs=[pl.BlockSpec((bm, k), lambda i, j: (i, 0)),
                  pl.BlockSpec((bm, LANES), lambda i, j: (i, 0)),
                  pl.BlockSpec((None, k, bn), lambda i, j: (layer, 0, first + j))],
        out_specs=pl.BlockSpec((bm, bn), lambda i, j: (i, j)),
        out_shape=jax.ShapeDtypeStruct((m, n), out_dtype),
        compiler_params=_params("parallel", "arbitrary"),
        name=name,
    )(a, ss, w)


def _mlp_up_kernel(a_ref, ss_ref, w_ref, wd_ref, o_ref, wd_bf16_ref):
    _mm_kernel(a_ref, ss_ref, w_ref, o_ref, relu_sq=True)
    wd_bf16_ref[...] = wd_ref[...].astype(BF16)


def _mlp_up(a, ss, w, w_down, layer, bn, bm=MLP_ROW_TILE):
    m, k = a.shape
    n = w.shape[2]
    n_j = n // bn
    steps = (m // bm) * n_j
    slab = n // steps
    assert slab * steps == n
    return pl.pallas_call(
        _mlp_up_kernel,
        grid=(m // bm, n_j),
        in_specs=[pl.BlockSpec((bm, k), lambda i, j: (i, 0), pipeline_mode=pl.Buffered(1)),
                  pl.BlockSpec((bm, LANES), lambda i, j: (i, 0)),
                  pl.BlockSpec((None, k, bn), lambda i, j: (layer, 0, j)),
                  pl.BlockSpec((None, slab, k), lambda i, j: (layer, i * n_j + j, 0))],
        out_specs=[pl.BlockSpec((None, None, bm, bn), lambda i, j: (i, j, 0, 0)),
                   pl.BlockSpec((slab, k), lambda i, j: (i * n_j + j, 0))],
        out_shape=[jax.ShapeDtypeStruct((m // bm, n_j, bm, bn), BF16), jax.ShapeDtypeStruct((n, k), BF16)],
        compiler_params=_params("parallel", "arbitrary"),
        name="mlp_up",
    )(a, ss, w, w_down)


def _mm_transposed_kernel(wt_ref, a_ref, ss_ref, o_ref, *, t):
    acc = lax.dot_general(wt_ref[...], a_ref[...], (((1,), (1,)), ((), ())), preferred_element_type=F32)
    acc = acc * jnp.transpose(_row_scale(ss_ref))[0:1, :]
    for c in range(acc.shape[1] // t):
        o_ref[c] = acc[:, c * t:(c + 1) * t].astype(o_ref.dtype)


def _mm_transposed(a, ss, wt, layer, t, bm=1024):
    m, k = a.shape
    n = wt.shape[1]
    return pl.pallas_call(
        functools.partial(_mm_transposed_kernel, t=t),
        grid=(m // bm,),
        in_specs=[pl.BlockSpec((None, n, k), lambda i: (layer, 0, 0), pipeline_mode=pl.Buffered(1)),
                  pl.BlockSpec((bm, k), lambda i: (i, 0)),
                  pl.BlockSpec((bm, LANES), lambda i: (i, 0))],
        out_specs=pl.BlockSpec((bm // t, n, t), lambda i: (i, 0, 0)),
        out_shape=jax.ShapeDtypeStruct((m // t, n, t), BF16),
        compiler_params=_params("parallel"),
        name="in_proj_vt",
    )(wt, a, ss)


def _mm_residual_kernel(a_ref, w_ref, r_ref, *rest):
    o_ref = rest[0] if len(rest) == 1 else rest[2]
    kk = pl.program_id(2)

    @pl.when(kk == 0)
    def _():
        o_ref[...] = r_ref[...]

    a = jnp.concatenate([a_ref[c] for c in range(a_ref.shape[0])], axis=1)
    o_ref[...] += jnp.dot(a, w_ref[...], preferred_element_type=F32)

    if len(rest) > 1:
        g_ref, w_next_ref, _, xb_ref, ss_ref, w_next_bf16_ref = rest
        w_next_bf16_ref[...] = w_next_ref[...].astype(BF16)

        @pl.when(kk == pl.num_programs(2) - 1)
        def _():
            _store_norm_inputs(o_ref[...], g_ref, xb_ref, ss_ref, pl.program_id(1) == 0)


def _mm_residual(a, w, r, g=None, w_next=None, next_layer=None, bm=1024, bn=1024, bk=4096):
    row_tiles, col_tiles, tile_rows, tile_cols = a.shape
    m, k = row_tiles * tile_rows, col_tiles * tile_cols
    n = w.shape[1]
    sub = tile_rows // bm
    assert sub * bm == tile_rows
    tile_spec = pl.BlockSpec((bm, bn), lambda i, j, kk: (i, j))
    in_specs = [pl.BlockSpec((None, bk // tile_cols, bm, tile_cols), lambda i, j, kk: (i // sub, kk, i % sub, 0)),
                pl.BlockSpec((bk, bn), lambda i, j, kk: (kk, j)),
                pl.BlockSpec((bm, bn), lambda i, j, kk: (i, j), pipeline_mode=pl.Buffered(1))]
    out_specs = [tile_spec]
    out_shape = [jax.ShapeDtypeStruct((m, n), F32)]
    operands = [a, w, r]
    if g is not None:
        n_j, n_k = n // bn, k // bk
        steps = (m // bm) * n_j * n_k
        rows_next, cols_next = w_next.shape[1:]
        slab = rows_next // steps
        assert slab * steps == rows_next

        def slab_index(i, j, kk):
            return (i * n_j + j) * n_k + kk

        in_specs += [pl.BlockSpec((1, bn), lambda i, j, kk: (0, j)),
                     pl.BlockSpec((None, slab, cols_next), lambda i, j, kk: (next_layer, slab_index(i, j, kk), 0))]
        out_specs += [tile_spec, pl.BlockSpec((bm, LANES), lambda i, j, kk: (i, 0)),
                      pl.BlockSpec((slab, cols_next), lambda i, j, kk: (slab_index(i, j, kk), 0))]
        out_shape += [jax.ShapeDtypeStruct((m, n), BF16), jax.ShapeDtypeStruct((m, LANES), F32),
                      jax.ShapeDtypeStruct((rows_next, cols_next), BF16)]
        operands += [g.reshape(1, n), w_next]
    return pl.pallas_call(
        _mm_residual_kernel,
        grid=(m // bm, n // bn, k // bk),
        in_specs=in_specs,
        out_specs=out_specs,
        out_shape=out_shape,
        compiler_params=_params("parallel", "arbitrary", "arbitrary"),
        name="mlp_down",
    )(*operands)


def _out_proj_kernel(a_ref, b_ref, c_ref, d_ref, w_ref, r_ref, g_ref, o_ref, xb_ref, ss_ref):
    mix = jnp.concatenate([a_ref[...]] + [b_ref[g] for g in range(len(B_GROUPS))] + [c_ref[...], d_ref[...]], axis=1)
    x_new = r_ref[...] + jnp.dot(mix, w_ref[...].astype(BF16), preferred_element_type=F32)
    o_ref[...] = x_new
    _store_norm_inputs(x_new, g_ref, xb_ref, ss_ref, pl.program_id(1) == 0)


def _out_proj(out_a, out_b, out_c, out_d, w, layer, r, g, bm=1024, bn=512):
    m = out_a.shape[0]
    n = w.shape[2]
    n_groups = len(B_GROUPS)
    return pl.pallas_call(
        _out_proj_kernel,
        grid=(m // bm, n // bn),
        in_specs=[pl.BlockSpec((bm, A_WIDTH), lambda i, j: (i, 0)),
                  pl.BlockSpec((n_groups, bm, B_GROUP_WIDTH), lambda i, j: (0, i, 0)),
                  pl.BlockSpec((bm, C_WIDTH), lambda i, j: (i, 0)),
                  pl.BlockSpec((bm, D_WIDTH), lambda i, j: (i, 0)),
                  pl.BlockSpec((None, D_MODEL, bn), lambda i, j: (layer, 0, j)),
                  pl.BlockSpec((bm, bn), lambda i, j: (i, j)),
                  pl.BlockSpec((1, bn), lambda i, j: (0, j))],
        out_specs=[pl.BlockSpec((bm, bn), lambda i, j: (i, j)),
                   pl.BlockSpec((bm, bn), lambda i, j: (i, j)),
                   pl.BlockSpec((bm, LANES), lambda i, j: (i, 0))],
        out_shape=[jax.ShapeDtypeStruct((m, n), F32), jax.ShapeDtypeStruct((m, n), BF16),
                   jax.ShapeDtypeStruct((m, LANES), F32)],
        compiler_params=_params("parallel", "arbitrary"),
        name="out_proj",
    )(out_a, out_b, out_c, out_d, w, r, g.reshape(1, n))


def _attn_a_kernel(q_ref, k_ref, vt_ref, lam_ref, g_ref, o_ref, *scratch, t, heads, lam_init):
    qi = pl.program_id(2)
    lane = lax.broadcasted_iota(jnp.int32, (t, HEAD_DIM), 1)
    qqs = []
    for h in range(heads):
        q = q_ref[:, h * HEAD_DIM:(h + 1) * HEAD_DIM]
        zero = jnp.zeros_like(q)
        qqs.append(jnp.concatenate([jnp.where(lane < A_SUB, q, zero), jnp.where(lane >= A_SUB, q, zero)], axis=0))

    ones = jnp.ones((A_ONES_ROWS, t), BF16)
    m_scr, acc_scr = scratch[:heads], scratch[heads:]
    for h in range(heads):
        m_scr[h][...] = jnp.full(m_scr[h].shape, MASKED_SCORE, F32)
        acc_scr[h][...] = jnp.zeros(acc_scr[h].shape, F32)

    def step(j, masked):
        def scores(h):
            k = k_ref[pl.ds(pl.multiple_of(j * t, t), t), h * HEAD_DIM:(h + 1) * HEAD_DIM]
            s = lax.dot_general(k, qqs[h], (((1,), (1,)), ((), ())), preferred_element_type=F32)
            if masked:
                key = lax.broadcasted_iota(jnp.int32, s.shape, 0)
                lane_q = lax.broadcasted_iota(jnp.int32, s.shape, 1)
                qry = jnp.where(lane_q >= t, lane_q - t, lane_q)
                s = jnp.where(key <= qry, s, MASKED_SCORE)
            return s

        s_next = scores(0)
        for h in range(heads):
            s = s_next
            if h + 1 < heads:
                s_next = scores(h + 1)
            vt = jnp.concatenate([vt_ref[j, h * HEAD_DIM:(h + 1) * HEAD_DIM, :], ones], axis=0)
            m = m_scr[h][...]
            m_new = jnp.maximum(m, jnp.max(s, axis=0, keepdims=True))
            alpha = jnp.exp2(m - m_new)
            p = jnp.exp2(s - m_new)
            m_scr[h][...] = m_new
            acc_scr[h][...] = alpha * acc_scr[h][...] + jnp.dot(vt, p.astype(BF16), preferred_element_type=F32)

    def body(j, carry):
        step(j, masked=False)
        return carry

    lax.fori_loop(0, qi, body, 0)
    step(qi, masked=True)

    lp = lam_ref[...]
    lam = (jnp.exp(jnp.sum(lp[0:1] * lp[1:2], axis=-1, keepdims=True))
           - jnp.exp(jnp.sum(lp[2:3] * lp[3:4], axis=-1, keepdims=True)) + lam_init)
    for h in range(heads):
        acc = acc_scr[h][...]
        o = acc[:HEAD_DIM] / acc[HEAD_DIM:HEAD_DIM + 1]
        o = (o[:, :t] - lam * o[:, t:]).T
        ms = jnp.mean(o * o, axis=-1, keepdims=True)
        o = o * lax.rsqrt(ms + DIFF_EPS) * g_ref[...]
        o_ref[:, h * HEAD_DIM:(h + 1) * HEAD_DIM] = (o * (1.0 - lam_init)).astype(o_ref.dtype)


def _attn_a(qk, vt, lam_params, subln_g, lam_init, batch, seq, t, heads=8):
    nq = seq // t
    width = heads * HEAD_DIM
    n_pairs = A_HEADS // heads
    return pl.pallas_call(
        functools.partial(_attn_a_kernel, t=t, heads=heads, lam_init=lam_init),
        grid=(batch, n_pairs, nq),
        in_specs=[pl.BlockSpec((t, width), lambda b, h, i: (b * nq + i, h)),
                  pl.BlockSpec((seq, width), lambda b, h, i: (b, n_pairs + h)),
                  pl.BlockSpec((nq, width, t), lambda b, h, i: (b, h, 0)),
                  pl.BlockSpec((4, A_SUB), lambda b, h, i: (0, 0)),
                  pl.BlockSpec((1, HEAD_DIM), lambda b, h, i: (0, 0))],
        out_specs=pl.BlockSpec((t, width), lambda b, h, i: (b * nq + i, h)),
        out_shape=jax.ShapeDtypeStruct((batch * seq, A_WIDTH), BF16),
        scratch_shapes=([pltpu.VMEM((1, 2 * t), F32)] * heads
                        + [pltpu.VMEM((HEAD_DIM + A_ONES_ROWS, 2 * t), F32)] * heads),
        compiler_params=_params("parallel", "parallel", "arbitrary"),
        name="diff_attention",
    )(qk, qk, vt, lam_params, subln_g.reshape(1, HEAD_DIM))


def _attn_b_kernel(*refs, tile):
    n_groups = len(B_GROUPS)
    q_refs = refs[0:n_groups]
    k_refs = refs[n_groups:2 * n_groups]
    v_refs = refs[2 * n_groups:3 * n_groups]
    o_ref = refs[3 * n_groups]
    out_scr, lse_scr = refs[3 * n_groups + 1:]
    ti = pl.program_id(2)
    scale = HEAD_DIM ** -0.5
    base_delta = {}
    chunks = []

    for g, (window, dilation) in enumerate(B_GROUPS):
        back = window // dilation
        seg = tile // dilation
        tq = min(B_QUERY_CHUNK, seg)
        tk = tq + back
        if tq not in base_delta:
            base_delta[tq] = (lax.broadcasted_iota(jnp.int32, (tq, tk), 0)
                              - lax.broadcasted_iota(jnp.int32, (tq, tk), 1))
        for r in range(dilation):
            for c in range(seg // tq):
                chunks.append((g, dilation, back, tq, tk, r, c * tq, ti * seg + c * tq))

    def scores(chunk):
        g, dilation, back, tq, tk, r, i_loc, i0 = chunk
        ks = pl.multiple_of(jnp.maximum(i0 - back, 0), back)
        q = q_refs[g][r, i_loc:i_loc + tq, :]
        k = k_refs[g][r, pl.ds(ks, tk), :]
        s = lax.dot_general(q, k, (((1,), (1,)), ((), ())), preferred_element_type=F32) * scale
        delta = base_delta[tq] + (i0 - ks)
        return jnp.where((delta >= 0) & (delta <= back), s, MASKED_SCORE), ks

    nxt = scores(chunks[0])
    for idx, (g, dilation, back, tq, tk, r, i_loc, i0) in enumerate(chunks):
        s, ks = nxt
        if idx + 1 < len(chunks):
            nxt = scores(chunks[idx + 1])
        v = v_refs[g][r, pl.ds(ks, tk), :]
        m = jnp.max(s, axis=-1, keepdims=True)
        p = jnp.exp(s - m)
        l = jnp.sum(p, axis=-1, keepdims=True)
        o = jnp.dot(p.astype(BF16), v, preferred_element_type=F32) / l
        lse = jnp.broadcast_to(m + jnp.log(l), o.shape)
        rows = pl.ds(dilation * i_loc + r, tq, stride=dilation) if dilation > 1 else pl.ds(i_loc, tq)
        out_scr[g, rows, :] = o
        lse_scr[g, rows, :] = lse

    for c in range(tile // B_QUERY_CHUNK):
        rows = pl.ds(c * B_QUERY_CHUNK, B_QUERY_CHUNK)
        lses = [lse_scr[g, rows, :] for g in range(n_groups)]
        lse_max = functools.reduce(jnp.maximum, lses)
        weights = [jnp.exp(x - lse_max) for x in lses]
        denom = functools.reduce(lambda a, b: a + b, weights)
        for g in range(n_groups):
            o_ref[g, rows, :] = (out_scr[g, rows, :] * (weights[g] / denom)).astype(o_ref.dtype)


def _attn_b(qkvs, batch, seq, tile=B_TOKEN_TILE):
    n_tiles = seq // tile
    n_groups = len(B_GROUPS)

    def q_spec(g):
        d = B_GROUPS[g][1]
        return pl.BlockSpec((None, d, tile // d, HEAD_DIM), lambda b, h, i: (b, 0, i, h))

    def kv_spec(g, part):
        d = B_GROUPS[g][1]
        return pl.BlockSpec((None, d, seq // d, HEAD_DIM),
                            lambda b, h, i: (b, 0, 0, part * B_HEADS_PER_GROUP + h))

    in_specs = ([q_spec(g) for g in range(n_groups)] + [kv_spec(g, 1) for g in range(n_groups)]
                + [kv_spec(g, 2) for g in range(n_groups)])
    return pl.pallas_call(
        functools.partial(_attn_b_kernel, tile=tile),
        grid=(batch, B_HEADS_PER_GROUP, n_tiles),
        in_specs=in_specs,
        out_specs=pl.BlockSpec((n_groups, tile, HEAD_DIM), lambda b, h, i: (0, b * n_tiles + i, h)),
        out_shape=jax.ShapeDtypeStruct((n_groups, batch * seq, B_GROUP_WIDTH), BF16),
        scratch_shapes=[pltpu.VMEM((n_groups, tile, HEAD_DIM), F32),
                        pltpu.VMEM((n_groups, tile, HEAD_DIM), F32)],
        compiler_params=_params("parallel", "parallel", "arbitrary"),
        name="dilated_attention",
    )(*(list(qkvs) * 3))


def _shift_rows(x, k, row):
    return jnp.where(row >= k, pltpu.roll(x, k, 0), 0.0)


def _pool_kernel(u_lo_ref, u_hi_ref, w_ref, scale_ref, o_ref):
    g = pl.program_id(1)
    u = jnp.concatenate([u_lo_ref[...], u_hi_ref[...]], axis=1)
    row = lax.broadcasted_iota(jnp.int32, u.shape, 0)
    window = jnp.left_shift(2, g)
    total = u
    step = 1
    while step < max(C_WINDOWS):
        widened = total + _shift_rows(total, step, row)
        total = jnp.where(step < window, widened, total)
        step *= 2
    count = jnp.minimum(row + 1, window).astype(F32)
    pooled = total / count - u
    y = jnp.dot(pooled.astype(BF16), w_ref[0], preferred_element_type=F32)
    o_ref[...] = (y * scale_ref[...]).astype(o_ref.dtype)


def _pool(proj, col0, pool_w, pool_scale, batch, seq):
    n_groups = len(C_WINDOWS)
    tiles_per_group = C_GROUP_DIM // LANES
    assert tiles_per_group == 2 and col0 % LANES == 0
    first = col0 // LANES

    def u_spec(half):
        return pl.BlockSpec((seq, LANES), lambda b, g: (b, first + tiles_per_group * g + half))

    return pl.pallas_call(
        _pool_kernel,
        grid=(batch, n_groups),
        in_specs=[u_spec(0), u_spec(1),
                  pl.BlockSpec((1, C_GROUP_DIM, C_GROUP_DIM), lambda b, g: (g, 0, 0)),
                  pl.BlockSpec((1, C_GROUP_DIM), lambda b, g: (0, g))],
        out_specs=pl.BlockSpec((seq, C_GROUP_DIM), lambda b, g: (b, g)),
        out_shape=jax.ShapeDtypeStruct((batch * seq, C_WIDTH), BF16),
        compiler_params=_params("parallel", "parallel"),
        name="multiscale_pool",
    )(proj, proj, pool_w, pool_scale.reshape(1, C_WIDTH))


def _conv_kernel(gb_ref, gc_ref, h_ref, w_ref, o_ref):
    z = gc_ref[...] * h_ref[...]
    row = lax.broadcasted_iota(jnp.int32, z.shape, 0)
    w = w_ref[...]
    y = w[CONV_WIDTH - 1:CONV_WIDTH] * z
    for tap in range(1, CONV_WIDTH):
        y = y + w[CONV_WIDTH - 1 - tap:CONV_WIDTH - tap] * _shift_rows(z, tap, row)
    o_ref[...] = (gb_ref[...] * y).astype(o_ref.dtype)


def _short_conv(proj, col0, conv_w, batch, seq):
    n_col = D_WIDTH // LANES
    assert col0 % LANES == 0
    first = col0 // LANES

    def spec(part):
        return pl.BlockSpec((seq, LANES), lambda b, c: (b, first + part * n_col + c))

    return pl.pallas_call(
        _conv_kernel,
        grid=(batch, n_col),
        in_specs=[spec(0), spec(1), spec(2), pl.BlockSpec((CONV_WIDTH, LANES), lambda b, c: (0, c))],
        out_specs=pl.BlockSpec((seq, LANES), lambda b, c: (b, c)),
        out_shape=jax.ShapeDtypeStruct((batch * seq, D_WIDTH), BF16),
        compiler_params=_params("parallel", "parallel"),
        name="short_conv",
    )(proj, proj, proj, conv_w)


def _rope_tables(seq, dim):
    half = dim // 2
    inv = ROPE_THETA ** (-jnp.arange(0, dim, 2, dtype=F32) / dim)
    ang = jnp.arange(seq, dtype=F32)[:, None] * inv[None, :]
    cos, sin = jnp.cos(ang), jnp.sin(ang)
    reps = LANES // dim
    cos_t = jnp.tile(jnp.concatenate([cos, cos], axis=1), (1, reps))
    zeros = jnp.zeros_like(sin)
    if dim == LANES:
        return (cos_t, jnp.concatenate([-sin, sin], axis=1)), (half,)
    upper = jnp.tile(jnp.concatenate([zeros, sin], axis=1), (1, reps))
    lower = jnp.tile(jnp.concatenate([-sin, zeros], axis=1), (1, reps))
    return (cos_t, upper, lower), (half, LANES - half)


def kernel(x, w_in, w_out, norm_mix, norm_mlp, diff_lambda, diff_subln, pool_w, pool_scale, conv_w,
           w_up, w_down, norm_final):
    batch, seq, d_model = x.shape
    depth = w_in.shape[0]
    m = batch * seq
    x = x.reshape(m, d_model)

    tables_a, shifts_a = _rope_tables(seq, A_SUB)
    tables_b, shifts_b = _rope_tables(seq, HEAD_DIM)

    a_qk_end = 2 * A_WIDTH
    a_end = 3 * A_WIDTH
    b_end = a_end + 3 * B_WIDTH
    c_end = b_end + C_WIDTH
    t_a = 512
    tail0 = (b_end // IN_PROJ_COL_TILE) * IN_PROJ_COL_TILE
    in_width = w_in.shape[2]

    w_in_l = _cast_layer(w_in, 0)[None]

    h, ss = _prep(x, norm_mix[0])
    for l in range(depth):
        lam_init = 0.8 - 0.6 * math.exp(-0.3 * l)
        w_a_vt = jnp.swapaxes(w_in_l[:, :, a_qk_end:a_end], 1, 2)

        qk_a = _mm_rope(h, ss, w_in_l, 0, a_qk_end, tables_a, shifts_a, A_SCORE_SCALE, A_WIDTH, seq)
        vt_a = _mm_transposed(h, ss, w_a_vt, 0, t_a)
        qkv_b = [_mm_rope_permute(h, ss, w_in_l, 0, a_end, g, tables_b, shifts_b[0], B_GROUPS[g][1], batch, seq)
                 for g in range(len(B_GROUPS))]
        tail = _mm_cols(h, ss, w_in_l, 0, tail0, in_width - tail0, F32, name="in_proj_tail")

        out_a = _attn_a(qk_a, vt_a, diff_lambda[l], diff_subln[l], lam_init, batch, seq, t_a)
        out_b = _attn_b(qkv_b, batch, seq)
        out_c = _pool(tail, b_end - tail0, pool_w[l].astype(BF16), pool_scale[l], batch, seq)
        out_d = _short_conv(tail, c_end - tail0, conv_w[l], batch, seq)

        x, h, ss = _out_proj(out_a, out_b, out_c, out_d, w_out, l, x, norm_mlp[l])
        act, w_down_l = _mlp_up(h, ss, w_up, w_down, l, 512)
        if l + 1 < depth:
            x, h, ss, w_next = _mm_residual(act, w_down_l, x, norm_mix[l + 1], w_in, l + 1)
            w_in_l = w_next[None]
        else:
            x, = _mm_residual(act, w_down_l, x)

    out = _rmsnorm(x, norm_final, F32)
    return out.reshape(batch, seq, d_model)
```

```python
import functools
import math

import jax
import jax.numpy as jnp
from jax import lax
from jax.experimental import pallas as pl
from jax.experimental.pallas import tpu as pltpu

D_MODEL = 4096
HEAD_DIM = 128
A_HEADS = 8
A_WIDTH = A_HEADS * HEAD_DIM
A_SUB = HEAD_DIM // 2
A_SCORE_SCALE = A_SUB ** -0.5 * math.log2(math.e)
A_ONES_ROWS = 16
B_GROUPS = ((128, 1), (512, 4), (2048, 16))
B_HEADS_PER_GROUP = 3
B_GROUP_WIDTH = B_HEADS_PER_GROUP * HEAD_DIM
B_WIDTH = B_GROUP_WIDTH * len(B_GROUPS)
B_TOKEN_TILE = 2048
B_QUERY_CHUNK = 256
C_WINDOWS = (2, 4, 8, 16)
C_WIDTH = 1024
C_GROUP_DIM = C_WIDTH // len(C_WINDOWS)
D_WIDTH = D_MODEL - A_WIDTH - B_WIDTH - C_WIDTH
CONV_WIDTH = 3
D_FF = 4 * D_MODEL
ROPE_THETA = 10000.0
NORM_EPS = 1e-6
DIFF_EPS = 1e-5

LANES = 128
MXU_COLS = 256
MLP_ROW_TILE = 2048
IN_PROJ_COL_TILE = 1024
IN_PROJ_ROW_TILE = 1024
VMEM_LIMIT_BYTES = 56 * 1024 * 1024
MASKED_SCORE = -1e30

F32 = jnp.float32
BF16 = jnp.bfloat16


def _params(*semantics):
    return pltpu.CompilerParams(dimension_semantics=semantics, vmem_limit_bytes=VMEM_LIMIT_BYTES)


def _rmsnorm_kernel(x_ref, g_ref, o_ref, *, eps):
    x = x_ref[...]
    ms = jnp.mean(x * x, axis=-1, keepdims=True)
    o_ref[...] = (x * lax.rsqrt(ms + eps) * g_ref[...]).astype(o_ref.dtype)


def _rmsnorm(x, g, out_dtype, tm=256):
    m, d = x.shape
    return pl.pallas_call(
        functools.partial(_rmsnorm_kernel, eps=NORM_EPS),
        grid=(m // tm,),
        in_specs=[pl.BlockSpec((tm, d), lambda i: (i, 0)),
                  pl.BlockSpec((1, d), lambda i: (0, 0))],
        out_specs=pl.BlockSpec((tm, d), lambda i: (i, 0)),
        out_shape=jax.ShapeDtypeStruct((m, d), out_dtype),
        compiler_params=_params("parallel"),
        name="rmsnorm",
    )(x, g.reshape(1, d))


def _prep_kernel(x_ref, g_ref, xb_ref, ss_ref):
    x = x_ref[...]
    xb_ref[...] = (x * g_ref[...]).astype(xb_ref.dtype)
    ss_ref[...] = jnp.broadcast_to(jnp.sum(x * x, axis=-1, keepdims=True), ss_ref.shape)


def _prep(x, g, tm=256):
    m, d = x.shape
    return pl.pallas_call(
        _prep_kernel,
        grid=(m // tm,),
        in_specs=[pl.BlockSpec((tm, d), lambda i: (i, 0)),
                  pl.BlockSpec((1, d), lambda i: (0, 0))],
        out_specs=[pl.BlockSpec((tm, d), lambda i: (i, 0)),
                   pl.BlockSpec((tm, LANES), lambda i: (i, 0))],
        out_shape=[jax.ShapeDtypeStruct((m, d), BF16), jax.ShapeDtypeStruct((m, LANES), F32)],
        compiler_params=_params("parallel"),
        name="norm_prep",
    )(x, g.reshape(1, d))


def _cast_kernel(w_ref, o_ref):
    o_ref[...] = w_ref[...].astype(o_ref.dtype)


def _cast_layer(w, layer, rows=256):
    _, k, n = w.shape
    return pl.pallas_call(
        _cast_kernel,
        grid=(k // rows,),
        in_specs=[pl.BlockSpec((None, rows, n), lambda i: (layer, i, 0))],
        out_specs=pl.BlockSpec((rows, n), lambda i: (i, 0)),
        out_shape=jax.ShapeDtypeStruct((k, n), BF16),
        compiler_params=_params("parallel"),
        name="cast_weights",
    )(w)


def _row_scale(ss_ref):
    return lax.rsqrt(ss_ref[...] * (1.0 / D_MODEL) + NORM_EPS)


def _scale_rows(y, rs):
    return y * jnp.concatenate([rs] * (y.shape[1] // LANES), axis=1)


def _store_norm_inputs(x_new, g_ref, xb_ref, ss_ref, first):
    xb_ref[...] = (x_new * g_ref[...]).astype(xb_ref.dtype)
    part = jnp.broadcast_to(jnp.sum(x_new * x_new, axis=-1, keepdims=True), ss_ref.shape)

    @pl.when(first)
    def _():
        ss_ref[...] = part

    @pl.when(jnp.logical_not(first))
    def _():
        ss_ref[...] += part


def _rotate(y, cos_ref, sin_refs, shifts):
    outs = []
    for c in range(y.shape[1] // LANES):
        yc = y[:, c * LANES:(c + 1) * LANES]
        oc = yc * cos_ref[...]
        for s_ref, shift in zip(sin_refs, shifts):
            oc = oc + pltpu.roll(yc, shift, 1) * s_ref[...]
        outs.append(oc)
    return jnp.concatenate(outs, axis=1)


def _mm_rope_kernel(a_ref, ss_ref, w_ref, cos_ref, *rest, shifts, q_scale, n_q_blocks):
    sin_refs, o_ref = rest[:-1], rest[-1]
    scale = jnp.where(pl.program_id(1) < n_q_blocks, q_scale, 1.0).astype(F32)
    a = a_ref[...]
    rs = _row_scale(ss_ref)
    for c in range(o_ref.shape[1] // MXU_COLS):
        cols = slice(c * MXU_COLS, (c + 1) * MXU_COLS)
        acc = _scale_rows(jnp.dot(a, w_ref[:, cols].astype(BF16), preferred_element_type=F32), rs)
        o_ref[:, cols] = (_rotate(acc, cos_ref, sin_refs, shifts) * scale).astype(o_ref.dtype)


def _mm_rope(a, ss, w, layer, n, tables, shifts, q_scale, n_q_cols, seq, bn=IN_PROJ_COL_TILE,
             bm=IN_PROJ_ROW_TILE):
    m, k = a.shape
    tab_blocks = seq // bm
    tab_spec = pl.BlockSpec((bm, LANES), lambda i, j: (i % tab_blocks, 0))
    return pl.pallas_call(
        functools.partial(_mm_rope_kernel, shifts=shifts, q_scale=q_scale, n_q_blocks=n_q_cols // bn),
        grid=(m // bm, n // bn),
        in_specs=[pl.BlockSpec((bm, k), lambda i, j: (i, 0)),
                  pl.BlockSpec((bm, LANES), lambda i, j: (i, 0)),
                  pl.BlockSpec((None, k, bn), lambda i, j: (layer, 0, j))] + [tab_spec] * len(tables),
        out_specs=pl.BlockSpec((bm, bn), lambda i, j: (i, j)),
        out_shape=jax.ShapeDtypeStruct((m, n), BF16),
        compiler_params=_params("parallel", "arbitrary"),
        name="in_proj_rope",
    )(a, ss, w, *tables)


def _mm_rope_permute_kernel(a_ref, ss_ref, wq_ref, wk_ref, wv_ref, cos_ref, sin_ref, o_ref, scr_ref, *,
                            shift, n_rope_cols, dilation):
    a = a_ref[...]
    w = jnp.concatenate([wq_ref[...], wk_ref[...], wv_ref[...]], axis=1)
    rs = _row_scale(ss_ref)
    n = o_ref.shape[2]
    rows = a.shape[0] // dilation
    for lo in range(0, n, 2 * MXU_COLS):
        hi = min(lo + 2 * MXU_COLS, n)
        acc = _scale_rows(jnp.dot(a, w[:, lo:hi], preferred_element_type=F32), rs)
        rope_hi = min(hi, n_rope_cols)
        if lo < rope_hi:
            rot = _rotate(acc[:, :rope_hi - lo], cos_ref, (sin_ref,), (shift,))
            acc = rot if rope_hi == hi else jnp.concatenate([rot, acc[:, rope_hi - lo:]], axis=1)
        if dilation == 1:
            o_ref[0, :, lo:hi] = acc.astype(o_ref.dtype)
        else:
            for c in range(lo // LANES, hi // LANES):
                cols = slice(c * LANES, (c + 1) * LANES)
                scr_ref[c] = acc[:, c * LANES - lo:(c + 1) * LANES - lo]
                for r in range(dilation):
                    o_ref[r, :, cols] = scr_ref[c, pl.ds(r, rows, stride=dilation), :].astype(o_ref.dtype)


def _mm_rope_permute(a, ss, w, layer, col0, group, tables, shift, dilation, batch, seq, bm=IN_PROJ_ROW_TILE):
    m, k = a.shape
    n = 3 * B_GROUP_WIDTH
    n_rope_cols = 2 * B_GROUP_WIDTH
    tiles = seq // bm
    tab_spec = pl.BlockSpec((bm, LANES), lambda i: (i % tiles, 0))
    n_groups = len(B_GROUPS)
    first = col0 // B_GROUP_WIDTH
    assert first * B_GROUP_WIDTH == col0

    def w_spec(part):
        return pl.BlockSpec((None, k, B_GROUP_WIDTH), lambda i: (layer, 0, first + part * n_groups + group),
                            pipeline_mode=pl.Buffered(1))

    return pl.pallas_call(
        functools.partial(_mm_rope_permute_kernel, shift=shift, n_rope_cols=n_rope_cols, dilation=dilation),
        grid=(m // bm,),
        in_specs=[pl.BlockSpec((bm, k), lambda i: (i, 0)), pl.BlockSpec((bm, LANES), lambda i: (i, 0)),
                  w_spec(0), w_spec(1), w_spec(2), tab_spec, tab_spec],
        out_specs=pl.BlockSpec((None, dilation, bm // dilation, n), lambda i: (i // tiles, 0, i % tiles, 0)),
        out_shape=jax.ShapeDtypeStruct((batch, dilation, seq // dilation, n), BF16),
        scratch_shapes=[pltpu.VMEM((n // LANES, bm, LANES), F32)],
        compiler_params=_params("parallel"),
        name="in_proj_dilated",
    )(a, ss, w, w, w, *tables)


def _mm_kernel(a_ref, ss_ref, w_ref, o_ref, *, relu_sq):
    acc = jnp.dot(a_ref[...], w_ref[...].astype(BF16), preferred_element_type=F32)
    acc = _scale_rows(acc, _row_scale(ss_ref))
    if relu_sq:
        acc = jnp.square(jnp.maximum(acc, 0.0))
    o_ref[...] = acc.astype(o_ref.dtype)


def _mm_cols(a, ss, w, layer, col0, n, out_dtype, bn=IN_PROJ_COL_TILE, bm=IN_PROJ_ROW_TILE, name="matmul"):
    m, k = a.shape
    first = col0 // bn
    assert first * bn == col0 and n % bn == 0
    return pl.pallas_call(
        functools.partial(_mm_kernel, relu_sq=False),
        grid=(m // bm, n // bn),
        in_specs=[pl.BlockSpec((bm, k), lambda i, j: (i, 0)),
                  pl.BlockSpec((bm, LANES), lambda i, j: (i, 0)),
                  pl.BlockSpec((None, k, bn), lambda i, j: (layer, 0, first + j))],
        out_specs=pl.BlockSpec((bm, bn), lambda i, j: (i, j)),
        out_shape=jax.ShapeDtypeStruct((m, n), out_dtype),
        compiler_params=_params("parallel", "arbitrary"),
        name=name,
    )(a, ss, w)


def _mlp_up_kernel(a_ref, ss_ref, w_ref, wd_ref, o_ref, wd_bf16_ref):
    _mm_kernel(a_ref, ss_ref, w_ref, o_ref, relu_sq=True)
    wd_bf16_ref[...] = wd_ref[...].astype(BF16)


def _mlp_up(a, ss, w, w_down, layer, bn, bm=MLP_ROW_TILE):
    m, k = a.shape
    n = w.shape[2]
    n_j = n // bn
    steps = (m // bm) * n_j
    slab = n // steps
    assert slab * steps == n
    return pl.pallas_call(
        _mlp_up_kernel,
        grid=(m // bm, n_j),
        in_specs=[pl.BlockSpec((bm, k), lambda i, j: (i, 0), pipeline_mode=pl.Buffered(1)),
                  pl.BlockSpec((bm, LANES), lambda i, j: (i, 0)),
                  pl.BlockSpec((None, k, bn), lambda i, j: (layer, 0, j)),
                  pl.BlockSpec((None, slab, k), lambda i, j: (layer, i * n_j + j, 0))],
        out_specs=[pl.BlockSpec((None, None, bm, bn), lambda i, j: (i, j, 0, 0)),
                   pl.BlockSpec((slab, k), lambda i, j: (i * n_j + j, 0))],
        out_shape=[jax.ShapeDtypeStruct((m // bm, n_j, bm, bn), BF16), jax.ShapeDtypeStruct((n, k), BF16)],
        compiler_params=_params("parallel", "arbitrary"),
        name="mlp_up",
    )(a, ss, w, w_down)


def _mm_transposed_kernel(wt_ref, a_ref, ss_ref, o_ref, *, t):
    acc = lax.dot_general(wt_ref[...], a_ref[...], (((1,), (1,)), ((), ())), preferred_element_type=F32)
    acc = acc * jnp.transpose(_row_scale(ss_ref))[0:1, :]
    for c in range(acc.shape[1] // t):
        o_ref[c] = acc[:, c * t:(c + 1) * t].astype(o_ref.dtype)


def _mm_transposed(a, ss, wt, layer, t, bm=1024):
    m, k = a.shape
    n = wt.shape[1]
    return pl.pallas_call(
        functools.partial(_mm_transposed_kernel, t=t),
        grid=(m // bm,),
        in_specs=[pl.BlockSpec((None, n, k), lambda i: (layer, 0, 0), pipeline_mode=pl.Buffered(1)),
                  pl.BlockSpec((bm, k), lambda i: (i, 0)),
                  pl.BlockSpec((bm, LANES), lambda i: (i, 0))],
        out_specs=pl.BlockSpec((bm // t, n, t), lambda i: (i, 0, 0)),
        out_shape=jax.ShapeDtypeStruct((m // t, n, t), BF16),
        compiler_params=_params("parallel"),
        name="in_proj_vt",
    )(wt, a, ss)


def _mm_residual_kernel(a_ref, w_ref, r_ref, *rest):
    o_ref = rest[0] if len(rest) == 1 else rest[2]
    kk = pl.program_id(2)

    @pl.when(kk == 0)
    def _():
        o_ref[...] = r_ref[...]

    a = jnp.concatenate([a_ref[c] for c in range(a_ref.shape[0])], axis=1)
    o_ref[...] += jnp.dot(a, w_ref[...], preferred_element_type=F32)

    if len(rest) > 1:
        g_ref, w_next_ref, _, xb_ref, ss_ref, w_next_bf16_ref = rest
        w_next_bf16_ref[...] = w_next_ref[...].astype(BF16)

        @pl.when(kk == pl.num_programs(2) - 1)
        def _():
            _store_norm_inputs(o_ref[...], g_ref, xb_ref, ss_ref, pl.program_id(1) == 0)


def _mm_residual(a, w, r, g=None, w_next=None, next_layer=None, bm=1024, bn=1024, bk=4096):
    row_tiles, col_tiles, tile_rows, tile_cols = a.shape
    m, k = row_tiles * tile_rows, col_tiles * tile_cols
    n = w.shape[1]
    sub = tile_rows // bm
    assert sub * bm == tile_rows
    tile_spec = pl.BlockSpec((bm, bn), lambda i, j, kk: (i, j))
    r_spec = tile_spec if g is None else pl.BlockSpec((bm, bn), lambda i, j, kk: (i, j),
                                                      pipeline_mode=pl.Buffered(1))
    in_specs = [pl.BlockSpec((None, bk // tile_cols, bm, tile_cols), lambda i, j, kk: (i // sub, kk, i % sub, 0)),
                pl.BlockSpec((bk, bn), lambda i, j, kk: (kk, j)),
                r_spec]
    out_specs = [tile_spec]
    out_shape = [jax.ShapeDtypeStruct((m, n), F32)]
    operands = [a, w, r]
    if g is not None:
        n_j, n_k = n // bn, k // bk
        steps = (m // bm) * n_j * n_k
        rows_next, cols_next = w_next.shape[1:]
        slab = rows_next // steps
        assert slab * steps == rows_next

        def slab_index(i, j, kk):
            return (i * n_j + j) * n_k + kk

        in_specs += [pl.BlockSpec((1, bn), lambda i, j, kk: (0, j)),
                     pl.BlockSpec((None, slab, cols_next), lambda i, j, kk: (next_layer, slab_index(i, j, kk), 0))]
        out_specs += [tile_spec, pl.BlockSpec((bm, LANES), lambda i, j, kk: (i, 0)),
                      pl.BlockSpec((slab, cols_next), lambda i, j, kk: (slab_index(i, j, kk), 0))]
        out_shape += [jax.ShapeDtypeStruct((m, n), BF16), jax.ShapeDtypeStruct((m, LANES), F32),
                      jax.ShapeDtypeStruct((rows_next, cols_next), BF16)]
        operands += [g.reshape(1, n), w_next]
    return pl.pallas_call(
        _mm_residual_kernel,
        grid=(m // bm, n // bn, k // bk),
        in_specs=in_specs,
        out_specs=out_specs,
        out_shape=out_shape,
        compiler_params=_params("parallel", "arbitrary", "arbitrary"),
        name="mlp_down",
    )(*operands)


def _out_proj_kernel(a_ref, b_ref, c_ref, d_ref, w_ref, r_ref, g_ref, o_ref, xb_ref, ss_ref):
    mix = jnp.concatenate([a_ref[...]] + [b_ref[g] for g in range(len(B_GROUPS))] + [c_ref[...], d_ref[...]], axis=1)
    x_new = r_ref[...] + jnp.dot(mix, w_ref[...].astype(BF16), preferred_element_type=F32)
    o_ref[...] = x_new
    _store_norm_inputs(x_new, g_ref, xb_ref, ss_ref, pl.program_id(1) == 0)


def _out_proj(out_a, out_b, out_c, out_d, w, layer, r, g, bm=1024, bn=512):
    m = out_a.shape[0]
    n = w.shape[2]
    n_groups = len(B_GROUPS)
    return pl.pallas_call(
        _out_proj_kernel,
        grid=(m // bm, n // bn),
        in_specs=[pl.BlockSpec((bm, A_WIDTH), lambda i, j: (i, 0)),
                  pl.BlockSpec((n_groups, bm, B_GROUP_WIDTH), lambda i, j: (0, i, 0)),
                  pl.BlockSpec((bm, C_WIDTH), lambda i, j: (i, 0)),
                  pl.BlockSpec((bm, D_WIDTH), lambda i, j: (i, 0)),
                  pl.BlockSpec((None, D_MODEL, bn), lambda i, j: (layer, 0, j)),
                  pl.BlockSpec((bm, bn), lambda i, j: (i, j)),
                  pl.BlockSpec((1, bn), lambda i, j: (0, j))],
        out_specs=[pl.BlockSpec((bm, bn), lambda i, j: (i, j)),
                   pl.BlockSpec((bm, bn), lambda i, j: (i, j)),
                   pl.BlockSpec((bm, LANES), lambda i, j: (i, 0))],
        out_shape=[jax.ShapeDtypeStruct((m, n), F32), jax.ShapeDtypeStruct((m, n), BF16),
                   jax.ShapeDtypeStruct((m, LANES), F32)],
        compiler_params=_params("parallel", "arbitrary"),
        name="out_proj",
    )(out_a, out_b, out_c, out_d, w, r, g.reshape(1, n))


def _attn_a_kernel(q_ref, k_ref, vt_ref, lam_ref, g_ref, o_ref, *scratch, t, heads, lam_init):
    qi = pl.program_id(2)
    lane = lax.broadcasted_iota(jnp.int32, (t, HEAD_DIM), 1)
    qqs = []
    for h in range(heads):
        q = q_ref[:, h * HEAD_DIM:(h + 1) * HEAD_DIM]
        zero = jnp.zeros_like(q)
        qqs.append(jnp.concatenate([jnp.where(lane < A_SUB, q, zero), jnp.where(lane >= A_SUB, q, zero)], axis=0))

    ones = jnp.ones((A_ONES_ROWS, t), BF16)
    m_scr, acc_scr = scratch[:heads], scratch[heads:]
    for h in range(heads):
        m_scr[h][...] = jnp.full(m_scr[h].shape, MASKED_SCORE, F32)
        acc_scr[h][...] = jnp.zeros(acc_scr[h].shape, F32)

    def step(j, masked):
        def scores(h):
            k = k_ref[pl.ds(pl.multiple_of(j * t, t), t), h * HEAD_DIM:(h + 1) * HEAD_DIM]
            s = lax.dot_general(k, qqs[h], (((1,), (1,)), ((), ())), preferred_element_type=F32)
            if masked:
                key = lax.broadcasted_iota(jnp.int32, s.shape, 0)
                lane_q = lax.broadcasted_iota(jnp.int32, s.shape, 1)
                qry = jnp.where(lane_q >= t, lane_q - t, lane_q)
                s = jnp.where(key <= qry, s, MASKED_SCORE)
            return s

        s_next = scores(0)
        for h in range(heads):
            s = s_next
            if h + 1 < heads:
                s_next = scores(h + 1)
            vt = jnp.concatenate([vt_ref[j, h * HEAD_DIM:(h + 1) * HEAD_DIM, :], ones], axis=0)
            m = m_scr[h][...]
            m_new = jnp.maximum(m, jnp.max(s, axis=0, keepdims=True))
            alpha = jnp.exp2(m - m_new)
            p = jnp.exp2(s - m_new)
            m_scr[h][...] = m_new
            acc_scr[h][...] = alpha * acc_scr[h][...] + jnp.dot(vt, p.astype(BF16), preferred_element_type=F32)

    def body(j, carry):
        step(j, masked=False)
        return carry

    lax.fori_loop(0, qi, body, 0)
    step(qi, masked=True)

    lp = lam_ref[...]
    lam = (jnp.exp(jnp.sum(lp[0:1] * lp[1:2], axis=-1, keepdims=True))
           - jnp.exp(jnp.sum(lp[2:3] * lp[3:4], axis=-1, keepdims=True)) + lam_init)
    for h in range(heads):
        acc = acc_scr[h][...]
        o = acc[:HEAD_DIM] / acc[HEAD_DIM:HEAD_DIM + 1]
        o = (o[:, :t] - lam * o[:, t:]).T
        ms = jnp.mean(o * o, axis=-1, keepdims=True)
        o = o * lax.rsqrt(ms + DIFF_EPS) * g_ref[...]
        o_ref[:, h * HEAD_DIM:(h + 1) * HEAD_DIM] = (o * (1.0 - lam_init)).astype(o_ref.dtype)


def _attn_a(qk, vt, lam_params, subln_g, lam_init, batch, seq, t, heads=8):
    nq = seq // t
    width = heads * HEAD_DIM
    n_pairs = A_HEADS // heads
    return pl.pallas_call(
        functools.partial(_attn_a_kernel, t=t, heads=heads, lam_init=lam_init),
        grid=(batch, n_pairs, nq),
        in_specs=[pl.BlockSpec((t, width), lambda b, h, i: (b * nq + i, h)),
                  pl.BlockSpec((seq, width), lambda b, h, i: (b, n_pairs + h)),
                  pl.BlockSpec((nq, width, t), lambda b, h, i: (b, h, 0)),
                  pl.BlockSpec((4, A_SUB), lambda b, h, i: (0, 0)),
                  pl.BlockSpec((1, HEAD_DIM), lambda b, h, i: (0, 0))],
        out_specs=pl.BlockSpec((t, width), lambda b, h, i: (b * nq + i, h)),
        out_shape=jax.ShapeDtypeStruct((batch * seq, A_WIDTH), BF16),
        scratch_shapes=([pltpu.VMEM((1, 2 * t), F32)] * heads
                        + [pltpu.VMEM((HEAD_DIM + A_ONES_ROWS, 2 * t), F32)] * heads),
        compiler_params=_params("parallel", "parallel", "arbitrary"),
        name="diff_attention",
    )(qk, qk, vt, lam_params, subln_g.reshape(1, HEAD_DIM))


def _attn_b_kernel(*refs, tile):
    n_groups = len(B_GROUPS)
    q_refs = refs[0:n_groups]
    k_refs = refs[n_groups:2 * n_groups]
    v_refs = refs[2 * n_groups:3 * n_groups]
    o_ref = refs[3 * n_groups]
    out_scr, lse_scr = refs[3 * n_groups + 1:]
    ti = pl.program_id(2)
    scale = HEAD_DIM ** -0.5
    base_delta = {}
    chunks = []

    for g, (window, dilation) in enumerate(B_GROUPS):
        back = window // dilation
        seg = tile // dilation
        tq = min(B_QUERY_CHUNK, seg)
        tk = tq + back
        if tq not in base_delta:
            base_delta[tq] = (lax.broadcasted_iota(jnp.int32, (tq, tk), 0)
                              - lax.broadcasted_iota(jnp.int32, (tq, tk), 1))
        for r in range(dilation):
            for c in range(seg // tq):
                chunks.append((g, dilation, back, tq, tk, r, c * tq, ti * seg + c * tq))

    def scores(chunk):
        g, dilation, back, tq, tk, r, i_loc, i0 = chunk
        ks = pl.multiple_of(jnp.maximum(i0 - back, 0), back)
        q = q_refs[g][r, i_loc:i_loc + tq, :]
        k = k_refs[g][r, pl.ds(ks, tk), :]
        s = lax.dot_general(q, k, (((1,), (1,)), ((), ())), preferred_element_type=F32) * scale
        delta = base_delta[tq] + (i0 - ks)
        return jnp.where((delta >= 0) & (delta <= back), s, MASKED_SCORE), ks

    nxt = scores(chunks[0])
    for idx, (g, dilation, back, tq, tk, r, i_loc, i0) in enumerate(chunks):
        s, ks = nxt
        if idx + 1 < len(chunks):
            nxt = scores(chunks[idx + 1])
        v = v_refs[g][r, pl.ds(ks, tk), :]
        m = jnp.max(s, axis=-1, keepdims=True)
        p = jnp.exp(s - m)
        l = jnp.sum(p, axis=-1, keepdims=True)
        o = jnp.dot(p.astype(BF16), v, preferred_element_type=F32) / l
        lse = jnp.broadcast_to(m + jnp.log(l), o.shape)
        rows = pl.ds(dilation * i_loc + r, tq, stride=dilation) if dilation > 1 else pl.ds(i_loc, tq)
        out_scr[g, rows, :] = o
        lse_scr[g, rows, :] = lse

    for c in range(tile // B_QUERY_CHUNK):
        rows = pl.ds(c * B_QUERY_CHUNK, B_QUERY_CHUNK)
        lses = [lse_scr[g, rows, :] for g in range(n_groups)]
        lse_max = functools.reduce(jnp.maximum, lses)
        weights = [jnp.exp(x - lse_max) for x in lses]
        denom = functools.reduce(lambda a, b: a + b, weights)
        for g in range(n_groups):
            o_ref[g, rows, :] = (out_scr[g, rows, :] * (weights[g] / denom)).astype(o_ref.dtype)


def _attn_b(qkvs, batch, seq, tile=B_TOKEN_TILE):
    n_tiles = seq // tile
    n_groups = len(B_GROUPS)

    def q_spec(g):
        d = B_GROUPS[g][1]
        return pl.BlockSpec((None, d, tile // d, HEAD_DIM), lambda b, h, i: (b, 0, i, h))

    def kv_spec(g, part):
        d = B_GROUPS[g][1]
        return pl.BlockSpec((None, d, seq // d, HEAD_DIM),
                            lambda b, h, i: (b, 0, 0, part * B_HEADS_PER_GROUP + h))

    in_specs = ([q_spec(g) for g in range(n_groups)] + [kv_spec(g, 1) for g in range(n_groups)]
                + [kv_spec(g, 2) for g in range(n_groups)])
    return pl.pallas_call(
        functools.partial(_attn_b_kernel, tile=tile),
        grid=(batch, B_HEADS_PER_GROUP, n_tiles),
        in_specs=in_specs,
        out_specs=pl.BlockSpec((n_groups, tile, HEAD_DIM), lambda b, h, i: (0, b * n_tiles + i, h)),
        out_shape=jax.ShapeDtypeStruct((n_groups, batch * seq, B_GROUP_WIDTH), BF16),
        scratch_shapes=[pltpu.VMEM((n_groups, tile, HEAD_DIM), F32),
                        pltpu.VMEM((n_groups, tile, HEAD_DIM), F32)],
        compiler_params=_params("parallel", "parallel", "arbitrary"),
        name="dilated_attention",
    )(*(list(qkvs) * 3))


def _shift_rows(x, k, row):
    return jnp.where(row >= k, pltpu.roll(x, k, 0), 0.0)


def _pool_kernel(u_lo_ref, u_hi_ref, w_ref, scale_ref, o_ref):
    g = pl.program_id(1)
    u = jnp.concatenate([u_lo_ref[...], u_hi_ref[...]], axis=1)
    row = lax.broadcasted_iota(jnp.int32, u.shape, 0)
    window = jnp.left_shift(2, g)
    total = u
    step = 1
    while step < max(C_WINDOWS):
        widened = total + _shift_rows(total, step, row)
        total = jnp.where(step < window, widened, total)
        step *= 2
    count = jnp.minimum(row + 1, window).astype(F32)
    pooled = total / count - u
    y = jnp.dot(pooled.astype(BF16), w_ref[0], preferred_element_type=F32)
    o_ref[...] = (y * scale_ref[...]).astype(o_ref.dtype)


def _pool(proj, col0, pool_w, pool_scale, batch, seq):
    n_groups = len(C_WINDOWS)
    tiles_per_group = C_GROUP_DIM // LANES
    assert tiles_per_group == 2 and col0 % LANES == 0
    first = col0 // LANES

    def u_spec(half):
        return pl.BlockSpec((seq, LANES), lambda b, g: (b, first + tiles_per_group * g + half))

    return pl.pallas_call(
        _pool_kernel,
        grid=(batch, n_groups),
        in_specs=[u_spec(0), u_spec(1),
                  pl.BlockSpec((1, C_GROUP_DIM, C_GROUP_DIM), lambda b, g: (g, 0, 0)),
                  pl.BlockSpec((1, C_GROUP_DIM), lambda b, g: (0, g))],
        out_specs=pl.BlockSpec((seq, C_GROUP_DIM), lambda b, g: (b, g)),
        out_shape=jax.ShapeDtypeStruct((batch * seq, C_WIDTH), BF16),
        compiler_params=_params("parallel", "parallel"),
        name="multiscale_pool",
    )(proj, proj, pool_w, pool_scale.reshape(1, C_WIDTH))


def _conv_kernel(gb_ref, gc_ref, h_ref, w_ref, o_ref):
    z = gc_ref[...] * h_ref[...]
    row = lax.broadcasted_iota(jnp.int32, z.shape, 0)
    w = w_ref[...]
    y = w[CONV_WIDTH - 1:CONV_WIDTH] * z
    for tap in range(1, CONV_WIDTH):
        y = y + w[CONV_WIDTH - 1 - tap:CONV_WIDTH - tap] * _shift_rows(z, tap, row)
    o_ref[...] = (gb_ref[...] * y).astype(o_ref.dtype)


def _short_conv(proj, col0, conv_w, batch, seq):
    n_col = D_WIDTH // LANES
    assert col0 % LANES == 0
    first = col0 // LANES

    def spec(part):
        return pl.BlockSpec((seq, LANES), lambda b, c: (b, first + part * n_col + c))

    return pl.pallas_call(
        _conv_kernel,
        grid=(batch, n_col),
        in_specs=[spec(0), spec(1), spec(2), pl.BlockSpec((CONV_WIDTH, LANES), lambda b, c: (0, c))],
        out_specs=pl.BlockSpec((seq, LANES), lambda b, c: (b, c)),
        out_shape=jax.ShapeDtypeStruct((batch * seq, D_WIDTH), BF16),
        compiler_params=_params("parallel", "parallel"),
        name="short_conv",
    )(proj, proj, proj, conv_w)


def _rope_tables(seq, dim):
    half = dim // 2
    inv = ROPE_THETA ** (-jnp.arange(0, dim, 2, dtype=F32) / dim)
    ang = jnp.arange(seq, dtype=F32)[:, None] * inv[None, :]
    cos, sin = jnp.cos(ang), jnp.sin(ang)
    reps = LANES // dim
    cos_t = jnp.tile(jnp.concatenate([cos, cos], axis=1), (1, reps))
    zeros = jnp.zeros_like(sin)
    if dim == LANES:
        return (cos_t, jnp.concatenate([-sin, sin], axis=1)), (half,)
    upper = jnp.tile(jnp.concatenate([zeros, sin], axis=1), (1, reps))
    lower = jnp.tile(jnp.concatenate([-sin, zeros], axis=1), (1, reps))
    return (cos_t, upper, lower), (half, LANES - half)


def kernel(x, w_in, w_out, norm_mix, norm_mlp, diff_lambda, diff_subln, pool_w, pool_scale, conv_w,
           w_up, w_down, norm_final):
    batch, seq, d_model = x.shape
    depth = w_in.shape[0]
    m = batch * seq
    x = x.reshape(m, d_model)

    tables_a, shifts_a = _rope_tables(seq, A_SUB)
    tables_b, shifts_b = _rope_tables(seq, HEAD_DIM)

    a_qk_end = 2 * A_WIDTH
    a_end = 3 * A_WIDTH
    b_end = a_end + 3 * B_WIDTH
    c_end = b_end + C_WIDTH
    t_a = 512
    tail0 = (b_end // IN_PROJ_COL_TILE) * IN_PROJ_COL_TILE
    in_width = w_in.shape[2]

    w_in_l = _cast_layer(w_in, 0)[None]

    h, ss = _prep(x, norm_mix[0])
    for l in range(depth):
        lam_init = 0.8 - 0.6 * math.exp(-0.3 * l)
        w_a_vt = jnp.swapaxes(w_in_l[:, :, a_qk_end:a_end], 1, 2)

        qk_a = _mm_rope(h, ss, w_in_l, 0, a_qk_end, tables_a, shifts_a, A_SCORE_SCALE, A_WIDTH, seq)
        vt_a = _mm_transposed(h, ss, w_a_vt, 0, t_a)
        qkv_b = [_mm_rope_permute(h, ss, w_in_l, 0, a_end, g, tables_b, shifts_b[0], B_GROUPS[g][1], batch, seq)
                 for g in range(len(B_GROUPS))]
        tail = _mm_cols(h, ss, w_in_l, 0, tail0, in_width - tail0, F32, name="in_proj_tail")

        out_a = _attn_a(qk_a, vt_a, diff_lambda[l], diff_subln[l], lam_init, batch, seq, t_a)
        out_b = _attn_b(qkv_b, batch, seq)
        out_c = _pool(tail, b_end - tail0, pool_w[l].astype(BF16), pool_scale[l], batch, seq)
        out_d = _short_conv(tail, c_end - tail0, conv_w[l], batch, seq)

        x, h, ss = _out_proj(out_a, out_b, out_c, out_d, w_out, l, x, norm_mlp[l])
        act, w_down_l = _mlp_up(h, ss, w_up, w_down, l, 512)
        if l + 1 < depth:
            x, h, ss, w_next = _mm_residual(act, w_down_l, x, norm_mix[l + 1], w_in, l + 1)
            w_in_l = w_next[None]
        else:
            x, = _mm_residual(act, w_down_l, x)

    out = _rmsnorm(x, norm_final, F32)
    return out.reshape(batch, seq, d_model)
```

```python
import functools
import math

import jax
import jax.numpy as jnp
from jax import lax
from jax.experimental import pallas as pl
from jax.experimental.pallas import tpu as pltpu

D_MODEL = 4096
HEAD_DIM = 128
A_HEADS = 8
A_WIDTH = A_HEADS * HEAD_DIM
A_SUB = HEAD_DIM // 2
A_SCORE_SCALE = A_SUB ** -0.5 * math.log2(math.e)
A_ONES_ROWS = 16
B_GROUPS = ((128, 1), (512, 4), (2048, 16))
B_HEADS_PER_GROUP = 3
B_GROUP_WIDTH = B_HEADS_PER_GROUP * HEAD_DIM
B_WIDTH = B_GROUP_WIDTH * len(B_GROUPS)
B_TOKEN_TILE = 2048
B_QUERY_CHUNK = 256
C_WINDOWS = (2, 4, 8, 16)
C_WIDTH = 1024
C_GROUP_DIM = C_WIDTH // len(C_WINDOWS)
D_WIDTH = D_MODEL - A_WIDTH - B_WIDTH - C_WIDTH
CONV_WIDTH = 3
D_FF = 4 * D_MODEL
ROPE_THETA = 10000.0
NORM_EPS = 1e-6
DIFF_EPS = 1e-5

LANES = 128
MXU_COLS = 256
MLP_ROW_TILE = 2048
IN_PROJ_COL_TILE = 1024
IN_PROJ_ROW_TILE = 1024
VMEM_LIMIT_BYTES = 56 * 1024 * 1024
MLP_DOWN_VMEM_LIMIT_BYTES = 58 * 1024 * 1024
MASKED_SCORE = -1e30

F32 = jnp.float32
BF16 = jnp.bfloat16


def _params(*semantics):
    return pltpu.CompilerParams(dimension_semantics=semantics, vmem_limit_bytes=VMEM_LIMIT_BYTES)


def _rmsnorm_kernel(x_ref, g_ref, o_ref, *, eps):
    x = x_ref[...]
    ms = jnp.mean(x * x, axis=-1, keepdims=True)
    o_ref[...] = (x * lax.rsqrt(ms + eps) * g_ref[...]).astype(o_ref.dtype)


def _rmsnorm(x, g, out_dtype, tm=256):
    m, d = x.shape
    return pl.pallas_call(
        functools.partial(_rmsnorm_kernel, eps=NORM_EPS),
        grid=(m // tm,),
        in_specs=[pl.BlockSpec((tm, d), lambda i: (i, 0)),
                  pl.BlockSpec((1, d), lambda i: (0, 0))],
        out_specs=pl.BlockSpec((tm, d), lambda i: (i, 0)),
        out_shape=jax.ShapeDtypeStruct((m, d), out_dtype),
        compiler_params=_params("parallel"),
        name="rmsnorm",
    )(x, g.reshape(1, d))


def _prep_kernel(x_ref, g_ref, xb_ref, ss_ref):
    x = x_ref[...]
    xb_ref[...] = (x * g_ref[...]).astype(xb_ref.dtype)
    ss_ref[...] = jnp.broadcast_to(jnp.sum(x * x, axis=-1, keepdims=True), ss_ref.shape)


def _prep(x, g, tm=256):
    m, d = x.shape
    return pl.pallas_call(
        _prep_kernel,
        grid=(m // tm,),
        in_specs=[pl.BlockSpec((tm, d), lambda i: (i, 0)),
                  pl.BlockSpec((1, d), lambda i: (0, 0))],
        out_specs=[pl.BlockSpec((tm, d), lambda i: (i, 0)),
                   pl.BlockSpec((tm, LANES), lambda i: (i, 0))],
        out_shape=[jax.ShapeDtypeStruct((m, d), BF16), jax.ShapeDtypeStruct((m, LANES), F32)],
        compiler_params=_params("parallel"),
        name="norm_prep",
    )(x, g.reshape(1, d))


def _cast_kernel(w_ref, o_ref):
    o_ref[...] = w_ref[...].astype(o_ref.dtype)


def _cast_layer(w, layer, rows=256):
    _, k, n = w.shape
    return pl.pallas_call(
        _cast_kernel,
        grid=(k // rows,),
        in_specs=[pl.BlockSpec((None, rows, n), lambda i: (layer, i, 0))],
        out_specs=pl.BlockSpec((rows, n), lambda i: (i, 0)),
        out_shape=jax.ShapeDtypeStruct((k, n), BF16),
        compiler_params=_params("parallel"),
        name="cast_weights",
    )(w)


def _row_scale(ss_ref):
    return lax.rsqrt(ss_ref[...] * (1.0 / D_MODEL) + NORM_EPS)


def _scale_rows(y, rs):
    return y * jnp.concatenate([rs] * (y.shape[1] // LANES), axis=1)


def _store_norm_inputs(x_new, g_ref, xb_ref, ss_ref, first):
    xb_ref[...] = (x_new * g_ref[...]).astype(xb_ref.dtype)
    part = jnp.broadcast_to(jnp.sum(x_new * x_new, axis=-1, keepdims=True), ss_ref.shape)

    @pl.when(first)
    def _():
        ss_ref[...] = part

    @pl.when(jnp.logical_not(first))
    def _():
        ss_ref[...] += part


def _rotate(y, cos_ref, sin_refs, shifts):
    outs = []
    for c in range(y.shape[1] // LANES):
        yc = y[:, c * LANES:(c + 1) * LANES]
        oc = yc * cos_ref[...]
        for s_ref, shift in zip(sin_refs, shifts):
            oc = oc + pltpu.roll(yc, shift, 1) * s_ref[...]
        outs.append(oc)
    return jnp.concatenate(outs, axis=1)


def _mm_rope_kernel(a_ref, ss_ref, w_ref, cos_ref, *rest, shifts, q_scale, n_q_blocks):
    sin_refs, o_ref = rest[:-1], rest[-1]
    scale = jnp.where(pl.program_id(1) < n_q_blocks, q_scale, 1.0).astype(F32)
    a = a_ref[...]
    rs = _row_scale(ss_ref)
    for c in range(o_ref.shape[1] // MXU_COLS):
        cols = slice(c * MXU_COLS, (c + 1) * MXU_COLS)
        acc = _scale_rows(jnp.dot(a, w_ref[:, cols].astype(BF16), preferred_element_type=F32), rs)
        o_ref[:, cols] = (_rotate(acc, cos_ref, sin_refs, shifts) * scale).astype(o_ref.dtype)


def _mm_rope(a, ss, w, layer, n, tables, shifts, q_scale, n_q_cols, seq, bn=IN_PROJ_COL_TILE,
             bm=IN_PROJ_ROW_TILE):
    m, k = a.shape
    tab_blocks = seq // bm
    tab_spec = pl.BlockSpec((bm, LANES), lambda i, j: (i % tab_blocks, 0))
    return pl.pallas_call(
        functools.partial(_mm_rope_kernel, shifts=shifts, q_scale=q_scale, n_q_blocks=n_q_cols // bn),
        grid=(m // bm, n // bn),
        in_specs=[pl.BlockSpec((bm, k), lambda i, j: (i, 0)),
                  pl.BlockSpec((bm, LANES), lambda i, j: (i, 0)),
                  pl.BlockSpec((None, k, bn), lambda i, j: (layer, 0, j))] + [tab_spec] * len(tables),
        out_specs=pl.BlockSpec((bm, bn), lambda i, j: (i, j)),
        out_shape=jax.ShapeDtypeStruct((m, n), BF16),
        compiler_params=_params("parallel", "arbitrary"),
        name="in_proj_rope",
    )(a, ss, w, *tables)


def _mm_rope_permute_kernel(a_ref, ss_ref, wq_ref, wk_ref, wv_ref, cos_ref, sin_ref, o_ref, scr_ref, *,
                            shift, n_rope_cols, dilation):
    a = a_ref[...]
    w = jnp.concatenate([wq_ref[...], wk_ref[...], wv_ref[...]], axis=1)
    rs = _row_scale(ss_ref)
    n = o_ref.shape[2]
    rows = a.shape[0] // dilation
    for lo in range(0, n, 2 * MXU_COLS):
        hi = min(lo + 2 * MXU_COLS, n)
        acc = _scale_rows(jnp.dot(a, w[:, lo:hi], preferred_element_type=F32), rs)
        rope_hi = min(hi, n_rope_cols)
        if lo < rope_hi:
            rot = _rotate(acc[:, :rope_hi - lo], cos_ref, (sin_ref,), (shift,))
            acc = rot if rope_hi == hi else jnp.concatenate([rot, acc[:, rope_hi - lo:]], axis=1)
        if dilation == 1:
            o_ref[0, :, lo:hi] = acc.astype(o_ref.dtype)
        else:
            for c in range(lo // LANES, hi // LANES):
                cols = slice(c * LANES, (c + 1) * LANES)
                scr_ref[c] = acc[:, c * LANES - lo:(c + 1) * LANES - lo]
                for r in range(dilation):
                    o_ref[r, :, cols] = scr_ref[c, pl.ds(r, rows, stride=dilation), :].astype(o_ref.dtype)


def _mm_rope_permute(a, ss, w, layer, col0, group, tables, shift, dilation, batch, seq, bm=IN_PROJ_ROW_TILE):
    m, k = a.shape
    n = 3 * B_GROUP_WIDTH
    n_rope_cols = 2 * B_GROUP_WIDTH
    tiles = seq // bm
    tab_spec = pl.BlockSpec((bm, LANES), lambda i: (i % tiles, 0))
    n_groups = len(B_GROUPS)
    first = col0 // B_GROUP_WIDTH
    assert first * B_GROUP_WIDTH == col0

    def w_spec(part):
        return pl.BlockSpec((None, k, B_GROUP_WIDTH), lambda i: (layer, 0, first + part * n_groups + group),
                            pipeline_mode=pl.Buffered(1))

    return pl.pallas_call(
        functools.partial(_mm_rope_permute_kernel, shift=shift, n_rope_cols=n_rope_cols, dilation=dilation),
        grid=(m // bm,),
        in_specs=[pl.BlockSpec((bm, k), lambda i: (i, 0)), pl.BlockSpec((bm, LANES), lambda i: (i, 0)),
                  w_spec(0), w_spec(1), w_spec(2), tab_spec, tab_spec],
        out_specs=pl.BlockSpec((None, dilation, bm // dilation, n), lambda i: (i // tiles, 0, i % tiles, 0)),
        out_shape=jax.ShapeDtypeStruct((batch, dilation, seq // dilation, n), BF16),
        scratch_shapes=[pltpu.VMEM((n // LANES, bm, LANES), F32)],
        compiler_params=_params("parallel"),
        name="in_proj_dilated",
    )(a, ss, w, w, w, *tables)


def _mm_kernel(a_ref, ss_ref, w_ref, o_ref, *, relu_sq):
    acc = jnp.dot(a_ref[...], w_ref[...].astype(BF16), preferred_element_type=F32)
    acc = _scale_rows(acc, _row_scale(ss_ref))
    if relu_sq:
        acc = jnp.square(jnp.maximum(acc, 0.0))
    o_ref[...] = acc.astype(o_ref.dtype)


def _mm_cols(a, ss, w, layer, col0, n, out_dtype, bn=IN_PROJ_COL_TILE, bm=IN_PROJ_ROW_TILE, name="matmul"):
    m, k = a.shape
    first = col0 // bn
    assert first * bn == col0 and n % bn == 0
    return pl.pallas_call(
        functools.partial(_mm_kernel, relu_sq=False),
        grid=(m // bm, n // bn),
        in_specs=[pl.BlockSpec((bm, k), lambda i, j: (i, 0)),
                  pl.BlockSpec((bm, LANES), lambda i, j: (i, 0)),
                  pl.BlockSpec((None, k, bn), lambda i, j: (layer, 0, first + j))],
        out_specs=pl.BlockSpec((bm, bn), lambda i, j: (i, j)),
        out_shape=jax.ShapeDtypeStruct((m, n), out_dtype),
        compiler_params=_params("parallel", "arbitrary"),
        name=name,
    )(a, ss, w)


def _mlp_up_kernel(a_ref, ss_ref, w_ref, wd_ref, o_ref, wd_bf16_ref):
    _mm_kernel(a_ref, ss_ref, w_ref, o_ref, relu_sq=True)
    wd_bf16_ref[...] = wd_ref[...].astype(BF16)


def _mlp_up(a, ss, w, w_down, layer, bn, bm=MLP_ROW_TILE):
    m, k = a.shape
    n = w.shape[2]
    n_j = n // bn
    steps = (m // bm) * n_j
    slab = n // steps
    assert slab * steps == n
    return pl.pallas_call(
        _mlp_up_kernel,
        grid=(m // bm, n_j),
        in_specs=[pl.BlockSpec((bm, k), lambda i, j: (i, 0), pipeline_mode=pl.Buffered(1)),
                  pl.BlockSpec((bm, LANES), lambda i, j: (i, 0)),
                  pl.BlockSpec((None, k, bn), lambda i, j: (layer, 0, j)),
                  pl.BlockSpec((None, slab, k), lambda i, j: (layer, i * n_j + j, 0))],
        out_specs=[pl.BlockSpec((None, None, bm, bn), lambda i, j: (i, j, 0, 0)),
                   pl.BlockSpec((slab, k), lambda i, j: (i * n_j + j, 0))],
        out_shape=[jax.ShapeDtypeStruct((m // bm, n_j, bm, bn), BF16), jax.ShapeDtypeStruct((n, k), BF16)],
        compiler_params=_params("parallel", "arbitrary"),
        name="mlp_up",
    )(a, ss, w, w_down)


def _mm_transposed_kernel(wt_ref, a_ref, ss_ref, o_ref, *, t):
    acc = lax.dot_general(wt_ref[...], a_ref[...], (((1,), (1,)), ((), ())), preferred_element_type=F32)
    acc = acc * jnp.transpose(_row_scale(ss_ref))[0:1, :]
    for c in range(acc.shape[1] // t):
        o_ref[c] = acc[:, c * t:(c + 1) * t].astype(o_ref.dtype)


def _mm_transposed(a, ss, wt, layer, t, bm=1024):
    m, k = a.shape
    n = wt.shape[1]
    return pl.pallas_call(
        functools.partial(_mm_transposed_kernel, t=t),
        grid=(m // bm,),
        in_specs=[pl.BlockSpec((None, n, k), lambda i: (layer, 0, 0), pipeline_mode=pl.Buffered(1)),
                  pl.BlockSpec((bm, k), lambda i: (i, 0)),
                  pl.BlockSpec((bm, LANES), lambda i: (i, 0))],
        out_specs=pl.BlockSpec((bm // t, n, t), lambda i: (i, 0, 0)),
        out_shape=jax.ShapeDtypeStruct((m // t, n, t), BF16),
        compiler_params=_params("parallel"),
        name="in_proj_vt",
    )(wt, a, ss)


def _mm_residual_kernel(a_ref, w_ref, r_ref, *rest):
    o_ref = rest[0] if len(rest) == 1 else rest[2]
    kk = pl.program_id(2)

    @pl.when(kk == 0)
    def _():
        o_ref[...] = r_ref[...]

    a = jnp.concatenate([a_ref[c] for c in range(a_ref.shape[0])], axis=1)
    o_ref[...] += jnp.dot(a, w_ref[...], preferred_element_type=F32)

    if len(rest) > 1:
        g_ref, w_next_ref, _, xb_ref, ss_ref, w_next_bf16_ref = rest
        w_next_bf16_ref[...] = w_next_ref[...].astype(BF16)

        @pl.when(kk == pl.num_programs(2) - 1)
        def _():
            _store_norm_inputs(o_ref[...], g_ref, xb_ref, ss_ref, pl.program_id(1) == 0)


def _mm_residual(a, w, r, g=None, w_next=None, next_layer=None, bm=1024, bn=1024, bk=4096):
    row_tiles, col_tiles, tile_rows, tile_cols = a.shape
    m, k = row_tiles * tile_rows, col_tiles * tile_cols
    n = w.shape[1]
    sub = tile_rows // bm
    assert sub * bm == tile_rows
    tile_spec = pl.BlockSpec((bm, bn), lambda i, j, kk: (i, j))
    in_specs = [pl.BlockSpec((None, bk // tile_cols, bm, tile_cols), lambda i, j, kk: (i // sub, kk, i % sub, 0)),
                pl.BlockSpec((bk, bn), lambda i, j, kk: (kk, j)),
                tile_spec]
    out_specs = [tile_spec]
    out_shape = [jax.ShapeDtypeStruct((m, n), F32)]
    operands = [a, w, r]
    if g is not None:
        n_j, n_k = n // bn, k // bk
        steps = (m // bm) * n_j * n_k
        rows_next, cols_next = w_next.shape[1:]
        slab = rows_next // steps
        assert slab * steps == rows_next

        def slab_index(i, j, kk):
            return (i * n_j + j) * n_k + kk

        in_specs += [pl.BlockSpec((1, bn), lambda i, j, kk: (0, j)),
                     pl.BlockSpec((None, slab, cols_next), lambda i, j, kk: (next_layer, slab_index(i, j, kk), 0))]
        out_specs += [tile_spec, pl.BlockSpec((bm, LANES), lambda i, j, kk: (i, 0)),
                      pl.BlockSpec((slab, cols_next), lambda i, j, kk: (slab_index(i, j, kk), 0))]
        out_shape += [jax.ShapeDtypeStruct((m, n), BF16), jax.ShapeDtypeStruct((m, LANES), F32),
                      jax.ShapeDtypeStruct((rows_next, cols_next), BF16)]
        operands += [g.reshape(1, n), w_next]
    return pl.pallas_call(
        _mm_residual_kernel,
        grid=(m // bm, n // bn, k // bk),
        in_specs=in_specs,
        out_specs=out_specs,
        out_shape=out_shape,
        compiler_params=pltpu.CompilerParams(dimension_semantics=("parallel", "arbitrary", "arbitrary"),
                                             vmem_limit_bytes=MLP_DOWN_VMEM_LIMIT_BYTES),
        name="mlp_down",
    )(*operands)


def _out_proj_kernel(a_ref, b_ref, c_ref, d_ref, w_ref, r_ref, g_ref, o_ref, xb_ref, ss_ref):
    mix = jnp.concatenate([a_ref[...]] + [b_ref[g] for g in range(len(B_GROUPS))] + [c_ref[...], d_ref[...]], axis=1)
    x_new = r_ref[...] + jnp.dot(mix, w_ref[...].astype(BF16), preferred_element_type=F32)
    o_ref[...] = x_new
    _store_norm_inputs(x_new, g_ref, xb_ref, ss_ref, pl.program_id(1) == 0)


def _out_proj(out_a, out_b, out_c, out_d, w, layer, r, g, bm=1024, bn=512):
    m = out_a.shape[0]
    n = w.shape[2]
    n_groups = len(B_GROUPS)
    return pl.pallas_call(
        _out_proj_kernel,
        grid=(m // bm, n // bn),
        in_specs=[pl.BlockSpec((bm, A_WIDTH), lambda i, j: (i, 0)),
                  pl.BlockSpec((n_groups, bm, B_GROUP_WIDTH), lambda i, j: (0, i, 0)),
                  pl.BlockSpec((bm, C_WIDTH), lambda i, j: (i, 0)),
                  pl.BlockSpec((bm, D_WIDTH), lambda i, j: (i, 0)),
                  pl.BlockSpec((None, D_MODEL, bn), lambda i, j: (layer, 0, j)),
                  pl.BlockSpec((bm, bn), lambda i, j: (i, j)),
                  pl.BlockSpec((1, bn), lambda i, j: (0, j))],
        out_specs=[pl.BlockSpec((bm, bn), lambda i, j: (i, j)),
                   pl.BlockSpec((bm, bn), lambda i, j: (i, j)),
                   pl.BlockSpec((bm, LANES), lambda i, j: (i, 0))],
        out_shape=[jax.ShapeDtypeStruct((m, n), F32), jax.ShapeDtypeStruct((m, n), BF16),
                   jax.ShapeDtypeStruct((m, LANES), F32)],
        compiler_params=_params("parallel", "arbitrary"),
        name="out_proj",
    )(out_a, out_b, out_c, out_d, w, r, g.reshape(1, n))


def _attn_a_kernel(q_ref, k_ref, vt_ref, lam_ref, g_ref, o_ref, *scratch, t, heads, lam_init):
    qi = pl.program_id(2)
    lane = lax.broadcasted_iota(jnp.int32, (t, HEAD_DIM), 1)
    qqs = []
    for h in range(heads):
        q = q_ref[:, h * HEAD_DIM:(h + 1) * HEAD_DIM]
        zero = jnp.zeros_like(q)
        qqs.append(jnp.concatenate([jnp.where(lane < A_SUB, q, zero), jnp.where(lane >= A_SUB, q, zero)], axis=0))

    ones = jnp.ones((A_ONES_ROWS, t), BF16)
    m_scr, acc_scr = scratch[:heads], scratch[heads:]
    for h in range(heads):
        m_scr[h][...] = jnp.full(m_scr[h].shape, MASKED_SCORE, F32)
        acc_scr[h][...] = jnp.zeros(acc_scr[h].shape, F32)

    def step(j, masked):
        def scores(h):
            k = k_ref[pl.ds(pl.multiple_of(j * t, t), t), h * HEAD_DIM:(h + 1) * HEAD_DIM]
            s = lax.dot_general(k, qqs[h], (((1,), (1,)), ((), ())), preferred_element_type=F32)
            if masked:
                key = lax.broadcasted_iota(jnp.int32, s.shape, 0)
                lane_q = lax.broadcasted_iota(jnp.int32, s.shape, 1)
                qry = jnp.where(lane_q >= t, lane_q - t, lane_q)
                s = jnp.where(key <= qry, s, MASKED_SCORE)
            return s

        s_next = scores(0)
        for h in range(heads):
            s = s_next
            if h + 1 < heads:
                s_next = scores(h + 1)
            vt = jnp.concatenate([vt_ref[j, h * HEAD_DIM:(h + 1) * HEAD_DIM, :], ones], axis=0)
            m = m_scr[h][...]
            m_new = jnp.maximum(m, jnp.max(s, axis=0, keepdims=True))
            alpha = jnp.exp2(m - m_new)
            p = jnp.exp2(s - m_new)
            m_scr[h][...] = m_new
            acc_scr[h][...] = alpha * acc_scr[h][...] + jnp.dot(vt, p.astype(BF16), preferred_element_type=F32)

    def body(j, carry):
        step(j, masked=False)
        return carry

    lax.fori_loop(0, qi, body, 0)
    step(qi, masked=True)

    lp = lam_ref[...]
    lam = (jnp.exp(jnp.sum(lp[0:1] * lp[1:2], axis=-1, keepdims=True))
           - jnp.exp(jnp.sum(lp[2:3] * lp[3:4], axis=-1, keepdims=True)) + lam_init)
    for h in range(heads):
        acc = acc_scr[h][...]
        o = acc[:HEAD_DIM] / acc[HEAD_DIM:HEAD_DIM + 1]
        o = (o[:, :t] - lam * o[:, t:]).T
        ms = jnp.mean(o * o, axis=-1, keepdims=True)
        o = o * lax.rsqrt(ms + DIFF_EPS) * g_ref[...]
        o_ref[:, h * HEAD_DIM:(h + 1) * HEAD_DIM] = (o * (1.0 - lam_init)).astype(o_ref.dtype)


def _attn_a(qk, vt, lam_params, subln_g, lam_init, batch, seq, t, heads=8):
    nq = seq // t
    width = heads * HEAD_DIM
    n_pairs = A_HEADS // heads
    return pl.pallas_call(
        functools.partial(_attn_a_kernel, t=t, heads=heads, lam_init=lam_init),
        grid=(batch, n_pairs, nq),
        in_specs=[pl.BlockSpec((t, width), lambda b, h, i: (b * nq + i, h)),
                  pl.BlockSpec((seq, width), lambda b, h, i: (b, n_pairs + h)),
                  pl.BlockSpec((nq, width, t), lambda b, h, i: (b, h, 0)),
                  pl.BlockSpec((4, A_SUB), lambda b, h, i: (0, 0)),
                  pl.BlockSpec((1, HEAD_DIM), lambda b, h, i: (0, 0))],
        out_specs=pl.BlockSpec((t, width), lambda b, h, i: (b * nq + i, h)),
        out_shape=jax.ShapeDtypeStruct((batch * seq, A_WIDTH), BF16),
        scratch_shapes=([pltpu.VMEM((1, 2 * t), F32)] * heads
                        + [pltpu.VMEM((HEAD_DIM + A_ONES_ROWS, 2 * t), F32)] * heads),
        compiler_params=_params("parallel", "parallel", "arbitrary"),
        name="diff_attention",
    )(qk, qk, vt, lam_params, subln_g.reshape(1, HEAD_DIM))


def _attn_b_kernel(*refs, tile):
    n_groups = len(B_GROUPS)
    q_refs = refs[0:n_groups]
    k_refs = refs[n_groups:2 * n_groups]
    v_refs = refs[2 * n_groups:3 * n_groups]
    o_ref = refs[3 * n_groups]
    out_scr, lse_scr = refs[3 * n_groups + 1:]
    ti = pl.program_id(2)
    scale = HEAD_DIM ** -0.5
    base_delta = {}
    chunks = []

    for g, (window, dilation) in enumerate(B_GROUPS):
        back = window // dilation
        seg = tile // dilation
        tq = min(B_QUERY_CHUNK, seg)
        tk = tq + back
        if tq not in base_delta:
            base_delta[tq] = (lax.broadcasted_iota(jnp.int32, (tq, tk), 0)
                              - lax.broadcasted_iota(jnp.int32, (tq, tk), 1))
        for r in range(dilation):
            for c in range(seg // tq):
                chunks.append((g, dilation, back, tq, tk, r, c * tq, ti * seg + c * tq))

    def scores(chunk):
        g, dilation, back, tq, tk, r, i_loc, i0 = chunk
        ks = pl.multiple_of(jnp.maximum(i0 - back, 0), back)
        q = q_refs[g][r, i_loc:i_loc + tq, :]
        k = k_refs[g][r, pl.ds(ks, tk), :]
        s = lax.dot_general(q, k, (((1,), (1,)), ((), ())), preferred_element_type=F32) * scale
        delta = base_delta[tq] + (i0 - ks)
        return jnp.where((delta >= 0) & (delta <= back), s, MASKED_SCORE), ks

    nxt = scores(chunks[0])
    for idx, (g, dilation, back, tq, tk, r, i_loc, i0) in enumerate(chunks):
        s, ks = nxt
        if idx + 1 < len(chunks):
            nxt = scores(chunks[idx + 1])
        v = v_refs[g][r, pl.ds(ks, tk), :]
        m = jnp.max(s, axis=-1, keepdims=True)
        p = jnp.exp(s - m)
        l = jnp.sum(p, axis=-1, keepdims=True)
        o = jnp.dot(p.astype(BF16), v, preferred_element_type=F32) / l
        lse = jnp.broadcast_to(m + jnp.log(l), o.shape)
        rows = pl.ds(dilation * i_loc + r, tq, stride=dilation) if dilation > 1 else pl.ds(i_loc, tq)
        out_scr[g, rows, :] = o
        lse_scr[g, rows, :] = lse

    for c in range(tile // B_QUERY_CHUNK):
        rows = pl.ds(c * B_QUERY_CHUNK, B_QUERY_CHUNK)
        lses = [lse_scr[g, rows, :] for g in range(n_groups)]
        lse_max = functools.reduce(jnp.maximum, lses)
        weights = [jnp.exp(x - lse_max) for x in lses]
        denom = functools.reduce(lambda a, b: a + b, weights)
        for g in range(n_groups):
            o_ref[g, rows, :] = (out_scr[g, rows, :] * (weights[g] / denom)).astype(o_ref.dtype)


def _attn_b(qkvs, batch, seq, tile=B_TOKEN_TILE):
    n_tiles = seq // tile
    n_groups = len(B_GROUPS)

    def q_spec(g):
        d = B_GROUPS[g][1]
        return pl.BlockSpec((None, d, tile // d, HEAD_DIM), lambda b, h, i: (b, 0, i, h))

    def kv_spec(g, part):
        d = B_GROUPS[g][1]
        return pl.BlockSpec((None, d, seq // d, HEAD_DIM),
                            lambda b, h, i: (b, 0, 0, part * B_HEADS_PER_GROUP + h))

    in_specs = ([q_spec(g) for g in range(n_groups)] + [kv_spec(g, 1) for g in range(n_groups)]
                + [kv_spec(g, 2) for g in range(n_groups)])
    return pl.pallas_call(
        functools.partial(_attn_b_kernel, tile=tile),
        grid=(batch, B_HEADS_PER_GROUP, n_tiles),
        in_specs=in_specs,
        out_specs=pl.BlockSpec((n_groups, tile, HEAD_DIM), lambda b, h, i: (0, b * n_tiles + i, h)),
        out_shape=jax.ShapeDtypeStruct((n_groups, batch * seq, B_GROUP_WIDTH), BF16),
        scratch_shapes=[pltpu.VMEM((n_groups, tile, HEAD_DIM), F32),
                        pltpu.VMEM((n_groups, tile, HEAD_DIM), F32)],
        compiler_params=_params("parallel", "parallel", "arbitrary"),
        name="dilated_attention",
    )(*(list(qkvs) * 3))


def _shift_rows(x, k, row):
    return jnp.where(row >= k, pltpu.roll(x, k, 0), 0.0)


def _pool_kernel(u_lo_ref, u_hi_ref, w_ref, scale_ref, o_ref):
    g = pl.program_id(1)
    u = jnp.concatenate([u_lo_ref[...], u_hi_ref[...]], axis=1)
    row = lax.broadcasted_iota(jnp.int32, u.shape, 0)
    window = jnp.left_shift(2, g)
    total = u
    step = 1
    while step < max(C_WINDOWS):
        widened = total + _shift_rows(total, step, row)
        total = jnp.where(step < window, widened, total)
        step *= 2
    count = jnp.minimum(row + 1, window).astype(F32)
    pooled = total / count - u
    y = jnp.dot(pooled.astype(BF16), w_ref[0], preferred_element_type=F32)
    o_ref[...] = (y * scale_ref[...]).astype(o_ref.dtype)


def _pool(proj, col0, pool_w, pool_scale, batch, seq):
    n_groups = len(C_WINDOWS)
    tiles_per_group = C_GROUP_DIM // LANES
    assert tiles_per_group == 2 and col0 % LANES == 0
    first = col0 // LANES

    def u_spec(half):
        return pl.BlockSpec((seq, LANES), lambda b, g: (b, first + tiles_per_group * g + half))

    return pl.pallas_call(
        _pool_kernel,
        grid=(batch, n_groups),
        in_specs=[u_spec(0), u_spec(1),
                  pl.BlockSpec((1, C_GROUP_DIM, C_GROUP_DIM), lambda b, g: (g, 0, 0)),
                  pl.BlockSpec((1, C_GROUP_DIM), lambda b, g: (0, g))],
        out_specs=pl.BlockSpec((seq, C_GROUP_DIM), lambda b, g: (b, g)),
        out_shape=jax.ShapeDtypeStruct((batch * seq, C_WIDTH), BF16),
        compiler_params=_params("parallel", "parallel"),
        name="multiscale_pool",
    )(proj, proj, pool_w, pool_scale.reshape(1, C_WIDTH))


def _conv_kernel(gb_ref, gc_ref, h_ref, w_ref, o_ref):
    z = gc_ref[...] * h_ref[...]
    row = lax.broadcasted_iota(jnp.int32, z.shape, 0)
    w = w_ref[...]
    y = w[CONV_WIDTH - 1:CONV_WIDTH] * z
    for tap in range(1, CONV_WIDTH):
        y = y + w[CONV_WIDTH - 1 - tap:CONV_WIDTH - tap] * _shift_rows(z, tap, row)
    o_ref[...] = (gb_ref[...] * y).astype(o_ref.dtype)


def _short_conv(proj, col0, conv_w, batch, seq):
    n_col = D_WIDTH // LANES
    assert col0 % LANES == 0
    first = col0 // LANES

    def spec(part):
        return pl.BlockSpec((seq, LANES), lambda b, c: (b, first + part * n_col + c))

    return pl.pallas_call(
        _conv_kernel,
        grid=(batch, n_col),
        in_specs=[spec(0), spec(1), spec(2), pl.BlockSpec((CONV_WIDTH, LANES), lambda b, c: (0, c))],
        out_specs=pl.BlockSpec((seq, LANES), lambda b, c: (b, c)),
        out_shape=jax.ShapeDtypeStruct((batch * seq, D_WIDTH), BF16),
        compiler_params=_params("parallel", "parallel"),
        name="short_conv",
    )(proj, proj, proj, conv_w)


def _rope_tables(seq, dim):
    half = dim // 2
    inv = ROPE_THETA ** (-jnp.arange(0, dim, 2, dtype=F32) / dim)
    ang = jnp.arange(seq, dtype=F32)[:, None] * inv[None, :]
    cos, sin = jnp.cos(ang), jnp.sin(ang)
    reps = LANES // dim
    cos_t = jnp.tile(jnp.concatenate([cos, cos], axis=1), (1, reps))
    zeros = jnp.zeros_like(sin)
    if dim == LANES:
        return (cos_t, jnp.concatenate([-sin, sin], axis=1)), (half,)
    upper = jnp.tile(jnp.concatenate([zeros, sin], axis=1), (1, reps))
    lower = jnp.tile(jnp.concatenate([-sin, zeros], axis=1), (1, reps))
    return (cos_t, upper, lower), (half, LANES - half)


def kernel(x, w_in, w_out, norm_mix, norm_mlp, diff_lambda, diff_subln, pool_w, pool_scale, conv_w,
           w_up, w_down, norm_final):
    batch, seq, d_model = x.shape
    depth = w_in.shape[0]
    m = batch * seq
    x = x.reshape(m, d_model)

    tables_a, shifts_a = _rope_tables(seq, A_SUB)
    tables_b, shifts_b = _rope_tables(seq, HEAD_DIM)

    a_qk_end = 2 * A_WIDTH
    a_end = 3 * A_WIDTH
    b_end = a_end + 3 * B_WIDTH
    c_end = b_end + C_WIDTH
    t_a = 512
    tail0 = (b_end // IN_PROJ_COL_TILE) * IN_PROJ_COL_TILE
    in_width = w_in.shape[2]

    w_in_l = _cast_layer(w_in, 0)[None]

    h, ss = _prep(x, norm_mix[0])
    for l in range(depth):
        lam_init = 0.8 - 0.6 * math.exp(-0.3 * l)
        w_a_vt = jnp.swapaxes(w_in_l[:, :, a_qk_end:a_end], 1, 2)

        qk_a = _mm_rope(h, ss, w_in_l, 0, a_qk_end, tables_a, shifts_a, A_SCORE_SCALE, A_WIDTH, seq)
        vt_a = _mm_transposed(h, ss, w_a_vt, 0, t_a)
        qkv_b = [_mm_rope_permute(h, ss, w_in_l, 0, a_end, g, tables_b, shifts_b[0], B_GROUPS[g][1], batch, seq)
                 for g in range(len(B_GROUPS))]
        tail = _mm_cols(h, ss, w_in_l, 0, tail0, in_width - tail0, F32, name="in_proj_tail")

        out_a = _attn_a(qk_a, vt_a, diff_lambda[l], diff_subln[l], lam_init, batch, seq, t_a)
        out_b = _attn_b(qkv_b, batch, seq)
        out_c = _pool(tail, b_end - tail0, pool_w[l].astype(BF16), pool_scale[l], batch, seq)
        out_d = _short_conv(tail, c_end - tail0, conv_w[l], batch, seq)

        x, h, ss = _out_proj(out_a, out_b, out_c, out_d, w_out, l, x, norm_mlp[l])
        act, w_down_l = _mlp_up(h, ss, w_up, w_down, l, 512)
        if l + 1 < depth:
            x, h, ss, w_next = _mm_residual(act, w_down_l, x, norm_mix[l + 1], w_in, l + 1)
            w_in_l = w_next[None]
        else:
            x, = _mm_residual(act, w_down_l, x)

    out = _rmsnorm(x, norm_final, F32)
    return out.reshape(batch, seq, d_model)
```

```python
import functools
import math

import jax
import jax.numpy as jnp
from jax import lax
from jax.experimental import pallas as pl
from jax.experimental.pallas import tpu as pltpu

D_MODEL = 4096
HEAD_DIM = 128
A_HEADS = 8
A_WIDTH = A_HEADS * HEAD_DIM
A_SUB = HEAD_DIM // 2
A_SCORE_SCALE = A_SUB ** -0.5 * math.log2(math.e)
A_ONES_ROWS = 16
B_GROUPS = ((128, 1), (512, 4), (2048, 16))
B_HEADS_PER_GROUP = 3
B_GROUP_WIDTH = B_HEADS_PER_GROUP * HEAD_DIM
B_WIDTH = B_GROUP_WIDTH * len(B_GROUPS)
B_TOKEN_TILE = 2048
B_QUERY_CHUNK = 256
C_WINDOWS = (2, 4, 8, 16)
C_WIDTH = 1024
C_GROUP_DIM = C_WIDTH // len(C_WINDOWS)
D_WIDTH = D_MODEL - A_WIDTH - B_WIDTH - C_WIDTH
CONV_WIDTH = 3
D_FF = 4 * D_MODEL
ROPE_THETA = 10000.0
NORM_EPS = 1e-6
DIFF_EPS = 1e-5

LANES = 128
MXU_COLS = 256
MLP_ROW_TILE = 2048
IN_PROJ_COL_TILE = 1024
IN_PROJ_ROW_TILE = 1024
VMEM_LIMIT_BYTES = 56 * 1024 * 1024
MLP_DOWN_VMEM_LIMIT_BYTES = 58 * 1024 * 1024
OUT_PROJ_VMEM_LIMIT_BYTES = 58 * 1024 * 1024
MASKED_SCORE = -1e30

F32 = jnp.float32
BF16 = jnp.bfloat16


def _params(*semantics):
    return pltpu.CompilerParams(dimension_semantics=semantics, vmem_limit_bytes=VMEM_LIMIT_BYTES)


def _rmsnorm_kernel(x_ref, g_ref, o_ref, *, eps):
    x = x_ref[...]
    ms = jnp.mean(x * x, axis=-1, keepdims=True)
    o_ref[...] = (x * lax.rsqrt(ms + eps) * g_ref[...]).astype(o_ref.dtype)


def _rmsnorm(x, g, out_dtype, tm=256):
    m, d = x.shape
    return pl.pallas_call(
        functools.partial(_rmsnorm_kernel, eps=NORM_EPS),
        grid=(m // tm,),
        in_specs=[pl.BlockSpec((tm, d), lambda i: (i, 0)),
                  pl.BlockSpec((1, d), lambda i: (0, 0))],
        out_specs=pl.BlockSpec((tm, d), lambda i: (i, 0)),
        out_shape=jax.ShapeDtypeStruct((m, d), out_dtype),
        compiler_params=_params("parallel"),
        name="rmsnorm",
    )(x, g.reshape(1, d))


def _prep_kernel(x_ref, g_ref, xb_ref, ss_ref):
    x = x_ref[...]
    xb_ref[...] = (x * g_ref[...]).astype(xb_ref.dtype)
    ss_ref[...] = jnp.broadcast_to(jnp.sum(x * x, axis=-1, keepdims=True), ss_ref.shape)


def _prep(x, g, tm=256):
    m, d = x.shape
    return pl.pallas_call(
        _prep_kernel,
        grid=(m // tm,),
        in_specs=[pl.BlockSpec((tm, d), lambda i: (i, 0)),
                  pl.BlockSpec((1, d), lambda i: (0, 0))],
        out_specs=[pl.BlockSpec((tm, d), lambda i: (i, 0)),
                   pl.BlockSpec((tm, LANES), lambda i: (i, 0))],
        out_shape=[jax.ShapeDtypeStruct((m, d), BF16), jax.ShapeDtypeStruct((m, LANES), F32)],
        compiler_params=_params("parallel"),
        name="norm_prep",
    )(x, g.reshape(1, d))


def _cast_kernel(w_ref, o_ref):
    o_ref[...] = w_ref[...].astype(o_ref.dtype)


def _cast_layer(w, layer, rows=256):
    _, k, n = w.shape
    return pl.pallas_call(
        _cast_kernel,
        grid=(k // rows,),
        in_specs=[pl.BlockSpec((None, rows, n), lambda i: (layer, i, 0))],
        out_specs=pl.BlockSpec((rows, n), lambda i: (i, 0)),
        out_shape=jax.ShapeDtypeStruct((k, n), BF16),
        compiler_params=_params("parallel"),
        name="cast_weights",
    )(w)


def _row_scale(ss_ref):
    return lax.rsqrt(ss_ref[...] * (1.0 / D_MODEL) + NORM_EPS)


def _scale_rows(y, rs):
    return y * jnp.concatenate([rs] * (y.shape[1] // LANES), axis=1)


def _store_norm_inputs(x_new, g_ref, xb_ref, ss_ref, first):
    xb_ref[...] = (x_new * g_ref[...]).astype(xb_ref.dtype)
    part = jnp.broadcast_to(jnp.sum(x_new * x_new, axis=-1, keepdims=True), ss_ref.shape)

    @pl.when(first)
    def _():
        ss_ref[...] = part

    @pl.when(jnp.logical_not(first))
    def _():
        ss_ref[...] += part


def _rotate(y, cos_ref, sin_refs, shifts):
    outs = []
    for c in range(y.shape[1] // LANES):
        yc = y[:, c * LANES:(c + 1) * LANES]
        oc = yc * cos_ref[...]
        for s_ref, shift in zip(sin_refs, shifts):
            oc = oc + pltpu.roll(yc, shift, 1) * s_ref[...]
        outs.append(oc)
    return jnp.concatenate(outs, axis=1)


def _mm_rope_kernel(a_ref, ss_ref, w_ref, cos_ref, *rest, shifts, q_scale, n_q_blocks):
    sin_refs, o_ref = rest[:-1], rest[-1]
    scale = jnp.where(pl.program_id(1) < n_q_blocks, q_scale, 1.0).astype(F32)
    a = a_ref[...]
    rs = _row_scale(ss_ref)
    for c in range(o_ref.shape[1] // MXU_COLS):
        cols = slice(c * MXU_COLS, (c + 1) * MXU_COLS)
        acc = _scale_rows(jnp.dot(a, w_ref[:, cols].astype(BF16), preferred_element_type=F32), rs)
        o_ref[:, cols] = (_rotate(acc, cos_ref, sin_refs, shifts) * scale).astype(o_ref.dtype)


def _mm_rope(a, ss, w, layer, n, tables, shifts, q_scale, n_q_cols, seq, bn=IN_PROJ_COL_TILE,
             bm=IN_PROJ_ROW_TILE):
    m, k = a.shape
    tab_blocks = seq // bm
    tab_spec = pl.BlockSpec((bm, LANES), lambda i, j: (i % tab_blocks, 0))
    return pl.pallas_call(
        functools.partial(_mm_rope_kernel, shifts=shifts, q_scale=q_scale, n_q_blocks=n_q_cols // bn),
        grid=(m // bm, n // bn),
        in_specs=[pl.BlockSpec((bm, k), lambda i, j: (i, 0)),
                  pl.BlockSpec((bm, LANES), lambda i, j: (i, 0)),
                  pl.BlockSpec((None, k, bn), lambda i, j: (layer, 0, j))] + [tab_spec] * len(tables),
        out_specs=pl.BlockSpec((bm, bn), lambda i, j: (i, j)),
        out_shape=jax.ShapeDtypeStruct((m, n), BF16),
        compiler_params=_params("parallel", "arbitrary"),
        name="in_proj_rope",
    )(a, ss, w, *tables)


def _mm_rope_permute_kernel(a_ref, ss_ref, wq_ref, wk_ref, wv_ref, cos_ref, sin_ref, o_ref, scr_ref, *,
                            shift, n_rope_cols, dilation):
    a = a_ref[...]
    w = jnp.concatenate([wq_ref[...], wk_ref[...], wv_ref[...]], axis=1)
    rs = _row_scale(ss_ref)
    n = o_ref.shape[2]
    rows = a.shape[0] // dilation
    for lo in range(0, n, 2 * MXU_COLS):
        hi = min(lo + 2 * MXU_COLS, n)
        acc = _scale_rows(jnp.dot(a, w[:, lo:hi], preferred_element_type=F32), rs)
        rope_hi = min(hi, n_rope_cols)
        if lo < rope_hi:
            rot = _rotate(acc[:, :rope_hi - lo], cos_ref, (sin_ref,), (shift,))
            acc = rot if rope_hi == hi else jnp.concatenate([rot, acc[:, rope_hi - lo:]], axis=1)
        if dilation == 1:
            o_ref[0, :, lo:hi] = acc.astype(o_ref.dtype)
        else:
            for c in range(lo // LANES, hi // LANES):
                cols = slice(c * LANES, (c + 1) * LANES)
                scr_ref[c] = acc[:, c * LANES - lo:(c + 1) * LANES - lo]
                for r in range(dilation):
                    o_ref[r, :, cols] = scr_ref[c, pl.ds(r, rows, stride=dilation), :].astype(o_ref.dtype)


def _mm_rope_permute(a, ss, w, layer, col0, group, tables, shift, dilation, batch, seq, bm=IN_PROJ_ROW_TILE):
    m, k = a.shape
    n = 3 * B_GROUP_WIDTH
    n_rope_cols = 2 * B_GROUP_WIDTH
    tiles = seq // bm
    tab_spec = pl.BlockSpec((bm, LANES), lambda i: (i % tiles, 0))
    n_groups = len(B_GROUPS)
    first = col0 // B_GROUP_WIDTH
    assert first * B_GROUP_WIDTH == col0

    def w_spec(part):
        return pl.BlockSpec((None, k, B_GROUP_WIDTH), lambda i: (layer, 0, first + part * n_groups + group),
                            pipeline_mode=pl.Buffered(1))

    return pl.pallas_call(
        functools.partial(_mm_rope_permute_kernel, shift=shift, n_rope_cols=n_rope_cols, dilation=dilation),
        grid=(m // bm,),
        in_specs=[pl.BlockSpec((bm, k), lambda i: (i, 0)), pl.BlockSpec((bm, LANES), lambda i: (i, 0)),
                  w_spec(0), w_spec(1), w_spec(2), tab_spec, tab_spec],
        out_specs=pl.BlockSpec((None, dilation, bm // dilation, n), lambda i: (i // tiles, 0, i % tiles, 0)),
        out_shape=jax.ShapeDtypeStruct((batch, dilation, seq // dilation, n), BF16),
        scratch_shapes=[pltpu.VMEM((n // LANES, bm, LANES), F32)],
        compiler_params=_params("parallel"),
        name="in_proj_dilated",
    )(a, ss, w, w, w, *tables)


def _mm_kernel(a_ref, ss_ref, w_ref, o_ref, *, relu_sq):
    acc = jnp.dot(a_ref[...], w_ref[...].astype(BF16), preferred_element_type=F32)
    acc = _scale_rows(acc, _row_scale(ss_ref))
    if relu_sq:
        acc = jnp.square(jnp.maximum(acc, 0.0))
    o_ref[...] = acc.astype(o_ref.dtype)


def _mm_cols_kernel(a_ref, ss_ref, w_ref, side_ref, o_ref, side_bf16_ref):
    _mm_kernel(a_ref, ss_ref, w_ref, o_ref, relu_sq=False)
    side_bf16_ref[...] = side_ref[...].astype(BF16)


def _mm_cols(a, ss, w, layer, col0, n, out_dtype, w_side, side_layer, bn=IN_PROJ_COL_TILE, bm=IN_PROJ_ROW_TILE,
             name="matmul"):
    m, k = a.shape
    first = col0 // bn
    assert first * bn == col0 and n % bn == 0
    n_j = n // bn
    steps = (m // bm) * n_j
    rows_side, cols_side = w_side.shape[1:]
    slab = rows_side // steps
    assert slab * steps == rows_side
    return pl.pallas_call(
        _mm_cols_kernel,
        grid=(m // bm, n_j),
        in_specs=[pl.BlockSpec((bm, k), lambda i, j: (i, 0)),
                  pl.BlockSpec((bm, LANES), lambda i, j: (i, 0)),
                  pl.BlockSpec((None, k, bn), lambda i, j: (layer, 0, first + j)),
                  pl.BlockSpec((None, slab, cols_side), lambda i, j: (side_layer, i * n_j + j, 0))],
        out_specs=[pl.BlockSpec((bm, bn), lambda i, j: (i, j)),
                   pl.BlockSpec((slab, cols_side), lambda i, j: (i * n_j + j, 0))],
        out_shape=[jax.ShapeDtypeStruct((m, n), out_dtype), jax.ShapeDtypeStruct((rows_side, cols_side), BF16)],
        compiler_params=_params("parallel", "arbitrary"),
        name=name,
    )(a, ss, w, w_side)


def _mlp_up_kernel(a_ref, ss_ref, w_ref, wd_ref, o_ref, wd_bf16_ref):
    _mm_kernel(a_ref, ss_ref, w_ref, o_ref, relu_sq=True)
    wd_bf16_ref[...] = wd_ref[...].astype(BF16)


def _mlp_up(a, ss, w, w_down, layer, bn, bm=MLP_ROW_TILE):
    m, k = a.shape
    n = w.shape[2]
    n_j = n // bn
    steps = (m // bm) * n_j
    slab = n // steps
    assert slab * steps == n
    return pl.pallas_call(
        _mlp_up_kernel,
        grid=(m // bm, n_j),
        in_specs=[pl.BlockSpec((bm, k), lambda i, j: (i, 0), pipeline_mode=pl.Buffered(1)),
                  pl.BlockSpec((bm, LANES), lambda i, j: (i, 0)),
                  pl.BlockSpec((None, k, bn), lambda i, j: (layer, 0, j)),
                  pl.BlockSpec((None, slab, k), lambda i, j: (layer, i * n_j + j, 0))],
        out_specs=[pl.BlockSpec((None, None, bm, bn), lambda i, j: (i, j, 0, 0)),
                   pl.BlockSpec((slab, k), lambda i, j: (i * n_j + j, 0))],
        out_shape=[jax.ShapeDtypeStruct((m // bm, n_j, bm, bn), BF16), jax.ShapeDtypeStruct((n, k), BF16)],
        compiler_params=_params("parallel", "arbitrary"),
        name="mlp_up",
    )(a, ss, w, w_down)


def _mm_transposed_kernel(wt_ref, a_ref, ss_ref, o_ref, *, t):
    acc = lax.dot_general(wt_ref[...], a_ref[...], (((1,), (1,)), ((), ())), preferred_element_type=F32)
    acc = acc * jnp.transpose(_row_scale(ss_ref))[0:1, :]
    for c in range(acc.shape[1] // t):
        o_ref[c] = acc[:, c * t:(c + 1) * t].astype(o_ref.dtype)


def _mm_transposed(a, ss, wt, layer, t, bm=1024):
    m, k = a.shape
    n = wt.shape[1]
    return pl.pallas_call(
        functools.partial(_mm_transposed_kernel, t=t),
        grid=(m // bm,),
        in_specs=[pl.BlockSpec((None, n, k), lambda i: (layer, 0, 0), pipeline_mode=pl.Buffered(1)),
                  pl.BlockSpec((bm, k), lambda i: (i, 0)),
                  pl.BlockSpec((bm, LANES), lambda i: (i, 0))],
        out_specs=pl.BlockSpec((bm // t, n, t), lambda i: (i, 0, 0)),
        out_shape=jax.ShapeDtypeStruct((m // t, n, t), BF16),
        compiler_params=_params("parallel"),
        name="in_proj_vt",
    )(wt, a, ss)


def _mm_residual_kernel(a_ref, w_ref, r_ref, *rest):
    o_ref = rest[0] if len(rest) == 1 else rest[2]
    kk = pl.program_id(2)

    @pl.when(kk == 0)
    def _():
        o_ref[...] = r_ref[...]

    a = jnp.concatenate([a_ref[c] for c in range(a_ref.shape[0])], axis=1)
    o_ref[...] += jnp.dot(a, w_ref[...], preferred_element_type=F32)

    if len(rest) > 1:
        g_ref, w_next_ref, _, xb_ref, ss_ref, w_next_bf16_ref = rest
        w_next_bf16_ref[...] = w_next_ref[...].astype(BF16)

        @pl.when(kk == pl.num_programs(2) - 1)
        def _():
            _store_norm_inputs(o_ref[...], g_ref, xb_ref, ss_ref, pl.program_id(1) == 0)


def _mm_residual(a, w, r, g=None, w_next=None, next_layer=None, bm=1024, bn=1024, bk=4096):
    row_tiles, col_tiles, tile_rows, tile_cols = a.shape
    m, k = row_tiles * tile_rows, col_tiles * tile_cols
    n = w.shape[1]
    sub = tile_rows // bm
    assert sub * bm == tile_rows
    tile_spec = pl.BlockSpec((bm, bn), lambda i, j, kk: (i, j))
    in_specs = [pl.BlockSpec((None, bk // tile_cols, bm, tile_cols), lambda i, j, kk: (i // sub, kk, i % sub, 0)),
                pl.BlockSpec((bk, bn), lambda i, j, kk: (kk, j)),
                tile_spec]
    out_specs = [tile_spec]
    out_shape = [jax.ShapeDtypeStruct((m, n), F32)]
    operands = [a, w, r]
    if g is not None:
        n_j, n_k = n // bn, k // bk
        steps = (m // bm) * n_j * n_k
        rows_next, cols_next = w_next.shape[1:]
        slab = rows_next // steps
        assert slab * steps == rows_next

        def slab_index(i, j, kk):
            return (i * n_j + j) * n_k + kk

        in_specs += [pl.BlockSpec((1, bn), lambda i, j, kk: (0, j)),
                     pl.BlockSpec((None, slab, cols_next), lambda i, j, kk: (next_layer, slab_index(i, j, kk), 0))]
        out_specs += [tile_spec, pl.BlockSpec((bm, LANES), lambda i, j, kk: (i, 0)),
                      pl.BlockSpec((slab, cols_next), lambda i, j, kk: (slab_index(i, j, kk), 0))]
        out_shape += [jax.ShapeDtypeStruct((m, n), BF16), jax.ShapeDtypeStruct((m, LANES), F32),
                      jax.ShapeDtypeStruct((rows_next, cols_next), BF16)]
        operands += [g.reshape(1, n), w_next]
    return pl.pallas_call(
        _mm_residual_kernel,
        grid=(m // bm, n // bn, k // bk),
        in_specs=in_specs,
        out_specs=out_specs,
        out_shape=out_shape,
        compiler_params=pltpu.CompilerParams(dimension_semantics=("parallel", "arbitrary", "arbitrary"),
                                             vmem_limit_bytes=MLP_DOWN_VMEM_LIMIT_BYTES),
        name="mlp_down",
    )(*operands)


def _out_proj_kernel(a_ref, b_ref, c_ref, d_ref, w_ref, r_ref, g_ref, o_ref, xb_ref, ss_ref):
    mix = jnp.concatenate([a_ref[...]] + [b_ref[g] for g in range(len(B_GROUPS))] + [c_ref[...], d_ref[...]], axis=1)
    x_new = r_ref[...] + jnp.dot(mix, w_ref[...], preferred_element_type=F32)
    o_ref[...] = x_new
    _store_norm_inputs(x_new, g_ref, xb_ref, ss_ref, pl.program_id(1) == 0)


def _out_proj(out_a, out_b, out_c, out_d, w, r, g, bm=1024, bn=1024):
    m = out_a.shape[0]
    n = w.shape[1]
    n_groups = len(B_GROUPS)
    return pl.pallas_call(
        _out_proj_kernel,
        grid=(m // bm, n // bn),
        in_specs=[pl.BlockSpec((bm, A_WIDTH), lambda i, j: (i, 0)),
                  pl.BlockSpec((n_groups, bm, B_GROUP_WIDTH), lambda i, j: (0, i, 0)),
                  pl.BlockSpec((bm, C_WIDTH), lambda i, j: (i, 0)),
                  pl.BlockSpec((bm, D_WIDTH), lambda i, j: (i, 0)),
                  pl.BlockSpec((D_MODEL, bn), lambda i, j: (0, j)),
                  pl.BlockSpec((bm, bn), lambda i, j: (i, j)),
                  pl.BlockSpec((1, bn), lambda i, j: (0, j))],
        out_specs=[pl.BlockSpec((bm, bn), lambda i, j: (i, j)),
                   pl.BlockSpec((bm, bn), lambda i, j: (i, j)),
                   pl.BlockSpec((bm, LANES), lambda i, j: (i, 0))],
        out_shape=[jax.ShapeDtypeStruct((m, n), F32), jax.ShapeDtypeStruct((m, n), BF16),
                   jax.ShapeDtypeStruct((m, LANES), F32)],
        compiler_params=pltpu.CompilerParams(dimension_semantics=("parallel", "arbitrary"),
                                             vmem_limit_bytes=OUT_PROJ_VMEM_LIMIT_BYTES),
        name="out_proj",
    )(out_a, out_b, out_c, out_d, w, r, g.reshape(1, n))


def _attn_a_kernel(q_ref, k_ref, vt_ref, lam_ref, g_ref, o_ref, *scratch, t, heads, lam_init):
    qi = pl.program_id(2)
    lane = lax.broadcasted_iota(jnp.int32, (t, HEAD_DIM), 1)
    qqs = []
    for h in range(heads):
        q = q_ref[:, h * HEAD_DIM:(h + 1) * HEAD_DIM]
        zero = jnp.zeros_like(q)
        qqs.append(jnp.concatenate([jnp.where(lane < A_SUB, q, zero), jnp.where(lane >= A_SUB, q, zero)], axis=0))

    ones = jnp.ones((A_ONES_ROWS, t), BF16)
    m_scr, acc_scr = scratch[:heads], scratch[heads:]
    for h in range(heads):
        m_scr[h][...] = jnp.full(m_scr[h].shape, MASKED_SCORE, F32)
        acc_scr[h][...] = jnp.zeros(acc_scr[h].shape, F32)

    def step(j, masked):
        def scores(h):
            k = k_ref[pl.ds(pl.multiple_of(j * t, t), t), h * HEAD_DIM:(h + 1) * HEAD_DIM]
            s = lax.dot_general(k, qqs[h], (((1,), (1,)), ((), ())), preferred_element_type=F32)
            if masked:
                key = lax.broadcasted_iota(jnp.int32, s.shape, 0)
                lane_q = lax.broadcasted_iota(jnp.int32, s.shape, 1)
                qry = jnp.where(lane_q >= t, lane_q - t, lane_q)
                s = jnp.where(key <= qry, s, MASKED_SCORE)
            return s

        s_next = scores(0)
        for h in range(heads):
            s = s_next
            if h + 1 < heads:
                s_next = scores(h + 1)
            vt = jnp.concatenate([vt_ref[j, h * HEAD_DIM:(h + 1) * HEAD_DIM, :], ones], axis=0)
            m = m_scr[h][...]
            m_new = jnp.maximum(m, jnp.max(s, axis=0, keepdims=True))
            alpha = jnp.exp2(m - m_new)
            p = jnp.exp2(s - m_new)
            m_scr[h][...] = m_new
            acc_scr[h][...] = alpha * acc_scr[h][...] + jnp.dot(vt, p.astype(BF16), preferred_element_type=F32)

    def body(j, carry):
        step(j, masked=False)
        return carry

    lax.fori_loop(0, qi, body, 0)
    step(qi, masked=True)

    lp = lam_ref[...]
    lam = (jnp.exp(jnp.sum(lp[0:1] * lp[1:2], axis=-1, keepdims=True))
           - jnp.exp(jnp.sum(lp[2:3] * lp[3:4], axis=-1, keepdims=True)) + lam_init)
    for h in range(heads):
        acc = acc_scr[h][...]
        o = acc[:HEAD_DIM] / acc[HEAD_DIM:HEAD_DIM + 1]
        o = (o[:, :t] - lam * o[:, t:]).T
        ms = jnp.mean(o * o, axis=-1, keepdims=True)
        o = o * lax.rsqrt(ms + DIFF_EPS) * g_ref[...]
        o_ref[:, h * HEAD_DIM:(h + 1) * HEAD_DIM] = (o * (1.0 - lam_init)).astype(o_ref.dtype)


def _attn_a(qk, vt, lam_params, subln_g, lam_init, batch, seq, t, heads=8):
    nq = seq // t
    width = heads * HEAD_DIM
    n_pairs = A_HEADS // heads
    return pl.pallas_call(
        functools.partial(_attn_a_kernel, t=t, heads=heads, lam_init=lam_init),
        grid=(batch, n_pairs, nq),
        in_specs=[pl.BlockSpec((t, width), lambda b, h, i: (b * nq + i, h)),
                  pl.BlockSpec((seq, width), lambda b, h, i: (b, n_pairs + h)),
                  pl.BlockSpec((nq, width, t), lambda b, h, i: (b, h, 0)),
                  pl.BlockSpec((4, A_SUB), lambda b, h, i: (0, 0)),
                  pl.BlockSpec((1, HEAD_DIM), lambda b, h, i: (0, 0))],
        out_specs=pl.BlockSpec((t, width), lambda b, h, i: (b * nq + i, h)),
        out_shape=jax.ShapeDtypeStruct((batch * seq, A_WIDTH), BF16),
        scratch_shapes=([pltpu.VMEM((1, 2 * t), F32)] * heads
                        + [pltpu.VMEM((HEAD_DIM + A_ONES_ROWS, 2 * t), F32)] * heads),
        compiler_params=_params("parallel", "parallel", "arbitrary"),
        name="diff_attention",
    )(qk, qk, vt, lam_params, subln_g.reshape(1, HEAD_DIM))


def _attn_b_kernel(*refs, tile):
    n_groups = len(B_GROUPS)
    q_refs = refs[0:n_groups]
    k_refs = refs[n_groups:2 * n_groups]
    v_refs = refs[2 * n_groups:3 * n_groups]
    o_ref = refs[3 * n_groups]
    out_scr, lse_scr = refs[3 * n_groups + 1:]
    ti = pl.program_id(2)
    scale = HEAD_DIM ** -0.5
    base_delta = {}
    chunks = []

    for g, (window, dilation) in enumerate(B_GROUPS):
        back = window // dilation
        seg = tile // dilation
        tq = min(B_QUERY_CHUNK, seg)
        tk = tq + back
        if tq not in base_delta:
            base_delta[tq] = (lax.broadcasted_iota(jnp.int32, (tq, tk), 0)
                              - lax.broadcasted_iota(jnp.int32, (tq, tk), 1))
        for r in range(dilation):
            for c in range(seg // tq):
                chunks.append((g, dilation, back, tq, tk, r, c * tq, ti * seg + c * tq))

    def scores(chunk):
        g, dilation, back, tq, tk, r, i_loc, i0 = chunk
        ks = pl.multiple_of(jnp.maximum(i0 - back, 0), back)
        q = q_refs[g][r, i_loc:i_loc + tq, :]
        k = k_refs[g][r, pl.ds(ks, tk), :]
        s = lax.dot_general(q, k, (((1,), (1,)), ((), ())), preferred_element_type=F32) * scale
        delta = base_delta[tq] + (i0 - ks)
        return jnp.where((delta >= 0) & (delta <= back), s, MASKED_SCORE), ks

    nxt = scores(chunks[0])
    for idx, (g, dilation, back, tq, tk, r, i_loc, i0) in enumerate(chunks):
        s, ks = nxt
        if idx + 1 < len(chunks):
            nxt = scores(chunks[idx + 1])
        v = v_refs[g][r, pl.ds(ks, tk), :]
        m = jnp.max(s, axis=-1, keepdims=True)
        p = jnp.exp(s - m)
        l = jnp.sum(p, axis=-1, keepdims=True)
        o = jnp.dot(p.astype(BF16), v, preferred_element_type=F32) / l
        lse = jnp.broadcast_to(m + jnp.log(l), o.shape)
        rows = pl.ds(dilation * i_loc + r, tq, stride=dilation) if dilation > 1 else pl.ds(i_loc, tq)
        out_scr[g, rows, :] = o
        lse_scr[g, rows, :] = lse

    for c in range(tile // B_QUERY_CHUNK):
        rows = pl.ds(c * B_QUERY_CHUNK, B_QUERY_CHUNK)
        lses = [lse_scr[g, rows, :] for g in range(n_groups)]
        lse_max = functools.reduce(jnp.maximum, lses)
        weights = [jnp.exp(x - lse_max) for x in lses]
        denom = functools.reduce(lambda a, b: a + b, weights)
        for g in range(n_groups):
            o_ref[g, rows, :] = (out_scr[g, rows, :] * (weights[g] / denom)).astype(o_ref.dtype)


def _attn_b(qkvs, batch, seq, tile=B_TOKEN_TILE):
    n_tiles = seq // tile
    n_groups = len(B_GROUPS)

    def q_spec(g):
        d = B_GROUPS[g][1]
        return pl.BlockSpec((None, d, tile // d, HEAD_DIM), lambda b, h, i: (b, 0, i, h))

    def kv_spec(g, part):
        d = B_GROUPS[g][1]
        return pl.BlockSpec((None, d, seq // d, HEAD_DIM),
                            lambda b, h, i: (b, 0, 0, part * B_HEADS_PER_GROUP + h))

    in_specs = ([q_spec(g) for g in range(n_groups)] + [kv_spec(g, 1) for g in range(n_groups)]
                + [kv_spec(g, 2) for g in range(n_groups)])
    return pl.pallas_call(
        functools.partial(_attn_b_kernel, tile=tile),
        grid=(batch, B_HEADS_PER_GROUP, n_tiles),
        in_specs=in_specs,
        out_specs=pl.BlockSpec((n_groups, tile, HEAD_DIM), lambda b, h, i: (0, b * n_tiles + i, h)),
        out_shape=jax.ShapeDtypeStruct((n_groups, batch * seq, B_GROUP_WIDTH), BF16),
        scratch_shapes=[pltpu.VMEM((n_groups, tile, HEAD_DIM), F32),
                        pltpu.VMEM((n_groups, tile, HEAD_DIM), F32)],
        compiler_params=_params("parallel", "parallel", "arbitrary"),
        name="dilated_attention",
    )(*(list(qkvs) * 3))


def _shift_rows(x, k, row):
    return jnp.where(row >= k, pltpu.roll(x, k, 0), 0.0)


def _pool_kernel(u_lo_ref, u_hi_ref, w_ref, scale_ref, o_ref):
    g = pl.program_id(1)
    u = jnp.concatenate([u_lo_ref[...], u_hi_ref[...]], axis=1)
    row = lax.broadcasted_iota(jnp.int32, u.shape, 0)
    window = jnp.left_shift(2, g)
    total = u
    step = 1
    while step < max(C_WINDOWS):
        widened = total + _shift_rows(total, step, row)
        total = jnp.where(step < window, widened, total)
        step *= 2
    count = jnp.minimum(row + 1, window).astype(F32)
    pooled = total / count - u
    y = jnp.dot(pooled.astype(BF16), w_ref[0], preferred_element_type=F32)
    o_ref[...] = (y * scale_ref[...]).astype(o_ref.dtype)


def _pool(proj, col0, pool_w, pool_scale, batch, seq):
    n_groups = len(C_WINDOWS)
    tiles_per_group = C_GROUP_DIM // LANES
    assert tiles_per_group == 2 and col0 % LANES == 0
    first = col0 // LANES

    def u_spec(half):
        return pl.BlockSpec((seq, LANES), lambda b, g: (b, first + tiles_per_group * g + half))

    return pl.pallas_call(
        _pool_kernel,
        grid=(batch, n_groups),
        in_specs=[u_spec(0), u_spec(1),
                  pl.BlockSpec((1, C_GROUP_DIM, C_GROUP_DIM), lambda b, g: (g, 0, 0)),
                  pl.BlockSpec((1, C_GROUP_DIM), lambda b, g: (0, g))],
        out_specs=pl.BlockSpec((seq, C_GROUP_DIM), lambda b, g: (b, g)),
        out_shape=jax.ShapeDtypeStruct((batch * seq, C_WIDTH), BF16),
        compiler_params=_params("parallel", "parallel"),
        name="multiscale_pool",
    )(proj, proj, pool_w, pool_scale.reshape(1, C_WIDTH))


def _conv_kernel(gb_ref, gc_ref, h_ref, w_ref, o_ref):
    z = gc_ref[...] * h_ref[...]
    row = lax.broadcasted_iota(jnp.int32, z.shape, 0)
    w = w_ref[...]
    y = w[CONV_WIDTH - 1:CONV_WIDTH] * z
    for tap in range(1, CONV_WIDTH):
        y = y + w[CONV_WIDTH - 1 - tap:CONV_WIDTH - tap] * _shift_rows(z, tap, row)
    o_ref[...] = (gb_ref[...] * y).astype(o_ref.dtype)


def _short_conv(proj, col0, conv_w, batch, seq):
    n_col = D_WIDTH // LANES
    assert col0 % LANES == 0
    first = col0 // LANES

    def spec(part):
        return pl.BlockSpec((seq, LANES), lambda b, c: (b, first + part * n_col + c))

    return pl.pallas_call(
        _conv_kernel,
        grid=(batch, n_col),
        in_specs=[spec(0), spec(1), spec(2), pl.BlockSpec((CONV_WIDTH, LANES), lambda b, c: (0, c))],
        out_specs=pl.BlockSpec((seq, LANES), lambda b, c: (b, c)),
        out_shape=jax.ShapeDtypeStruct((batch * seq, D_WIDTH), BF16),
        compiler_params=_params("parallel", "parallel"),
        name="short_conv",
    )(proj, proj, proj, conv_w)


def _rope_tables(seq, dim):
    half = dim // 2
    inv = ROPE_THETA ** (-jnp.arange(0, dim, 2, dtype=F32) / dim)
    ang = jnp.arange(seq, dtype=F32)[:, None] * inv[None, :]
    cos, sin = jnp.cos(ang), jnp.sin(ang)
    reps = LANES // dim
    cos_t = jnp.tile(jnp.concatenate([cos, cos], axis=1), (1, reps))
    zeros = jnp.zeros_like(sin)
    if dim == LANES:
        return (cos_t, jnp.concatenate([-sin, sin], axis=1)), (half,)
    upper = jnp.tile(jnp.concatenate([zeros, sin], axis=1), (1, reps))
    lower = jnp.tile(jnp.concatenate([-sin, zeros], axis=1), (1, reps))
    return (cos_t, upper, lower), (half, LANES - half)


def kernel(x, w_in, w_out, norm_mix, norm_mlp, diff_lambda, diff_subln, pool_w, pool_scale, conv_w,
           w_up, w_down, norm_final):
    batch, seq, d_model = x.shape
    depth = w_in.shape[0]
    m = batch * seq
    x = x.reshape(m, d_model)

    tables_a, shifts_a = _rope_tables(seq, A_SUB)
    tables_b, shifts_b = _rope_tables(seq, HEAD_DIM)

    a_qk_end = 2 * A_WIDTH
    a_end = 3 * A_WIDTH
    b_end = a_end + 3 * B_WIDTH
    c_end = b_end + C_WIDTH
    t_a = 512
    tail0 = (b_end // IN_PROJ_COL_TILE) * IN_PROJ_COL_TILE
    in_width = w_in.shape[2]

    w_in_l = _cast_layer(w_in, 0)[None]

    h, ss = _prep(x, norm_mix[0])
    for l in range(depth):
        lam_init = 0.8 - 0.6 * math.exp(-0.3 * l)
        w_a_vt = jnp.swapaxes(w_in_l[:, :, a_qk_end:a_end], 1, 2)

        qk_a = _mm_rope(h, ss, w_in_l, 0, a_qk_end, tables_a, shifts_a, A_SCORE_SCALE, A_WIDTH, seq)
        vt_a = _mm_transposed(h, ss, w_a_vt, 0, t_a)
        qkv_b = [_mm_rope_permute(h, ss, w_in_l, 0, a_end, g, tables_b, shifts_b[0], B_GROUPS[g][1], batch, seq)
                 for g in range(len(B_GROUPS))]
        tail, w_out_l = _mm_cols(h, ss, w_in_l, 0, tail0, in_width - tail0, F32, w_out, l, name="in_proj_tail")

        out_a = _attn_a(qk_a, vt_a, diff_lambda[l], diff_subln[l], lam_init, batch, seq, t_a)
        out_b = _attn_b(qkv_b, batch, seq)
        out_c = _pool(tail, b_end - tail0, pool_w[l].astype(BF16), pool_scale[l], batch, seq)
        out_d = _short_conv(tail, c_end - tail0, conv_w[l], batch, seq)

        x, h, ss = _out_proj(out_a, out_b, out_c, out_d, w_out_l, x, norm_mlp[l])
        act, w_down_l = _mlp_up(h, ss, w_up, w_down, l, 512)
        if l + 1 < depth:
            x, h, ss, w_next = _mm_residual(act, w_down_l, x, norm_mix[l + 1], w_in, l + 1)
            w_in_l = w_next[None]
        else:
            x, = _mm_residual(act, w_down_l, x)

    out = _rmsnorm(x, norm_final, F32)
    return out.reshape(batch, seq, d_model)
```

```python
import functools
import math

import jax
import jax.numpy as jnp
from jax import lax
from jax.experimental import pallas as pl
from jax.experimental.pallas import tpu as pltpu

D_MODEL = 4096
HEAD_DIM = 128
A_HEADS = 8
A_WIDTH = A_HEADS * HEAD_DIM
A_SUB = HEAD_DIM // 2
A_SCORE_SCALE = A_SUB ** -0.5 * math.log2(math.e)
A_ONES_ROWS = 16
B_GROUPS = ((128, 1), (512, 4), (2048, 16))
B_HEADS_PER_GROUP = 3
B_GROUP_WIDTH = B_HEADS_PER_GROUP * HEAD_DIM
B_WIDTH = B_GROUP_WIDTH * len(B_GROUPS)
B_TOKEN_TILE = 2048
B_QUERY_CHUNK = 256
C_WINDOWS = (2, 4, 8, 16)
C_WIDTH = 1024
C_GROUP_DIM = C_WIDTH // len(C_WINDOWS)
D_WIDTH = D_MODEL - A_WIDTH - B_WIDTH - C_WIDTH
CONV_WIDTH = 3
D_FF = 4 * D_MODEL
ROPE_THETA = 10000.0
NORM_EPS = 1e-6
DIFF_EPS = 1e-5

LANES = 128
MXU_COLS = 256
MLP_ROW_TILE = 2048
IN_PROJ_COL_TILE = 1024
IN_PROJ_ROW_TILE = 1024
VMEM_LIMIT_BYTES = 56 * 1024 * 1024
MLP_DOWN_VMEM_LIMIT_BYTES = 58 * 1024 * 1024
OUT_PROJ_VMEM_LIMIT_BYTES = 58 * 1024 * 1024
MASKED_SCORE = -1e30

F32 = jnp.float32
BF16 = jnp.bfloat16


def _params(*semantics):
    return pltpu.CompilerParams(dimension_semantics=semantics, vmem_limit_bytes=VMEM_LIMIT_BYTES)


def _rmsnorm_kernel(x_ref, g_ref, o_ref, *, eps):
    x = x_ref[...]
    ms = jnp.mean(x * x, axis=-1, keepdims=True)
    o_ref[...] = (x * lax.rsqrt(ms + eps) * g_ref[...]).astype(o_ref.dtype)


def _rmsnorm(x, g, out_dtype, tm=256):
    m, d = x.shape
    return pl.pallas_call(
        functools.partial(_rmsnorm_kernel, eps=NORM_EPS),
        grid=(m // tm,),
        in_specs=[pl.BlockSpec((tm, d), lambda i: (i, 0)),
                  pl.BlockSpec((1, d), lambda i: (0, 0))],
        out_specs=pl.BlockSpec((tm, d), lambda i: (i, 0)),
        out_shape=jax.ShapeDtypeStruct((m, d), out_dtype),
        compiler_params=_params("parallel"),
        name="rmsnorm",
    )(x, g.reshape(1, d))


def _prep_kernel(x_ref, g_ref, xb_ref, ss_ref):
    x = x_ref[...]
    xb_ref[...] = (x * g_ref[...]).astype(xb_ref.dtype)
    ss_ref[...] = jnp.broadcast_to(jnp.sum(x * x, axis=-1, keepdims=True), ss_ref.shape)


def _prep(x, g, tm=256):
    m, d = x.shape
    return pl.pallas_call(
        _prep_kernel,
        grid=(m // tm,),
        in_specs=[pl.BlockSpec((tm, d), lambda i: (i, 0)),
                  pl.BlockSpec((1, d), lambda i: (0, 0))],
        out_specs=[pl.BlockSpec((tm, d), lambda i: (i, 0)),
                   pl.BlockSpec((tm, LANES), lambda i: (i, 0))],
        out_shape=[jax.ShapeDtypeStruct((m, d), BF16), jax.ShapeDtypeStruct((m, LANES), F32)],
        compiler_params=_params("parallel"),
        name="norm_prep",
    )(x, g.reshape(1, d))


def _cast_kernel(w_ref, o_ref):
    o_ref[...] = w_ref[...].astype(o_ref.dtype)


def _cast_layer(w, layer, rows=256):
    _, k, n = w.shape
    return pl.pallas_call(
        _cast_kernel,
        grid=(k // rows,),
        in_specs=[pl.BlockSpec((None, rows, n), lambda i: (layer, i, 0))],
        out_specs=pl.BlockSpec((rows, n), lambda i: (i, 0)),
        out_shape=jax.ShapeDtypeStruct((k, n), BF16),
        compiler_params=_params("parallel"),
        name="cast_weights",
    )(w)


def _row_scale(ss_ref):
    return lax.rsqrt(ss_ref[...] * (1.0 / D_MODEL) + NORM_EPS)


def _scale_rows(y, rs):
    return y * jnp.concatenate([rs] * (y.shape[1] // LANES), axis=1)


def _store_norm_inputs(x_new, g_ref, xb_ref, ss_ref, first):
    xb_ref[...] = (x_new * g_ref[...]).astype(xb_ref.dtype)
    part = jnp.broadcast_to(jnp.sum(x_new * x_new, axis=-1, keepdims=True), ss_ref.shape)

    @pl.when(first)
    def _():
        ss_ref[...] = part

    @pl.when(jnp.logical_not(first))
    def _():
        ss_ref[...] += part


def _rotate(y, cos_ref, sin_refs, shifts):
    outs = []
    for c in range(y.shape[1] // LANES):
        yc = y[:, c * LANES:(c + 1) * LANES]
        oc = yc * cos_ref[...]
        for s_ref, shift in zip(sin_refs, shifts):
            oc = oc + pltpu.roll(yc, shift, 1) * s_ref[...]
        outs.append(oc)
    return jnp.concatenate(outs, axis=1)


def _mm_rope_kernel(a_ref, ss_ref, w_ref, cos_ref, *rest, shifts, q_scale, n_q_blocks):
    sin_refs, o_ref = rest[:-1], rest[-1]
    scale = jnp.where(pl.program_id(1) < n_q_blocks, q_scale, 1.0).astype(F32)
    a = a_ref[...]
    rs = _row_scale(ss_ref)
    for c in range(o_ref.shape[1] // MXU_COLS):
        cols = slice(c * MXU_COLS, (c + 1) * MXU_COLS)
        acc = _scale_rows(jnp.dot(a, w_ref[:, cols].astype(BF16), preferred_element_type=F32), rs)
        o_ref[:, cols] = (_rotate(acc, cos_ref, sin_refs, shifts) * scale).astype(o_ref.dtype)


def _mm_rope(a, ss, w, layer, n, tables, shifts, q_scale, n_q_cols, seq, bn=IN_PROJ_COL_TILE,
             bm=IN_PROJ_ROW_TILE):
    m, k = a.shape
    tab_blocks = seq // bm
    tab_spec = pl.BlockSpec((bm, LANES), lambda i, j: (i % tab_blocks, 0))
    return pl.pallas_call(
        functools.partial(_mm_rope_kernel, shifts=shifts, q_scale=q_scale, n_q_blocks=n_q_cols // bn),
        grid=(m // bm, n // bn),
        in_specs=[pl.BlockSpec((bm, k), lambda i, j: (i, 0)),
                  pl.BlockSpec((bm, LANES), lambda i, j: (i, 0)),
                  pl.BlockSpec((None, k, bn), lambda i, j: (layer, 0, j))] + [tab_spec] * len(tables),
        out_specs=pl.BlockSpec((bm, bn), lambda i, j: (i, j)),
        out_shape=jax.ShapeDtypeStruct((m, n), BF16),
        compiler_params=_params("parallel", "arbitrary"),
        name="in_proj_rope",
    )(a, ss, w, *tables)


def _mm_rope_permute_kernel(a_ref, ss_ref, wq_ref, wk_ref, wv_ref, cos_ref, sin_ref, o_ref, scr_ref, *,
                            shift, n_rope_cols, dilation):
    a = a_ref[...]
    w = jnp.concatenate([wq_ref[...], wk_ref[...], wv_ref[...]], axis=1)
    rs = _row_scale(ss_ref)
    n = o_ref.shape[2]
    rows = a.shape[0] // dilation
    for lo in range(0, n, 2 * MXU_COLS):
        hi = min(lo + 2 * MXU_COLS, n)
        acc = _scale_rows(jnp.dot(a, w[:, lo:hi], preferred_element_type=F32), rs)
        rope_hi = min(hi, n_rope_cols)
        if lo < rope_hi:
            rot = _rotate(acc[:, :rope_hi - lo], cos_ref, (sin_ref,), (shift,))
            acc = rot if rope_hi == hi else jnp.concatenate([rot, acc[:, rope_hi - lo:]], axis=1)
        if dilation == 1:
            o_ref[0, :, lo:hi] = acc.astype(o_ref.dtype)
        else:
            for c in range(lo // LANES, hi // LANES):
                cols = slice(c * LANES, (c + 1) * LANES)
                scr_ref[c] = acc[:, c * LANES - lo:(c + 1) * LANES - lo]
                for r in range(dilation):
                    o_ref[r, :, cols] = scr_ref[c, pl.ds(r, rows, stride=dilation), :].astype(o_ref.dtype)


def _mm_rope_permute(a, ss, w, layer, col0, group, tables, shift, dilation, batch, seq, bm=IN_PROJ_ROW_TILE):
    m, k = a.shape
    n = 3 * B_GROUP_WIDTH
    n_rope_cols = 2 * B_GROUP_WIDTH
    tiles = seq // bm
    tab_spec = pl.BlockSpec((bm, LANES), lambda i: (i % tiles, 0))
    n_groups = len(B_GROUPS)
    first = col0 // B_GROUP_WIDTH
    assert first * B_GROUP_WIDTH == col0

    def w_spec(part):
        return pl.BlockSpec((None, k, B_GROUP_WIDTH), lambda i: (layer, 0, first + part * n_groups + group),
                            pipeline_mode=pl.Buffered(1))

    return pl.pallas_call(
        functools.partial(_mm_rope_permute_kernel, shift=shift, n_rope_cols=n_rope_cols, dilation=dilation),
        grid=(m // bm,),
        in_specs=[pl.BlockSpec((bm, k), lambda i: (i, 0)), pl.BlockSpec((bm, LANES), lambda i: (i, 0)),
                  w_spec(0), w_spec(1), w_spec(2), tab_spec, tab_spec],
        out_specs=pl.BlockSpec((None, dilation, bm // dilation, n), lambda i: (i // tiles, 0, i % tiles, 0)),
        out_shape=jax.ShapeDtypeStruct((batch, dilation, seq // dilation, n), BF16),
        scratch_shapes=[pltpu.VMEM((n // LANES, bm, LANES), F32)],
        compiler_params=_params("parallel"),
        name="in_proj_dilated",
    )(a, ss, w, w, w, *tables)


def _mm_kernel(a_ref, ss_ref, w_ref, o_ref, *, relu_sq):
    acc = jnp.dot(a_ref[...], w_ref[...].astype(BF16), preferred_element_type=F32)
    acc = _scale_rows(acc, _row_scale(ss_ref))
    if relu_sq:
        acc = jnp.square(jnp.maximum(acc, 0.0))
    o_ref[...] = acc.astype(o_ref.dtype)


def _mm_cols_kernel(a_ref, ss_ref, w_ref, side_ref, o_ref, side_bf16_ref):
    _mm_kernel(a_ref, ss_ref, w_ref, o_ref, relu_sq=False)
    side_bf16_ref[...] = side_ref[...].astype(BF16)


def _mm_cols(a, ss, w, layer, col0, n, out_dtype, w_side, side_layer, bn=IN_PROJ_COL_TILE, bm=IN_PROJ_ROW_TILE,
             name="matmul"):
    m, k = a.shape
    first = col0 // bn
    assert first * bn == col0 and n % bn == 0
    n_j = n // bn
    steps = (m // bm) * n_j
    rows_side, cols_side = w_side.shape[1:]
    slab = rows_side // steps
    assert slab * steps == rows_side
    return pl.pallas_call(
        _mm_cols_kernel,
        grid=(m // bm, n_j),
        in_specs=[pl.BlockSpec((bm, k), lambda i, j: (i, 0)),
                  pl.BlockSpec((bm, LANES), lambda i, j: (i, 0)),
                  pl.BlockSpec((None, k, bn), lambda i, j: (layer, 0, first + j)),
                  pl.BlockSpec((None, slab, cols_side), lambda i, j: (side_layer, i * n_j + j, 0))],
        out_specs=[pl.BlockSpec((bm, bn), lambda i, j: (i, j)),
                   pl.BlockSpec((slab, cols_side), lambda i, j: (i * n_j + j, 0))],
        out_shape=[jax.ShapeDtypeStruct((m, n), out_dtype), jax.ShapeDtypeStruct((rows_side, cols_side), BF16)],
        compiler_params=_params("parallel", "arbitrary"),
        name=name,
    )(a, ss, w, w_side)


def _mlp_up_kernel(a_ref, ss_ref, w_ref, wd_ref, o_ref, wd_bf16_ref):
    _mm_kernel(a_ref, ss_ref, w_ref, o_ref, relu_sq=True)
    wd_bf16_ref[...] = wd_ref[...].astype(BF16)


def _mlp_up(a, ss, w, w_down, layer, bn, bm=MLP_ROW_TILE):
    m, k = a.shape
    n = w.shape[2]
    n_j = n // bn
    steps = (m // bm) * n_j
    slab = n // steps
    assert slab * steps == n
    return pl.pallas_call(
        _mlp_up_kernel,
        grid=(m // bm, n_j),
        in_specs=[pl.BlockSpec((bm, k), lambda i, j: (i, 0), pipeline_mode=pl.Buffered(1)),
                  pl.BlockSpec((bm, LANES), lambda i, j: (i, 0)),
                  pl.BlockSpec((None, k, bn), lambda i, j: (layer, 0, j)),
                  pl.BlockSpec((None, slab, k), lambda i, j: (layer, i * n_j + j, 0))],
        out_specs=[pl.BlockSpec((None, None, bm, bn), lambda i, j: (i, j, 0, 0)),
                   pl.BlockSpec((slab, k), lambda i, j: (i * n_j + j, 0))],
        out_shape=[jax.ShapeDtypeStruct((m // bm, n_j, bm, bn), BF16), jax.ShapeDtypeStruct((n, k), BF16)],
        compiler_params=_params("parallel", "arbitrary"),
        name="mlp_up",
    )(a, ss, w, w_down)


def _mm_transposed_kernel(wt_ref, a_ref, ss_ref, o_ref, *, t):
    acc = lax.dot_general(wt_ref[...], a_ref[...], (((1,), (1,)), ((), ())), preferred_element_type=F32)
    acc = acc * jnp.transpose(_row_scale(ss_ref))[0:1, :]
    for c in range(acc.shape[1] // t):
        o_ref[c] = acc[:, c * t:(c + 1) * t].astype(o_ref.dtype)


def _mm_transposed(a, ss, wt, layer, t, bm=1024):
    m, k = a.shape
    n = wt.shape[1]
    return pl.pallas_call(
        functools.partial(_mm_transposed_kernel, t=t),
        grid=(m // bm,),
        in_specs=[pl.BlockSpec((None, n, k), lambda i: (layer, 0, 0), pipeline_mode=pl.Buffered(1)),
                  pl.BlockSpec((bm, k), lambda i: (i, 0)),
                  pl.BlockSpec((bm, LANES), lambda i: (i, 0))],
        out_specs=pl.BlockSpec((bm // t, n, t), lambda i: (i, 0, 0)),
        out_shape=jax.ShapeDtypeStruct((m // t, n, t), BF16),
        compiler_params=_params("parallel"),
        name="in_proj_vt",
    )(wt, a, ss)


def _mm_residual_kernel(a_ref, w_ref, r_ref, *rest):
    o_ref = rest[0] if len(rest) == 1 else rest[2]
    kk = pl.program_id(2)

    @pl.when(kk == 0)
    def _():
        o_ref[...] = r_ref[...]

    a = jnp.concatenate([a_ref[c] for c in range(a_ref.shape[0])], axis=1)
    o_ref[...] += jnp.dot(a, w_ref[...], preferred_element_type=F32)

    if len(rest) > 1:
        g_ref, w_next_ref, _, xb_ref, ss_ref, w_next_bf16_ref = rest
        w_next_bf16_ref[...] = w_next_ref[...].astype(BF16)

        @pl.when(kk == pl.num_programs(2) - 1)
        def _():
            _store_norm_inputs(o_ref[...], g_ref, xb_ref, ss_ref, pl.program_id(1) == 0)


def _mm_residual(a, w, r, g=None, w_next=None, next_layer=None, bm=1024, bn=1024, bk=4096):
    row_tiles, col_tiles, tile_rows, tile_cols = a.shape
    m, k = row_tiles * tile_rows, col_tiles * tile_cols
    n = w.shape[1]
    sub = tile_rows // bm
    assert sub * bm == tile_rows
    tile_spec = pl.BlockSpec((bm, bn), lambda i, j, kk: (i, j))
    in_specs = [pl.BlockSpec((None, bk // tile_cols, bm, tile_cols), lambda i, j, kk: (i // sub, kk, i % sub, 0)),
                pl.BlockSpec((bk, bn), lambda i, j, kk: (kk, j)),
                tile_spec]
    out_specs = [tile_spec]
    out_shape = [jax.ShapeDtypeStruct((m, n), F32)]
    operands = [a, w, r]
    if g is not None:
        n_j, n_k = n // bn, k // bk
        steps = (m // bm) * n_j * n_k
        rows_next, cols_next = w_next.shape[1:]
        slab = rows_next // steps
        assert slab * steps == rows_next

        def slab_index(i, j, kk):
            return (i * n_j + j) * n_k + kk

        in_specs += [pl.BlockSpec((1, bn), lambda i, j, kk: (0, j)),
                     pl.BlockSpec((None, slab, cols_next), lambda i, j, kk: (next_layer, slab_index(i, j, kk), 0))]
        out_specs += [tile_spec, pl.BlockSpec((bm, LANES), lambda i, j, kk: (i, 0)),
                      pl.BlockSpec((slab, cols_next), lambda i, j, kk: (slab_index(i, j, kk), 0))]
        out_shape += [jax.ShapeDtypeStruct((m, n), BF16), jax.ShapeDtypeStruct((m, LANES), F32),
                      jax.ShapeDtypeStruct((rows_next, cols_next), BF16)]
        operands += [g.reshape(1, n), w_next]
    return pl.pallas_call(
        _mm_residual_kernel,
        grid=(m // bm, n // bn, k // bk),
        in_specs=in_specs,
        out_specs=out_specs,
        out_shape=out_shape,
        compiler_params=pltpu.CompilerParams(dimension_semantics=("parallel", "arbitrary", "arbitrary"),
                                             vmem_limit_bytes=MLP_DOWN_VMEM_LIMIT_BYTES),
        name="mlp_down",
    )(*operands)


def _out_proj_kernel(a_ref, b_ref, c_ref, d_ref, w_ref, r_ref, g_ref, o_ref, xb_ref, ss_ref):
    mix = jnp.concatenate([a_ref[...]] + [b_ref[g] for g in range(len(B_GROUPS))] + [c_ref[...], d_ref[...]], axis=1)
    x_new = r_ref[...] + jnp.dot(mix, w_ref[...], preferred_element_type=F32)
    o_ref[...] = x_new
    _store_norm_inputs(x_new, g_ref, xb_ref, ss_ref, pl.program_id(1) == 0)


def _out_proj(out_a, out_b, out_c, out_d, w, r, g, bm=1024, bn=1024):
    m = out_a.shape[0]
    n = w.shape[1]
    n_groups = len(B_GROUPS)
    return pl.pallas_call(
        _out_proj_kernel,
        grid=(m // bm, n // bn),
        in_specs=[pl.BlockSpec((bm, A_WIDTH), lambda i, j: (i, 0)),
                  pl.BlockSpec((n_groups, bm, B_GROUP_WIDTH), lambda i, j: (0, i, 0)),
                  pl.BlockSpec((bm, C_WIDTH), lambda i, j: (i, 0)),
                  pl.BlockSpec((bm, D_WIDTH), lambda i, j: (i, 0)),
                  pl.BlockSpec((D_MODEL, bn), lambda i, j: (0, j)),
                  pl.BlockSpec((bm, bn), lambda i, j: (i, j)),
                  pl.BlockSpec((1, bn), lambda i, j: (0, j))],
        out_specs=[pl.BlockSpec((bm, bn), lambda i, j: (i, j)),
                   pl.BlockSpec((bm, bn), lambda i, j: (i, j)),
                   pl.BlockSpec((bm, LANES), lambda i, j: (i, 0))],
        out_shape=[jax.ShapeDtypeStruct((m, n), F32), jax.ShapeDtypeStruct((m, n), BF16),
                   jax.ShapeDtypeStruct((m, LANES), F32)],
        compiler_params=pltpu.CompilerParams(dimension_semantics=("parallel", "arbitrary"),
                                             vmem_limit_bytes=OUT_PROJ_VMEM_LIMIT_BYTES),
        name="out_proj",
    )(out_a, out_b, out_c, out_d, w, r, g.reshape(1, n))


def _attn_a_kernel(q_ref, k_ref, vt_ref, lam_ref, g_ref, o_ref, *scratch, t, heads, lam_init):
    qi = pl.program_id(2)
    lane = lax.broadcasted_iota(jnp.int32, (t, HEAD_DIM), 1)
    qqs = []
    for h in range(heads):
        q = q_ref[:, h * HEAD_DIM:(h + 1) * HEAD_DIM]
        zero = jnp.zeros_like(q)
        qqs.append(jnp.concatenate([jnp.where(lane < A_SUB, q, zero), jnp.where(lane >= A_SUB, q, zero)], axis=0))

    ones = jnp.ones((A_ONES_ROWS, t), BF16)
    m_scr, acc_scr = scratch[:heads], scratch[heads:]
    for h in range(heads):
        m_scr[h][...] = jnp.full(m_scr[h].shape, MASKED_SCORE, F32)
        acc_scr[h][...] = jnp.zeros(acc_scr[h].shape, F32)

    def steps(blocks):
        items = [(j, masked, h) for j, masked in blocks for h in range(heads)]

        def scores(item):
            j, masked, h = item
            k = k_ref[pl.ds(pl.multiple_of(j * t, t), t), h * HEAD_DIM:(h + 1) * HEAD_DIM]
            s = lax.dot_general(k, qqs[h], (((1,), (1,)), ((), ())), preferred_element_type=F32)
            if masked:
                key = lax.broadcasted_iota(jnp.int32, s.shape, 0)
                lane_q = lax.broadcasted_iota(jnp.int32, s.shape, 1)
                qry = jnp.where(lane_q >= t, lane_q - t, lane_q)
                s = jnp.where(key <= qry, s, MASKED_SCORE)
            return s

        s_next = scores(items[0])
        for idx, (j, _, h) in enumerate(items):
            s = s_next
            if idx + 1 < len(items):
                s_next = scores(items[idx + 1])
            vt = jnp.concatenate([vt_ref[j, h * HEAD_DIM:(h + 1) * HEAD_DIM, :], ones], axis=0)
            m = m_scr[h][...]
            m_new = jnp.maximum(m, jnp.max(s, axis=0, keepdims=True))
            alpha = jnp.exp2(m - m_new)
            p = jnp.exp2(s - m_new)
            m_scr[h][...] = m_new
            acc_scr[h][...] = alpha * acc_scr[h][...] + jnp.dot(vt, p.astype(BF16), preferred_element_type=F32)

    def body(jj, carry):
        steps([(2 * jj, False), (2 * jj + 1, False)])
        return carry

    lax.fori_loop(0, qi // 2, body, 0)

    @pl.when(qi % 2 == 1)
    def _():
        steps([(qi - 1, False), (qi, True)])

    @pl.when(qi % 2 == 0)
    def _():
        steps([(qi, True)])

    lp = lam_ref[...]
    lam = (jnp.exp(jnp.sum(lp[0:1] * lp[1:2], axis=-1, keepdims=True))
           - jnp.exp(jnp.sum(lp[2:3] * lp[3:4], axis=-1, keepdims=True)) + lam_init)
    for h in range(heads):
        acc = acc_scr[h][...]
        o = acc[:HEAD_DIM] / acc[HEAD_DIM:HEAD_DIM + 1]
        o = (o[:, :t] - lam * o[:, t:]).T
        ms = jnp.mean(o * o, axis=-1, keepdims=True)
        o = o * lax.rsqrt(ms + DIFF_EPS) * g_ref[...]
        o_ref[:, h * HEAD_DIM:(h + 1) * HEAD_DIM] = (o * (1.0 - lam_init)).astype(o_ref.dtype)


def _attn_a(qk, vt, lam_params, subln_g, lam_init, batch, seq, t, heads=8):
    nq = seq // t
    width = heads * HEAD_DIM
    n_pairs = A_HEADS // heads
    return pl.pallas_call(
        functools.partial(_attn_a_kernel, t=t, heads=heads, lam_init=lam_init),
        grid=(batch, n_pairs, nq),
        in_specs=[pl.BlockSpec((t, width), lambda b, h, i: (b * nq + i, h)),
                  pl.BlockSpec((seq, width), lambda b, h, i: (b, n_pairs + h)),
                  pl.BlockSpec((nq, width, t), lambda b, h, i: (b, h, 0)),
                  pl.BlockSpec((4, A_SUB), lambda b, h, i: (0, 0)),
                  pl.BlockSpec((1, HEAD_DIM), lambda b, h, i: (0, 0))],
        out_specs=pl.BlockSpec((t, width), lambda b, h, i: (b * nq + i, h)),
        out_shape=jax.ShapeDtypeStruct((batch * seq, A_WIDTH), BF16),
        scratch_shapes=([pltpu.VMEM((1, 2 * t), F32)] * heads
                        + [pltpu.VMEM((HEAD_DIM + A_ONES_ROWS, 2 * t), F32)] * heads),
        compiler_params=_params("parallel", "parallel", "arbitrary"),
        name="diff_attention",
    )(qk, qk, vt, lam_params, subln_g.reshape(1, HEAD_DIM))


def _attn_b_kernel(*refs, tile):
    n_groups = len(B_GROUPS)
    q_refs = refs[0:n_groups]
    k_refs = refs[n_groups:2 * n_groups]
    v_refs = refs[2 * n_groups:3 * n_groups]
    o_ref = refs[3 * n_groups]
    out_scr, lse_scr = refs[3 * n_groups + 1:]
    ti = pl.program_id(2)
    scale = HEAD_DIM ** -0.5
    base_delta = {}
    chunks = []

    for g, (window, dilation) in enumerate(B_GROUPS):
        back = window // dilation
        seg = tile // dilation
        tq = min(B_QUERY_CHUNK, seg)
        tk = tq + back
        if tq not in base_delta:
            base_delta[tq] = (lax.broadcasted_iota(jnp.int32, (tq, tk), 0)
                              - lax.broadcasted_iota(jnp.int32, (tq, tk), 1))
        for r in range(dilation):
            for c in range(seg // tq):
                chunks.append((g, dilation, back, tq, tk, r, c * tq, ti * seg + c * tq))

    def scores(chunk):
        g, dilation, back, tq, tk, r, i_loc, i0 = chunk
        ks = pl.multiple_of(jnp.maximum(i0 - back, 0), back)
        q = q_refs[g][r, i_loc:i_loc + tq, :]
        k = k_refs[g][r, pl.ds(ks, tk), :]
        s = lax.dot_general(q, k, (((1,), (1,)), ((), ())), preferred_element_type=F32) * scale
        delta = base_delta[tq] + (i0 - ks)
        return jnp.where((delta >= 0) & (delta <= back), s, MASKED_SCORE), ks

    nxt = scores(chunks[0])
    for idx, (g, dilation, back, tq, tk, r, i_loc, i0) in enumerate(chunks):
        s, ks = nxt
        if idx + 1 < len(chunks):
            nxt = scores(chunks[idx + 1])
        v = v_refs[g][r, pl.ds(ks, tk), :]
        m = jnp.max(s, axis=-1, keepdims=True)
        p = jnp.exp(s - m)
        l = jnp.sum(p, axis=-1, keepdims=True)
        o = jnp.dot(p.astype(BF16), v, preferred_element_type=F32) / l
        lse = jnp.broadcast_to(m + jnp.log(l), o.shape)
        rows = pl.ds(dilation * i_loc + r, tq, stride=dilation) if dilation > 1 else pl.ds(i_loc, tq)
        out_scr[g, rows, :] = o
        lse_scr[g, rows, :] = lse

    for c in range(tile // B_QUERY_CHUNK):
        rows = pl.ds(c * B_QUERY_CHUNK, B_QUERY_CHUNK)
        lses = [lse_scr[g, rows, :] for g in range(n_groups)]
        lse_max = functools.reduce(jnp.maximum, lses)
        weights = [jnp.exp(x - lse_max) for x in lses]
        denom = functools.reduce(lambda a, b: a + b, weights)
        for g in range(n_groups):
            o_ref[g, rows, :] = (out_scr[g, rows, :] * (weights[g] / denom)).astype(o_ref.dtype)


def _attn_b(qkvs, batch, seq, tile=B_TOKEN_TILE):
    n_tiles = seq // tile
    n_groups = len(B_GROUPS)

    def q_spec(g):
        d = B_GROUPS[g][1]
        return pl.BlockSpec((None, d, tile // d, HEAD_DIM), lambda b, h, i: (b, 0, i, h))

    def kv_spec(g, part):
        d = B_GROUPS[g][1]
        return pl.BlockSpec((None, d, seq // d, HEAD_DIM),
                            lambda b, h, i: (b, 0, 0, part * B_HEADS_PER_GROUP + h))

    in_specs = ([q_spec(g) for g in range(n_groups)] + [kv_spec(g, 1) for g in range(n_groups)]
                + [kv_spec(g, 2) for g in range(n_groups)])
    return pl.pallas_call(
        functools.partial(_attn_b_kernel, tile=tile),
        grid=(batch, B_HEADS_PER_GROUP, n_tiles),
        in_specs=in_specs,
        out_specs=pl.BlockSpec((n_groups, tile, HEAD_DIM), lambda b, h, i: (0, b * n_tiles + i, h)),
        out_shape=jax.ShapeDtypeStruct((n_groups, batch * seq, B_GROUP_WIDTH), BF16),
        scratch_shapes=[pltpu.VMEM((n_groups, tile, HEAD_DIM), F32),
                        pltpu.VMEM((n_groups, tile, HEAD_DIM), F32)],
        compiler_params=_params("parallel", "parallel", "arbitrary"),
        name="dilated_attention",
    )(*(list(qkvs) * 3))


def _shift_rows(x, k, row):
    return jnp.where(row >= k, pltpu.roll(x, k, 0), 0.0)


def _pool_kernel(u_lo_ref, u_hi_ref, w_ref, scale_ref, o_ref):
    g = pl.program_id(1)
    u = jnp.concatenate([u_lo_ref[...], u_hi_ref[...]], axis=1)
    row = lax.broadcasted_iota(jnp.int32, u.shape, 0)
    window = jnp.left_shift(2, g)
    total = u
    step = 1
    while step < max(C_WINDOWS):
        widened = total + _shift_rows(total, step, row)
        total = jnp.where(step < window, widened, total)
        step *= 2
    count = jnp.minimum(row + 1, window).astype(F32)
    pooled = total / count - u
    y = jnp.dot(pooled.astype(BF16), w_ref[0], preferred_element_type=F32)
    o_ref[...] = (y * scale_ref[...]).astype(o_ref.dtype)


def _pool(proj, col0, pool_w, pool_scale, batch, seq):
    n_groups = len(C_WINDOWS)
    tiles_per_group = C_GROUP_DIM // LANES
    assert tiles_per_group == 2 and col0 % LANES == 0
    first = col0 // LANES

    def u_spec(half):
        return pl.BlockSpec((seq, LANES), lambda b, g: (b, first + tiles_per_group * g + half))

    return pl.pallas_call(
        _pool_kernel,
        grid=(batch, n_groups),
        in_specs=[u_spec(0), u_spec(1),
                  pl.BlockSpec((1, C_GROUP_DIM, C_GROUP_DIM), lambda b, g: (g, 0, 0)),
                  pl.BlockSpec((1, C_GROUP_DIM), lambda b, g: (0, g))],
        out_specs=pl.BlockSpec((seq, C_GROUP_DIM), lambda b, g: (b, g)),
        out_shape=jax.ShapeDtypeStruct((batch * seq, C_WIDTH), BF16),
        compiler_params=_params("parallel", "parallel"),
        name="multiscale_pool",
    )(proj, proj, pool_w, pool_scale.reshape(1, C_WIDTH))


def _conv_kernel(gb_ref, gc_ref, h_ref, w_ref, o_ref):
    z = gc_ref[...] * h_ref[...]
    row = lax.broadcasted_iota(jnp.int32, z.shape, 0)
    w = w_ref[...]
    y = w[CONV_WIDTH - 1:CONV_WIDTH] * z
    for tap in range(1, CONV_WIDTH):
        y = y + w[CONV_WIDTH - 1 - tap:CONV_WIDTH - tap] * _shift_rows(z, tap, row)
    o_ref[...] = (gb_ref[...] * y).astype(o_ref.dtype)


def _short_conv(proj, col0, conv_w, batch, seq):
    n_col = D_WIDTH // LANES
    assert col0 % LANES == 0
    first = col0 // LANES

    def spec(part):
        return pl.BlockSpec((seq, LANES), lambda b, c: (b, first + part * n_col + c))

    return pl.pallas_call(
        _conv_kernel,
        grid=(batch, n_col),
        in_specs=[spec(0), spec(1), spec(2), pl.BlockSpec((CONV_WIDTH, LANES), lambda b, c: (0, c))],
        out_specs=pl.BlockSpec((seq, LANES), lambda b, c: (b, c)),
        out_shape=jax.ShapeDtypeStruct((batch * seq, D_WIDTH), BF16),
        compiler_params=_params("parallel", "parallel"),
        name="short_conv",
    )(proj, proj, proj, conv_w)


def _rope_tables(seq, dim):
    half = dim // 2
    inv = ROPE_THETA ** (-jnp.arange(0, dim, 2, dtype=F32) / dim)
    ang = jnp.arange(seq, dtype=F32)[:, None] * inv[None, :]
    cos, sin = jnp.cos(ang), jnp.sin(ang)
    reps = LANES // dim
    cos_t = jnp.tile(jnp.concatenate([cos, cos], axis=1), (1, reps))
    zeros = jnp.zeros_like(sin)
    if dim == LANES:
        return (cos_t, jnp.concatenate([-sin, sin], axis=1)), (half,)
    upper = jnp.tile(jnp.concatenate([zeros, sin], axis=1), (1, reps))
    lower = jnp.tile(jnp.concatenate([-sin, zeros], axis=1), (1, reps))
    return (cos_t, upper, lower), (half, LANES - half)


def kernel(x, w_in, w_out, norm_mix, norm_mlp, diff_lambda, diff_subln, pool_w, pool_scale, conv_w,
           w_up, w_down, norm_final):
    batch, seq, d_model = x.shape
    depth = w_in.shape[0]
    m = batch * seq
    x = x.reshape(m, d_model)

    tables_a, shifts_a = _rope_tables(seq, A_SUB)
    tables_b, shifts_b = _rope_tables(seq, HEAD_DIM)

    a_qk_end = 2 * A_WIDTH
    a_end = 3 * A_WIDTH
    b_end = a_end + 3 * B_WIDTH
    c_end = b_end + C_WIDTH
    t_a = 512
    tail0 = (b_end // IN_PROJ_COL_TILE) * IN_PROJ_COL_TILE
    in_width = w_in.shape[2]

    w_in_l = _cast_layer(w_in, 0)[None]

    h, ss = _prep(x, norm_mix[0])
    for l in range(depth):
        lam_init = 0.8 - 0.6 * math.exp(-0.3 * l)
        w_a_vt = jnp.swapaxes(w_in_l[:, :, a_qk_end:a_end], 1, 2)

        qk_a = _mm_rope(h, ss, w_in_l, 0, a_qk_end, tables_a, shifts_a, A_SCORE_SCALE, A_WIDTH, seq)
        vt_a = _mm_transposed(h, ss, w_a_vt, 0, t_a)
        qkv_b = [_mm_rope_permute(h, ss, w_in_l, 0, a_end, g, tables_b, shifts_b[0], B_GROUPS[g][1], batch, seq)
                 for g in range(len(B_GROUPS))]
        tail, w_out_l = _mm_cols(h, ss, w_in_l, 0, tail0, in_width - tail0, F32, w_out, l, name="in_proj_tail")

        out_a = _attn_a(qk_a, vt_a, diff_lambda[l], diff_subln[l], lam_init, batch, seq, t_a)
        out_b = _attn_b(qkv_b, batch, seq)
        out_c = _pool(tail, b_end - tail0, pool_w[l].astype(BF16), pool_scale[l], batch, seq)
        out_d = _short_conv(tail, c_end - tail0, conv_w[l], batch, seq)

        x, h, ss = _out_proj(out_a, out_b, out_c, out_d, w_out_l, x, norm_mlp[l])
        act, w_down_l = _mlp_up(h, ss, w_up, w_down, l, 512)
        if l + 1 < depth:
            x, h, ss, w_next = _mm_residual(act, w_down_l, x, norm_mix[l + 1], w_in, l + 1)
            w_in_l = w_next[None]
        else:
            x, = _mm_residual(act, w_down_l, x)

    out = _rmsnorm(x, norm_final, F32)
    return out.reshape(batch, seq, d_model)
```
